```python
import jax, jax.numpy as jnp
from jax import lax
import numpy as np

D_MODEL = 1024
BATCH = 8
SEQ = 2048
DEPTH = 4
DEC_BATCH = 16
DEC_SEQ = 16
PAST_LEN = 4096

CHUNK = 64
HEAD_DIM = 64
SB_HEADS = 8
SB_DIM = SB_HEADS * HEAD_DIM
SB_BLOCK = 128
CONV_DIM = 256
CONV_W = 3
GLA_HEADS = 4
GLA_DK = 64
GLA_DV = 64
GLA_QK = GLA_HEADS * GLA_DK
GLA_DIM = GLA_HEADS * GLA_DV
GLA_GATE_RANK = 16
GLA_TAU = 16.0
MIX_DIM = SB_DIM + CONV_DIM + GLA_DIM
SPLIT_SIZES = (SB_DIM, SB_DIM, SB_DIM, CONV_DIM, CONV_DIM, CONV_DIM, GLA_QK, GLA_QK, GLA_DIM, GLA_DIM, GLA_GATE_RANK)
IN_DIM = sum(SPLIT_SIZES)
N_EXPERTS = 32
TOP_K = 4
D_FF = D_MODEL
SWIGLU_LIMIT = 7.0
SWIGLU_ALPHA = 1.702
DN_ALPHA = (2 * DEPTH) ** 0.25
DN_BETA = (8 * DEPTH) ** -0.25
NORM_EPS = 1e-5

kernel_name = 'hybrid_stickbreak_conv_gla_moe_stream_step'

F32 = jnp.float32


def layer_norm(x, g, b):
    xf = x.astype(F32)
    xc = xf - jnp.mean(xf, axis=-1, keepdims=True)
    var = jnp.mean(xc * xc, axis=-1, keepdims=True)
    return (xc * lax.rsqrt(var + NORM_EPS) * g + b).astype(x.dtype)


def split_proj(p):
    idx = [int(i) for i in np.cumsum(SPLIT_SIZES)[:-1]]
    return jnp.split(p, idx, axis=-1)


def sb_attend(q, k, v, q_pos, k_pos):
    z = jnp.einsum('bqhd,bshd->bhqs', q, k).astype(F32) * (HEAD_DIM ** -0.5)
    mask = k_pos[None, :] < q_pos[:, None]
    log_1m = jnp.where(mask, jax.nn.log_sigmoid(-z), 0.0)
    rest = lax.cumsum(log_1m, axis=3, reverse=True) - log_1m
    a = jnp.where(mask, jnp.exp(jax.nn.log_sigmoid(z) + rest), 0.0)
    return jnp.einsum('bhqs,bshd->bqhd', a.astype(v.dtype), v)


def sb_prompt(q, k, v):
    B, S, H, d = q.shape
    nb = S // SB_BLOCK
    qb = q.reshape(B, nb, SB_BLOCK, H, d).swapaxes(0, 1)
    k_pos = jnp.arange(S)

    def one_block(args):
        qi, i = args
        return sb_attend(qi, k, v, i * SB_BLOCK + jnp.arange(SB_BLOCK), k_pos)

    o = lax.map(one_block, (qb, jnp.arange(nb)))
    return o.swapaxes(0, 1).reshape(B, S, H, d)


def short_conv(z, prev, w):
    T = z.shape[1]
    ze = jnp.concatenate([prev.astype(z.dtype), z], axis=1)
    y = ze[:, 0:T] * w[0]
    for j in range(1, CONV_W):
        y = y + ze[:, j:j + T] * w[j]
    return y, ze[:, T:]


def gla_chunk_step(S, inp):
    q, k, v, la = inp
    L = q.shape[1]
    b = jnp.cumsum(la, axis=1)
    qe = q * jnp.exp(b)
    ke = k * jnp.exp(-b)
    kd = k * jnp.exp(b[:, -1:] - b)
    causal = jnp.tril(jnp.ones((L, L), dtype=bool))
    a = jnp.where(causal, jnp.einsum('bthk,bshk->bhts', qe, ke), 0.0)
    o = jnp.einsum('bthk,bhkv->bthv', qe, S) + jnp.einsum('bhts,bshv->bthv', a, v)
    S_new = jnp.exp(b[:, -1])[..., None] * S + jnp.einsum('bshk,bshv->bhkv', kd, v)
    return S_new, o


def gla(q, k, v, la, S0):
    B, T = q.shape[0], q.shape[1]
    L = min(CHUNK, T)
    n = T // L

    def to_chunks(t):
        return t.reshape(B, n, L, t.shape[2], t.shape[3]).swapaxes(0, 1)

    S, o = lax.scan(gla_chunk_step, S0, (to_chunks(q), to_chunks(k), to_chunks(v), to_chunks(la)))
    return o.swapaxes(0, 1).reshape(B, T, GLA_HEADS, GLA_DV), S


def mixer(x, k_past, v_past, conv_prev, gla_prev, w_in, conv_w, w_gate, b_gate, gla_norm_g, w_out):
    B, T, _ = x.shape
    p = x @ w_in
    qa, ka, va, bg, cg, u, qc, kc, vc, gc, al = split_proj(p)
    qa = qa.reshape(B, T, SB_HEADS, HEAD_DIM)
    ka = ka.reshape(B, T, SB_HEADS, HEAD_DIM)
    va = va.reshape(B, T, SB_HEADS, HEAD_DIM)
    if k_past is None:
        oa = sb_prompt(qa, ka, va)
    else:
        P = k_past.shape[1]
        k_all = jnp.concatenate([k_past.astype(ka.dtype), ka], axis=1)
        v_all = jnp.concatenate([v_past.astype(va.dtype), va], axis=1)
        oa = sb_attend(qa, k_all, v_all, P + jnp.arange(T), jnp.arange(P + T))
    conv_out, conv_new = short_conv(cg * u, conv_prev, conv_w)
    ob = bg * conv_out
    log_a = jax.nn.log_sigmoid((al @ w_gate + b_gate).astype(F32)) / GLA_TAU
    oc, s_new = gla(qc.reshape(B, T, GLA_HEADS, GLA_DK).astype(F32) * (GLA_DK ** -0.5),
                    kc.reshape(B, T, GLA_HEADS, GLA_DK).astype(F32),
                    vc.reshape(B, T, GLA_HEADS, GLA_DV).astype(F32),
                    log_a.reshape(B, T, GLA_HEADS, GLA_DK),
                    gla_prev.astype(F32))
    oc = oc * lax.rsqrt(jnp.mean(oc * oc, axis=-1, keepdims=True) + NORM_EPS) * gla_norm_g.reshape(GLA_HEADS, GLA_DV)
    oc = oc.reshape(B, T, GLA_DIM) * jax.nn.silu(gc.astype(F32))
    mixed = jnp.concatenate([oa.reshape(B, T, SB_DIM), ob, oc.astype(x.dtype)], axis=-1) @ w_out
    return mixed, ka, va, conv_new, s_new.astype(x.dtype)


def moe(x, w_router, b_router, w_up, b_up, w_down, b_down):
    B, T, D = x.shape
    xt = x.reshape(B * T, D)
    logits = (xt @ w_router + b_router).astype(F32)
    vals, idx = lax.top_k(logits, TOP_K)
    gates = jax.nn.softmax(vals, axis=-1)
    dense_gate = jnp.sum(jax.nn.one_hot(idx, N_EXPERTS, dtype=F32) * gates[..., None], axis=1)
    y = jnp.zeros((B * T, D), F32)
    for e in range(N_EXPERTS):
        h = xt @ w_up[e] + b_up[e]
        glu = jnp.minimum(h[:, :D_FF], SWIGLU_LIMIT)
        lin = jnp.clip(h[:, D_FF:], -SWIGLU_LIMIT, SWIGLU_LIMIT)
        act = glu * jax.nn.sigmoid(SWIGLU_ALPHA * glu) * (lin + 1.0)
        y = y + dense_gate[:, e:e + 1] * (act @ w_down[e] + b_down[e])
    return y.astype(x.dtype).reshape(B, T, D)


def block(h, k_past, v_past, conv_prev, gla_prev, w_in, conv_w, w_gate, b_gate, gla_norm_g, w_out,
          ln1_g, ln1_b, w_router, b_router, w_up, b_up, w_down, b_down, ln2_g, ln2_b):
    m, k_new, v_new, conv_new, s_new = mixer(h, k_past, v_past, conv_prev, gla_prev,
                                             w_in, conv_w, w_gate, b_gate, gla_norm_g, w_out)
    h = layer_norm(DN_ALPHA * h + m, ln1_g, ln1_b)
    h = layer_norm(DN_ALPHA * h + moe(h, w_router, b_router, w_up, b_up, w_down, b_down), ln2_g, ln2_b)
    return h, k_new, v_new, conv_new, s_new


def setup_inputs(seed: int = 0) -> dict:
    key = jax.random.key(seed)
    ks = jax.random.split(key, 24)
    n = lambda i, shape: jax.random.normal(ks[i], shape, F32)
    return {
        'x_prompt': n(0, (BATCH, SEQ, D_MODEL)),
        'x_sample': n(1, (DEC_BATCH, DEC_SEQ, D_MODEL)),
        'cache_k': n(2, (DEPTH, DEC_BATCH, PAST_LEN, SB_HEADS, HEAD_DIM)),
        'cache_v': n(3, (DEPTH, DEC_BATCH, PAST_LEN, SB_HEADS, HEAD_DIM)),
        'state_conv': n(4, (DEPTH, DEC_BATCH, CONV_W - 1, CONV_DIM)),
        'state_gla': n(5, (DEPTH, DEC_BATCH, GLA_HEADS, GLA_DK, GLA_DV)),
        'w_in': n(6, (DEPTH, D_MODEL, IN_DIM)) * D_MODEL ** -0.5,
        'conv_w': n(7, (DEPTH, CONV_W, CONV_DIM)) * CONV_W ** -0.5,
        'w_gate': n(8, (DEPTH, GLA_GATE_RANK, GLA_QK)) * GLA_GATE_RANK ** -0.5,
        'b_gate': 0.1 * n(9, (DEPTH, GLA_QK)),
        'gla_norm_g': 1.0 + 0.02 * n(10, (DEPTH, GLA_DIM)),
        'w_out': n(11, (DEPTH, MIX_DIM, D_MODEL)) * (MIX_DIM ** -0.5 * DN_BETA),
        'ln1_g': 1.0 + 0.02 * n(12, (DEPTH, D_MODEL)),
        'ln1_b': 0.02 * n(13, (DEPTH, D_MODEL)),
        'w_router': n(14, (DEPTH, D_MODEL, N_EXPERTS)) * D_MODEL ** -0.5,
        'b_router': 0.01 * n(15, (DEPTH, N_EXPERTS)),
        'w_up': n(16, (DEPTH, N_EXPERTS, D_MODEL, 2 * D_FF)) * D_MODEL ** -0.5,
        'b_up': 0.02 * n(17, (DEPTH, N_EXPERTS, 2 * D_FF)),
        'w_down': n(18, (DEPTH, N_EXPERTS, D_FF, D_MODEL)) * (D_FF ** -0.5 * DN_BETA),
        'b_down': 0.02 * n(19, (DEPTH, N_EXPERTS, D_MODEL)),
        'ln2_g': 1.0 + 0.02 * n(20, (DEPTH, D_MODEL)),
        'ln2_b': 0.02 * n(21, (DEPTH, D_MODEL)),
    }


def reference(x_prompt, x_sample, cache_k, cache_v, state_conv, state_gla, w_in, conv_w, w_gate, b_gate,
              gla_norm_g, w_out, ln1_g, ln1_b, w_router, b_router, w_up, b_up, w_down, b_down, ln2_g, ln2_b):
    hp, hs = x_prompt, x_sample
    bp = x_prompt.shape[0]
    kp, vp, cp, sp, ksm, vsm, csm, ssm = [], [], [], [], [], [], [], []
    for l in range(DEPTH):
        lw = (w_in[l], conv_w[l], w_gate[l], b_gate[l], gla_norm_g[l], w_out[l], ln1_g[l], ln1_b[l],
              w_router[l], b_router[l], w_up[l], b_up[l], w_down[l], b_down[l], ln2_g[l], ln2_b[l])
        hp, k1, v1, c1, s1 = block(hp, None, None,
                                   jnp.zeros((bp, CONV_W - 1, CONV_DIM), hp.dtype),
                                   jnp.zeros((bp, GLA_HEADS, GLA_DK, GLA_DV), hp.dtype), *lw)
        hs, k2, v2, c2, s2 = block(hs, cache_k[l], cache_v[l], state_conv[l], state_gla[l], *lw)
        kp.append(k1); vp.append(v1); cp.append(c1); sp.append(s1)
        ksm.append(k2); vsm.append(v2); csm.append(c2); ssm.append(s2)
    return (hp, hs, jnp.stack(kp), jnp.stack(vp), jnp.stack(cp), jnp.stack(sp),
            jnp.stack(ksm), jnp.stack(vsm), jnp.stack(csm), jnp.stack(ssm))
```

```python
import functools

import jax
import jax.numpy as jnp
from jax import lax
from jax.experimental import pallas as pl
from jax.experimental.pallas import tpu as pltpu

F32 = jnp.float32
BF16 = jnp.bfloat16
I32 = jnp.int32

HEAD_DIM = 64
SB_HEADS = 8
SB_DIM = SB_HEADS * HEAD_DIM
CONV_DIM = 256
CONV_W = 3
GLA_HEADS = 4
GLA_DK = 64
GLA_DIM = GLA_HEADS * GLA_DK
GLA_RANK = 16
GLA_TAU = 16.0
GLA_CHUNK = 64
N_EXPERTS = 32
TOP_K = 4
SWIGLU_LIMIT = 7.0
SWIGLU_ALPHA = 1.702
NORM_EPS = 1e-5

LANES = 128
TOKEN_TILE = 256
EXPERT_TILE = 256
SB_TQ = 256
SB_BK = 256
DEC_TK = 512
GLA_TC = 256
VMEM_LIMIT = 48 * 1024 * 1024


def _dot(a, b):
    return jnp.dot(a, b, preferred_element_type=F32)


def _dot_nt(a, b):
    return lax.dot_general(a, b, (((1,), (1,)), ((), ())), preferred_element_type=F32)


def _split_bf16(x):
    hi = x.astype(BF16)
    lo = (x - hi.astype(F32)).astype(BF16)
    return hi, lo


def _dot_exact_rhs(x, m):
    hi, lo = _split_bf16(x)
    return _dot(hi, m) + _dot(lo, m)


def _softplus(z):
    return jnp.maximum(z, 0.0) + jnp.log1p(jnp.exp(-jnp.abs(z)))


def _layer_norm(y, g, b):
    mu = jnp.mean(y, axis=-1, keepdims=True)
    yc = y - mu
    var = jnp.mean(yc * yc, axis=-1, keepdims=True)
    return yc * lax.rsqrt(var + NORM_EPS) * g + b


def _in_proj_kernel(x_ref, wq_ref, wk_ref, wv_ref, wr_ref, wal_ref, wg_ref, bg_ref,
                    q_ref, k_ref, v_ref, r_ref, la_ref):
    xb = x_ref[...].astype(BF16)
    for c in range(0, SB_DIM, 256):
        q_ref[:, c:c + 256] = (_dot(xb, wq_ref[:, c:c + 256]) * (HEAD_DIM ** -0.5)).astype(BF16)
        k_ref[:, c:c + 256] = _dot(xb, wk_ref[:, c:c + 256])
        v_ref[:, c:c + 256] = _dot(xb, wv_ref[:, c:c + 256])
    for c in range(0, r_ref.shape[1], 256):
        r_ref[:, c:c + 256] = _dot(xb, wr_ref[:, c:c + 256])
    al = _dot(xb, wal_ref[...])
    g = _dot(al.astype(BF16), wg_ref[...]) + bg_ref[...]
    la_ref[...] = -_softplus(-g) * (1.0 / GLA_TAU)


def _in_proj(x, w_in, w_gate, b_gate):
    n, d = x.shape
    tm = TOKEN_TILE
    wb = w_in.astype(BF16)
    o_r = 3 * SB_DIM
    n_r = 3 * CONV_DIM + 4 * GLA_DIM
    wq, wk, wv = wb[:, :SB_DIM], wb[:, SB_DIM:2 * SB_DIM], wb[:, 2 * SB_DIM:o_r]
    wr, wal = wb[:, o_r:o_r + n_r], wb[:, o_r + n_r:]
    full = lambda a: pl.BlockSpec(a.shape, lambda i: (0, 0))
    row = lambda w: pl.BlockSpec((tm, w), lambda i: (i, 0))
    wg = w_gate.astype(BF16)
    bg = b_gate.reshape(1, -1)
    return pl.pallas_call(
        _in_proj_kernel,
        grid=(n // tm,),
        in_specs=[row(d), full(wq), full(wk), full(wv), full(wr), full(wal), full(wg), full(bg)],
        out_specs=[row(SB_DIM), row(SB_DIM), row(SB_DIM), row(n_r), row(GLA_DIM)],
        out_shape=[jax.ShapeDtypeStruct((n, SB_DIM), BF16),
                   jax.ShapeDtypeStruct((n, SB_DIM), F32),
                   jax.ShapeDtypeStruct((n, SB_DIM), F32),
                   jax.ShapeDtypeStruct((n, n_r), F32),
                   jax.ShapeDtypeStruct((n, GLA_DIM), F32)],
        compiler_params=pltpu.CompilerParams(dimension_semantics=("arbitrary",),
                                             vmem_limit_bytes=VMEM_LIMIT),
        name="in_proj",
    )(x, wq, wk, wv, wr, wal, wg, bg)


def _sb_block(qh, kb, vb, tri, run, mask):
    z = _dot_nt(qh, kb)
    sp = _softplus(z)
    l1m = -sp
    lsig = z - sp
    if mask is not None:
        l1m = jnp.where(mask, l1m, 0.0)
    rest = _dot_exact_rhs(l1m, tri) + run
    a = jnp.exp(lsig + rest)
    if mask is not None:
        a = jnp.where(mask, a, 0.0)
    pv = _dot(a.astype(BF16), vb)
    return pv, run + jnp.sum(l1m, axis=1, keepdims=True)


def _strict_upper(n):
    r = lax.broadcasted_iota(I32, (n, n), 0)
    c = lax.broadcasted_iota(I32, (n, n), 1)
    return jnp.where(r > c, 1.0, 0.0).astype(BF16)


def _sb_prompt_kernel(q_ref, k_ref, v_ref, o_ref, *, tq, bk):
    qi = pl.program_id(2)
    q = q_ref[...]
    lane = lax.broadcasted_iota(I32, (1, LANES), 1)
    tri = _strict_upper(bk)
    n_full = (qi * tq) // bk
    qpos = qi * tq + lax.broadcasted_iota(I32, (tq, bk), 0)
    kpos = n_full * bk + lax.broadcasted_iota(I32, (tq, bk), 1)
    diag_mask = kpos < qpos
    out = jnp.zeros((tq, LANES), F32)
    for h in range(LANES // HEAD_DIM):
        in_head = (lane // HEAD_DIM) == h
        qh = jnp.where(in_head, q, jnp.zeros_like(q))

        def tile(jb, run, mask):
            ks = pl.multiple_of(jb * bk, bk)
            kb = k_ref[pl.ds(ks, bk), :].astype(BF16)
            vb = v_ref[pl.ds(ks, bk), :].astype(BF16)
            return _sb_block(qh, kb, vb, tri, run, mask)

        acc, run = tile(n_full, jnp.zeros((tq, 1), F32), diag_mask)

        def body(j, carry):
            acc, run = carry
            pv, run = tile(n_full - 1 - j, run, None)
            return acc + pv, run

        acc, run = lax.fori_loop(0, n_full, body, (acc, run))
        out = jnp.where(in_head, acc, out)
    o_ref[...] = out.astype(o_ref.dtype)


def _sb_prompt(q, k, v, batch, seq):
    tq, bk = min(SB_TQ, seq), min(SB_BK, seq)
    nq = seq // tq
    hp = SB_DIM // LANES
    return pl.pallas_call(
        functools.partial(_sb_prompt_kernel, tq=tq, bk=bk),
        grid=(batch, hp, nq),
        in_specs=[pl.BlockSpec((tq, LANES), lambda b, p, i: (b * nq + i, p)),
                  pl.BlockSpec((seq, LANES), lambda b, p, i: (b, p)),
                  pl.BlockSpec((seq, LANES), lambda b, p, i: (b, p))],
        out_specs=pl.BlockSpec((tq, LANES), lambda b, p, i: (b * nq + i, p)),
        out_shape=jax.ShapeDtypeStruct((batch * seq, SB_DIM), BF16),
        compiler_params=pltpu.CompilerParams(
            dimension_semantics=("arbitrary", "arbitrary", "arbitrary"),
            vmem_limit_bytes=VMEM_LIMIT),
        name="sb_prompt",
    )(q, k, v)


def _sb_decode_kernel(q_ref, kn_ref, vn_ref, kc_ref, vc_ref, o_ref, acc_ref, run_ref, *, t, tk, bk):
    j = pl.program_id(1)
    nj = pl.num_programs(1)
    q = q_ref[...]
    lane = lax.broadcasted_iota(I32, (1, SB_DIM), 1)

    @pl.when(j == 0)
    def _():
        kn = kn_ref[...].astype(BF16)
        vn = vn_ref[...].astype(BF16)
        r = lax.broadcasted_iota(I32, (t, t), 0)
        c = lax.broadcasted_iota(I32, (t, t), 1)
        mask = c < r
        tri = _strict_upper(t)
        for h in range(SB_HEADS):
            in_head = (lane // HEAD_DIM) == h
            qh = jnp.where(in_head, q, jnp.zeros_like(q))
            pv, run = _sb_block(qh, kn, vn, tri, jnp.zeros((t, 1), F32), mask)
            acc_ref[h] = pv[:, h * HEAD_DIM:(h + 1) * HEAD_DIM]
            run_ref[h] = run

    tri = _strict_upper(bk)
    for h in range(SB_HEADS):
        qh = q[:, h * HEAD_DIM:(h + 1) * HEAD_DIM]
        acc = acc_ref[h]
        run = run_ref[h]
        for c in range(tk // bk - 1, -1, -1):
            rows = pl.ds(c * bk * SB_HEADS + h, bk, stride=SB_HEADS)
            kb = kc_ref[rows, :].astype(BF16)
            vb = vc_ref[rows, :].astype(BF16)
            pv, run = _sb_block(qh, kb, vb, tri, run, None)
            acc = acc + pv
        acc_ref[h] = acc
        run_ref[h] = run

    @pl.when(j == nj - 1)
    def _():
        for h in range(SB_HEADS):
            o_ref[:, h * HEAD_DIM:(h + 1) * HEAD_DIM] = acc_ref[h].astype(o_ref.dtype)


def _sb_decode(q, k, v, cache_k, cache_v, row0, batch, t):
    past = cache_k.shape[1] // SB_HEADS
    tk = min(DEC_TK, past)
    bk = min(SB_BK, tk)
    nkb = past // tk
    rb = row0 // t
    new = pl.BlockSpec((t, SB_DIM), lambda b, j: (rb + b, 0))
    cache = pl.BlockSpec((None, tk * SB_HEADS, HEAD_DIM), lambda b, j: (b, nkb - 1 - j, 0))
    return pl.pallas_call(
        functools.partial(_sb_decode_kernel, t=t, tk=tk, bk=bk),
        grid=(batch, nkb),
        in_specs=[new, new, new, cache, cache],
        out_specs=pl.BlockSpec((t, SB_DIM), lambda b, j: (b, 0)),
        out_shape=jax.ShapeDtypeStruct((batch * t, SB_DIM), BF16),
        scratch_shapes=[pltpu.VMEM((SB_HEADS, t, HEAD_DIM), F32),
                        pltpu.VMEM((SB_HEADS, t, 1), F32)],
        compiler_params=pltpu.CompilerParams(dimension_semantics=("arbitrary", "arbitrary"),
                                             vmem_limit_bytes=VMEM_LIMIT),
        name="sb_decode",
    )(q, k, v, cache_k, cache_v)


def _conv_gla_kernel(bg_ref, cg_ref, u_ref, qc_ref, kc_ref, vc_ref, gc_ref, la_ref,
                     cw_ref, ng_ref, cprev_ref, sprev_ref,
                     o_ref, cnew_ref, snew_ref, ctail_ref, st_ref, *, tc, chunk):
    ti = pl.program_id(1)
    nt = pl.num_programs(1)
    n_pair = GLA_DIM // LANES

    @pl.when(ti == 0)
    def _():
        ctail_ref[...] = cprev_ref[...]
        for p in range(n_pair):
            for hh in range(LANES // GLA_DK):
                h = p * (LANES // GLA_DK) + hh
                st_ref[p, hh * GLA_DK:(hh + 1) * GLA_DK, :] = jnp.zeros((GLA_DK, LANES), F32)
            for hh in range(LANES // GLA_DK):
                h = p * (LANES // GLA_DK) + hh
                st_ref[p, hh * GLA_DK:(hh + 1) * GLA_DK, hh * GLA_DK:(hh + 1) * GLA_DK] = sprev_ref[h].T

    z = cg_ref[...] * u_ref[...]
    tail = ctail_ref[...]
    row = lax.broadcasted_iota(I32, z.shape, 0)
    z1 = jnp.where(row < 1, tail[1:2, :], pltpu.roll(z, 1, 0))
    z2 = jnp.where(row < 2, jnp.where(row < 1, tail[0:1, :], tail[1:2, :]), pltpu.roll(z, 2, 0))
    cw = cw_ref[...]
    y = z2 * cw[0:1, :] + z1 * cw[1:2, :] + z * cw[2:3, :]
    o_ref[:, 0:CONV_DIM] = (bg_ref[...] * y).astype(o_ref.dtype)
    if tc >= 2:
        ctail_ref[...] = z[tc - 2:tc, :]
    else:
        ctail_ref[...] = jnp.concatenate([tail[1:2, :], z], axis=0)

    r = lax.broadcasted_iota(I32, (chunk, chunk), 0)
    c = lax.broadcasted_iota(I32, (chunk, chunk), 1)
    lower_incl = jnp.where(c <= r, 1.0, 0.0).astype(BF16)
    lane = lax.broadcasted_iota(I32, (1, LANES), 1)
    lr = lax.broadcasted_iota(I32, (LANES, LANES), 0) // GLA_DK
    lc = lax.broadcasted_iota(I32, (LANES, LANES), 1) // GLA_DK
    same_head = lr == lc
    head_mean = jnp.where(same_head, 1.0 / GLA_DK, 0.0).astype(BF16)
    n_hh = LANES // GLA_DK
    r2 = lax.broadcasted_iota(I32, (chunk, n_hh * chunk), 0)
    c2 = lax.broadcasted_iota(I32, (chunk, n_hh * chunk), 1)
    causal2 = c2 - jnp.where(c2 >= chunk, chunk, 0) <= r2
    for p in range(n_pair):
        cols = slice(p * LANES, (p + 1) * LANES)
        st = st_ref[p]
        for ci in range(tc // chunk):
            rows = slice(ci * chunk, (ci + 1) * chunk)
            q = qc_ref[rows, cols] * (GLA_DK ** -0.5)
            k = kc_ref[rows, cols]
            v = vc_ref[rows, cols]
            la = la_ref[rows, cols]
            b = _dot_exact_rhs_left(lower_incl, la)
            b_last = b[chunk - 1:chunk, :]
            qe = (q * jnp.exp(b)).astype(BF16)
            ke = (k * jnp.exp(-b)).astype(BF16)
            kd = (k * jnp.exp(b_last - b)).astype(BF16)
            vb = v.astype(BF16)
            zeros = jnp.zeros_like(ke)
            ke_st = jnp.concatenate([jnp.where((lane // GLA_DK) == hh, ke, zeros)
                                     for hh in range(LANES // GLA_DK)], axis=0)
            v_st = jnp.concatenate([jnp.where((lane // GLA_DK) == hh, vb, zeros)
                                    for hh in range(LANES // GLA_DK)], axis=0)
            a = jnp.where(causal2, _dot_nt(qe, ke_st), 0.0)
            o = _dot_nt(qe, st.astype(BF16)) + _dot(a.astype(BF16), v_st)
            upd = _dot(vb.T, kd)
            st = jnp.where(same_head, st * jnp.exp(b_last) + upd, 0.0)
            ms = _dot_exact_rhs(o * o, head_mean)
            o = o * lax.rsqrt(ms + NORM_EPS) * ng_ref[:, cols]
            g = gc_ref[rows, cols]
            o = o * (g * (1.0 / (1.0 + jnp.exp(-g))))
            o_ref[rows, CONV_DIM + p * LANES:CONV_DIM + (p + 1) * LANES] = o.astype(o_ref.dtype)
        st_ref[p] = st

    @pl.when(ti == nt - 1)
    def _():
        cnew_ref[...] = ctail_ref[...]
        for p in range(n_pair):
            for hh in range(LANES // GLA_DK):
                h = p * (LANES // GLA_DK) + hh
                blk = st_ref[p, hh * GLA_DK:(hh + 1) * GLA_DK, :]
                snew_ref[h] = blk.T[hh * GLA_DK:(hh + 1) * GLA_DK, :]


def _dot_exact_rhs_left(m, x):
    hi, lo = _split_bf16(x)
    return _dot(m, hi) + _dot(m, lo)


def _conv_gla(rest, la, conv_w, norm_g, conv_prev, gla_prev, row0, batch, t):
    tc = min(GLA_TC, t)
    chunk = min(GLA_CHUNK, t)
    nt = t // tc
    rb = row0 // tc
    col = lambda j: pl.BlockSpec((tc, CONV_DIM), lambda b, i: (rb + b * nt + i, j))
    const2 = lambda a: pl.BlockSpec(a.shape, lambda b, i: (0, 0))
    ng = norm_g.reshape(1, GLA_DIM)
    return pl.pallas_call(
        functools.partial(_conv_gla_kernel, tc=tc, chunk=chunk),
        grid=(batch, nt),
        in_specs=[col(0), col(1), col(2), col(3), col(4), col(5), col(6),
                  pl.BlockSpec((tc, GLA_DIM), lambda b, i: (rb + b * nt + i, 0)),
                  const2(conv_w), const2(ng),
                  pl.BlockSpec((None, CONV_W - 1, CONV_DIM), lambda b, i: (b, 0, 0)),
                  pl.BlockSpec((None, GLA_HEADS, GLA_DK, GLA_DK), lambda b, i: (b, 0, 0, 0))],
        out_specs=[pl.BlockSpec((tc, CONV_DIM + GLA_DIM), lambda b, i: (b * nt + i, 0)),
                   pl.BlockSpec((None, CONV_W - 1, CONV_DIM), lambda b, i: (b, 0, 0)),
                   pl.BlockSpec((None, GLA_HEADS, GLA_DK, GLA_DK), lambda b, i: (b, 0, 0, 0))],
        out_shape=[jax.ShapeDtypeStruct((batch * t, CONV_DIM + GLA_DIM), BF16),
                   jax.ShapeDtypeStruct((batch, CONV_W - 1, CONV_DIM), F32),
                   jax.ShapeDtypeStruct((batch, GLA_HEADS, GLA_DK, GLA_DK), F32)],
        scratch_shapes=[pltpu.VMEM((CONV_W - 1, CONV_DIM), F32),
                        pltpu.VMEM((GLA_DIM // LANES, LANES, LANES), F32)],
        compiler_params=pltpu.CompilerParams(dimension_semantics=("arbitrary", "arbitrary"),
                                             vmem_limit_bytes=VMEM_LIMIT),
        name="conv_gla",
    )(rest, rest, rest, rest, rest, rest, rest, la, conv_w, ng, conv_prev, gla_prev)


def _out_router_kernel(oa_ref, obc_ref, x_ref, wa_ref, wb_ref, g_ref, b_ref, wr_ref, br_ref,
                       h_ref, hb_ref, idx_ref, gate_ref, rank_ref, cnt_ref, carry_ref, *, alpha):
    i = pl.program_id(0)
    tm = x_ref.shape[0]

    @pl.when(i == 0)
    def _():
        carry_ref[...] = jnp.zeros_like(carry_ref)

    m = _dot(oa_ref[...], wa_ref[...]) + _dot(obc_ref[...], wb_ref[...])
    h = _layer_norm(alpha * x_ref[...] + m, g_ref[...], b_ref[...])
    h_ref[...] = h
    hb_ref[...] = h.astype(BF16)

    h_hi, h_lo = _split_bf16(h)
    w_hi, w_lo = _split_bf16(wr_ref[...])
    logit = _dot_nt(w_hi, h_hi) + _dot_nt(w_hi, h_lo) + _dot_nt(w_lo, h_hi) + br_ref[...]
    eid = lax.broadcasted_iota(I32, (N_EXPERTS, tm), 0)
    r = lax.broadcasted_iota(I32, (tm, tm), 0)
    c = lax.broadcasted_iota(I32, (tm, tm), 1)
    before = jnp.where(r < c, 1.0, 0.0).astype(BF16)
    base = carry_ref[...]
    vals, idxs, ranks = [], [], []
    for _ in range(TOP_K):
        mx = jnp.max(logit, axis=0, keepdims=True)
        sel = jnp.min(jnp.where(logit == mx, eid, N_EXPERTS), axis=0, keepdims=True)
        hit = eid == sel
        logit = jnp.where(hit, -jnp.inf, logit)
        onehot = jnp.where(hit, 1.0, 0.0)
        prior = _dot(onehot.astype(BF16), before) + base
        ranks.append(jnp.sum(onehot * prior, axis=0, keepdims=True))
        base = base + jnp.sum(onehot, axis=1, keepdims=True)
        vals.append(mx)
        idxs.append(sel)
    carry_ref[...] = base
    e = [jnp.exp(v - vals[0]) for v in vals]
    inv = 1.0 / (e[0] + e[1] + e[2] + e[3])
    idx_ref[...] = jnp.concatenate(idxs, axis=0)
    gate_ref[...] = jnp.concatenate([ek * inv for ek in e], axis=0)
    rank_ref[...] = jnp.concatenate(ranks, axis=0).astype(I32)
    cnt_ref[...] = jnp.broadcast_to(base, cnt_ref.shape).astype(I32)


def _out_router(oa, obc, x, w_out, ln_g, ln_b, w_router, b_router, alpha):
    n, d = x.shape
    tm = TOKEN_TILE
    wb16 = w_out.astype(BF16)
    wa, wb = wb16[:SB_DIM], wb16[SB_DIM:]
    wr = w_router.T
    br = b_router.reshape(N_EXPERTS, 1)
    g, b = ln_g.reshape(1, d), ln_b.reshape(1, d)
    full = lambda a: pl.BlockSpec(a.shape, lambda i: (0, 0))
    row = lambda w: pl.BlockSpec((tm, w), lambda i: (i, 0))
    colb = pl.BlockSpec((TOP_K, tm), lambda i: (0, i))
    return pl.pallas_call(
        functools.partial(_out_router_kernel, alpha=alpha),
        grid=(n // tm,),
        in_specs=[row(SB_DIM), row(CONV_DIM + GLA_DIM), row(d), full(wa), full(wb), full(g), full(b),
                  full(wr), full(br)],
        out_specs=[row(d), row(d), colb, colb, colb,
                   pl.BlockSpec((N_EXPERTS, LANES), lambda i: (0, 0))],
        out_shape=[jax.ShapeDtypeStruct((n, d), F32),
                   jax.ShapeDtypeStruct((n, d), BF16),
                   jax.ShapeDtypeStruct((TOP_K, n), I32),
                   jax.ShapeDtypeStruct((TOP_K, n), F32),
                   jax.ShapeDtypeStruct((TOP_K, n), I32),
                   jax.ShapeDtypeStruct((N_EXPERTS, LANES), I32)],
        scratch_shapes=[pltpu.VMEM((N_EXPERTS, 1), F32)],
        compiler_params=pltpu.CompilerParams(dimension_semantics=("arbitrary",),
                                             vmem_limit_bytes=VMEM_LIMIT),
        name="out_router",
    )(oa, obc, x, wa, wb, g, b, wr, br)


def _expert_kernel(te_ref, nu_ref, x_ref, wu_ref, bu_ref, wd_ref, bd_ref, y_ref, wu16_ref, wd16_ref):
    i = pl.program_id(0)
    e = te_ref[i]
    prev = te_ref[jnp.maximum(i - 1, 0)]
    dff = wd_ref.shape[0]

    @pl.when((i == 0) | (e != prev))
    def _():
        step = 128

        def cast(s, _):
            rows = pl.ds(pl.multiple_of(s * step, step), step)
            wu16_ref[rows, :] = wu_ref[rows, :].astype(BF16)
            return 0

        lax.fori_loop(0, wu_ref.shape[0] // step, cast, 0)

        def cast_d(s, _):
            rows = pl.ds(pl.multiple_of(s * step, step), step)
            wd16_ref[rows, :] = wd_ref[rows, :].astype(BF16)
            return 0

        lax.fori_loop(0, dff // step, cast_d, 0)

    @pl.when(i < nu_ref[0])
    def _():
        x = x_ref[...]
        glu = jnp.minimum(_dot(x, wu16_ref[:, :dff]) + bu_ref[:, :dff], SWIGLU_LIMIT)
        lin = jnp.clip(_dot(x, wu16_ref[:, dff:]) + bu_ref[:, dff:], -SWIGLU_LIMIT, SWIGLU_LIMIT)
        act = glu * (1.0 / (1.0 + jnp.exp(-SWIGLU_ALPHA * glu))) * (lin + 1.0)
        y_ref[...] = _dot(act.astype(BF16), wd16_ref[...]) + bd_ref[...]

    @pl.when(i >= nu_ref[0])
    def _():
        y_ref[...] = jnp.zeros_like(y_ref)


def _experts(x_sorted, tile_expert, n_used, w_up, b_up, w_down, b_down):
    ns, d = x_sorted.shape
    tm = EXPERT_TILE
    dff = w_down.shape[1]
    bu = b_up.reshape(N_EXPERTS, 1, 2 * dff)
    bd = b_down.reshape(N_EXPERTS, 1, d)
    grid_spec = pltpu.PrefetchScalarGridSpec(
        num_scalar_prefetch=2,
        grid=(ns // tm,),
        in_specs=[pl.BlockSpec((tm, d), lambda i, te, nu: (i, 0)),
                  pl.BlockSpec((None, d, 2 * dff), lambda i, te, nu: (te[i], 0, 0)),
                  pl.BlockSpec((None, 1, 2 * dff), lambda i, te, nu: (te[i], 0, 0)),
                  pl.BlockSpec((None, dff, d), lambda i, te, nu: (te[i], 0, 0)),
                  pl.BlockSpec((None, 1, d), lambda i, te, nu: (te[i], 0, 0))],
        out_specs=pl.BlockSpec((tm, d), lambda i, te, nu: (i, 0)),
        scratch_shapes=[pltpu.VMEM((d, 2 * dff), BF16), pltpu.VMEM((dff, d), BF16)],
    )
    return pl.pallas_call(
        _expert_kernel,
        grid_spec=grid_spec,
        out_shape=jax.ShapeDtypeStruct((ns, d), F32),
        compiler_params=pltpu.CompilerParams(dimension_semantics=("arbitrary",),
                                             vmem_limit_bytes=VMEM_LIMIT),
        name="experts",
    )(tile_expert, n_used, x_sorted, w_up, bu, w_down, bd)


def _combine_kernel(y_ref, gate_ref, h_ref, g_ref, b_ref, o_ref, *, alpha):
    gate = gate_ref[...]
    acc = jnp.zeros(h_ref.shape, F32)
    for k in range(TOP_K):
        acc = acc + gate[:, k:k + 1] * y_ref[k]
    o_ref[...] = _layer_norm(alpha * h_ref[...] + acc, g_ref[...], b_ref[...])


def _combine(y_tok, gates, h, ln_g, ln_b, alpha):
    n, d = h.shape
    tm = TOKEN_TILE
    g, b = ln_g.reshape(1, d), ln_b.reshape(1, d)
    full = lambda a: pl.BlockSpec(a.shape, lambda i: (0, 0))
    return pl.pallas_call(
        functools.partial(_combine_kernel, alpha=alpha),
        grid=(n // tm,),
        in_specs=[pl.BlockSpec((TOP_K, tm, d), lambda i: (0, i, 0)),
                  pl.BlockSpec((tm, TOP_K), lambda i: (i, 0)),
                  pl.BlockSpec((tm, d), lambda i: (i, 0)), full(g), full(b)],
        out_specs=pl.BlockSpec((tm, d), lambda i: (i, 0)),
        out_shape=jax.ShapeDtypeStruct((n, d), F32),
        compiler_params=pltpu.CompilerParams(dimension_semantics=("arbitrary",),
                                             vmem_limit_bytes=VMEM_LIMIT),
        name="combine",
    )(y_tok, gates, h, g, b)


def _moe(h, hb, idx, gates, rank, counts, w_up, b_up, w_down, b_down, ln_g, ln_b, alpha):
    n, d = h.shape
    tm = EXPERT_TILE
    ns = TOP_K * n + N_EXPERTS * tm
    cnt = counts[:, 0]
    padded = ((cnt + tm - 1) // tm) * tm
    ends = jnp.cumsum(padded)
    offs = ends - padded
    pos = jnp.take(offs, idx) + rank
    n_used = (ends[-1] // tm).astype(I32).reshape(1)
    tile_start = jnp.arange(ns // tm, dtype=I32) * tm
    tile_expert = jnp.minimum(jnp.searchsorted(ends, tile_start, side="right"), N_EXPERTS - 1)
    last_used = jnp.take(tile_expert, jnp.maximum(n_used[0] - 1, 0))
    tile_expert = jnp.where(tile_start < ends[-1], tile_expert, last_used).astype(I32)
    token = jnp.broadcast_to(jnp.arange(n, dtype=I32)[None, :], (TOP_K, n))
    src = jnp.zeros((ns,), I32).at[pos.reshape(-1)].set(token.reshape(-1))
    x_sorted = jnp.take(hb, src, axis=0)
    y_sorted = _experts(x_sorted, tile_expert, n_used, w_up, b_up, w_down, b_down)
    y_tok = jnp.take(y_sorted, pos.reshape(-1), axis=0).reshape(TOP_K, n, d)
    return _combine(y_tok, gates.T, h, ln_g, ln_b, alpha)


def kernel(x_prompt, x_sample, cache_k, cache_v, state_conv, state_gla, w_in, conv_w, w_gate, b_gate,
           gla_norm_g, w_out, ln1_g, ln1_b, w_router, b_router, w_up, b_up, w_down, b_down, ln2_g, ln2_b):
    depth = w_in.shape[0]
    bp, seq, d = x_prompt.shape
    bs, ts, _ = x_sample.shape
    past = cache_k.shape[2]
    n_p = bp * seq
    alpha = float((2 * depth) ** 0.25)
    x = jnp.concatenate([x_prompt.reshape(n_p, d), x_sample.reshape(bs * ts, d)], axis=0)
    ck = cache_k.reshape(depth, bs, past * SB_HEADS, HEAD_DIM)
    cv = cache_v.reshape(depth, bs, past * SB_HEADS, HEAD_DIM)
    zero_conv = jnp.zeros((bp, CONV_W - 1, CONV_DIM), F32)
    zero_gla = jnp.zeros((bp, GLA_HEADS, GLA_DK, GLA_DK), F32)
    outs = [[] for _ in range(8)]
    for l in range(depth):
        q, k, v, rest, la = _in_proj(x, w_in[l], w_gate[l], b_gate[l])
        oa_p = _sb_prompt(q, k, v, bp, seq)
        oa_s = _sb_decode(q, k, v, ck[l], cv[l], n_p, bs, ts)
        obc_p, conv_p, gla_p = _conv_gla(rest, la, conv_w[l], gla_norm_g[l], zero_conv, zero_gla, 0, bp, seq)
        obc_s, conv_s, gla_s = _conv_gla(rest, la, conv_w[l], gla_norm_g[l], state_conv[l], state_gla[l],
                                         n_p, bs, ts)
        oa = jnp.concatenate([oa_p, oa_s], axis=0)
        obc = jnp.concatenate([obc_p, obc_s], axis=0)
        h, hb, idx, gates, rank, counts = _out_router(oa, obc, x, w_out[l], ln1_g[l], ln1_b[l],
                                                      w_router[l], b_router[l], alpha)
        x = _moe(h, hb, idx, gates, rank, counts, w_up[l], b_up[l], w_down[l], b_down[l],
                 ln2_g[l], ln2_b[l], alpha)
        outs[0].append(k[:n_p].reshape(bp, seq, SB_HEADS, HEAD_DIM))
        outs[1].append(v[:n_p].reshape(bp, seq, SB_HEADS, HEAD_DIM))
        outs[2].append(conv_p)
        outs[3].append(gla_p)
        outs[4].append(k[n_p:].reshape(bs, ts, SB_HEADS, HEAD_DIM))
        outs[5].append(v[n_p:].reshape(bs, ts, SB_HEADS, HEAD_DIM))
        outs[6].append(conv_s)
        outs[7].append(gla_s)
    return (x[:n_p].reshape(bp, seq, d), x[n_p:].reshape(bs, ts, d)) + tuple(jnp.stack(o) for o in outs)
```

```python
import functools

import jax
import jax.numpy as jnp
from jax import lax
from jax.experimental import pallas as pl
from jax.experimental.pallas import tpu as pltpu
from jax.experimental.pallas import tpu_sc as plsc

F32 = jnp.float32
BF16 = jnp.bfloat16
I32 = jnp.int32

HEAD_DIM = 64
SB_HEADS = 8
SB_DIM = SB_HEADS * HEAD_DIM
CONV_DIM = 256
CONV_W = 3
GLA_HEADS = 4
GLA_DK = 64
GLA_DIM = GLA_HEADS * GLA_DK
GLA_RANK = 16
GLA_TAU = 16.0
GLA_CHUNK = 64
N_EXPERTS = 32
TOP_K = 4
SWIGLU_LIMIT = 7.0
SWIGLU_ALPHA = 1.702
NORM_EPS = 1e-5

LANES = 128
SC_CORES = 2
SC_SUBCORES = 16
SC_CHUNK = 64
SC_ROW_ALIGN = SC_CORES * SC_SUBCORES * SC_CHUNK
TOKEN_TILE = 256
EXPERT_TILE = 256
SB_TQ = 256
SB_BK = 256
DEC_TK = 1024
GLA_TC = 256
VMEM_LIMIT = 48 * 1024 * 1024
SB_DEAD = -100.0


def _dot(a, b):
    return jnp.dot(a, b, preferred_element_type=F32)


def _dot_nt(a, b):
    return lax.dot_general(a, b, (((1,), (1,)), ((), ())), preferred_element_type=F32)


def _split_bf16(x):
    hi = x.astype(BF16)
    lo = (x - hi.astype(F32)).astype(BF16)
    return hi, lo


def _dot_exact_rhs(x, m):
    hi, lo = _split_bf16(x)
    return _dot(hi, m) + _dot(lo, m)


def _dot_exact_lhs(m, x):
    hi, lo = _split_bf16(x)
    return _dot(m, hi) + _dot(m, lo)


def _softplus(z):
    return jnp.maximum(z, 0.0) + jnp.log(1.0 + jnp.exp(-jnp.abs(z)))


def _layer_norm(y, g, b):
    mu = jnp.mean(y, axis=-1, keepdims=True)
    yc = y - mu
    var = jnp.mean(yc * yc, axis=-1, keepdims=True)
    return yc * lax.rsqrt(var + NORM_EPS) * g + b


def _strict_upper(n):
    r = lax.broadcasted_iota(I32, (n, n), 0)
    c = lax.broadcasted_iota(I32, (n, n), 1)
    return jnp.where(r > c, 1.0, 0.0).astype(BF16)


def _in_proj_kernel(*refs, n_prompt_tiles):
    (_, _, x_ref, wq_ref, wk_ref, wv_ref, wkt_ref, wvt_ref, wr_ref, wal_ref, wg_ref, bg_ref,
     q_ref, kt_ref, vt_ref, ks_ref, vs_ref, r_ref, la_ref) = refs
    i = pl.program_id(0)
    xb = x_ref[...].astype(BF16)
    for c in range(0, SB_DIM, 256):
        q_ref[:, c:c + 256] = (_dot(xb, wq_ref[:, c:c + 256]) * (HEAD_DIM ** -0.5)).astype(BF16)
    for c in range(0, r_ref.shape[1], 256):
        r_ref[:, c:c + 256] = _dot(xb, wr_ref[:, c:c + 256])
    al = _dot(xb, wal_ref[...])
    g = _dot(al.astype(BF16), wg_ref[...]) + bg_ref[...]
    la_ref[...] = -_softplus(-g) * (1.0 / GLA_TAU)

    @pl.when(i < n_prompt_tiles)
    def _():
        for c in range(0, SB_DIM, 256):
            kt_ref[c:c + 256, :] = _dot_nt(wkt_ref[c:c + 256, :], xb)
            vt_ref[c:c + 256, :] = _dot_nt(wvt_ref[c:c + 256, :], xb)

    @pl.when(i >= n_prompt_tiles)
    def _():
        for c in range(0, SB_DIM, 256):
            ks_ref[:, c:c + 256] = _dot(xb, wk_ref[:, c:c + 256])
            vs_ref[:, c:c + 256] = _dot(xb, wv_ref[:, c:c + 256])


def _in_proj(x, wparts, layer, depth, batch, seq, kt_prev, vt_prev):
    n, d = x.shape
    tm = TOKEN_TILE
    n_p = batch * seq
    n_pt = n_p // tm
    tps = seq // tm
    wq, wk, wv, wkt, wvt, wr, wal, wg, bg = wparts
    n_r = wr.shape[1]
    full = lambda a: pl.BlockSpec(a.shape, lambda i: (0,) * a.ndim)
    row = lambda w: pl.BlockSpec((tm, w), lambda i: (i, 0))

    def kt_map(i):
        ip = jnp.minimum(i, n_pt - 1)
        return (layer, ip // tps, 0, ip % tps)

    kt_spec = pl.BlockSpec((None, None, SB_DIM, tm), kt_map)
    s_spec = pl.BlockSpec((tm, SB_DIM), lambda i: (jnp.maximum(i - n_pt, 0), 0))
    any_spec = pl.BlockSpec(memory_space=pl.ANY)
    kt_shape = jax.ShapeDtypeStruct((depth, batch, SB_DIM, seq), F32)
    weights = (wq, wk, wv, wkt, wvt, wr, wal, wg, bg)
    return pl.pallas_call(
        functools.partial(_in_proj_kernel, n_prompt_tiles=n_pt),
        grid=(n // tm,),
        in_specs=[any_spec, any_spec, row(d)] + [full(w) for w in weights],
        out_specs=[row(SB_DIM), kt_spec, kt_spec, s_spec, s_spec, row(n_r), row(GLA_DIM)],
        out_shape=[jax.ShapeDtypeStruct((n, SB_DIM), BF16), kt_shape, kt_shape,
                   jax.ShapeDtypeStruct((n - n_p, SB_DIM), F32),
                   jax.ShapeDtypeStruct((n - n_p, SB_DIM), F32),
                   jax.ShapeDtypeStruct((n, n_r), F32),
                   jax.ShapeDtypeStruct((n, GLA_DIM), F32)],
        input_output_aliases={0: 1, 1: 2},
        compiler_params=pltpu.CompilerParams(dimension_semantics=("arbitrary",),
                                             vmem_limit_bytes=VMEM_LIMIT),
        name="in_proj",
    )(kt_prev, vt_prev, x, *weights)


def _sb_weights(z, tri, run, mask):
    sp = _softplus(z)
    l1m = -sp
    lsig = z - sp
    if mask is not None:
        l1m = jnp.where(mask, l1m, 0.0)
    rest = _dot_exact_rhs(l1m, tri) + run
    a = jnp.exp(lsig + rest)
    if mask is not None:
        a = jnp.where(mask, a, 0.0)
    return a.astype(BF16), run + jnp.sum(l1m, axis=1, keepdims=True)


def _sb_prompt_kernel(oa_ref, q_ref, kt_ref, vt_ref, o_ref, *, tq, bk):
    del oa_ref
    qi = pl.program_id(2)
    q = q_ref[...]
    n_hh = LANES // HEAD_DIM
    lane = lax.broadcasted_iota(I32, (1, LANES), 1)
    in_head = [(lane // HEAD_DIM) == h for h in range(n_hh)]
    qh = [jnp.where(m, q, jnp.zeros_like(q)) for m in in_head]
    tri = _strict_upper(bk)
    n_full = (qi * tq) // bk
    qpos = qi * tq + lax.broadcasted_iota(I32, (tq, bk), 0)
    kpos = n_full * bk + lax.broadcasted_iota(I32, (tq, bk), 1)
    diag_mask = kpos < qpos

    def tile(jb, runs, mask):
        ks = pl.multiple_of(jb * bk, bk)
        kt = kt_ref[:, pl.ds(ks, bk)].astype(BF16)
        vt = vt_ref[:, pl.ds(ks, bk)].astype(BF16)
        out = jnp.zeros((tq, LANES), F32)
        new_runs = []
        for h in range(n_hh):
            a, run = _sb_weights(_dot(qh[h], kt), tri, runs[h], mask)
            out = jnp.where(in_head[h], _dot_nt(a, vt), out)
            new_runs.append(run)
        return out, tuple(new_runs)

    def alive_of(runs):
        m = jnp.max(runs[0])
        for r in runs[1:]:
            m = jnp.maximum(m, jnp.max(r))
        return m > SB_DEAD

    acc, runs = tile(n_full, tuple(jnp.zeros((tq, 1), F32) for _ in range(n_hh)), diag_mask)

    def cond(carry):
        j, alive, _, _ = carry
        return (j >= 0) & alive

    def body(carry):
        j, _, acc, runs = carry
        pv, runs = tile(j, runs, None)
        return j - 1, alive_of(runs), acc + pv, runs

    _, _, acc, _ = lax.while_loop(cond, body, (n_full - 1, alive_of(runs), acc, runs))
    o_ref[...] = acc.astype(o_ref.dtype)


def _sb_prompt(oa, q, kt, vt, layer, batch, seq):
    tq, bk = min(SB_TQ, seq), min(SB_BK, seq)
    nq = seq // tq
    hp = SB_DIM // LANES
    kv_spec = pl.BlockSpec((None, None, LANES, seq), lambda b, p, i: (layer, b, p, 0))
    return pl.pallas_call(
        functools.partial(_sb_prompt_kernel, tq=tq, bk=bk),
        grid=(batch, hp, nq),
        in_specs=[pl.BlockSpec(memory_space=pl.ANY),
                  pl.BlockSpec((tq, LANES), lambda b, p, i: (b * nq + i, p)), kv_spec, kv_spec],
        out_specs=pl.BlockSpec((tq, LANES), lambda b, p, i: (b * nq + i, p)),
        out_shape=jax.ShapeDtypeStruct(oa.shape, oa.dtype),
        input_output_aliases={0: 0},
        compiler_params=pltpu.CompilerParams(
            dimension_semantics=("arbitrary", "arbitrary", "arbitrary"),
            vmem_limit_bytes=VMEM_LIMIT),
        name="sb_prompt",
    )(oa, q, kt, vt)


def _sb_decode_kernel(oa_ref, q_ref, kn_ref, vn_ref, kc_ref, vc_ref, o_ref, acc_ref, run_ref, alive_ref,
                      *, t, tk, bk):
    del oa_ref
    j = pl.program_id(1)
    nj = pl.num_programs(1)
    q = q_ref[...]

    @pl.when(j == 0)
    def _():
        kn = kn_ref[...].astype(BF16)
        vn = vn_ref[...].astype(BF16)
        lane = lax.broadcasted_iota(I32, (1, SB_DIM), 1)
        r = lax.broadcasted_iota(I32, (t, t), 0)
        c = lax.broadcasted_iota(I32, (t, t), 1)
        mask = c < r
        tri_new = _strict_upper(t)
        top = jnp.full((1, 1), -jnp.inf, F32)
        for h in range(SB_HEADS):
            qh = jnp.where((lane // HEAD_DIM) == h, q, jnp.zeros_like(q))
            a, run = _sb_weights(_dot_nt(qh, kn), tri_new, jnp.zeros((t, 1), F32), mask)
            acc_ref[h] = _dot(a, vn)[:, h * HEAD_DIM:(h + 1) * HEAD_DIM]
            run_ref[h * t:(h + 1) * t, :] = run
            top = jnp.maximum(top, jnp.max(run, axis=0, keepdims=True))
        alive_ref[0] = (jnp.max(top) > SB_DEAD).astype(I32)

    tri = _strict_upper(bk)
    for c in range(tk // bk - 1, -1, -1):
        cols = slice(c * bk, (c + 1) * bk)

        @pl.when(alive_ref[0] > 0)
        def _():
            z = jnp.concatenate(
                [_dot(q[:, h * HEAD_DIM:(h + 1) * HEAD_DIM],
                      kc_ref[h * HEAD_DIM:(h + 1) * HEAD_DIM, cols].astype(BF16))
                 for h in range(SB_HEADS)], axis=0)
            a, run = _sb_weights(z, tri, run_ref[...], None)
            for h in range(SB_HEADS):
                vt = vc_ref[h * HEAD_DIM:(h + 1) * HEAD_DIM, cols].astype(BF16)
                acc_ref[h] = acc_ref[h] + _dot_nt(a[h * t:(h + 1) * t, :], vt)
            run_ref[...] = run
            alive_ref[0] = (jnp.max(run) > SB_DEAD).astype(I32)

    @pl.when(j == nj - 1)
    def _():
        for h in range(SB_HEADS):
            o_ref[:, h * HEAD_DIM:(h + 1) * HEAD_DIM] = acc_ref[h].astype(o_ref.dtype)


def _sb_decode(oa, q, ks, vs, cache_kt, cache_vt, layer, row0, batch, t):
    past = cache_kt.shape[3]
    tk = min(DEC_TK, past)
    bk = min(SB_BK, tk)
    nkb = past // tk
    rb = row0 // t
    new = pl.BlockSpec((t, SB_DIM), lambda b, j: (b, 0))
    cache = pl.BlockSpec((None, None, SB_DIM, tk), lambda b, j: (layer, b, 0, nkb - 1 - j))
    return pl.pallas_call(
        functools.partial(_sb_decode_kernel, t=t, tk=tk, bk=bk),
        grid=(batch, nkb),
        in_specs=[pl.BlockSpec(memory_space=pl.ANY),
                  pl.BlockSpec((t, SB_DIM), lambda b, j: (rb + b, 0)), new, new, cache, cache],
        out_specs=pl.BlockSpec((t, SB_DIM), lambda b, j: (rb + b, 0)),
        out_shape=jax.ShapeDtypeStruct(oa.shape, oa.dtype),
        input_output_aliases={0: 0},
        scratch_shapes=[pltpu.VMEM((SB_HEADS, t, HEAD_DIM), F32),
                        pltpu.VMEM((SB_HEADS * t, 1), F32),
                        pltpu.SMEM((1,), I32)],
        compiler_params=pltpu.CompilerParams(dimension_semantics=("arbitrary", "arbitrary"),
                                             vmem_limit_bytes=VMEM_LIMIT),
        name="sb_decode",
    )(oa, q, ks, vs, cache_kt, cache_vt)


def _conv_gla_kernel(*refs, tc, chunk):
    (_, bg_ref, cg_ref, u_ref, qc_ref, kc_ref, vc_ref, gc_ref, la_ref, cw_ref, ng_ref, cprev_ref, sprev_ref,
     o_ref, cnew_ref, snew_ref, ctail_ref, st_ref) = refs
    ti = pl.program_id(1)
    nt = pl.num_programs(1)
    n_pair = GLA_DIM // LANES
    n_hh = LANES // GLA_DK

    @pl.when(ti == 0)
    def _():
        ctail_ref[...] = cprev_ref[...]
        for p in range(n_pair):
            st_ref[p] = jnp.zeros((LANES, LANES), F32)
            for hh in range(n_hh):
                st_ref[p, hh * GLA_DK:(hh + 1) * GLA_DK, hh * GLA_DK:(hh + 1) * GLA_DK] = sprev_ref[p * n_hh + hh].T

    z = cg_ref[...] * u_ref[...]
    tail = ctail_ref[...]
    row = lax.broadcasted_iota(I32, z.shape, 0)
    z1 = jnp.where(row < 1, tail[1:2, :], pltpu.roll(z, 1, 0))
    z2 = jnp.where(row < 2, jnp.where(row < 1, tail[0:1, :], tail[1:2, :]), pltpu.roll(z, 2, 0))
    cw = cw_ref[...]
    y = z2 * cw[0:1, :] + z1 * cw[1:2, :] + z * cw[2:3, :]
    o_ref[:, 0:CONV_DIM] = (bg_ref[...] * y).astype(o_ref.dtype)
    ctail_ref[...] = z[tc - 2:tc, :]

    r = lax.broadcasted_iota(I32, (chunk, chunk), 0)
    c = lax.broadcasted_iota(I32, (chunk, chunk), 1)
    lower_incl = jnp.where(c <= r, 1.0, 0.0).astype(BF16)
    lane = lax.broadcasted_iota(I32, (1, LANES), 1)
    lr = lax.broadcasted_iota(I32, (LANES, LANES), 0) // GLA_DK
    lc = lax.broadcasted_iota(I32, (LANES, LANES), 1) // GLA_DK
    same_head = lr == lc
    head_mean = jnp.where(same_head, 1.0 / GLA_DK, 0.0).astype(BF16)
    r2 = lax.broadcasted_iota(I32, (chunk, n_hh * chunk), 0)
    c2 = lax.broadcasted_iota(I32, (chunk, n_hh * chunk), 1)
    causal2 = c2 - jnp.where(c2 >= chunk, chunk, 0) <= r2
    for p in range(n_pair):
        cols = slice(p * LANES, (p + 1) * LANES)
        st = st_ref[p]
        for ci in range(tc // chunk):
            rows = slice(ci * chunk, (ci + 1) * chunk)
            q = qc_ref[rows, cols] * (GLA_DK ** -0.5)
            k = kc_ref[rows, cols]
            v = vc_ref[rows, cols]
            b = _dot_exact_lhs(lower_incl, la_ref[rows, cols])
            b_last = b[chunk - 1:chunk, :]
            qe = (q * jnp.exp(b)).astype(BF16)
            ke = (k * jnp.exp(-b)).astype(BF16)
            kd = (k * jnp.exp(b_last - b)).astype(BF16)
            vb = v.astype(BF16)
            zeros = jnp.zeros_like(ke)
            ke_st = jnp.concatenate([jnp.where((lane // GLA_DK) == hh, ke, zeros)
                                     for hh in range(n_hh)], axis=0)
            v_st = jnp.concatenate([jnp.where((lane // GLA_DK) == hh, vb, zeros)
                                    for hh in range(n_hh)], axis=0)
            a = jnp.where(causal2, _dot_nt(qe, ke_st), 0.0)
            o = _dot_nt(qe, st.astype(BF16)) + _dot(a.astype(BF16), v_st)
            upd = _dot(vb.T, kd)
            st = jnp.where(same_head, st * jnp.exp(b_last) + upd, 0.0)
            ms = _dot_exact_rhs(o * o, head_mean)
            o = o * lax.rsqrt(ms + NORM_EPS) * ng_ref[:, cols]
            g = gc_ref[rows, cols]
            o = o * (g * (1.0 / (1.0 + jnp.exp(-g))))
            o_ref[rows, CONV_DIM + p * LANES:CONV_DIM + (p + 1) * LANES] = o.astype(o_ref.dtype)
        st_ref[p] = st

    @pl.when(ti == nt - 1)
    def _():
        cnew_ref[...] = ctail_ref[...]
        for p in range(n_pair):
            for hh in range(n_hh):
                blk = st_ref[p, hh * GLA_DK:(hh + 1) * GLA_DK, :]
                snew_ref[p * n_hh + hh] = blk.T[hh * GLA_DK:(hh + 1) * GLA_DK, :]


def _conv_gla(obc_prev, n, rest, la, conv_w, norm_g, conv_prev, gla_prev, row0, batch, t):
    tc = min(GLA_TC, t)
    chunk = min(GLA_CHUNK, t)
    nt = t // tc
    rb = row0 // tc
    col = lambda j: pl.BlockSpec((tc, CONV_DIM), lambda b, i: (rb + b * nt + i, j))
    const2 = lambda a: pl.BlockSpec(a.shape, lambda b, i: (0, 0))
    ng = norm_g.reshape(1, GLA_DIM)
    args = (rest,) * 7 + (la, conv_w, ng, conv_prev, gla_prev)
    return pl.pallas_call(
        functools.partial(_conv_gla_kernel, tc=tc, chunk=chunk),
        grid=(batch, nt),
        in_specs=[
            pl.BlockSpec(memory_space=pl.ANY), col(0), col(1), col(2), col(3), col(4), col(5), col(6),
            pl.BlockSpec((tc, GLA_DIM), lambda b, i: (rb + b * nt + i, 0)),
            const2(conv_w), const2(ng),
            pl.BlockSpec((None, CONV_W - 1, CONV_DIM), lambda b, i: (b, 0, 0)),
            pl.BlockSpec((None, GLA_HEADS, GLA_DK, GLA_DK), lambda b, i: (b, 0, 0, 0))],
        out_specs=[pl.BlockSpec((tc, CONV_DIM + GLA_DIM), lambda b, i: (rb + b * nt + i, 0)),
                   pl.BlockSpec((None, CONV_W - 1, CONV_DIM), lambda b, i: (b, 0, 0)),
                   pl.BlockSpec((None, GLA_HEADS, GLA_DK, GLA_DK), lambda b, i: (b, 0, 0, 0))],
        out_shape=[jax.ShapeDtypeStruct((n, CONV_DIM + GLA_DIM), BF16),
                   jax.ShapeDtypeStruct((batch, CONV_W - 1, CONV_DIM), F32),
                   jax.ShapeDtypeStruct((batch, GLA_HEADS, GLA_DK, GLA_DK), F32)],
        input_output_aliases={0: 0},
        scratch_shapes=[pltpu.VMEM((CONV_W - 1, CONV_DIM), F32),
                        pltpu.VMEM((GLA_DIM // LANES, LANES, LANES), F32)],
        compiler_params=pltpu.CompilerParams(dimension_semantics=("arbitrary", "arbitrary"),
                                             vmem_limit_bytes=VMEM_LIMIT),
        name="conv_gla",
    )(obc_prev, *args)


def _out_router_kernel(oa_ref, obc_ref, x_ref, wa_ref, wb_ref, g_ref, b_ref, wr_ref, br_ref,
                       h_ref, idx_ref, gate_ref, rank_ref, cnt_ref, carry_ref, *, alpha):
    i = pl.program_id(0)
    tm = x_ref.shape[0]

    @pl.when(i == 0)
    def _():
        carry_ref[...] = jnp.zeros_like(carry_ref)

    m = _dot(oa_ref[...], wa_ref[...]) + _dot(obc_ref[...], wb_ref[...])
    h = _layer_norm(alpha * x_ref[...] + m, g_ref[...], b_ref[...])
    h_ref[...] = h

    h_hi, h_lo = _split_bf16(h)
    w_hi, w_lo = _split_bf16(wr_ref[...])
    logit = _dot_nt(w_hi, h_hi) + _dot_nt(w_hi, h_lo) + _dot_nt(w_lo, h_hi) + br_ref[...]
    eid = lax.broadcasted_iota(I32, (N_EXPERTS, tm), 0)
    r = lax.broadcasted_iota(I32, (tm, tm), 0)
    c = lax.broadcasted_iota(I32, (tm, tm), 1)
    before = jnp.where(r < c, 1.0, 0.0).astype(BF16)
    base = carry_ref[...]
    vals, idxs, ranks = [], [], []
    for _ in range(TOP_K):
        mx = jnp.max(logit, axis=0, keepdims=True)
        sel = jnp.min(jnp.where(logit == mx, eid, N_EXPERTS), axis=0, keepdims=True)
        hit = eid == sel
        logit = jnp.where(hit, -jnp.inf, logit)
        onehot = jnp.where(hit, 1.0, 0.0)
        prior = _dot(onehot.astype(BF16), before) + base
        ranks.append(jnp.sum(onehot * prior, axis=0, keepdims=True))
        base = base + jnp.sum(onehot, axis=1, keepdims=True)
        vals.append(mx)
        idxs.append(sel)
    carry_ref[...] = base
    e = [jnp.exp(v - vals[0]) for v in vals]
    inv = 1.0 / (e[0] + e[1] + e[2] + e[3])
    idx_ref[...] = jnp.concatenate(idxs, axis=0)
    gate_ref[...] = jnp.concatenate([ek * inv for ek in e], axis=0)
    rank_ref[...] = jnp.concatenate(ranks, axis=0).astype(I32)
    cnt_ref[...] = jnp.broadcast_to(base, cnt_ref.shape).astype(I32)


def _out_router(oa, obc, x, wa, wb, ln_g, ln_b, w_router, b_router, alpha):
    n, d = x.shape
    tm = TOKEN_TILE
    wr = w_router.T
    br = b_router.reshape(N_EXPERTS, 1)
    g, b = ln_g.reshape(1, d), ln_b.reshape(1, d)
    full = lambda a: pl.BlockSpec(a.shape, lambda i: (0, 0))
    row = lambda w: pl.BlockSpec((tm, w), lambda i: (i, 0))
    colb = pl.BlockSpec((TOP_K, tm), lambda i: (0, i))
    return pl.pallas_call(
        functools.partial(_out_router_kernel, alpha=alpha),
        grid=(n // tm,),
        in_specs=[row(SB_DIM), row(CONV_DIM + GLA_DIM), row(d), full(wa), full(wb), full(g), full(b),
                  full(wr), full(br)],
        out_specs=[row(d), colb, colb, colb,
                   pl.BlockSpec((N_EXPERTS, LANES), lambda i: (0, 0))],
        out_shape=[jax.ShapeDtypeStruct((n, d), F32),
                   jax.ShapeDtypeStruct((TOP_K, n), I32),
                   jax.ShapeDtypeStruct((TOP_K, n), F32),
                   jax.ShapeDtypeStruct((TOP_K, n), I32),
                   jax.ShapeDtypeStruct((N_EXPERTS, LANES), I32)],
        scratch_shapes=[pltpu.VMEM((N_EXPERTS, 1), F32)],
        compiler_params=pltpu.CompilerParams(dimension_semantics=("arbitrary",),
                                             vmem_limit_bytes=VMEM_LIMIT),
        name="out_router",
    )(oa, obc, x, wa, wb, g, b, wr, br)


def _sc_gather(table, idx):
    m = idx.shape[0]
    d = table.shape[1]
    per_worker = m // (SC_CORES * SC_SUBCORES)
    n_chunks = per_worker // SC_CHUNK
    mesh = plsc.VectorSubcoreMesh(core_axis_name="c", subcore_axis_name="s")

    @functools.partial(
        pl.kernel, mesh=mesh,
        out_type=jax.ShapeDtypeStruct((m, d), table.dtype),
        scratch_types=[pltpu.VMEM((SC_CHUNK,), I32),
                       pltpu.VMEM((SC_CHUNK, d), table.dtype),
                       pltpu.SemaphoreType.DMA],
        name="sc_gather",
    )
    def gather(table_hbm, idx_hbm, out_hbm, idx_v, rows_v, sem):
        wid = lax.axis_index("s") * SC_CORES + lax.axis_index("c")
        base = wid * per_worker

        @pl.loop(0, n_chunks)
        def _(c):
            off = pl.multiple_of(base + c * SC_CHUNK, SC_CHUNK)
            pltpu.sync_copy(idx_hbm.at[pl.ds(off, SC_CHUNK)], idx_v)
            pltpu.async_copy(table_hbm.at[idx_v], rows_v, sem).wait()
            pltpu.sync_copy(rows_v, out_hbm.at[pl.ds(off, SC_CHUNK)])

    return gather(table, idx)


def _expert_kernel(te_ref, nu_ref, x_ref, wu_ref, bu_ref, wd_ref, bd_ref, y_ref, wu16_ref, wd16_ref):
    i = pl.program_id(0)
    e = te_ref[i]
    prev = te_ref[jnp.maximum(i - 1, 0)]
    dff = wd_ref.shape[0]

    @pl.when((i == 0) | (e != prev))
    def _():
        step = 128

        def cast(s, _):
            rows = pl.ds(pl.multiple_of(s * step, step), step)
            wu16_ref[rows, :] = wu_ref[rows, :].astype(BF16)
            return 0

        lax.fori_loop(0, wu_ref.shape[0] // step, cast, 0)

        def cast_d(s, _):
            rows = pl.ds(pl.multiple_of(s * step, step), step)
            wd16_ref[rows, :] = wd_ref[rows, :].astype(BF16)
            return 0

        lax.fori_loop(0, dff // step, cast_d, 0)

    @pl.when(i < nu_ref[0])
    def _():
        x = x_ref[...].astype(BF16)
        glu = jnp.minimum(_dot(x, wu16_ref[:, :dff]) + bu_ref[:, :dff], SWIGLU_LIMIT)
        lin = jnp.clip(_dot(x, wu16_ref[:, dff:]) + bu_ref[:, dff:], -SWIGLU_LIMIT, SWIGLU_LIMIT)
        act = glu * (1.0 / (1.0 + jnp.exp(-SWIGLU_ALPHA * glu))) * (lin + 1.0)
        y_ref[...] = _dot(act.astype(BF16), wd16_ref[...]) + bd_ref[...]

    @pl.when(i >= nu_ref[0])
    def _():
        y_ref[...] = jnp.zeros_like(y_ref)


def _experts(x_sorted, tile_expert, n_used, w_up, b_up, w_down, b_down, layer):
    ns, d = x_sorted.shape
    tm = EXPERT_TILE
    dff = w_down.shape[2]
    bu = b_up.reshape(b_up.shape[0], N_EXPERTS, 1, 2 * dff)
    bd = b_down.reshape(b_down.shape[0], N_EXPERTS, 1, d)
    wmap = lambda i, te, nu: (layer, te[i], 0, 0)
    grid_spec = pltpu.PrefetchScalarGridSpec(
        num_scalar_prefetch=2,
        grid=(ns // tm,),
        in_specs=[pl.BlockSpec((tm, d), lambda i, te, nu: (i, 0)),
                  pl.BlockSpec((None, None, d, 2 * dff), wmap),
                  pl.BlockSpec((None, None, 1, 2 * dff), wmap),
                  pl.BlockSpec((None, None, dff, d), wmap),
                  pl.BlockSpec((None, None, 1, d), wmap)],
        out_specs=pl.BlockSpec((tm, d), lambda i, te, nu: (i, 0)),
        scratch_shapes=[pltpu.VMEM((d, 2 * dff), BF16), pltpu.VMEM((dff, d), BF16)],
    )
    return pl.pallas_call(
        _expert_kernel,
        grid_spec=grid_spec,
        out_shape=jax.ShapeDtypeStruct((ns, d), F32),
        compiler_params=pltpu.CompilerParams(dimension_semantics=("arbitrary",),
                                             vmem_limit_bytes=VMEM_LIMIT),
        name="experts",
    )(tile_expert, n_used, x_sorted, w_up, bu, w_down, bd)


def _combine_kernel(y0_ref, y1_ref, y2_ref, y3_ref, gate_ref, h_ref, g_ref, b_ref, o_ref, *, alpha):
    gate = gate_ref[...]
    acc = jnp.zeros(h_ref.shape, F32)
    for k, y_ref in enumerate((y0_ref, y1_ref, y2_ref, y3_ref)):
        acc = acc + gate[:, k:k + 1] * y_ref[...]
    o_ref[...] = _layer_norm(alpha * h_ref[...] + acc, g_ref[...], b_ref[...])


def _combine(y_tok, gates, h, ln_g, ln_b, alpha):
    n, d = h.shape
    tm = TOKEN_TILE
    nt = n // tm
    g, b = ln_g.reshape(1, d), ln_b.reshape(1, d)
    full = lambda a: pl.BlockSpec(a.shape, lambda i: (0, 0))
    ysp = lambda k: pl.BlockSpec((tm, d), lambda i: (k * nt + i, 0))
    return pl.pallas_call(
        functools.partial(_combine_kernel, alpha=alpha),
        grid=(nt,),
        in_specs=[ysp(0), ysp(1), ysp(2), ysp(3),
                  pl.BlockSpec((tm, TOP_K), lambda i: (i, 0)),
                  pl.BlockSpec((tm, d), lambda i: (i, 0)), full(g), full(b)],
        out_specs=pl.BlockSpec((tm, d), lambda i: (i, 0)),
        out_shape=jax.ShapeDtypeStruct((n, d), F32),
        compiler_params=pltpu.CompilerParams(dimension_semantics=("arbitrary",),
                                             vmem_limit_bytes=VMEM_LIMIT),
        name="combine",
    )(y_tok, y_tok, y_tok, y_tok, gates, h, g, b)


def _round_up(a, m):
    return (a + m - 1) // m * m


def _moe(h, idx, gates, rank, counts, w_up, b_up, w_down, b_down, ln_g, ln_b, alpha, layer):
    n, d = h.shape
    tm = EXPERT_TILE
    ns = _round_up(TOP_K * n + N_EXPERTS * tm, SC_ROW_ALIGN)
    cnt = counts[:, 0]
    padded = ((cnt + tm - 1) // tm) * tm
    ends = jnp.cumsum(padded)
    offs = ends - padded
    pos = (jnp.take(offs, idx) + rank).reshape(-1)
    n_used = (ends[-1] // tm).astype(I32).reshape(1)
    tile_start = jnp.arange(ns // tm, dtype=I32) * tm
    tile_expert = jnp.sum((ends[None, :] <= tile_start[:, None]).astype(I32), axis=1)
    last_used = jnp.sum((ends <= ends[-1] - tm).astype(I32))
    tile_expert = jnp.minimum(tile_expert, last_used).astype(I32)
    token = jnp.broadcast_to(jnp.arange(n, dtype=I32)[None, :], (TOP_K, n)).reshape(-1)
    src = jnp.zeros((ns,), I32).at[pos].set(token)
    x_sorted = _sc_gather(h, src)
    y_sorted = _experts(x_sorted, tile_expert, n_used, w_up, b_up, w_down, b_down, layer)
    m2 = _round_up(TOP_K * n, SC_ROW_ALIGN)
    y_tok = _sc_gather(y_sorted, jnp.pad(pos, (0, m2 - TOP_K * n)))
    return _combine(y_tok, gates.T, h, ln_g, ln_b, alpha)


def kernel(x_prompt, x_sample, cache_k, cache_v, state_conv, state_gla, w_in, conv_w, w_gate, b_gate,
           gla_norm_g, w_out, ln1_g, ln1_b, w_router, b_router, w_up, b_up, w_down, b_down, ln2_g, ln2_b):
    depth = w_in.shape[0]
    bp, seq, d = x_prompt.shape
    bs, ts, _ = x_sample.shape
    past = cache_k.shape[2]
    n_p = bp * seq
    n = n_p + bs * ts
    alpha = float((2 * depth) ** 0.25)
    x = jnp.concatenate([x_prompt.reshape(n_p, d), x_sample.reshape(bs * ts, d)], axis=0)
    ckt = cache_k.transpose(0, 1, 3, 4, 2).reshape(depth, bs, SB_DIM, past)
    cvt = cache_v.transpose(0, 1, 3, 4, 2).reshape(depth, bs, SB_DIM, past)
    zero_conv = jnp.zeros((bp, CONV_W - 1, CONV_DIM), F32)
    zero_gla = jnp.zeros((bp, GLA_HEADS, GLA_DK, GLA_DK), F32)
    wb = w_in.astype(BF16)
    o_r = 3 * SB_DIM
    n_r = 3 * CONV_DIM + 4 * GLA_DIM
    wq, wk, wv = wb[:, :, :SB_DIM], wb[:, :, SB_DIM:2 * SB_DIM], wb[:, :, 2 * SB_DIM:o_r]
    wkt, wvt = wk.transpose(0, 2, 1), wv.transpose(0, 2, 1)
    wr, wal = wb[:, :, o_r:o_r + n_r], wb[:, :, o_r + n_r:]
    wg = w_gate.astype(BF16)
    wo = w_out.astype(BF16)
    kt = jnp.zeros((depth, bp, SB_DIM, seq), F32)
    vt = jnp.zeros((depth, bp, SB_DIM, seq), F32)
    outs = [[] for _ in range(6)]
    for l in range(depth):
        wparts = (wq[l], wk[l], wv[l], wkt[l], wvt[l], wr[l], wal[l], wg[l], b_gate[l].reshape(1, -1))
        q, kt, vt, ks, vs, rest, la = _in_proj(x, wparts, l, depth, bp, seq, kt, vt)
        oa = _sb_prompt(jnp.zeros((n, SB_DIM), BF16), q, kt, vt, l, bp, seq)
        oa = _sb_decode(oa, q, ks, vs, ckt, cvt, l, n_p, bs, ts)
        obc, conv_p, gla_p = _conv_gla(jnp.zeros((n, CONV_DIM + GLA_DIM), BF16), n, rest, la, conv_w[l],
                                       gla_norm_g[l], zero_conv, zero_gla, 0, bp, seq)
        obc, conv_s, gla_s = _conv_gla(obc, n, rest, la, conv_w[l], gla_norm_g[l], state_conv[l],
                                       state_gla[l], n_p, bs, ts)
        h, idx, gates, rank, counts = _out_router(oa, obc, x, wo[l, :SB_DIM], wo[l, SB_DIM:], ln1_g[l],
                                                  ln1_b[l], w_router[l], b_router[l], alpha)
        x = _moe(h, idx, gates, rank, counts, w_up, b_up, w_down, b_down, ln2_g[l], ln2_b[l], alpha, l)
        outs[0].append(conv_p)
        outs[1].append(gla_p)
        outs[2].append(ks.reshape(bs, ts, SB_HEADS, HEAD_DIM))
        outs[3].append(vs.reshape(bs, ts, SB_HEADS, HEAD_DIM))
        outs[4].append(conv_s)
        outs[5].append(gla_s)
    k_prompt = kt.reshape(depth, bp, SB_HEADS, HEAD_DIM, seq).transpose(0, 1, 4, 2, 3)
    v_prompt = vt.reshape(depth, bp, SB_HEADS, HEAD_DIM, seq).transpose(0, 1, 4, 2, 3)
    st = [jnp.stack(o) for o in outs]
    return (x[:n_p].reshape(bp, seq, d), x[n_p:].reshape(bs, ts, d), k_prompt, v_prompt,
            st[0], st[1], st[2], st[3], st[4], st[5])
```

```python
import functools

import jax
import jax.numpy as jnp
from jax import lax
from jax.experimental import pallas as pl
from jax.experimental.pallas import tpu as pltpu
from jax.experimental.pallas import tpu_sc as plsc

F32 = jnp.float32
BF16 = jnp.bfloat16
I32 = jnp.int32
U32 = jnp.uint32

HEAD_DIM = 64
SB_HEADS = 8
SB_DIM = SB_HEADS * HEAD_DIM
CONV_DIM = 256
CONV_W = 3
GLA_HEADS = 4
GLA_DK = 64
GLA_DIM = GLA_HEADS * GLA_DK
GLA_RANK = 16
GLA_TAU = 16.0
GLA_CHUNK = 64
N_EXPERTS = 32
TOP_K = 4
SWIGLU_LIMIT = 7.0
SWIGLU_ALPHA = 1.702
NORM_EPS = 1e-5

LANES = 128
SC_CORES = 2
SC_SUBCORES = 16
SC_CHUNK = 64
SC_ROW_ALIGN = SC_CORES * SC_SUBCORES * SC_CHUNK
SC_SCATTER_MAX = 104
TOKEN_TILE = 256
EXPERT_TILE = 256
SB_TQ = 256
SB_BK = 256
SB_GROUP = 256
DEC_TK = 1024
GLA_TC = 256
VMEM_LIMIT = 48 * 1024 * 1024
SB_DEAD = -100.0


def _dot(a, b):
    return jnp.dot(a, b, preferred_element_type=F32)


def _dot_nt(a, b):
    return lax.dot_general(a, b, (((1,), (1,)), ((), ())), preferred_element_type=F32)


def _split_bf16(x):
    hi = x.astype(BF16)
    lo = (x - hi.astype(F32)).astype(BF16)
    return hi, lo


def _dot_exact_rhs(x, m):
    hi, lo = _split_bf16(x)
    return _dot(hi, m) + _dot(lo, m)


def _dot_exact_lhs(m, x):
    hi, lo = _split_bf16(x)
    return _dot(m, hi) + _dot(m, lo)


def _pack_bf16_pairs(x):
    c = x.shape[1] // 2
    bits = lax.bitcast_convert_type(x.astype(BF16).astype(F32), U32)
    return (bits[:, :c] >> 16) | (bits[:, c:] & jnp.uint32(0xFFFF0000))


def _unpack_bf16_pairs(w):
    lo = lax.bitcast_convert_type(w << 16, F32)
    hi = lax.bitcast_convert_type(w & jnp.uint32(0xFFFF0000), F32)
    return jnp.concatenate([lo, hi], axis=1).astype(BF16)


def _softplus(z):
    return jnp.maximum(z, 0.0) + jnp.log(1.0 + jnp.exp(-jnp.abs(z)))


def _layer_norm(y, g, b):
    mu = jnp.mean(y, axis=-1, keepdims=True)
    yc = y - mu
    var = jnp.mean(yc * yc, axis=-1, keepdims=True)
    return yc * lax.rsqrt(var + NORM_EPS) * g + b


def _strict_upper(n):
    r = lax.broadcasted_iota(I32, (n, n), 0)
    c = lax.broadcasted_iota(I32, (n, n), 1)
    return jnp.where(r > c, 1.0, 0.0).astype(BF16)


def _in_proj_kernel(*refs, n_prompt_tiles):
    (_, _, x_ref, wq_ref, wk_ref, wv_ref, wkt_ref, wvt_ref, wr_ref, wal_ref, wg_ref, bg_ref,
     q_ref, kt_ref, vt_ref, ks_ref, vs_ref, r_ref, la_ref) = refs
    i = pl.program_id(0)
    xb = x_ref[...].astype(BF16)
    for c in range(0, SB_DIM, 256):
        q_ref[:, c:c + 256] = (_dot(xb, wq_ref[:, c:c + 256]) * (HEAD_DIM ** -0.5)).astype(BF16)
    for c in range(0, r_ref.shape[1], 256):
        r_ref[:, c:c + 256] = _dot(xb, wr_ref[:, c:c + 256])
    al = _dot(xb, wal_ref[...])
    g = _dot(al.astype(BF16), wg_ref[...]) + bg_ref[...]
    la_ref[...] = -_softplus(-g) * (1.0 / GLA_TAU)

    @pl.when(i < n_prompt_tiles)
    def _():
        for c in range(0, SB_DIM, 256):
            kt_ref[c:c + 256, :] = _dot_nt(wkt_ref[c:c + 256, :], xb)
            vt_ref[c:c + 256, :] = _dot_nt(wvt_ref[c:c + 256, :], xb)

    @pl.when(i >= n_prompt_tiles)
    def _():
        for c in range(0, SB_DIM, 256):
            ks_ref[:, c:c + 256] = _dot(xb, wk_ref[:, c:c + 256])
            vs_ref[:, c:c + 256] = _dot(xb, wv_ref[:, c:c + 256])


def _in_proj(x, wparts, layer, depth, batch, seq, kt_prev, vt_prev):
    n, d = x.shape
    tm = TOKEN_TILE
    n_p = batch * seq
    n_pt = n_p // tm
    tps = seq // tm
    wq, wk, wv, wkt, wvt, wr, wal, wg, bg = wparts
    n_r = wr.shape[1]
    full = lambda a: pl.BlockSpec(a.shape, lambda i: (0,) * a.ndim)
    row = lambda w: pl.BlockSpec((tm, w), lambda i: (i, 0))

    def kt_map(i):
        ip = jnp.minimum(i, n_pt - 1)
        return (layer, ip // tps, 0, ip % tps)

    kt_spec = pl.BlockSpec((None, None, SB_DIM, tm), kt_map)
    s_spec = pl.BlockSpec((tm, SB_DIM), lambda i: (jnp.maximum(i - n_pt, 0), 0))
    any_spec = pl.BlockSpec(memory_space=pl.ANY)
    kt_shape = jax.ShapeDtypeStruct((depth, batch, SB_DIM, seq), F32)
    weights = (wq, wk, wv, wkt, wvt, wr, wal, wg, bg)
    return pl.pallas_call(
        functools.partial(_in_proj_kernel, n_prompt_tiles=n_pt),
        grid=(n // tm,),
        in_specs=[any_spec, any_spec, row(d)] + [full(w) for w in weights],
        out_specs=[row(SB_DIM), kt_spec, kt_spec, s_spec, s_spec, row(n_r), row(GLA_DIM)],
        out_shape=[jax.ShapeDtypeStruct((n, SB_DIM), BF16), kt_shape, kt_shape,
                   jax.ShapeDtypeStruct((n - n_p, SB_DIM), F32),
                   jax.ShapeDtypeStruct((n - n_p, SB_DIM), F32),
                   jax.ShapeDtypeStruct((n, n_r), F32),
                   jax.ShapeDtypeStruct((n, GLA_DIM), F32)],
        input_output_aliases={0: 1, 1: 2},
        compiler_params=pltpu.CompilerParams(dimension_semantics=("arbitrary",),
                                             vmem_limit_bytes=VMEM_LIMIT),
        name="in_proj",
    )(kt_prev, vt_prev, x, *weights)


def _sb_weights(z, tri, run, mask):
    sp = _softplus(z)
    l1m = -sp
    lsig = z - sp
    if mask is not None:
        l1m = jnp.where(mask, l1m, 0.0)
    rest = _dot_exact_rhs(l1m, tri) + run
    a = jnp.exp(lsig + rest)
    if mask is not None:
        a = jnp.where(mask, a, 0.0)
    return a.astype(BF16), run + jnp.sum(l1m, axis=1, keepdims=True)


def _sb_prompt_kernel(oa_ref, q_ref, kt_ref, vt_ref, o_ref, *, tq, bk):
    del oa_ref
    qi = pl.program_id(2)
    q = q_ref[...]
    n_hh = SB_GROUP // HEAD_DIM
    lane = lax.broadcasted_iota(I32, (1, SB_GROUP), 1)
    in_head = [(lane // HEAD_DIM) == h for h in range(n_hh)]
    qh = [jnp.where(m, q, jnp.zeros_like(q)) for m in in_head]
    tri = _strict_upper(bk)
    n_full = (qi * tq) // bk
    qpos = qi * tq + lax.broadcasted_iota(I32, (tq, bk), 0)
    kpos = n_full * bk + lax.broadcasted_iota(I32, (tq, bk), 1)
    diag_mask = kpos < qpos

    def tile(jb, runs, mask):
        ks = pl.multiple_of(jb * bk, bk)
        kt = kt_ref[:, pl.ds(ks, bk)].astype(BF16)
        vt = vt_ref[:, pl.ds(ks, bk)].astype(BF16)
        out = jnp.zeros((tq, SB_GROUP), F32)
        new_runs = []
        for h in range(n_hh):
            a, run = _sb_weights(_dot(qh[h], kt), tri, runs[h], mask)
            out = jnp.where(in_head[h], _dot_nt(a, vt), out)
            new_runs.append(run)
        return out, tuple(new_runs)

    def alive_of(runs):
        m = jnp.max(runs[0])
        for r in runs[1:]:
            m = jnp.maximum(m, jnp.max(r))
        return m > SB_DEAD

    acc, runs = tile(n_full, tuple(jnp.zeros((tq, 1), F32) for _ in range(n_hh)), diag_mask)

    def cond(carry):
        j, alive, _, _ = carry
        return (j >= 0) & alive

    def body(carry):
        j, _, acc, runs = carry
        pv, runs = tile(j, runs, None)
        return j - 1, alive_of(runs), acc + pv, runs

    _, _, acc, _ = lax.while_loop(cond, body, (n_full - 1, alive_of(runs), acc, runs))
    o_ref[...] = acc.astype(o_ref.dtype)


def _sb_prompt(oa, q, kt, vt, layer, batch, seq):
    tq, bk = min(SB_TQ, seq), min(SB_BK, seq)
    nq = seq // tq
    hp = SB_DIM // SB_GROUP
    kv_spec = pl.BlockSpec((None, None, SB_GROUP, seq), lambda b, p, i: (layer, b, p, 0))
    return pl.pallas_call(
        functools.partial(_sb_prompt_kernel, tq=tq, bk=bk),
        grid=(batch, hp, nq),
        in_specs=[pl.BlockSpec(memory_space=pl.ANY),
                  pl.BlockSpec((tq, SB_GROUP), lambda b, p, i: (b * nq + i, p)), kv_spec, kv_spec],
        out_specs=pl.BlockSpec((tq, SB_GROUP), lambda b, p, i: (b * nq + i, p)),
        out_shape=jax.ShapeDtypeStruct(oa.shape, oa.dtype),
        input_output_aliases={0: 0},
        compiler_params=pltpu.CompilerParams(
            dimension_semantics=("arbitrary", "arbitrary", "arbitrary"),
            vmem_limit_bytes=VMEM_LIMIT),
        name="sb_prompt",
    )(oa, q, kt, vt)


def _sb_decode_kernel(oa_ref, q_ref, kn_ref, vn_ref, kc_ref, vc_ref, o_ref, acc_ref, run_ref, alive_ref,
                      *, t, tk, bk):
    del oa_ref
    j = pl.program_id(1)
    nj = pl.num_programs(1)
    q = q_ref[...]

    @pl.when(j == 0)
    def _():
        kn = kn_ref[...].astype(BF16)
        vn = vn_ref[...].astype(BF16)
        lane = lax.broadcasted_iota(I32, (1, SB_DIM), 1)
        r = lax.broadcasted_iota(I32, (t, t), 0)
        c = lax.broadcasted_iota(I32, (t, t), 1)
        mask = c < r
        tri_new = _strict_upper(t)
        top = jnp.full((1, 1), -jnp.inf, F32)
        for h in range(SB_HEADS):
            qh = jnp.where((lane // HEAD_DIM) == h, q, jnp.zeros_like(q))
            a, run = _sb_weights(_dot_nt(qh, kn), tri_new, jnp.zeros((t, 1), F32), mask)
            acc_ref[h] = _dot(a, vn)[:, h * HEAD_DIM:(h + 1) * HEAD_DIM]
            run_ref[h * t:(h + 1) * t, :] = run
            top = jnp.maximum(top, jnp.max(run, axis=0, keepdims=True))
        alive_ref[0] = (jnp.max(top) > SB_DEAD).astype(I32)

    tri = _strict_upper(bk)
    for c in range(tk // bk - 1, -1, -1):
        cols = slice(c * bk, (c + 1) * bk)

        @pl.when(alive_ref[0] > 0)
        def _():
            z = jnp.concatenate(
                [_dot(q[:, h * HEAD_DIM:(h + 1) * HEAD_DIM],
                      kc_ref[h * HEAD_DIM:(h + 1) * HEAD_DIM, cols].astype(BF16))
                 for h in range(SB_HEADS)], axis=0)
            a, run = _sb_weights(z, tri, run_ref[...], None)
            for h in range(SB_HEADS):
                vt = vc_ref[h * HEAD_DIM:(h + 1) * HEAD_DIM, cols].astype(BF16)
                acc_ref[h] = acc_ref[h] + _dot_nt(a[h * t:(h + 1) * t, :], vt)
            run_ref[...] = run
            alive_ref[0] = (jnp.max(run) > SB_DEAD).astype(I32)

    @pl.when(j == nj - 1)
    def _():
        for h in range(SB_HEADS):
            o_ref[:, h * HEAD_DIM:(h + 1) * HEAD_DIM] = acc_ref[h].astype(o_ref.dtype)


def _sb_decode(oa, q, ks, vs, cache_kt, cache_vt, layer, row0, batch, t):
    past = cache_kt.shape[3]
    tk = min(DEC_TK, past)
    bk = min(SB_BK, tk)
    nkb = past // tk
    rb = row0 // t
    new = pl.BlockSpec((t, SB_DIM), lambda b, j: (b, 0))
    cache = pl.BlockSpec((None, None, SB_DIM, tk), lambda b, j: (layer, b, 0, nkb - 1 - j))
    return pl.pallas_call(
        functools.partial(_sb_decode_kernel, t=t, tk=tk, bk=bk),
        grid=(batch, nkb),
        in_specs=[pl.BlockSpec(memory_space=pl.ANY),
                  pl.BlockSpec((t, SB_DIM), lambda b, j: (rb + b, 0)), new, new, cache, cache],
        out_specs=pl.BlockSpec((t, SB_DIM), lambda b, j: (rb + b, 0)),
        out_shape=jax.ShapeDtypeStruct(oa.shape, oa.dtype),
        input_output_aliases={0: 0},
        scratch_shapes=[pltpu.VMEM((SB_HEADS, t, HEAD_DIM), F32),
                        pltpu.VMEM((SB_HEADS * t, 1), F32),
                        pltpu.SMEM((1,), I32)],
        compiler_params=pltpu.CompilerParams(dimension_semantics=("arbitrary", "arbitrary"),
                                             vmem_limit_bytes=VMEM_LIMIT),
        name="sb_decode",
    )(oa, q, ks, vs, cache_kt, cache_vt)


def _conv_gla_kernel(*refs, tc, chunk):
    (_, bg_ref, cg_ref, u_ref, qc_ref, kc_ref, vc_ref, gc_ref, la_ref, cw_ref, ng_ref, cprev_ref, sprev_ref,
     o_ref, cnew_ref, snew_ref, ctail_ref, st_ref) = refs
    ti = pl.program_id(1)
    nt = pl.num_programs(1)
    n_pair = GLA_DIM // LANES
    n_hh = LANES // GLA_DK

    @pl.when(ti == 0)
    def _():
        ctail_ref[...] = cprev_ref[...]
        for p in range(n_pair):
            st_ref[p] = jnp.zeros((LANES, LANES), F32)
            for hh in range(n_hh):
                st_ref[p, hh * GLA_DK:(hh + 1) * GLA_DK, hh * GLA_DK:(hh + 1) * GLA_DK] = sprev_ref[p * n_hh + hh].T

    z = cg_ref[...] * u_ref[...]
    tail = ctail_ref[...]
    row = lax.broadcasted_iota(I32, z.shape, 0)
    z1 = jnp.where(row < 1, tail[1:2, :], pltpu.roll(z, 1, 0))
    z2 = jnp.where(row < 2, jnp.where(row < 1, tail[0:1, :], tail[1:2, :]), pltpu.roll(z, 2, 0))
    cw = cw_ref[...]
    y = z2 * cw[0:1, :] + z1 * cw[1:2, :] + z * cw[2:3, :]
    o_ref[:, 0:CONV_DIM] = (bg_ref[...] * y).astype(o_ref.dtype)
    ctail_ref[...] = z[tc - 2:tc, :]

    r = lax.broadcasted_iota(I32, (chunk, chunk), 0)
    c = lax.broadcasted_iota(I32, (chunk, chunk), 1)
    lower_incl = jnp.where(c <= r, 1.0, 0.0).astype(BF16)
    lane = lax.broadcasted_iota(I32, (1, LANES), 1)
    lr = lax.broadcasted_iota(I32, (LANES, LANES), 0) // GLA_DK
    lc = lax.broadcasted_iota(I32, (LANES, LANES), 1) // GLA_DK
    same_head = lr == lc
    head_mean = jnp.where(same_head, 1.0 / GLA_DK, 0.0).astype(BF16)
    r2 = lax.broadcasted_iota(I32, (chunk, n_hh * chunk), 0)
    c2 = lax.broadcasted_iota(I32, (chunk, n_hh * chunk), 1)
    causal2 = c2 - jnp.where(c2 >= chunk, chunk, 0) <= r2
    for p in range(n_pair):
        cols = slice(p * LANES, (p + 1) * LANES)
        st = st_ref[p]
        for ci in range(tc // chunk):
            rows = slice(ci * chunk, (ci + 1) * chunk)
            q = qc_ref[rows, cols] * (GLA_DK ** -0.5)
            k = kc_ref[rows, cols]
            v = vc_ref[rows, cols]
            b = _dot_exact_lhs(lower_incl, la_ref[rows, cols])
            b_last = b[chunk - 1:chunk, :]
            qe = (q * jnp.exp(b)).astype(BF16)
            ke = (k * jnp.exp(-b)).astype(BF16)
            kd = (k * jnp.exp(b_last - b)).astype(BF16)
            vb = v.astype(BF16)
            zeros = jnp.zeros_like(ke)
            ke_st = jnp.concatenate([jnp.where((lane // GLA_DK) == hh, ke, zeros)
                                     for hh in range(n_hh)], axis=0)
            v_st = jnp.concatenate([jnp.where((lane // GLA_DK) == hh, vb, zeros)
                                    for hh in range(n_hh)], axis=0)
            a = jnp.where(causal2, _dot_nt(qe, ke_st), 0.0)
            o = _dot_nt(qe, st.astype(BF16)) + _dot(a.astype(BF16), v_st)
            upd = _dot(vb.T, kd)
            st = jnp.where(same_head, st * jnp.exp(b_last) + upd, 0.0)
            ms = _dot_exact_rhs(o * o, head_mean)
            o = o * lax.rsqrt(ms + NORM_EPS) * ng_ref[:, cols]
            g = gc_ref[rows, cols]
            o = o * (g * (1.0 / (1.0 + jnp.exp(-g))))
            o_ref[rows, CONV_DIM + p * LANES:CONV_DIM + (p + 1) * LANES] = o.astype(o_ref.dtype)
        st_ref[p] = st

    @pl.when(ti == nt - 1)
    def _():
        cnew_ref[...] = ctail_ref[...]
        for p in range(n_pair):
            for hh in range(n_hh):
                blk = st_ref[p, hh * GLA_DK:(hh + 1) * GLA_DK, :]
                snew_ref[p * n_hh + hh] = blk.T[hh * GLA_DK:(hh + 1) * GLA_DK, :]


def _conv_gla(obc_prev, n, rest, la, conv_w, norm_g, conv_prev, gla_prev, row0, batch, t):
    tc = min(GLA_TC, t)
    chunk = min(GLA_CHUNK, t)
    nt = t // tc
    rb = row0 // tc
    col = lambda j: pl.BlockSpec((tc, CONV_DIM), lambda b, i: (rb + b * nt + i, j))
    const2 = lambda a: pl.BlockSpec(a.shape, lambda b, i: (0, 0))
    ng = norm_g.reshape(1, GLA_DIM)
    args = (rest,) * 7 + (la, conv_w, ng, conv_prev, gla_prev)
    return pl.pallas_call(
        functools.partial(_conv_gla_kernel, tc=tc, chunk=chunk),
        grid=(batch, nt),
        in_specs=[
            pl.BlockSpec(memory_space=pl.ANY), col(0), col(1), col(2), col(3), col(4), col(5), col(6),
            pl.BlockSpec((tc, GLA_DIM), lambda b, i: (rb + b * nt + i, 0)),
            const2(conv_w), const2(ng),
            pl.BlockSpec((None, CONV_W - 1, CONV_DIM), lambda b, i: (b, 0, 0)),
            pl.BlockSpec((None, GLA_HEADS, GLA_DK, GLA_DK), lambda b, i: (b, 0, 0, 0))],
        out_specs=[pl.BlockSpec((tc, CONV_DIM + GLA_DIM), lambda b, i: (rb + b * nt + i, 0)),
                   pl.BlockSpec((None, CONV_W - 1, CONV_DIM), lambda b, i: (b, 0, 0)),
                   pl.BlockSpec((None, GLA_HEADS, GLA_DK, GLA_DK), lambda b, i: (b, 0, 0, 0))],
        out_shape=[jax.ShapeDtypeStruct((n, CONV_DIM + GLA_DIM), BF16),
                   jax.ShapeDtypeStruct((batch, CONV_W - 1, CONV_DIM), F32),
                   jax.ShapeDtypeStruct((batch, GLA_HEADS, GLA_DK, GLA_DK), F32)],
        input_output_aliases={0: 0},
        scratch_shapes=[pltpu.VMEM((CONV_W - 1, CONV_DIM), F32),
                        pltpu.VMEM((GLA_DIM // LANES, LANES, LANES), F32)],
        compiler_params=pltpu.CompilerParams(dimension_semantics=("arbitrary", "arbitrary"),
                                             vmem_limit_bytes=VMEM_LIMIT),
        name="conv_gla",
    )(obc_prev, *args)


def _out_router_kernel(oa_ref, obc_ref, x_ref, wa_ref, wb_ref, g_ref, b_ref, wr_ref, br_ref,
                       h_ref, hp_ref, idx_ref, gate_ref, rank_ref, cnt_ref, carry_ref, *, alpha):
    i = pl.program_id(0)
    tm = x_ref.shape[0]

    @pl.when(i == 0)
    def _():
        carry_ref[...] = jnp.zeros_like(carry_ref)

    m = _dot(oa_ref[...], wa_ref[...]) + _dot(obc_ref[...], wb_ref[...])
    h = _layer_norm(alpha * x_ref[...] + m, g_ref[...], b_ref[...])
    h_ref[...] = h
    hp_ref[...] = _pack_bf16_pairs(h)

    h_hi, h_lo = _split_bf16(h)
    w_hi, w_lo = _split_bf16(wr_ref[...])
    logit = _dot_nt(w_hi, h_hi) + _dot_nt(w_hi, h_lo) + _dot_nt(w_lo, h_hi) + br_ref[...]
    eid = lax.broadcasted_iota(I32, (N_EXPERTS, tm), 0)
    r = lax.broadcasted_iota(I32, (tm, tm), 0)
    c = lax.broadcasted_iota(I32, (tm, tm), 1)
    before = jnp.where(r < c, 1.0, 0.0).astype(BF16)
    base = carry_ref[...]
    vals, idxs, ranks = [], [], []
    for _ in range(TOP_K):
        mx = jnp.max(logit, axis=0, keepdims=True)
        sel = jnp.min(jnp.where(logit == mx, eid, N_EXPERTS), axis=0, keepdims=True)
        hit = eid == sel
        logit = jnp.where(hit, -jnp.inf, logit)
        onehot = jnp.where(hit, 1.0, 0.0)
        prior = _dot(onehot.astype(BF16), before) + base
        ranks.append(jnp.sum(onehot * prior, axis=0, keepdims=True))
        base = base + jnp.sum(onehot, axis=1, keepdims=True)
        vals.append(mx)
        idxs.append(sel)
    carry_ref[...] = base
    e = [jnp.exp(v - vals[0]) for v in vals]
    inv = 1.0 / (e[0] + e[1] + e[2] + e[3])
    idx_ref[...] = jnp.concatenate(idxs, axis=0)
    gate_ref[...] = jnp.concatenate([ek * inv for ek in e], axis=0)
    rank_ref[...] = jnp.concatenate(ranks, axis=0).astype(I32)
    cnt_ref[...] = jnp.broadcast_to(base, cnt_ref.shape).astype(I32)


def _out_router(oa, obc, x, wa, wb, ln_g, ln_b, w_router, b_router, alpha):
    n, d = x.shape
    tm = TOKEN_TILE
    wr = w_router.T
    br = b_router.reshape(N_EXPERTS, 1)
    g, b = ln_g.reshape(1, d), ln_b.reshape(1, d)
    full = lambda a: pl.BlockSpec(a.shape, lambda i: (0, 0))
    row = lambda w: pl.BlockSpec((tm, w), lambda i: (i, 0))
    colb = pl.BlockSpec((TOP_K, tm), lambda i: (0, i))
    return pl.pallas_call(
        functools.partial(_out_router_kernel, alpha=alpha),
        grid=(n // tm,),
        in_specs=[row(SB_DIM), row(CONV_DIM + GLA_DIM), row(d), full(wa), full(wb), full(g), full(b),
                  full(wr), full(br)],
        out_specs=[row(d), row(d // 2), colb, colb, colb,
                   pl.BlockSpec((N_EXPERTS, LANES), lambda i: (0, 0))],
        out_shape=[jax.ShapeDtypeStruct((n, d), F32),
                   jax.ShapeDtypeStruct((n, d // 2), U32),
                   jax.ShapeDtypeStruct((TOP_K, n), I32),
                   jax.ShapeDtypeStruct((TOP_K, n), F32),
                   jax.ShapeDtypeStruct((TOP_K, n), I32),
                   jax.ShapeDtypeStruct((N_EXPERTS, LANES), I32)],
        scratch_shapes=[pltpu.VMEM((N_EXPERTS, 1), F32)],
        compiler_params=pltpu.CompilerParams(dimension_semantics=("arbitrary",),
                                             vmem_limit_bytes=VMEM_LIMIT),
        name="out_router",
    )(oa, obc, x, wa, wb, g, b, wr, br)


def _sc_gather(table, idx):
    m = idx.shape[0]
    d = table.shape[1]
    per_worker = m // (SC_CORES * SC_SUBCORES)
    n_chunks = per_worker // SC_CHUNK
    mesh = plsc.VectorSubcoreMesh(core_axis_name="c", subcore_axis_name="s")

    @functools.partial(
        pl.kernel, mesh=mesh,
        out_type=jax.ShapeDtypeStruct((m, d), table.dtype),
        scratch_types=[pltpu.VMEM((SC_CHUNK,), I32),
                       pltpu.VMEM((SC_CHUNK, d), table.dtype),
                       pltpu.SemaphoreType.DMA],
        name="sc_gather",
    )
    def gather(table_hbm, idx_hbm, out_hbm, idx_v, rows_v, sem):
        wid = lax.axis_index("s") * SC_CORES + lax.axis_index("c")
        base = wid * per_worker

        @pl.loop(0, n_chunks)
        def _(c):
            off = pl.multiple_of(base + c * SC_CHUNK, SC_CHUNK)
            pltpu.sync_copy(idx_hbm.at[pl.ds(off, SC_CHUNK)], idx_v)
            pltpu.async_copy(table_hbm.at[idx_v], rows_v, sem).wait()
            pltpu.sync_copy(rows_v, out_hbm.at[pl.ds(off, SC_CHUNK)])

    return gather(table, idx)


def _sc_chunk(per_worker):
    return max(c for c in range(8, SC_SCATTER_MAX + 1, 8) if per_worker % c == 0)


def _sc_scatter_rows(h, pos):
    n, d = h.shape
    per_worker = n // (SC_CORES * SC_SUBCORES)
    ch = _sc_chunk(per_worker)
    n_chunks = per_worker // ch
    mesh = plsc.VectorSubcoreMesh(core_axis_name="c", subcore_axis_name="s")

    @functools.partial(
        pl.kernel, mesh=mesh,
        out_type=jax.ShapeDtypeStruct((TOP_K * n, d), h.dtype),
        scratch_types=[pltpu.VMEM((ch,), I32)] * TOP_K + [pltpu.VMEM((ch, d), h.dtype)]
                      + [pltpu.SemaphoreType.DMA] * TOP_K,
        name="sc_scatter",
    )
    def scatter(h_hbm, pos_hbm, out_hbm, *scratch):
        idx_v, rows_v, sems = scratch[:TOP_K], scratch[TOP_K], scratch[TOP_K + 1:]
        wid = lax.axis_index("s") * SC_CORES + lax.axis_index("c")
        base = wid * per_worker

        @pl.loop(0, n_chunks)
        def _(c):
            t0 = pl.multiple_of(base + c * ch, 8)
            pltpu.sync_copy(h_hbm.at[pl.ds(t0, ch)], rows_v)
            for k in range(TOP_K):
                pltpu.sync_copy(pos_hbm.at[pl.ds(k * n + t0, ch)], idx_v[k])
            copies = [pltpu.async_copy(rows_v, out_hbm.at[idx_v[k]], sems[k]) for k in range(TOP_K)]
            for cp in copies:
                cp.wait()

    return scatter(h, pos)


def _expert_kernel(vt_ref, ve_ref, lo_ref, hi_ref, x_ref, wu_ref, bu_ref, wd_ref, bd_ref, y_ref,
                   wu16_ref, wd16_ref):
    i = pl.program_id(0)
    ip = jnp.maximum(i - 1, 0)
    e = ve_ref[i]
    prev = ve_ref[ip]
    tile = vt_ref[i]
    first_visit = (i == 0) | (tile != vt_ref[ip])
    lo = lo_ref[i]
    hi = hi_ref[i]
    tm = x_ref.shape[0]
    dff = wd_ref.shape[0]

    @pl.when((i == 0) | (e != prev))
    def _():
        step = 128

        def cast(s, _):
            rows = pl.ds(pl.multiple_of(s * step, step), step)
            wu16_ref[rows, :] = wu_ref[rows, :].astype(BF16)
            return 0

        lax.fori_loop(0, wu_ref.shape[0] // step, cast, 0)

        def cast_d(s, _):
            rows = pl.ds(pl.multiple_of(s * step, step), step)
            wd16_ref[rows, :] = wd_ref[rows, :].astype(BF16)
            return 0

        lax.fori_loop(0, dff // step, cast_d, 0)

    @pl.when(hi > lo)
    def _():
        x = _unpack_bf16_pairs(lax.bitcast_convert_type(x_ref[...], U32))
        glu = jnp.minimum(_dot(x, wu16_ref[:, :dff]) + bu_ref[:, :dff], SWIGLU_LIMIT)
        lin = jnp.clip(_dot(x, wu16_ref[:, dff:]) + bu_ref[:, dff:], -SWIGLU_LIMIT, SWIGLU_LIMIT)
        act = glu * (1.0 / (1.0 + jnp.exp(-SWIGLU_ALPHA * glu))) * (lin + 1.0)
        y = _dot(act.astype(BF16), wd16_ref[...]) + bd_ref[...]
        row = tile * tm + lax.broadcasted_iota(I32, (tm, 1), 0)
        mine = (row >= lo) & (row < hi)

        @pl.when(first_visit)
        def _():
            y_ref[...] = jnp.where(mine, y, 0.0)

        @pl.when(jnp.logical_not(first_visit))
        def _():
            y_ref[...] = jnp.where(mine, y, y_ref[...])


def _experts(x_sorted, visits, w_up, b_up, w_down, b_down, layer):
    ns = x_sorted.shape[0]
    tm = EXPERT_TILE
    d, dff = w_down.shape[3], w_down.shape[2]
    bu = b_up.reshape(b_up.shape[0], N_EXPERTS, 1, 2 * dff)
    bd = b_down.reshape(b_down.shape[0], N_EXPERTS, 1, d)
    wmap = lambda i, vt, ve, lo, hi: (layer, ve[i], 0, 0)
    xmap = lambda i, vt, ve, lo, hi: (vt[i], 0)
    grid_spec = pltpu.PrefetchScalarGridSpec(
        num_scalar_prefetch=4,
        grid=(visits[0].shape[0],),
        in_specs=[pl.BlockSpec((tm, x_sorted.shape[1]), xmap),
                  pl.BlockSpec((None, None, d, 2 * dff), wmap),
                  pl.BlockSpec((None, None, 1, 2 * dff), wmap),
                  pl.BlockSpec((None, None, dff, d), wmap),
                  pl.BlockSpec((None, None, 1, d), wmap)],
        out_specs=pl.BlockSpec((tm, d), xmap),
        scratch_shapes=[pltpu.VMEM((d, 2 * dff), BF16), pltpu.VMEM((dff, d), BF16)],
    )
    return pl.pallas_call(
        _expert_kernel,
        grid_spec=grid_spec,
        out_shape=jax.ShapeDtypeStruct((ns, d), F32),
        compiler_params=pltpu.CompilerParams(dimension_semantics=("arbitrary",),
                                             vmem_limit_bytes=VMEM_LIMIT),
        name="experts",
    )(*visits, x_sorted, w_up, bu, w_down, bd)


def _expert_visits(cnt, n_rows):
    tm = EXPERT_TILE
    n_steps = n_rows // tm + N_EXPERTS
    ends = jnp.cumsum(cnt)
    starts = ends - cnt
    first_tile = starts // tm
    n_vis = jnp.where(cnt > 0, (ends - 1) // tm - first_tile + 1, 0)
    vis_end = jnp.cumsum(n_vis)
    vis_start = vis_end - n_vis
    v = jnp.arange(n_steps, dtype=I32)
    vc = jnp.minimum(v, vis_end[-1] - 1)
    onehot = ((vis_start[None, :] <= vc[:, None]) & (vc[:, None] < vis_end[None, :])).astype(I32)
    pick = lambda a: jnp.sum(onehot * a[None, :], axis=1).astype(I32)
    expert = pick(jnp.arange(N_EXPERTS, dtype=I32))
    tile = pick(first_tile) + vc - pick(vis_start)
    real = v < vis_end[-1]
    lo = jnp.where(real, pick(starts), 0).astype(I32)
    hi = jnp.where(real, pick(ends), 0).astype(I32)
    return tile.astype(I32), expert, lo, hi


def _combine_kernel(y0_ref, y1_ref, y2_ref, y3_ref, gate_ref, h_ref, g_ref, b_ref, o_ref, *, alpha):
    gate = gate_ref[...]
    acc = jnp.zeros(h_ref.shape, F32)
    for k, y_ref in enumerate((y0_ref, y1_ref, y2_ref, y3_ref)):
        acc = acc + gate[:, k:k + 1] * y_ref[...]
    o_ref[...] = _layer_norm(alpha * h_ref[...] + acc, g_ref[...], b_ref[...])


def _combine(y_tok, gates, h, ln_g, ln_b, alpha):
    n, d = h.shape
    tm = TOKEN_TILE
    nt = n // tm
    g, b = ln_g.reshape(1, d), ln_b.reshape(1, d)
    full = lambda a: pl.BlockSpec(a.shape, lambda i: (0, 0))
    ysp = lambda k: pl.BlockSpec((tm, d), lambda i: (k * nt + i, 0))
    return pl.pallas_call(
        functools.partial(_combine_kernel, alpha=alpha),
        grid=(nt,),
        in_specs=[ysp(0), ysp(1), ysp(2), ysp(3),
                  pl.BlockSpec((tm, TOP_K), lambda i: (i, 0)),
                  pl.BlockSpec((tm, d), lambda i: (i, 0)), full(g), full(b)],
        out_specs=pl.BlockSpec((tm, d), lambda i: (i, 0)),
        out_shape=jax.ShapeDtypeStruct((n, d), F32),
        compiler_params=pltpu.CompilerParams(dimension_semantics=("arbitrary",),
                                             vmem_limit_bytes=VMEM_LIMIT),
        name="combine",
    )(y_tok, y_tok, y_tok, y_tok, gates, h, g, b)


def _round_up(a, m):
    return (a + m - 1) // m * m


def _moe(h, hp, idx, gates, rank, counts, w_up, b_up, w_down, b_down, ln_g, ln_b, alpha, layer):
    n, d = h.shape
    ns = TOP_K * n
    cnt = counts[:, 0]
    starts = jnp.cumsum(cnt) - cnt
    experts = jnp.arange(N_EXPERTS, dtype=I32)
    offs = jnp.sum(jnp.where(idx[:, :, None] == experts, starts, 0), axis=-1)
    pos = (offs + rank).reshape(-1).astype(I32)
    x_sorted = _sc_scatter_rows(lax.bitcast_convert_type(hp, F32), pos)
    y_sorted = _experts(x_sorted, _expert_visits(cnt, ns), w_up, b_up, w_down, b_down, layer)
    m2 = _round_up(ns, SC_ROW_ALIGN)
    fill = jnp.arange(ns, m2, dtype=I32) - ns
    y_tok = _sc_gather(y_sorted, jnp.concatenate([pos, fill]))
    return _combine(y_tok, gates.T, h, ln_g, ln_b, alpha)


def kernel(x_prompt, x_sample, cache_k, cache_v, state_conv, state_gla, w_in, conv_w, w_gate, b_gate,
           gla_norm_g, w_out, ln1_g, ln1_b, w_router, b_router, w_up, b_up, w_down, b_down, ln2_g, ln2_b):
    depth = w_in.shape[0]
    bp, seq, d = x_prompt.shape
    bs, ts, _ = x_sample.shape
    past = cache_k.shape[2]
    n_p = bp * seq
    n = n_p + bs * ts
    alpha = float((2 * depth) ** 0.25)
    x = jnp.concatenate([x_prompt.reshape(n_p, d), x_sample.reshape(bs * ts, d)], axis=0)
    ckt = cache_k.transpose(0, 1, 3, 4, 2).reshape(depth, bs, SB_DIM, past)
    cvt = cache_v.transpose(0, 1, 3, 4, 2).reshape(depth, bs, SB_DIM, past)
    zero_conv = jnp.zeros((bp, CONV_W - 1, CONV_DIM), F32)
    zero_gla = jnp.zeros((bp, GLA_HEADS, GLA_DK, GLA_DK), F32)
    wb = w_in.astype(BF16)
    o_r = 3 * SB_DIM
    n_r = 3 * CONV_DIM + 4 * GLA_DIM
    wq, wk, wv = wb[:, :, :SB_DIM], wb[:, :, SB_DIM:2 * SB_DIM], wb[:, :, 2 * SB_DIM:o_r]
    wkt, wvt = wk.transpose(0, 2, 1), wv.transpose(0, 2, 1)
    wr, wal = wb[:, :, o_r:o_r + n_r], wb[:, :, o_r + n_r:]
    wg = w_gate.astype(BF16)
    wo = w_out.astype(BF16)
    kt = jnp.zeros((depth, bp, SB_DIM, seq), F32)
    vt = jnp.zeros((depth, bp, SB_DIM, seq), F32)
    outs = [[] for _ in range(6)]
    for l in range(depth):
        wparts = (wq[l], wk[l], wv[l], wkt[l], wvt[l], wr[l], wal[l], wg[l], b_gate[l].reshape(1, -1))
        q, kt, vt, ks, vs, rest, la = _in_proj(x, wparts, l, depth, bp, seq, kt, vt)
        oa = _sb_prompt(jnp.zeros((n, SB_DIM), BF16), q, kt, vt, l, bp, seq)
        oa = _sb_decode(oa, q, ks, vs, ckt, cvt, l, n_p, bs, ts)
        obc, conv_p, gla_p = _conv_gla(jnp.zeros((n, CONV_DIM + GLA_DIM), BF16), n, rest, la, conv_w[l],
                                       gla_norm_g[l], zero_conv, zero_gla, 0, bp, seq)
        obc, conv_s, gla_s = _conv_gla(obc, n, rest, la, conv_w[l], gla_norm_g[l], state_conv[l],
                                       state_gla[l], n_p, bs, ts)
        h, hp, idx, gates, rank, counts = _out_router(oa, obc, x, wo[l, :SB_DIM], wo[l, SB_DIM:], ln1_g[l],
                                                      ln1_b[l], w_router[l], b_router[l], alpha)
        x = _moe(h, hp, idx, gates, rank, counts, w_up, b_up, w_down, b_down, ln2_g[l], ln2_b[l],
                 alpha, l)
        outs[0].append(conv_p)
        outs[1].append(gla_p)
        outs[2].append(ks.reshape(bs, ts, SB_HEADS, HEAD_DIM))
        outs[3].append(vs.reshape(bs, ts, SB_HEADS, HEAD_DIM))
        outs[4].append(conv_s)
        outs[5].append(gla_s)
    k_prompt = kt.reshape(depth, bp, SB_HEADS, HEAD_DIM, seq).transpose(0, 1, 4, 2, 3)
    v_prompt = vt.reshape(depth, bp, SB_HEADS, HEAD_DIM, seq).transpose(0, 1, 4, 2, 3)
    st = [jnp.stack(o) for o in outs]
    return (x[:n_p].reshape(bp, seq, d), x[n_p:].reshape(bs, ts, d), k_prompt, v_prompt,
            st[0], st[1], st[2], st[3], st[4], st[5])
```

```python
import functools

import jax
import jax.numpy as jnp
from jax import lax
from jax.experimental import pallas as pl
from jax.experimental.pallas import tpu as pltpu
from jax.experimental.pallas import tpu_sc as plsc

F32 = jnp.float32
BF16 = jnp.bfloat16
I32 = jnp.int32
U32 = jnp.uint32

HEAD_DIM = 64
SB_HEADS = 8
SB_DIM = SB_HEADS * HEAD_DIM
CONV_DIM = 256
CONV_W = 3
GLA_HEADS = 4
GLA_DK = 64
GLA_DIM = GLA_HEADS * GLA_DK
GLA_RANK = 16
GLA_TAU = 16.0
GLA_CHUNK = 64
N_EXPERTS = 32
TOP_K = 4
SWIGLU_LIMIT = 7.0
SWIGLU_ALPHA = 1.702
NORM_EPS = 1e-5

LANES = 128
SC_CORES = 2
SC_SUBCORES = 16
SC_CHUNK = 64
SC_ROW_ALIGN = SC_CORES * SC_SUBCORES * SC_CHUNK
SC_SCATTER_MAX = 104
TOKEN_TILE = 256
EXPERT_TILE = 256
SB_TQ = 256
SB_BK = 256
SB_GROUP = 256
DEC_TK = 1024
GLA_TC = 256
VMEM_LIMIT = 48 * 1024 * 1024
SB_DEAD = -100.0


def _dot(a, b):
    return jnp.dot(a, b, preferred_element_type=F32)


def _dot_nt(a, b):
    return lax.dot_general(a, b, (((1,), (1,)), ((), ())), preferred_element_type=F32)


def _split_bf16(x):
    hi = x.astype(BF16)
    lo = (x - hi.astype(F32)).astype(BF16)
    return hi, lo


def _dot_exact_rhs(x, m):
    hi, lo = _split_bf16(x)
    return _dot(hi, m) + _dot(lo, m)


def _dot_exact_lhs(m, x):
    hi, lo = _split_bf16(x)
    return _dot(m, hi) + _dot(m, lo)


def _pack_bf16_pairs(x):
    c = x.shape[1] // 2
    bits = lax.bitcast_convert_type(x.astype(BF16).astype(F32), U32)
    return (bits[:, :c] >> 16) | (bits[:, c:] & jnp.uint32(0xFFFF0000))


def _unpack_bf16_pairs(w):
    lo = lax.bitcast_convert_type(w << 16, F32)
    hi = lax.bitcast_convert_type(w & jnp.uint32(0xFFFF0000), F32)
    return jnp.concatenate([lo, hi], axis=1).astype(BF16)


def _softplus(z):
    return jnp.maximum(z, 0.0) + jnp.log(1.0 + jnp.exp(-jnp.abs(z)))


def _layer_norm(y, g, b):
    mu = jnp.mean(y, axis=-1, keepdims=True)
    yc = y - mu
    var = jnp.mean(yc * yc, axis=-1, keepdims=True)
    return yc * lax.rsqrt(var + NORM_EPS) * g + b


def _strict_upper(n):
    r = lax.broadcasted_iota(I32, (n, n), 0)
    c = lax.broadcasted_iota(I32, (n, n), 1)
    return jnp.where(r > c, 1.0, 0.0).astype(BF16)


def _in_proj_kernel(*refs, n_prompt_tiles):
    (_, _, x_ref, wq_ref, wk_ref, wv_ref, wkt_ref, wvt_ref, wr_ref, wal_ref, wg_ref, bg_ref,
     q_ref, kt_ref, vt_ref, ks_ref, vs_ref, r_ref, la_ref) = refs
    i = pl.program_id(0)
    xb = x_ref[...].astype(BF16)
    for c in range(0, SB_DIM, 256):
        q_ref[:, c:c + 256] = (_dot(xb, wq_ref[:, c:c + 256]) * (HEAD_DIM ** -0.5)).astype(BF16)
    for c in range(0, r_ref.shape[1], 256):
        r_ref[:, c:c + 256] = _dot(xb, wr_ref[:, c:c + 256])
    al = _dot(xb, wal_ref[...])
    g = _dot(al.astype(BF16), wg_ref[...]) + bg_ref[...]
    la_ref[...] = -_softplus(-g) * (1.0 / GLA_TAU)

    @pl.when(i < n_prompt_tiles)
    def _():
        for c in range(0, SB_DIM, 256):
            kt_ref[c:c + 256, :] = _dot_nt(wkt_ref[c:c + 256, :], xb)
            vt_ref[c:c + 256, :] = _dot_nt(wvt_ref[c:c + 256, :], xb)

    @pl.when(i >= n_prompt_tiles)
    def _():
        for c in range(0, SB_DIM, 256):
            ks_ref[:, c:c + 256] = _dot(xb, wk_ref[:, c:c + 256])
            vs_ref[:, c:c + 256] = _dot(xb, wv_ref[:, c:c + 256])


def _in_proj(x, wparts, layer, depth, batch, seq, kt_prev, vt_prev):
    n, d = x.shape
    tm = TOKEN_TILE
    n_p = batch * seq
    n_pt = n_p // tm
    tps = seq // tm
    wq, wk, wv, wkt, wvt, wr, wal, wg, bg = wparts
    n_r = wr.shape[1]
    full = lambda a: pl.BlockSpec(a.shape, lambda i: (0,) * a.ndim)
    row = lambda w: pl.BlockSpec((tm, w), lambda i: (i, 0))

    def kt_map(i):
        ip = jnp.minimum(i, n_pt - 1)
        return (layer, ip // tps, 0, ip % tps)

    kt_spec = pl.BlockSpec((None, None, SB_DIM, tm), kt_map)
    s_spec = pl.BlockSpec((tm, SB_DIM), lambda i: (jnp.maximum(i - n_pt, 0), 0))
    any_spec = pl.BlockSpec(memory_space=pl.ANY)
    kt_shape = jax.ShapeDtypeStruct((depth, batch, SB_DIM, seq), F32)
    weights = (wq, wk, wv, wkt, wvt, wr, wal, wg, bg)
    return pl.pallas_call(
        functools.partial(_in_proj_kernel, n_prompt_tiles=n_pt),
        grid=(n // tm,),
        in_specs=[any_spec, any_spec, row(d)] + [full(w) for w in weights],
        out_specs=[row(SB_DIM), kt_spec, kt_spec, s_spec, s_spec, row(n_r), row(GLA_DIM)],
        out_shape=[jax.ShapeDtypeStruct((n, SB_DIM), BF16), kt_shape, kt_shape,
                   jax.ShapeDtypeStruct((n - n_p, SB_DIM), F32),
                   jax.ShapeDtypeStruct((n - n_p, SB_DIM), F32),
                   jax.ShapeDtypeStruct((n, n_r), F32),
                   jax.ShapeDtypeStruct((n, GLA_DIM), F32)],
        input_output_aliases={0: 1, 1: 2},
        compiler_params=pltpu.CompilerParams(dimension_semantics=("arbitrary",),
                                             vmem_limit_bytes=VMEM_LIMIT),
        name="in_proj",
    )(kt_prev, vt_prev, x, *weights)


def _sb_weights(z, tri, run, mask):
    sp = _softplus(z)
    l1m = -sp
    lsig = z - sp
    if mask is not None:
        l1m = jnp.where(mask, l1m, 0.0)
    rest = _dot_exact_rhs(l1m, tri) + run
    a = jnp.exp(lsig + rest)
    if mask is not None:
        a = jnp.where(mask, a, 0.0)
    return a.astype(BF16), run + jnp.sum(l1m, axis=1, keepdims=True)


def _sb_prompt_kernel(oa_ref, q_ref, kt_ref, vt_ref, o_ref, *, tq, bk):
    del oa_ref
    qi = pl.program_id(2)
    q = q_ref[...]
    n_hh = SB_GROUP // HEAD_DIM
    lane = lax.broadcasted_iota(I32, (1, SB_GROUP), 1)
    in_head = [(lane // HEAD_DIM) == h for h in range(n_hh)]
    qh = [jnp.where(m, q, jnp.zeros_like(q)) for m in in_head]
    tri = _strict_upper(bk)
    n_full = (qi * tq) // bk
    qpos = qi * tq + lax.broadcasted_iota(I32, (tq, bk), 0)
    kpos = n_full * bk + lax.broadcasted_iota(I32, (tq, bk), 1)
    diag_mask = kpos < qpos

    def tile(jb, runs, mask, rows):
        ks = pl.multiple_of(jb * bk, bk)
        kt = kt_ref[:, pl.ds(ks, bk)].astype(BF16)
        vt = vt_ref[:, pl.ds(ks, bk)].astype(BF16)
        out = jnp.zeros((rows, SB_GROUP), F32)
        new_runs = []
        for h in range(n_hh):
            a, run = _sb_weights(_dot(qh[h][:rows], kt), tri, runs[h], mask)
            out = jnp.where(in_head[h], _dot_nt(a, vt), out)
            new_runs.append(run)
        return out, tuple(new_runs)

    def alive_of(runs, first):
        m = jnp.max(runs[0][first:])
        for r in runs[1:]:
            m = jnp.maximum(m, jnp.max(r[first:]))
        return m > SB_DEAD

    def older_blocks(j, acc, runs, rows, first):
        def cond(carry):
            j, alive, _, _ = carry
            return (j >= 0) & alive

        def body(carry):
            j, _, acc, runs = carry
            pv, runs = tile(j, runs, None, rows)
            return j - 1, alive_of(runs, first), acc + pv, runs

        j, _, acc, runs = lax.while_loop(cond, body, (j, alive_of(runs, first), acc, runs))
        return j, acc, runs

    acc, runs = tile(n_full, tuple(jnp.zeros((tq, 1), F32) for _ in range(n_hh)), diag_mask, tq)
    half = tq // 2
    j, acc, runs = older_blocks(n_full - 1, acc, runs, tq, half)
    _, acc_top, _ = older_blocks(j, acc[:half], tuple(r[:half] for r in runs), half, 0)
    o_ref[:half, :] = acc_top.astype(o_ref.dtype)
    o_ref[half:, :] = acc[half:].astype(o_ref.dtype)


def _sb_prompt(oa, q, kt, vt, layer, batch, seq):
    tq, bk = min(SB_TQ, seq), min(SB_BK, seq)
    nq = seq // tq
    hp = SB_DIM // SB_GROUP
    kv_spec = pl.BlockSpec((None, None, SB_GROUP, seq), lambda b, p, i: (layer, b, p, 0))
    return pl.pallas_call(
        functools.partial(_sb_prompt_kernel, tq=tq, bk=bk),
        grid=(batch, hp, nq),
        in_specs=[pl.BlockSpec(memory_space=pl.ANY),
                  pl.BlockSpec((tq, SB_GROUP), lambda b, p, i: (b * nq + i, p)), kv_spec, kv_spec],
        out_specs=pl.BlockSpec((tq, SB_GROUP), lambda b, p, i: (b * nq + i, p)),
        out_shape=jax.ShapeDtypeStruct(oa.shape, oa.dtype),
        input_output_aliases={0: 0},
        compiler_params=pltpu.CompilerParams(
            dimension_semantics=("arbitrary", "arbitrary", "arbitrary"),
            vmem_limit_bytes=VMEM_LIMIT),
        name="sb_prompt",
    )(oa, q, kt, vt)


def _sb_decode_kernel(oa_ref, q_ref, kn_ref, vn_ref, kc_ref, vc_ref, o_ref, acc_ref, run_ref, alive_ref,
                      *, t, tk, bk):
    del oa_ref
    j = pl.program_id(1)
    nj = pl.num_programs(1)
    q = q_ref[...]

    @pl.when(j == 0)
    def _():
        kn = kn_ref[...].astype(BF16)
        vn = vn_ref[...].astype(BF16)
        lane = lax.broadcasted_iota(I32, (1, SB_DIM), 1)
        r = lax.broadcasted_iota(I32, (t, t), 0)
        c = lax.broadcasted_iota(I32, (t, t), 1)
        mask = c < r
        tri_new = _strict_upper(t)
        top = jnp.full((1, 1), -jnp.inf, F32)
        for h in range(SB_HEADS):
            qh = jnp.where((lane // HEAD_DIM) == h, q, jnp.zeros_like(q))
            a, run = _sb_weights(_dot_nt(qh, kn), tri_new, jnp.zeros((t, 1), F32), mask)
            acc_ref[h] = _dot(a, vn)[:, h * HEAD_DIM:(h + 1) * HEAD_DIM]
            run_ref[h * t:(h + 1) * t, :] = run
            top = jnp.maximum(top, jnp.max(run, axis=0, keepdims=True))
        alive_ref[0] = (jnp.max(top) > SB_DEAD).astype(I32)

    tri = _strict_upper(bk)
    for c in range(tk // bk - 1, -1, -1):
        cols = slice(c * bk, (c + 1) * bk)

        @pl.when(alive_ref[0] > 0)
        def _():
            z = jnp.concatenate(
                [_dot(q[:, h * HEAD_DIM:(h + 1) * HEAD_DIM],
                      kc_ref[h * HEAD_DIM:(h + 1) * HEAD_DIM, cols].astype(BF16))
                 for h in range(SB_HEADS)], axis=0)
            a, run = _sb_weights(z, tri, run_ref[...], None)
            for h in range(SB_HEADS):
                vt = vc_ref[h * HEAD_DIM:(h + 1) * HEAD_DIM, cols].astype(BF16)
                acc_ref[h] = acc_ref[h] + _dot_nt(a[h * t:(h + 1) * t, :], vt)
            run_ref[...] = run
            alive_ref[0] = (jnp.max(run) > SB_DEAD).astype(I32)

    @pl.when(j == nj - 1)
    def _():
        for h in range(SB_HEADS):
            o_ref[:, h * HEAD_DIM:(h + 1) * HEAD_DIM] = acc_ref[h].astype(o_ref.dtype)


def _sb_decode(oa, q, ks, vs, cache_kt, cache_vt, layer, row0, batch, t):
    past = cache_kt.shape[3]
    tk = min(DEC_TK, past)
    bk = min(SB_BK, tk)
    nkb = past // tk
    rb = row0 // t
    new = pl.BlockSpec((t, SB_DIM), lambda b, j: (b, 0))
    cache = pl.BlockSpec((None, None, SB_DIM, tk), lambda b, j: (layer, b, 0, nkb - 1 - j))
    return pl.pallas_call(
        functools.partial(_sb_decode_kernel, t=t, tk=tk, bk=bk),
        grid=(batch, nkb),
        in_specs=[pl.BlockSpec(memory_space=pl.ANY),
                  pl.BlockSpec((t, SB_DIM), lambda b, j: (rb + b, 0)), new, new, cache, cache],
        out_specs=pl.BlockSpec((t, SB_DIM), lambda b, j: (rb + b, 0)),
        out_shape=jax.ShapeDtypeStruct(oa.shape, oa.dtype),
        input_output_aliases={0: 0},
        scratch_shapes=[pltpu.VMEM((SB_HEADS, t, HEAD_DIM), F32),
                        pltpu.VMEM((SB_HEADS * t, 1), F32),
                        pltpu.SMEM((1,), I32)],
        compiler_params=pltpu.CompilerParams(dimension_semantics=("arbitrary", "arbitrary"),
                                             vmem_limit_bytes=VMEM_LIMIT),
        name="sb_decode",
    )(oa, q, ks, vs, cache_kt, cache_vt)


def _conv_gla_kernel(*refs, tc, chunk):
    (_, bg_ref, cg_ref, u_ref, qc_ref, kc_ref, vc_ref, gc_ref, la_ref, cw_ref, ng_ref, cprev_ref, sprev_ref,
     o_ref, cnew_ref, snew_ref, ctail_ref, st_ref) = refs
    ti = pl.program_id(1)
    nt = pl.num_programs(1)
    n_pair = GLA_DIM // LANES
    n_hh = LANES // GLA_DK

    @pl.when(ti == 0)
    def _():
        ctail_ref[...] = cprev_ref[...]
        for p in range(n_pair):
            st_ref[p] = jnp.zeros((LANES, LANES), F32)
            for hh in range(n_hh):
                st_ref[p, hh * GLA_DK:(hh + 1) * GLA_DK, hh * GLA_DK:(hh + 1) * GLA_DK] = sprev_ref[p * n_hh + hh].T

    z = cg_ref[...] * u_ref[...]
    tail = ctail_ref[...]
    row = lax.broadcasted_iota(I32, z.shape, 0)
    z1 = jnp.where(row < 1, tail[1:2, :], pltpu.roll(z, 1, 0))
    z2 = jnp.where(row < 2, jnp.where(row < 1, tail[0:1, :], tail[1:2, :]), pltpu.roll(z, 2, 0))
    cw = cw_ref[...]
    y = z2 * cw[0:1, :] + z1 * cw[1:2, :] + z * cw[2:3, :]
    o_ref[:, 0:CONV_DIM] = (bg_ref[...] * y).astype(o_ref.dtype)
    ctail_ref[...] = z[tc - 2:tc, :]

    r = lax.broadcasted_iota(I32, (tc, tc), 0)
    c = lax.broadcasted_iota(I32, (tc, tc), 1)
    same_chunk = (r // chunk) == (c // chunk)
    chunk_sum = jnp.where(same_chunk, 1.0, 0.0).astype(BF16)
    chunk_cumsum = jnp.where(same_chunk & (c <= r), 1.0, 0.0).astype(BF16)
    lane = lax.broadcasted_iota(I32, (1, LANES), 1)
    lr = lax.broadcasted_iota(I32, (LANES, LANES), 0) // GLA_DK
    lc = lax.broadcasted_iota(I32, (LANES, LANES), 1) // GLA_DK
    same_head = lr == lc
    head_mean = jnp.where(same_head, 1.0 / GLA_DK, 0.0).astype(BF16)
    r2 = lax.broadcasted_iota(I32, (tc, n_hh * tc), 0)
    c2 = lax.broadcasted_iota(I32, (tc, n_hh * tc), 1)
    c2 = c2 - jnp.where(c2 >= tc, tc, 0)
    causal2 = ((r2 // chunk) == (c2 // chunk)) & (c2 <= r2)
    for p in range(n_pair):
        cols = slice(p * LANES, (p + 1) * LANES)
        la = la_ref[:, cols]
        b = _dot_exact_lhs(chunk_cumsum, la)
        tot = _dot_exact_lhs(chunk_sum, la)
        k = kc_ref[:, cols]
        qe = (qc_ref[:, cols] * (GLA_DK ** -0.5) * jnp.exp(b)).astype(BF16)
        ke = (k * jnp.exp(-b)).astype(BF16)
        kd = (k * jnp.exp(tot - b)).astype(BF16)
        decay = jnp.exp(tot)
        vb = vc_ref[:, cols].astype(BF16)
        zeros = jnp.zeros_like(ke)
        ke_st = jnp.concatenate([jnp.where((lane // GLA_DK) == hh, ke, zeros)
                                 for hh in range(n_hh)], axis=0)
        v_st = jnp.concatenate([jnp.where((lane // GLA_DK) == hh, vb, zeros)
                                for hh in range(n_hh)], axis=0)
        a = jnp.where(causal2, _dot_nt(qe, ke_st), 0.0)
        o = _dot(a.astype(BF16), v_st)
        st = st_ref[p]
        from_state = []
        for ci in range(tc // chunk):
            rows = slice(ci * chunk, (ci + 1) * chunk)
            from_state.append(_dot_nt(qe[rows], st.astype(BF16)))
            upd = _dot(vb[rows].T, kd[rows])
            st = jnp.where(same_head, st * decay[ci * chunk:ci * chunk + 1, :] + upd, 0.0)
        st_ref[p] = st
        o = o + jnp.concatenate(from_state, axis=0)
        ms = _dot_exact_rhs(o * o, head_mean)
        o = o * lax.rsqrt(ms + NORM_EPS) * ng_ref[:, cols]
        g = gc_ref[:, cols]
        o = o * (g * (1.0 / (1.0 + jnp.exp(-g))))
        o_ref[:, CONV_DIM + p * LANES:CONV_DIM + (p + 1) * LANES] = o.astype(o_ref.dtype)

    @pl.when(ti == nt - 1)
    def _():
        cnew_ref[...] = ctail_ref[...]
        for p in range(n_pair):
            for hh in range(n_hh):
                blk = st_ref[p, hh * GLA_DK:(hh + 1) * GLA_DK, :]
                snew_ref[p * n_hh + hh] = blk.T[hh * GLA_DK:(hh + 1) * GLA_DK, :]


def _conv_gla(obc_prev, n, rest, la, conv_w, norm_g, conv_prev, gla_prev, row0, batch, t):
    tc = min(GLA_TC, t)
    chunk = min(GLA_CHUNK, t)
    nt = t // tc
    rb = row0 // tc
    col = lambda j: pl.BlockSpec((tc, CONV_DIM), lambda b, i: (rb + b * nt + i, j))
    const2 = lambda a: pl.BlockSpec(a.shape, lambda b, i: (0, 0))
    ng = norm_g.reshape(1, GLA_DIM)
    args = (rest,) * 7 + (la, conv_w, ng, conv_prev, gla_prev)
    return pl.pallas_call(
        functools.partial(_conv_gla_kernel, tc=tc, chunk=chunk),
        grid=(batch, nt),
        in_specs=[
            pl.BlockSpec(memory_space=pl.ANY), col(0), col(1), col(2), col(3), col(4), col(5), col(6),
            pl.BlockSpec((tc, GLA_DIM), lambda b, i: (rb + b * nt + i, 0)),
            const2(conv_w), const2(ng),
            pl.BlockSpec((None, CONV_W - 1, CONV_DIM), lambda b, i: (b, 0, 0)),
            pl.BlockSpec((None, GLA_HEADS, GLA_DK, GLA_DK), lambda b, i: (b, 0, 0, 0))],
        out_specs=[pl.BlockSpec((tc, CONV_DIM + GLA_DIM), lambda b, i: (rb + b * nt + i, 0)),
                   pl.BlockSpec((None, CONV_W - 1, CONV_DIM), lambda b, i: (b, 0, 0)),
                   pl.BlockSpec((None, GLA_HEADS, GLA_DK, GLA_DK), lambda b, i: (b, 0, 0, 0))],
        out_shape=[jax.ShapeDtypeStruct((n, CONV_DIM + GLA_DIM), BF16),
                   jax.ShapeDtypeStruct((batch, CONV_W - 1, CONV_DIM), F32),
                   jax.ShapeDtypeStruct((batch, GLA_HEADS, GLA_DK, GLA_DK), F32)],
        input_output_aliases={0: 0},
        scratch_shapes=[pltpu.VMEM((CONV_W - 1, CONV_DIM), F32),
                        pltpu.VMEM((GLA_DIM // LANES, LANES, LANES), F32)],
        compiler_params=pltpu.CompilerParams(dimension_semantics=("arbitrary", "arbitrary"),
                                             vmem_limit_bytes=VMEM_LIMIT),
        name="conv_gla",
    )(obc_prev, *args)


def _out_router_kernel(oa_ref, obc_ref, x_ref, wa_ref, wb_ref, g_ref, b_ref, wr_ref, br_ref,
                       h_ref, hp_ref, idx_ref, gate_ref, rank_ref, cnt_ref, carry_ref, *, alpha):
    i = pl.program_id(0)
    tm = x_ref.shape[0]

    @pl.when(i == 0)
    def _():
        carry_ref[...] = jnp.zeros_like(carry_ref)

    m = _dot(oa_ref[...], wa_ref[...]) + _dot(obc_ref[...], wb_ref[...])
    h = _layer_norm(alpha * x_ref[...] + m, g_ref[...], b_ref[...])
    h_ref[...] = h
    hp_ref[...] = lax.bitcast_convert_type(_pack_bf16_pairs(h), F32)

    h_hi, h_lo = _split_bf16(h)
    w_hi, w_lo = _split_bf16(wr_ref[...])
    logit = _dot_nt(w_hi, h_hi) + _dot_nt(w_hi, h_lo) + _dot_nt(w_lo, h_hi) + br_ref[...]
    eid = lax.broadcasted_iota(I32, (N_EXPERTS, tm), 0)
    r = lax.broadcasted_iota(I32, (tm, tm), 0)
    c = lax.broadcasted_iota(I32, (tm, tm), 1)
    before = jnp.where(r < c, 1.0, 0.0).astype(BF16)
    base = carry_ref[...]
    vals, idxs, ranks = [], [], []
    for _ in range(TOP_K):
        mx = jnp.max(logit, axis=0, keepdims=True)
        sel = jnp.min(jnp.where(logit == mx, eid, N_EXPERTS), axis=0, keepdims=True)
        hit = eid == sel
        logit = jnp.where(hit, -jnp.inf, logit)
        onehot = jnp.where(hit, 1.0, 0.0)
        prior = _dot(onehot.astype(BF16), before) + base
        ranks.append(jnp.sum(onehot * prior, axis=0, keepdims=True))
        base = base + jnp.sum(onehot, axis=1, keepdims=True)
        vals.append(mx)
        idxs.append(sel)
    carry_ref[...] = base
    e = [jnp.exp(v - vals[0]) for v in vals]
    inv = 1.0 / (e[0] + e[1] + e[2] + e[3])
    idx_ref[...] = jnp.concatenate(idxs, axis=0)
    gate_ref[...] = jnp.concatenate([ek * inv for ek in e], axis=0)
    rank_ref[...] = jnp.concatenate(ranks, axis=0).astype(I32)
    cnt_ref[...] = jnp.broadcast_to(base, cnt_ref.shape).astype(I32)


def _out_router(oa, obc, x, wa, wb, ln_g, ln_b, w_router, b_router, alpha):
    n, d = x.shape
    tm = TOKEN_TILE
    wr = w_router.T
    br = b_router.reshape(N_EXPERTS, 1)
    g, b = ln_g.reshape(1, d), ln_b.reshape(1, d)
    full = lambda a: pl.BlockSpec(a.shape, lambda i: (0, 0))
    row = lambda w: pl.BlockSpec((tm, w), lambda i: (i, 0))
    colb = pl.BlockSpec((TOP_K, tm), lambda i: (0, i))
    return pl.pallas_call(
        functools.partial(_out_router_kernel, alpha=alpha),
        grid=(n // tm,),
        in_specs=[row(SB_DIM), row(CONV_DIM + GLA_DIM), row(d), full(wa), full(wb), full(g), full(b),
                  full(wr), full(br)],
        out_specs=[row(d), row(d // 2), colb, colb, colb,
                   pl.BlockSpec((N_EXPERTS, LANES), lambda i: (0, 0))],
        out_shape=[jax.ShapeDtypeStruct((n, d), F32),
                   jax.ShapeDtypeStruct((n, d // 2), F32),
                   jax.ShapeDtypeStruct((TOP_K, n), I32),
                   jax.ShapeDtypeStruct((TOP_K, n), F32),
                   jax.ShapeDtypeStruct((TOP_K, n), I32),
                   jax.ShapeDtypeStruct((N_EXPERTS, LANES), I32)],
        scratch_shapes=[pltpu.VMEM((N_EXPERTS, 1), F32)],
        compiler_params=pltpu.CompilerParams(dimension_semantics=("arbitrary",),
                                             vmem_limit_bytes=VMEM_LIMIT),
        name="out_router",
    )(oa, obc, x, wa, wb, g, b, wr, br)


def _sc_gather(table, idx):
    m = idx.shape[0]
    d = table.shape[1]
    per_worker = m // (SC_CORES * SC_SUBCORES)
    n_chunks = per_worker // SC_CHUNK
    mesh = plsc.VectorSubcoreMesh(core_axis_name="c", subcore_axis_name="s")

    @functools.partial(
        pl.kernel, mesh=mesh,
        out_type=jax.ShapeDtypeStruct((m, d), table.dtype),
        scratch_types=[pltpu.VMEM((SC_CHUNK,), I32),
                       pltpu.VMEM((SC_CHUNK, d), table.dtype),
                       pltpu.SemaphoreType.DMA],
        name="sc_gather",
    )
    def gather(table_hbm, idx_hbm, out_hbm, idx_v, rows_v, sem):
        wid = lax.axis_index("s") * SC_CORES + lax.axis_index("c")
        base = wid * per_worker

        @pl.loop(0, n_chunks)
        def _(c):
            off = pl.multiple_of(base + c * SC_CHUNK, SC_CHUNK)
            pltpu.sync_copy(idx_hbm.at[pl.ds(off, SC_CHUNK)], idx_v)
            pltpu.async_copy(table_hbm.at[idx_v], rows_v, sem).wait()
            pltpu.sync_copy(rows_v, out_hbm.at[pl.ds(off, SC_CHUNK)])

    return gather(table, idx)


def _sc_chunk(per_worker):
    return max(c for c in range(8, SC_SCATTER_MAX + 1, 8) if per_worker % c == 0)


def _sc_scatter_rows(h, pos):
    n, d = h.shape
    per_worker = n // (SC_CORES * SC_SUBCORES)
    ch = _sc_chunk(per_worker)
    n_chunks = per_worker // ch
    mesh = plsc.VectorSubcoreMesh(core_axis_name="c", subcore_axis_name="s")

    @functools.partial(
        pl.kernel, mesh=mesh,
        out_type=jax.ShapeDtypeStruct((TOP_K * n, d), h.dtype),
        scratch_types=[pltpu.VMEM((ch,), I32)] * TOP_K + [pltpu.VMEM((ch, d), h.dtype)]
                      + [pltpu.SemaphoreType.DMA] * TOP_K,
        name="sc_scatter",
    )
    def scatter(h_hbm, pos_hbm, out_hbm, *scratch):
        idx_v, rows_v, sems = scratch[:TOP_K], scratch[TOP_K], scratch[TOP_K + 1:]
        wid = lax.axis_index("s") * SC_CORES + lax.axis_index("c")
        base = wid * per_worker

        @pl.loop(0, n_chunks)
        def _(c):
            t0 = pl.multiple_of(base + c * ch, 8)
            pltpu.sync_copy(h_hbm.at[pl.ds(t0, ch)], rows_v)
            for k in range(TOP_K):
                pltpu.sync_copy(pos_hbm.at[pl.ds(k * n + t0, ch)], idx_v[k])
            copies = [pltpu.async_copy(rows_v, out_hbm.at[idx_v[k]], sems[k]) for k in range(TOP_K)]
            for cp in copies:
                cp.wait()

    return scatter(h, pos)


def _expert_kernel(vt_ref, ve_ref, lo_ref, hi_ref, x_ref, wu_ref, bu_ref, wd_ref, bd_ref, y_ref,
                   wu16_ref, wd16_ref):
    i = pl.program_id(0)
    ip = jnp.maximum(i - 1, 0)
    e = ve_ref[i]
    prev = ve_ref[ip]
    tile = vt_ref[i]
    first_visit = (i == 0) | (tile != vt_ref[ip])
    lo = lo_ref[i]
    hi = hi_ref[i]
    tm = x_ref.shape[0]
    dff = wd_ref.shape[0]

    @pl.when((i == 0) | (e != prev))
    def _():
        step = 128

        def cast(s, _):
            rows = pl.ds(pl.multiple_of(s * step, step), step)
            wu16_ref[rows, :] = wu_ref[rows, :].astype(BF16)
            return 0

        lax.fori_loop(0, wu_ref.shape[0] // step, cast, 0)

        def cast_d(s, _):
            rows = pl.ds(pl.multiple_of(s * step, step), step)
            wd16_ref[rows, :] = wd_ref[rows, :].astype(BF16)
            return 0

        lax.fori_loop(0, dff // step, cast_d, 0)

    @pl.when(hi > lo)
    def _():
        x = _unpack_bf16_pairs(lax.bitcast_convert_type(x_ref[...], U32))
        glu = jnp.minimum(_dot(x, wu16_ref[:, :dff]) + bu_ref[:, :dff], SWIGLU_LIMIT)
        lin = jnp.clip(_dot(x, wu16_ref[:, dff:]) + bu_ref[:, dff:], -SWIGLU_LIMIT, SWIGLU_LIMIT)
        act = glu * (1.0 / (1.0 + jnp.exp(-SWIGLU_ALPHA * glu))) * (lin + 1.0)
        y = _dot(act.astype(BF16), wd16_ref[...]) + bd_ref[...]
        y = lax.bitcast_convert_type(_pack_bf16_pairs(y), F32)
        row = tile * tm + lax.broadcasted_iota(I32, (tm, 1), 0)
        mine = (row >= lo) & (row < hi)

        @pl.when(first_visit)
        def _():
            y_ref[...] = jnp.where(mine, y, 0.0)

        @pl.when(jnp.logical_not(first_visit))
        def _():
            y_ref[...] = jnp.where(mine, y, y_ref[...])


def _experts(x_sorted, visits, w_up, b_up, w_down, b_down, layer):
    ns = x_sorted.shape[0]
    tm = EXPERT_TILE
    d, dff = w_down.shape[3], w_down.shape[2]
    bu = b_up.reshape(b_up.shape[0], N_EXPERTS, 1, 2 * dff)
    bd = b_down.reshape(b_down.shape[0], N_EXPERTS, 1, d)
    wmap = lambda i, vt, ve, lo, hi: (layer, ve[i], 0, 0)
    xmap = lambda i, vt, ve, lo, hi: (vt[i], 0)
    grid_spec = pltpu.PrefetchScalarGridSpec(
        num_scalar_prefetch=4,
        grid=(visits[0].shape[0],),
        in_specs=[pl.BlockSpec((tm, x_sorted.shape[1]), xmap),
                  pl.BlockSpec((None, None, d, 2 * dff), wmap),
                  pl.BlockSpec((None, None, 1, 2 * dff), wmap),
                  pl.BlockSpec((None, None, dff, d), wmap),
                  pl.BlockSpec((None, None, 1, d), wmap)],
        out_specs=pl.BlockSpec((tm, d // 2), xmap),
        scratch_shapes=[pltpu.VMEM((d, 2 * dff), BF16), pltpu.VMEM((dff, d), BF16)],
    )
    return pl.pallas_call(
        _expert_kernel,
        grid_spec=grid_spec,
        out_shape=jax.ShapeDtypeStruct((ns, d // 2), F32),
        compiler_params=pltpu.CompilerParams(dimension_semantics=("arbitrary",),
                                             vmem_limit_bytes=VMEM_LIMIT),
        name="experts",
    )(*visits, x_sorted, w_up, bu, w_down, bd)


def _expert_visits(cnt, n_rows):
    tm = EXPERT_TILE
    n_steps = n_rows // tm + N_EXPERTS
    ends = jnp.cumsum(cnt)
    starts = ends - cnt
    first_tile = starts // tm
    n_vis = jnp.where(cnt > 0, (ends - 1) // tm - first_tile + 1, 0)
    vis_end = jnp.cumsum(n_vis)
    vis_start = vis_end - n_vis
    v = jnp.arange(n_steps, dtype=I32)
    vc = jnp.minimum(v, vis_end[-1] - 1)
    onehot = ((vis_start[None, :] <= vc[:, None]) & (vc[:, None] < vis_end[None, :])).astype(I32)
    pick = lambda a: jnp.sum(onehot * a[None, :], axis=1).astype(I32)
    expert = pick(jnp.arange(N_EXPERTS, dtype=I32))
    tile = pick(first_tile) + vc - pick(vis_start)
    real = v < vis_end[-1]
    lo = jnp.where(real, pick(starts), 0).astype(I32)
    hi = jnp.where(real, pick(ends), 0).astype(I32)
    return tile.astype(I32), expert, lo, hi


def _combine_kernel(y0_ref, y1_ref, y2_ref, y3_ref, gate_ref, h_ref, g_ref, b_ref, o_ref, *, alpha):
    gate = gate_ref[...]
    lo = jnp.zeros(y0_ref.shape, F32)
    hi = jnp.zeros(y0_ref.shape, F32)
    for k, y_ref in enumerate((y0_ref, y1_ref, y2_ref, y3_ref)):
        w = lax.bitcast_convert_type(y_ref[...], U32)
        lo = lo + gate[:, k:k + 1] * lax.bitcast_convert_type(w << 16, F32)
        hi = hi + gate[:, k:k + 1] * lax.bitcast_convert_type(w & jnp.uint32(0xFFFF0000), F32)
    acc = jnp.concatenate([lo, hi], axis=1)
    o_ref[...] = _layer_norm(alpha * h_ref[...] + acc, g_ref[...], b_ref[...])


def _combine(y_tok, gates, h, ln_g, ln_b, alpha):
    n, d = h.shape
    tm = TOKEN_TILE
    nt = n // tm
    g, b = ln_g.reshape(1, d), ln_b.reshape(1, d)
    full = lambda a: pl.BlockSpec(a.shape, lambda i: (0, 0))
    ysp = lambda k: pl.BlockSpec((tm, d // 2), lambda i: (k * nt + i, 0))
    return pl.pallas_call(
        functools.partial(_combine_kernel, alpha=alpha),
        grid=(nt,),
        in_specs=[ysp(0), ysp(1), ysp(2), ysp(3),
                  pl.BlockSpec((tm, TOP_K), lambda i: (i, 0)),
                  pl.BlockSpec((tm, d), lambda i: (i, 0)), full(g), full(b)],
        out_specs=pl.BlockSpec((tm, d), lambda i: (i, 0)),
        out_shape=jax.ShapeDtypeStruct((n, d), F32),
        compiler_params=pltpu.CompilerParams(dimension_semantics=("arbitrary",),
                                             vmem_limit_bytes=VMEM_LIMIT),
        name="combine",
    )(y_tok, y_tok, y_tok, y_tok, gates, h, g, b)


def _round_up(a, m):
    return (a + m - 1) // m * m


def _moe(h, hp, idx, gates, rank, counts, w_up, b_up, w_down, b_down, ln_g, ln_b, alpha, layer):
    n, d = h.shape
    ns = TOP_K * n
    cnt = counts[:, 0]
    starts = jnp.cumsum(cnt) - cnt
    experts = jnp.arange(N_EXPERTS, dtype=I32)
    offs = jnp.sum(jnp.where(idx[:, :, None] == experts, starts, 0), axis=-1)
    pos = (offs + rank).reshape(-1).astype(I32)
    x_sorted = _sc_scatter_rows(hp, pos)
    y_sorted = _experts(x_sorted, _expert_visits(cnt, ns), w_up, b_up, w_down, b_down, layer)
    m2 = _round_up(ns, SC_ROW_ALIGN)
    fill = jnp.arange(ns, m2, dtype=I32) - ns
    y_tok = _sc_gather(y_sorted, jnp.concatenate([pos, fill]))
    return _combine(y_tok, gates.T, h, ln_g, ln_b, alpha)


def kernel(x_prompt, x_sample, cache_k, cache_v, state_conv, state_gla, w_in, conv_w, w_gate, b_gate,
           gla_norm_g, w_out, ln1_g, ln1_b, w_router, b_router, w_up, b_up, w_down, b_down, ln2_g, ln2_b):
    depth = w_in.shape[0]
    bp, seq, d = x_prompt.shape
    bs, ts, _ = x_sample.shape
    past = cache_k.shape[2]
    n_p = bp * seq
    n = n_p + bs * ts
    alpha = float((2 * depth) ** 0.25)
    x = jnp.concatenate([x_prompt.reshape(n_p, d), x_sample.reshape(bs * ts, d)], axis=0)
    ckt = cache_k.transpose(0, 1, 3, 4, 2).reshape(depth, bs, SB_DIM, past)
    cvt = cache_v.transpose(0, 1, 3, 4, 2).reshape(depth, bs, SB_DIM, past)
    zero_conv = jnp.zeros((bp, CONV_W - 1, CONV_DIM), F32)
    zero_gla = jnp.zeros((bp, GLA_HEADS, GLA_DK, GLA_DK), F32)
    wb = w_in.astype(BF16)
    o_r = 3 * SB_DIM
    n_r = 3 * CONV_DIM + 4 * GLA_DIM
    wq, wk, wv = wb[:, :, :SB_DIM], wb[:, :, SB_DIM:2 * SB_DIM], wb[:, :, 2 * SB_DIM:o_r]
    wkt, wvt = wk.transpose(0, 2, 1), wv.transpose(0, 2, 1)
    wr, wal = wb[:, :, o_r:o_r + n_r], wb[:, :, o_r + n_r:]
    wg = w_gate.astype(BF16)
    wo = w_out.astype(BF16)
    kt = jnp.zeros((depth, bp, SB_DIM, seq), F32)
    vt = jnp.zeros((depth, bp, SB_DIM, seq), F32)
    outs = [[] for _ in range(6)]
    for l in range(depth):
        wparts = (wq[l], wk[l], wv[l], wkt[l], wvt[l], wr[l], wal[l], wg[l], b_gate[l].reshape(1, -1))
        q, kt, vt, ks, vs, rest, la = _in_proj(x, wparts, l, depth, bp, seq, kt, vt)
        oa = _sb_prompt(jnp.zeros((n, SB_DIM), BF16), q, kt, vt, l, bp, seq)
        oa = _sb_decode(oa, q, ks, vs, ckt, cvt, l, n_p, bs, ts)
        obc, conv_p, gla_p = _conv_gla(jnp.zeros((n, CONV_DIM + GLA_DIM), BF16), n, rest, la, conv_w[l],
                                       gla_norm_g[l], zero_conv, zero_gla, 0, bp, seq)
        obc, conv_s, gla_s = _conv_gla(obc, n, rest, la, conv_w[l], gla_norm_g[l], state_conv[l],
                                       state_gla[l], n_p, bs, ts)
        h, hp, idx, gates, rank, counts = _out_router(oa, obc, x, wo[l, :SB_DIM], wo[l, SB_DIM:], ln1_g[l],
                                                      ln1_b[l], w_router[l], b_router[l], alpha)
        x = _moe(h, hp, idx, gates, rank, counts, w_up, b_up, w_down, b_down, ln2_g[l], ln2_b[l],
                 alpha, l)
        outs[0].append(conv_p)
        outs[1].append(gla_p)
        outs[2].append(ks.reshape(bs, ts, SB_HEADS, HEAD_DIM))
        outs[3].append(vs.reshape(bs, ts, SB_HEADS, HEAD_DIM))
        outs[4].append(conv_s)
        outs[5].append(gla_s)
    k_prompt = kt.reshape(depth, bp, SB_HEADS, HEAD_DIM, seq).transpose(0, 1, 4, 2, 3)
    v_prompt = vt.reshape(depth, bp, SB_HEADS, HEAD_DIM, seq).transpose(0, 1, 4, 2, 3)
    st = [jnp.stack(o) for o in outs]
    return (x[:n_p].reshape(bp, seq, d), x[n_p:].reshape(bs, ts, d), k_prompt, v_prompt,
            st[0], st[1], st[2], st[3], st[4], st[5])
```

```python
import functools

import jax
import jax.numpy as jnp
from jax import lax
from jax.experimental import pallas as pl
from jax.experimental.pallas import tpu as pltpu
from jax.experimental.pallas import tpu_sc as plsc

F32 = jnp.float32
BF16 = jnp.bfloat16
I32 = jnp.int32
U32 = jnp.uint32

HEAD_DIM = 64
SB_HEADS = 8
SB_DIM = SB_HEADS * HEAD_DIM
CONV_DIM = 256
CONV_W = 3
GLA_HEADS = 4
GLA_DK = 64
GLA_DIM = GLA_HEADS * GLA_DK
GLA_RANK = 16
GLA_TAU = 16.0
GLA_CHUNK = 64
N_EXPERTS = 32
TOP_K = 4
SWIGLU_LIMIT = 7.0
SWIGLU_ALPHA = 1.702
NORM_EPS = 1e-5

LANES = 128
SC_CORES = 2
SC_SUBCORES = 16
SC_CHUNK = 64
SC_ROW_ALIGN = SC_CORES * SC_SUBCORES * SC_CHUNK
SC_SCATTER_MAX = 104
TOKEN_TILE = 256
WIDE_TILE_MAX = 1280
COMBINE_TILE_MAX = 640
PROMPT_TILE = 512
EXPERT_TILE = 256
SB_TQ = 256
SB_BK = 256
SB_GROUP = 256
DEC_TK = 512
GLA_TC = 256
VMEM_LIMIT = 48 * 1024 * 1024
SB_DEAD = -100.0


def _dot(a, b):
    return jnp.dot(a, b, preferred_element_type=F32)


def _dot_nt(a, b):
    return lax.dot_general(a, b, (((1,), (1,)), ((), ())), preferred_element_type=F32)


def _split_bf16(x):
    hi = x.astype(BF16)
    lo = (x - hi.astype(F32)).astype(BF16)
    return hi, lo


def _dot_exact_rhs(x, m):
    hi, lo = _split_bf16(x)
    return _dot(hi, m) + _dot(lo, m)


def _dot_exact_lhs(m, x):
    hi, lo = _split_bf16(x)
    return _dot(m, hi) + _dot(m, lo)


def _pack_bf16_pairs(x):
    c = x.shape[1] // 2
    bits = lax.bitcast_convert_type(x.astype(BF16).astype(F32), U32)
    return (bits[:, :c] >> 16) | (bits[:, c:] & jnp.uint32(0xFFFF0000))


def _unpack_bf16_pairs(w):
    lo = lax.bitcast_convert_type(w << 16, F32)
    hi = lax.bitcast_convert_type(w & jnp.uint32(0xFFFF0000), F32)
    return jnp.concatenate([lo, hi], axis=1).astype(BF16)


def _softplus(z):
    return jnp.maximum(z, 0.0) + jnp.log(1.0 + jnp.exp(-jnp.abs(z)))


def _layer_norm(y, g, b):
    mu = jnp.mean(y, axis=-1, keepdims=True)
    yc = y - mu
    var = jnp.mean(yc * yc, axis=-1, keepdims=True)
    return yc * lax.rsqrt(var + NORM_EPS) * g + b


def _strict_upper(n):
    r = lax.broadcasted_iota(I32, (n, n), 0)
    c = lax.broadcasted_iota(I32, (n, n), 1)
    return jnp.where(r > c, 1.0, 0.0).astype(BF16)


def _in_proj_kernel(*refs, transposed_kv):
    if transposed_kv:
        (_, _, x_ref, wq_ref, wk_ref, wv_ref, wr_ref, wal_ref, wg_ref, bg_ref,
         q_ref, k_ref, v_ref, r_ref, la_ref) = refs
    else:
        (x_ref, wq_ref, wk_ref, wv_ref, wr_ref, wal_ref, wg_ref, bg_ref,
         q_ref, k_ref, v_ref, r_ref, la_ref) = refs
    xb = x_ref[...].astype(BF16)
    for c in range(0, SB_DIM, 256):
        q_ref[:, c:c + 256] = (_dot(xb, wq_ref[:, c:c + 256]) * (HEAD_DIM ** -0.5)).astype(BF16)
    for c in range(0, r_ref.shape[1], 256):
        r_ref[:, c:c + 256] = _dot(xb, wr_ref[:, c:c + 256])
    al = _dot(xb, wal_ref[...])
    g = _dot(al.astype(BF16), wg_ref[...]) + bg_ref[...]
    la_ref[...] = -_softplus(-g) * (1.0 / GLA_TAU)
    for c in range(0, SB_DIM, 256):
        if transposed_kv:
            k_ref[c:c + 256, :] = _dot_nt(wk_ref[c:c + 256, :], xb)
            v_ref[c:c + 256, :] = _dot_nt(wv_ref[c:c + 256, :], xb)
        else:
            k_ref[:, c:c + 256] = _dot(xb, wk_ref[:, c:c + 256])
            v_ref[:, c:c + 256] = _dot(xb, wv_ref[:, c:c + 256])


def _in_proj(x, row0, rows, tm, weights, kv_prompt=None):
    d = x.shape[1]
    wq, wk, wv, wr, wal, wg, bg = weights
    n_r = wr.shape[1]
    rb = row0 // tm
    full = lambda a: pl.BlockSpec(a.shape, lambda i: (0,) * a.ndim)
    row = lambda w: pl.BlockSpec((tm, w), lambda i: (i, 0))
    x_spec = pl.BlockSpec((tm, d), lambda i: (rb + i, 0))
    sds = lambda w, dt: jax.ShapeDtypeStruct((rows, w), dt)
    if kv_prompt is not None:
        layer, seq, kt, vt = kv_prompt
        tps = seq // tm
        kv_spec = pl.BlockSpec((None, None, SB_DIM, tm), lambda i: (layer, i // tps, 0, i % tps))
        kv_shape = jax.ShapeDtypeStruct(kt.shape, kt.dtype)
        anywhere = pl.BlockSpec(memory_space=pl.ANY)
        extra_in, extra_args, aliases = [anywhere, anywhere], (kt, vt), {0: 1, 1: 2}
    else:
        kv_spec, kv_shape = row(SB_DIM), sds(SB_DIM, F32)
        extra_in, extra_args, aliases = [], (), {}
    return pl.pallas_call(
        functools.partial(_in_proj_kernel, transposed_kv=kv_prompt is not None),
        grid=(rows // tm,),
        in_specs=extra_in + [x_spec] + [full(w) for w in weights],
        out_specs=[row(SB_DIM), kv_spec, kv_spec, row(n_r), row(GLA_DIM)],
        out_shape=[sds(SB_DIM, BF16), kv_shape, kv_shape, sds(n_r, F32), sds(GLA_DIM, F32)],
        input_output_aliases=aliases,
        compiler_params=pltpu.CompilerParams(dimension_semantics=("arbitrary",),
                                             vmem_limit_bytes=VMEM_LIMIT),
        name="in_proj",
    )(*extra_args, x, *weights)


def _sb_weights(z, tri, run, mask):
    sp = _softplus(z)
    l1m = -sp
    lsig = z - sp
    if mask is not None:
        l1m = jnp.where(mask, l1m, 0.0)
    rest = _dot_exact_rhs(l1m, tri) + run
    a = jnp.exp(lsig + rest)
    if mask is not None:
        a = jnp.where(mask, a, 0.0)
    return a.astype(BF16), run + jnp.sum(l1m, axis=1, keepdims=True)


def _sb_prompt_kernel(oa_ref, q_ref, kt_ref, vt_ref, o_ref, *, tq, bk):
    del oa_ref
    qi = pl.program_id(2)
    q = q_ref[...]
    n_hh = SB_GROUP // HEAD_DIM
    lane = lax.broadcasted_iota(I32, (1, SB_GROUP), 1)
    in_head = [(lane // HEAD_DIM) == h for h in range(n_hh)]
    qh = [jnp.where(m, q, jnp.zeros_like(q)) for m in in_head]
    tri = _strict_upper(bk)
    n_full = (qi * tq) // bk
    qpos = qi * tq + lax.broadcasted_iota(I32, (tq, bk), 0)
    kpos = n_full * bk + lax.broadcasted_iota(I32, (tq, bk), 1)
    diag_mask = kpos < qpos

    def tile(jb, runs, mask):
        ks = pl.multiple_of(jb * bk, bk)
        kt = kt_ref[:, pl.ds(ks, bk)].astype(BF16)
        vt = vt_ref[:, pl.ds(ks, bk)].astype(BF16)
        out = jnp.zeros((tq, SB_GROUP), F32)
        new_runs = []
        for h in range(n_hh):
            a, run = _sb_weights(_dot(qh[h], kt), tri, runs[h], mask)
            out = jnp.where(in_head[h], _dot_nt(a, vt), out)
            new_runs.append(run)
        return out, tuple(new_runs)

    def alive_of(runs):
        m = jnp.max(runs[0])
        for r in runs[1:]:
            m = jnp.maximum(m, jnp.max(r))
        return m > SB_DEAD

    acc, runs = tile(n_full, tuple(jnp.zeros((tq, 1), F32) for _ in range(n_hh)), diag_mask)

    def cond(carry):
        j, alive, _, _ = carry
        return (j >= 0) & alive

    def body(carry):
        j, _, acc, runs = carry
        pv, runs = tile(j, runs, None)
        return j - 1, alive_of(runs), acc + pv, runs

    _, _, acc, _ = lax.while_loop(cond, body, (n_full - 1, alive_of(runs), acc, runs))
    o_ref[...] = acc.astype(o_ref.dtype)


def _sb_prompt(oa, q, kt, vt, layer, batch, seq):
    tq, bk = min(SB_TQ, seq), min(SB_BK, seq)
    nq = seq // tq
    hp = SB_DIM // SB_GROUP
    kv_spec = pl.BlockSpec((None, None, SB_GROUP, seq), lambda b, p, i: (layer, b, p, 0))
    return pl.pallas_call(
        functools.partial(_sb_prompt_kernel, tq=tq, bk=bk),
        grid=(batch, hp, nq),
        in_specs=[pl.BlockSpec(memory_space=pl.ANY),
                  pl.BlockSpec((tq, SB_GROUP), lambda b, p, i: (b * nq + i, p)), kv_spec, kv_spec],
        out_specs=pl.BlockSpec((tq, SB_GROUP), lambda b, p, i: (b * nq + i, p)),
        out_shape=jax.ShapeDtypeStruct(oa.shape, oa.dtype),
        input_output_aliases={0: 0},
        compiler_params=pltpu.CompilerParams(
            dimension_semantics=("arbitrary", "arbitrary", "arbitrary"),
            vmem_limit_bytes=VMEM_LIMIT),
        name="sb_prompt",
    )(oa, q, kt, vt)


def _sb_decode_kernel(oa_ref, q_ref, kn_ref, vn_ref, kc_hbm, vc_hbm, o_ref, kbuf, vbuf, sem, acc_ref, run_ref,
                      alive_ref, *, t, tk, bk, layer, nkb):
    del oa_ref
    b = pl.program_id(0)
    q = q_ref[...]

    def block_copies(j, slot):
        cols = pl.ds(pl.multiple_of((nkb - 1 - j) * tk, tk), tk)
        return (pltpu.make_async_copy(kc_hbm.at[layer, b, :, cols], kbuf.at[slot], sem.at[0, slot]),
                pltpu.make_async_copy(vc_hbm.at[layer, b, :, cols], vbuf.at[slot], sem.at[1, slot]))

    for cp in block_copies(0, 0):
        cp.start()

    kn = kn_ref[...].astype(BF16)
    vn = vn_ref[...].astype(BF16)
    lane = lax.broadcasted_iota(I32, (1, SB_DIM), 1)
    r = lax.broadcasted_iota(I32, (t, t), 0)
    c = lax.broadcasted_iota(I32, (t, t), 1)
    mask = c < r
    tri_new = _strict_upper(t)
    for h in range(SB_HEADS):
        qh = jnp.where((lane // HEAD_DIM) == h, q, jnp.zeros_like(q))
        a, run = _sb_weights(_dot_nt(qh, kn), tri_new, jnp.zeros((t, 1), F32), mask)
        acc_ref[h] = _dot(a, vn)[:, h * HEAD_DIM:(h + 1) * HEAD_DIM]
        run_ref[h * t:(h + 1) * t, :] = run
    alive_ref[0] = (jnp.max(run_ref[...]) > SB_DEAD).astype(I32)

    tri = _strict_upper(bk)

    def cond(carry):
        j, alive = carry
        return (j < nkb) & (alive > 0)

    def body(carry):
        j, _ = carry
        slot = j % 2

        @pl.when(j + 1 < nkb)
        def _():
            for cp in block_copies(j + 1, 1 - slot):
                cp.start()

        for cp in block_copies(j, slot):
            cp.wait()
        for c in range(tk // bk - 1, -1, -1):
            cols = slice(c * bk, (c + 1) * bk)

            @pl.when(alive_ref[0] > 0)
            def _():
                z = jnp.concatenate(
                    [_dot(q[:, h * HEAD_DIM:(h + 1) * HEAD_DIM],
                          kbuf[slot, h * HEAD_DIM:(h + 1) * HEAD_DIM, cols].astype(BF16))
                     for h in range(SB_HEADS)], axis=0)
                a, run = _sb_weights(z, tri, run_ref[...], None)
                for h in range(SB_HEADS):
                    vt = vbuf[slot, h * HEAD_DIM:(h + 1) * HEAD_DIM, cols].astype(BF16)
                    acc_ref[h] = acc_ref[h] + _dot_nt(a[h * t:(h + 1) * t, :], vt)
                run_ref[...] = run
                alive_ref[0] = (jnp.max(run) > SB_DEAD).astype(I32)

        return j + 1, alive_ref[0]

    j_end, _ = lax.while_loop(cond, body, (jnp.int32(0), alive_ref[0]))

    @pl.when(j_end < nkb)
    def _():
        for cp in block_copies(j_end, j_end % 2):
            cp.wait()

    for h in range(SB_HEADS):
        o_ref[:, h * HEAD_DIM:(h + 1) * HEAD_DIM] = acc_ref[h].astype(o_ref.dtype)


def _sb_decode(oa, q, ks, vs, cache_kt, cache_vt, layer, row0, batch, t):
    past = cache_kt.shape[3]
    tk = min(DEC_TK, past)
    bk = min(SB_BK, tk)
    nkb = past // tk
    rb = row0 // t
    new = pl.BlockSpec((t, SB_DIM), lambda b: (b, 0))
    anywhere = pl.BlockSpec(memory_space=pl.ANY)
    return pl.pallas_call(
        functools.partial(_sb_decode_kernel, t=t, tk=tk, bk=bk, layer=layer, nkb=nkb),
        grid=(batch,),
        in_specs=[anywhere, new, new, new, anywhere, anywhere],
        out_specs=pl.BlockSpec((t, SB_DIM), lambda b: (rb + b, 0)),
        out_shape=jax.ShapeDtypeStruct(oa.shape, oa.dtype),
        input_output_aliases={0: 0},
        scratch_shapes=[pltpu.VMEM((2, SB_DIM, tk), F32),
                        pltpu.VMEM((2, SB_DIM, tk), F32),
                        pltpu.SemaphoreType.DMA((2, 2)),
                        pltpu.VMEM((SB_HEADS, t, HEAD_DIM), F32),
                        pltpu.VMEM((SB_HEADS * t, 1), F32),
                        pltpu.SMEM((1,), I32)],
        compiler_params=pltpu.CompilerParams(dimension_semantics=("arbitrary",),
                                             vmem_limit_bytes=VMEM_LIMIT),
        name="sb_decode",
    )(oa, q, ks, vs, cache_kt, cache_vt)


def _conv_gla_kernel(*refs, tc, chunk):
    (_, bg_ref, cg_ref, u_ref, qc_ref, kc_ref, vc_ref, gc_ref, la_ref, cw_ref, ng_ref, cprev_ref, sprev_ref,
     o_ref, cnew_ref, snew_ref, ctail_ref, st_ref) = refs
    ti = pl.program_id(1)
    nt = pl.num_programs(1)
    n_pair = GLA_DIM // LANES
    n_hh = LANES // GLA_DK

    @pl.when(ti == 0)
    def _():
        ctail_ref[...] = cprev_ref[...]
        for p in range(n_pair):
            st_ref[p] = jnp.zeros((LANES, LANES), F32)
            for hh in range(n_hh):
                st_ref[p, hh * GLA_DK:(hh + 1) * GLA_DK, hh * GLA_DK:(hh + 1) * GLA_DK] = sprev_ref[p * n_hh + hh].T

    z = cg_ref[...] * u_ref[...]
    tail = ctail_ref[...]
    row = lax.broadcasted_iota(I32, z.shape, 0)
    z1 = jnp.where(row < 1, tail[1:2, :], pltpu.roll(z, 1, 0))
    z2 = jnp.where(row < 2, jnp.where(row < 1, tail[0:1, :], tail[1:2, :]), pltpu.roll(z, 2, 0))
    cw = cw_ref[...]
    y = z2 * cw[0:1, :] + z1 * cw[1:2, :] + z * cw[2:3, :]
    o_ref[:, 0:CONV_DIM] = (bg_ref[...] * y).astype(o_ref.dtype)
    ctail_ref[...] = z[tc - 2:tc, :]

    r = lax.broadcasted_iota(I32, (tc, tc), 0)
    c = lax.broadcasted_iota(I32, (tc, tc), 1)
    same_chunk = (r // chunk) == (c // chunk)
    chunk_sum = jnp.where(same_chunk, 1.0, 0.0).astype(BF16)
    chunk_cumsum = jnp.where(same_chunk & (c <= r), 1.0, 0.0).astype(BF16)
    lane = lax.broadcasted_iota(I32, (1, LANES), 1)
    lr = lax.broadcasted_iota(I32, (LANES, LANES), 0) // GLA_DK
    lc = lax.broadcasted_iota(I32, (LANES, LANES), 1) // GLA_DK
    same_head = lr == lc
    head_mean = jnp.where(same_head, 1.0 / GLA_DK, 0.0).astype(BF16)
    r2 = lax.broadcasted_iota(I32, (tc, n_hh * tc), 0)
    c2 = lax.broadcasted_iota(I32, (tc, n_hh * tc), 1)
    c2 = c2 - jnp.where(c2 >= tc, tc, 0)
    causal2 = ((r2 // chunk) == (c2 // chunk)) & (c2 <= r2)
    for p in range(n_pair):
        cols = slice(p * LANES, (p + 1) * LANES)
        la = la_ref[:, cols]
        b = _dot_exact_lhs(chunk_cumsum, la)
        tot = _dot_exact_lhs(chunk_sum, la)
        k = kc_ref[:, cols]
        qe = (qc_ref[:, cols] * (GLA_DK ** -0.5) * jnp.exp(b)).astype(BF16)
        ke = (k * jnp.exp(-b)).astype(BF16)
        kd = (k * jnp.exp(tot - b)).astype(BF16)
        decay = jnp.exp(tot)
        vb = vc_ref[:, cols].astype(BF16)
        zeros = jnp.zeros_like(ke)
        ke_st = jnp.concatenate([jnp.where((lane // GLA_DK) == hh, ke, zeros)
                                 for hh in range(n_hh)], axis=0)
        v_st = jnp.concatenate([jnp.where((lane // GLA_DK) == hh, vb, zeros)
                                for hh in range(n_hh)], axis=0)
        a = jnp.where(causal2, _dot_nt(qe, ke_st), 0.0)
        o = _dot(a.astype(BF16), v_st)
        st = st_ref[p]
        from_state = []
        for ci in range(tc // chunk):
            rows = slice(ci * chunk, (ci + 1) * chunk)
            from_state.append(_dot_nt(qe[rows], st.astype(BF16)))
            upd = _dot(vb[rows].T, kd[rows])
            st = jnp.where(same_head, st * decay[ci * chunk:ci * chunk + 1, :] + upd, 0.0)
        st_ref[p] = st
        o = o + jnp.concatenate(from_state, axis=0)
        ms = _dot_exact_rhs(o * o, head_mean)
        o = o * lax.rsqrt(ms + NORM_EPS) * ng_ref[:, cols]
        g = gc_ref[:, cols]
        o = o * (g * (1.0 / (1.0 + jnp.exp(-g))))
        o_ref[:, CONV_DIM + p * LANES:CONV_DIM + (p + 1) * LANES] = o.astype(o_ref.dtype)

    @pl.when(ti == nt - 1)
    def _():
        cnew_ref[...] = ctail_ref[...]
        for p in range(n_pair):
            for hh in range(n_hh):
                blk = st_ref[p, hh * GLA_DK:(hh + 1) * GLA_DK, :]
                snew_ref[p * n_hh + hh] = blk.T[hh * GLA_DK:(hh + 1) * GLA_DK, :]


def _conv_gla(obc_prev, n, rest, la, conv_w, norm_g, conv_prev, gla_prev, row0, batch, t):
    tc = min(GLA_TC, t)
    chunk = min(GLA_CHUNK, t)
    nt = t // tc
    rb = row0 // tc
    col = lambda j: pl.BlockSpec((tc, CONV_DIM), lambda b, i: (b * nt + i, j))
    const2 = lambda a: pl.BlockSpec(a.shape, lambda b, i: (0, 0))
    ng = norm_g.reshape(1, GLA_DIM)
    args = (rest,) * 7 + (la, conv_w, ng, conv_prev, gla_prev)
    return pl.pallas_call(
        functools.partial(_conv_gla_kernel, tc=tc, chunk=chunk),
        grid=(batch, nt),
        in_specs=[
            pl.BlockSpec(memory_space=pl.ANY), col(0), col(1), col(2), col(3), col(4), col(5), col(6),
            pl.BlockSpec((tc, GLA_DIM), lambda b, i: (b * nt + i, 0)),
            const2(conv_w), const2(ng),
            pl.BlockSpec((None, CONV_W - 1, CONV_DIM), lambda b, i: (b, 0, 0)),
            pl.BlockSpec((None, GLA_HEADS, GLA_DK, GLA_DK), lambda b, i: (b, 0, 0, 0))],
        out_specs=[pl.BlockSpec((tc, CONV_DIM + GLA_DIM), lambda b, i: (rb + b * nt + i, 0)),
                   pl.BlockSpec((None, CONV_W - 1, CONV_DIM), lambda b, i: (b, 0, 0)),
                   pl.BlockSpec((None, GLA_HEADS, GLA_DK, GLA_DK), lambda b, i: (b, 0, 0, 0))],
        out_shape=[jax.ShapeDtypeStruct((n, CONV_DIM + GLA_DIM), BF16),
                   jax.ShapeDtypeStruct((batch, CONV_W - 1, CONV_DIM), F32),
                   jax.ShapeDtypeStruct((batch, GLA_HEADS, GLA_DK, GLA_DK), F32)],
        input_output_aliases={0: 0},
        scratch_shapes=[pltpu.VMEM((CONV_W - 1, CONV_DIM), F32),
                        pltpu.VMEM((GLA_DIM // LANES, LANES, LANES), F32)],
        compiler_params=pltpu.CompilerParams(dimension_semantics=("arbitrary", "arbitrary"),
                                             vmem_limit_bytes=VMEM_LIMIT),
        name="conv_gla",
    )(obc_prev, *args)


def _out_router_kernel(oa_ref, obc_ref, x_ref, wa_ref, wb_ref, g_ref, b_ref, wr_ref, br_ref,
                       h_ref, hp_ref, idx_ref, gate_ref, rank_ref, cnt_ref, carry_ref, *, alpha):
    i = pl.program_id(0)
    tm = x_ref.shape[0]

    @pl.when(i == 0)
    def _():
        carry_ref[...] = jnp.zeros_like(carry_ref)

    m = _dot(oa_ref[...], wa_ref[...]) + _dot(obc_ref[...], wb_ref[...])
    h = _layer_norm(alpha * x_ref[...] + m, g_ref[...], b_ref[...])
    h_ref[...] = h
    hp_ref[...] = lax.bitcast_convert_type(_pack_bf16_pairs(h), F32)

    h_hi, h_lo = _split_bf16(h)
    w_hi, w_lo = _split_bf16(wr_ref[...])
    logit = _dot_nt(w_hi, h_hi) + _dot_nt(w_hi, h_lo) + _dot_nt(w_lo, h_hi) + br_ref[...]
    eid = lax.broadcasted_iota(I32, (N_EXPERTS, tm), 0)
    r = lax.broadcasted_iota(I32, (tm, tm), 0)
    c = lax.broadcasted_iota(I32, (tm, tm), 1)
    before = jnp.where(r < c, 1.0, 0.0).astype(BF16)
    base = carry_ref[...]
    vals, idxs, ranks = [], [], []
    for _ in range(TOP_K):
        mx = jnp.max(logit, axis=0, keepdims=True)
        sel = jnp.min(jnp.where(logit == mx, eid, N_EXPERTS), axis=0, keepdims=True)
        hit = eid == sel
        logit = jnp.where(hit, -jnp.inf, logit)
        onehot = jnp.where(hit, 1.0, 0.0)
        prior = _dot(onehot.astype(BF16), before) + base
        ranks.append(jnp.sum(onehot * prior, axis=0, keepdims=True))
        base = base + jnp.sum(onehot, axis=1, keepdims=True)
        vals.append(mx)
        idxs.append(sel)
    carry_ref[...] = base
    e = [jnp.exp(v - vals[0]) for v in vals]
    inv = 1.0 / (e[0] + e[1] + e[2] + e[3])
    idx_ref[...] = jnp.concatenate(idxs, axis=0)
    gate_ref[...] = jnp.concatenate([ek * inv for ek in e], axis=0)
    rank_ref[...] = jnp.concatenate(ranks, axis=0).astype(I32)
    cnt_ref[...] = jnp.broadcast_to(base, cnt_ref.shape).astype(I32)


def _out_router(oa, obc, x, wa, wb, ln_g, ln_b, w_router, b_router, alpha):
    n, d = x.shape
    tm = _wide_tile(n, WIDE_TILE_MAX)
    wr = w_router.T
    br = b_router.reshape(N_EXPERTS, 1)
    g, b = ln_g.reshape(1, d), ln_b.reshape(1, d)
    full = lambda a: pl.BlockSpec(a.shape, lambda i: (0, 0))
    row = lambda w: pl.BlockSpec((tm, w), lambda i: (i, 0))
    colb = pl.BlockSpec((TOP_K, tm), lambda i: (0, i))
    return pl.pallas_call(
        functools.partial(_out_router_kernel, alpha=alpha),
        grid=(n // tm,),
        in_specs=[row(SB_DIM), row(CONV_DIM + GLA_DIM), row(d), full(wa), full(wb), full(g), full(b),
                  full(wr), full(br)],
        out_specs=[row(d), row(d // 2), colb, colb, colb,
                   pl.BlockSpec((N_EXPERTS, LANES), lambda i: (0, 0))],
        out_shape=[jax.ShapeDtypeStruct((n, d), F32),
                   jax.ShapeDtypeStruct((n, d // 2), F32),
                   jax.ShapeDtypeStruct((TOP_K, n), I32),
                   jax.ShapeDtypeStruct((TOP_K, n), F32),
                   jax.ShapeDtypeStruct((TOP_K, n), I32),
                   jax.ShapeDtypeStruct((N_EXPERTS, LANES), I32)],
        scratch_shapes=[pltpu.VMEM((N_EXPERTS, 1), F32)],
        compiler_params=pltpu.CompilerParams(dimension_semantics=("arbitrary",),
                                             vmem_limit_bytes=VMEM_LIMIT),
        name="out_router",
    )(oa, obc, x, wa, wb, g, b, wr, br)


def _sc_gather(table, idx):
    m = idx.shape[0]
    d = table.shape[1]
    per_worker = m // (SC_CORES * SC_SUBCORES)
    n_chunks = per_worker // SC_CHUNK
    mesh = plsc.VectorSubcoreMesh(core_axis_name="c", subcore_axis_name="s")

    @functools.partial(
        pl.kernel, mesh=mesh,
        out_type=jax.ShapeDtypeStruct((m, d), table.dtype),
        scratch_types=[pltpu.VMEM((SC_CHUNK,), I32),
                       pltpu.VMEM((SC_CHUNK, d), table.dtype),
                       pltpu.SemaphoreType.DMA],
        name="sc_gather",
    )
    def gather(table_hbm, idx_hbm, out_hbm, idx_v, rows_v, sem):
        wid = lax.axis_index("s") * SC_CORES + lax.axis_index("c")
        base = wid * per_worker

        @pl.loop(0, n_chunks)
        def _(c):
            off = pl.multiple_of(base + c * SC_CHUNK, SC_CHUNK)
            pltpu.sync_copy(idx_hbm.at[pl.ds(off, SC_CHUNK)], idx_v)
            pltpu.async_copy(table_hbm.at[idx_v], rows_v, sem).wait()
            pltpu.sync_copy(rows_v, out_hbm.at[pl.ds(off, SC_CHUNK)])

    return gather(table, idx)


def _sc_chunk(per_worker):
    return max(c for c in range(8, SC_SCATTER_MAX + 1, 8) if per_worker % c == 0)


def _sc_scatter_rows(h, pos):
    n, d = h.shape
    per_worker = n // (SC_CORES * SC_SUBCORES)
    ch = _sc_chunk(per_worker)
    n_chunks = per_worker // ch
    mesh = plsc.VectorSubcoreMesh(core_axis_name="c", subcore_axis_name="s")

    @functools.partial(
        pl.kernel, mesh=mesh,
        out_type=jax.ShapeDtypeStruct((TOP_K * n, d), h.dtype),
        scratch_types=[pltpu.VMEM((ch,), I32)] * TOP_K + [pltpu.VMEM((ch, d), h.dtype)]
                      + [pltpu.SemaphoreType.DMA] * TOP_K,
        name="sc_scatter",
    )
    def scatter(h_hbm, pos_hbm, out_hbm, *scratch):
        idx_v, rows_v, sems = scratch[:TOP_K], scratch[TOP_K], scratch[TOP_K + 1:]
        wid = lax.axis_index("s") * SC_CORES + lax.axis_index("c")
        base = wid * per_worker

        @pl.loop(0, n_chunks)
        def _(c):
            t0 = pl.multiple_of(base + c * ch, 8)
            pltpu.sync_copy(h_hbm.at[pl.ds(t0, ch)], rows_v)
            for k in range(TOP_K):
                pltpu.sync_copy(pos_hbm.at[pl.ds(k * n + t0, ch)], idx_v[k])
            copies = [pltpu.async_copy(rows_v, out_hbm.at[idx_v[k]], sems[k]) for k in range(TOP_K)]
            for cp in copies:
                cp.wait()

    return scatter(h, pos)


def _expert_kernel(vt_ref, ve_ref, lo_ref, hi_ref, x_ref, wu_ref, bu_ref, wd_ref, bd_ref, y_ref,
                   wu16_ref, wd16_ref):
    i = pl.program_id(0)
    ip = jnp.maximum(i - 1, 0)
    e = ve_ref[i]
    prev = ve_ref[ip]
    tile = vt_ref[i]
    first_visit = (i == 0) | (tile != vt_ref[ip])
    lo = lo_ref[i]
    hi = hi_ref[i]
    tm = x_ref.shape[0]
    dff = wd_ref.shape[0]

    @pl.when((i == 0) | (e != prev))
    def _():
        step = 128

        def cast(s, _):
            rows = pl.ds(pl.multiple_of(s * step, step), step)
            wu16_ref[rows, :] = wu_ref[rows, :].astype(BF16)
            return 0

        lax.fori_loop(0, wu_ref.shape[0] // step, cast, 0)

        def cast_d(s, _):
            rows = pl.ds(pl.multiple_of(s * step, step), step)
            wd16_ref[rows, :] = wd_ref[rows, :].astype(BF16)
            return 0

        lax.fori_loop(0, dff // step, cast_d, 0)

    @pl.when(hi > lo)
    def _():
        x = _unpack_bf16_pairs(lax.bitcast_convert_type(x_ref[...], U32))
        glu = jnp.minimum(_dot(x, wu16_ref[:, :dff]) + bu_ref[:, :dff], SWIGLU_LIMIT)
        lin = jnp.clip(_dot(x, wu16_ref[:, dff:]) + bu_ref[:, dff:], -SWIGLU_LIMIT, SWIGLU_LIMIT)
        act = glu * (1.0 / (1.0 + jnp.exp(-SWIGLU_ALPHA * glu))) * (lin + 1.0)
        y = _dot(act.astype(BF16), wd16_ref[...]) + bd_ref[...]
        y = lax.bitcast_convert_type(_pack_bf16_pairs(y), F32)
        row = tile * tm + lax.broadcasted_iota(I32, (tm, 1), 0)
        mine = (row >= lo) & (row < hi)

        @pl.when(first_visit)
        def _():
            y_ref[...] = jnp.where(mine, y, 0.0)

        @pl.when(jnp.logical_not(first_visit))
        def _():
            y_ref[...] = jnp.where(mine, y, y_ref[...])


def _experts(x_sorted, visits, w_up, b_up, w_down, b_down, layer):
    ns = x_sorted.shape[0]
    tm = EXPERT_TILE
    d, dff = w_down.shape[3], w_down.shape[2]
    bu = b_up.reshape(b_up.shape[0], N_EXPERTS, 1, 2 * dff)
    bd = b_down.reshape(b_down.shape[0], N_EXPERTS, 1, d)
    wmap = lambda i, vt, ve, lo, hi: (layer, ve[i], 0, 0)
    xmap = lambda i, vt, ve, lo, hi: (vt[i], 0)
    grid_spec = pltpu.PrefetchScalarGridSpec(
        num_scalar_prefetch=4,
        grid=(visits[0].shape[0],),
        in_specs=[pl.BlockSpec((tm, x_sorted.shape[1]), xmap),
                  pl.BlockSpec((None, None, d, 2 * dff), wmap),
                  pl.BlockSpec((None, None, 1, 2 * dff), wmap),
                  pl.BlockSpec((None, None, dff, d), wmap),
                  pl.BlockSpec((None, None, 1, d), wmap)],
        out_specs=pl.BlockSpec((tm, d // 2), xmap),
        scratch_shapes=[pltpu.VMEM((d, 2 * dff), BF16), pltpu.VMEM((dff, d), BF16)],
    )
    return pl.pallas_call(
        _expert_kernel,
        grid_spec=grid_spec,
        out_shape=jax.ShapeDtypeStruct((ns, d // 2), F32),
        compiler_params=pltpu.CompilerParams(dimension_semantics=("arbitrary",),
                                             vmem_limit_bytes=VMEM_LIMIT),
        name="experts",
    )(*visits, x_sorted, w_up, bu, w_down, bd)


def _expert_visits(cnt, n_rows):
    tm = EXPERT_TILE
    n_steps = n_rows // tm + N_EXPERTS
    ends = jnp.cumsum(cnt)
    starts = ends - cnt
    first_tile = starts // tm
    n_vis = jnp.where(cnt > 0, (ends - 1) // tm - first_tile + 1, 0)
    vis_end = jnp.cumsum(n_vis)
    vis_start = vis_end - n_vis
    v = jnp.arange(n_steps, dtype=I32)
    vc = jnp.minimum(v, vis_end[-1] - 1)
    onehot = ((vis_start[None, :] <= vc[:, None]) & (vc[:, None] < vis_end[None, :])).astype(I32)
    pick = lambda a: jnp.sum(onehot * a[None, :], axis=1).astype(I32)
    expert = pick(jnp.arange(N_EXPERTS, dtype=I32))
    tile = pick(first_tile) + vc - pick(vis_start)
    real = v < vis_end[-1]
    lo = jnp.where(real, pick(starts), 0).astype(I32)
    hi = jnp.where(real, pick(ends), 0).astype(I32)
    return tile.astype(I32), expert, lo, hi


def _combine_kernel(y0_ref, y1_ref, y2_ref, y3_ref, gate_ref, h_ref, g_ref, b_ref, o_ref, *, alpha):
    gate = gate_ref[...]
    lo = jnp.zeros(y0_ref.shape, F32)
    hi = jnp.zeros(y0_ref.shape, F32)
    for k, y_ref in enumerate((y0_ref, y1_ref, y2_ref, y3_ref)):
        w = lax.bitcast_convert_type(y_ref[...], U32)
        lo = lo + gate[:, k:k + 1] * lax.bitcast_convert_type(w << 16, F32)
        hi = hi + gate[:, k:k + 1] * lax.bitcast_convert_type(w & jnp.uint32(0xFFFF0000), F32)
    acc = jnp.concatenate([lo, hi], axis=1)
    o_ref[...] = _layer_norm(alpha * h_ref[...] + acc, g_ref[...], b_ref[...])


def _combine(y_tok, gates, h, ln_g, ln_b, alpha):
    n, d = h.shape
    tm = _wide_tile(n, COMBINE_TILE_MAX)
    nt = n // tm
    g, b = ln_g.reshape(1, d), ln_b.reshape(1, d)
    full = lambda a: pl.BlockSpec(a.shape, lambda i: (0, 0))
    ysp = lambda k: pl.BlockSpec((tm, d // 2), lambda i: (k * nt + i, 0))
    return pl.pallas_call(
        functools.partial(_combine_kernel, alpha=alpha),
        grid=(nt,),
        in_specs=[ysp(0), ysp(1), ysp(2), ysp(3),
                  pl.BlockSpec((tm, TOP_K), lambda i: (i, 0)),
                  pl.BlockSpec((tm, d), lambda i: (i, 0)), full(g), full(b)],
        out_specs=pl.BlockSpec((tm, d), lambda i: (i, 0)),
        out_shape=jax.ShapeDtypeStruct((n, d), F32),
        compiler_params=pltpu.CompilerParams(dimension_semantics=("arbitrary",),
                                             vmem_limit_bytes=VMEM_LIMIT),
        name="combine",
    )(y_tok, y_tok, y_tok, y_tok, gates, h, g, b)


def _round_up(a, m):
    return (a + m - 1) // m * m


def _wide_tile(n, cap):
    return max(t for t in range(TOKEN_TILE, cap + 1, TOKEN_TILE) if n % t == 0)


def _moe(h, hp, idx, gates, rank, counts, w_up, b_up, w_down, b_down, ln_g, ln_b, alpha, layer):
    n, d = h.shape
    ns = TOP_K * n
    cnt = counts[:, 0]
    starts = jnp.cumsum(cnt) - cnt
    experts = jnp.arange(N_EXPERTS, dtype=I32)
    offs = jnp.sum(jnp.where(idx[:, :, None] == experts, starts, 0), axis=-1)
    pos = (offs + rank).reshape(-1).astype(I32)
    x_sorted = _sc_scatter_rows(hp, pos)
    y_sorted = _experts(x_sorted, _expert_visits(cnt, ns), w_up, b_up, w_down, b_down, layer)
    m2 = _round_up(ns, SC_ROW_ALIGN)
    fill = jnp.arange(ns, m2, dtype=I32) - ns
    y_tok = _sc_gather(y_sorted, jnp.concatenate([pos, fill]))
    return _combine(y_tok, gates.T, h, ln_g, ln_b, alpha)


def kernel(x_prompt, x_sample, cache_k, cache_v, state_conv, state_gla, w_in, conv_w, w_gate, b_gate,
           gla_norm_g, w_out, ln1_g, ln1_b, w_router, b_router, w_up, b_up, w_down, b_down, ln2_g, ln2_b):
    depth = w_in.shape[0]
    bp, seq, d = x_prompt.shape
    bs, ts, _ = x_sample.shape
    past = cache_k.shape[2]
    n_p = bp * seq
    n = n_p + bs * ts
    alpha = float((2 * depth) ** 0.25)
    x = jnp.concatenate([x_prompt.reshape(n_p, d), x_sample.reshape(bs * ts, d)], axis=0)
    ckt = cache_k.transpose(0, 1, 3, 4, 2).reshape(depth, bs, SB_DIM, past)
    cvt = cache_v.transpose(0, 1, 3, 4, 2).reshape(depth, bs, SB_DIM, past)
    zero_conv = jnp.zeros((bp, CONV_W - 1, CONV_DIM), F32)
    zero_gla = jnp.zeros((bp, GLA_HEADS, GLA_DK, GLA_DK), F32)
    wb = w_in.astype(BF16)
    o_r = 3 * SB_DIM
    n_r = 3 * CONV_DIM + 4 * GLA_DIM
    wq, wk, wv = wb[:, :, :SB_DIM], wb[:, :, SB_DIM:2 * SB_DIM], wb[:, :, 2 * SB_DIM:o_r]
    wkt, wvt = wk.transpose(0, 2, 1), wv.transpose(0, 2, 1)
    wr, wal = wb[:, :, o_r:o_r + n_r], wb[:, :, o_r + n_r:]
    wg = w_gate.astype(BF16)
    wo = w_out.astype(BF16)
    kt = jnp.zeros((depth, bp, SB_DIM, seq), F32)
    vt = jnp.zeros((depth, bp, SB_DIM, seq), F32)
    tm_p = min(PROMPT_TILE, seq)
    outs = [[] for _ in range(6)]
    for l in range(depth):
        shared = (wr[l], wal[l], wg[l], b_gate[l].reshape(1, -1))
        q_p, kt, vt, rest_p, la_p = _in_proj(x, 0, n_p, tm_p, (wq[l], wkt[l], wvt[l]) + shared,
                                             kv_prompt=(l, seq, kt, vt))
        q_s, ks, vs, rest_s, la_s = _in_proj(x, n_p, n - n_p, TOKEN_TILE, (wq[l], wk[l], wv[l]) + shared)
        oa = _sb_prompt(jnp.zeros((n, SB_DIM), BF16), q_p, kt, vt, l, bp, seq)
        oa = _sb_decode(oa, q_s, ks, vs, ckt, cvt, l, n_p, bs, ts)
        obc, conv_p, gla_p = _conv_gla(jnp.zeros((n, CONV_DIM + GLA_DIM), BF16), n, rest_p, la_p, conv_w[l],
                                       gla_norm_g[l], zero_conv, zero_gla, 0, bp, seq)
        obc, conv_s, gla_s = _conv_gla(obc, n, rest_s, la_s, conv_w[l], gla_norm_g[l], state_conv[l],
                                       state_gla[l], n_p, bs, ts)
        h, hp, idx, gates, rank, counts = _out_router(oa, obc, x, wo[l, :SB_DIM], wo[l, SB_DIM:], ln1_g[l],
                                                      ln1_b[l], w_router[l], b_router[l], alpha)
        x = _moe(h, hp, idx, gates, rank, counts, w_up, b_up, w_down, b_down, ln2_g[l], ln2_b[l],
                 alpha, l)
        outs[0].append(conv_p)
        outs[1].append(gla_p)
        outs[2].append(ks.reshape(bs, ts, SB_HEADS, HEAD_DIM))
        outs[3].append(vs.reshape(bs, ts, SB_HEADS, HEAD_DIM))
        outs[4].append(conv_s)
        outs[5].append(gla_s)
    k_prompt = kt.reshape(depth, bp, SB_HEADS, HEAD_DIM, seq).transpose(0, 1, 4, 2, 3)
    v_prompt = vt.reshape(depth, bp, SB_HEADS, HEAD_DIM, seq).transpose(0, 1, 4, 2, 3)
    st = [jnp.stack(o) for o in outs]
    return (x[:n_p].reshape(bp, seq, d), x[n_p:].reshape(bs, ts, d), k_prompt, v_prompt,
            st[0], st[1], st[2], st[3], st[4], st[5])
```

```python
import functools

import jax
import jax.numpy as jnp
from jax import lax
from jax.experimental import pallas as pl
from jax.experimental.pallas import tpu as pltpu
from jax.experimental.pallas import tpu_sc as plsc

F32 = jnp.float32
BF16 = jnp.bfloat16
I32 = jnp.int32
U32 = jnp.uint32

HEAD_DIM = 64
SB_HEADS = 8
SB_DIM = SB_HEADS * HEAD_DIM
CONV_DIM = 256
CONV_W = 3
GLA_HEADS = 4
GLA_DK = 64
GLA_DIM = GLA_HEADS * GLA_DK
GLA_RANK = 16
GLA_TAU = 16.0
GLA_CHUNK = 64
N_EXPERTS = 32
TOP_K = 4
SWIGLU_LIMIT = 7.0
SWIGLU_ALPHA = 1.702
NORM_EPS = 1e-5

LANES = 128
SC_CORES = 2
SC_SUBCORES = 16
SC_CHUNK = 128
SC_ROW_ALIGN = SC_CORES * SC_SUBCORES * SC_CHUNK
SC_SCATTER_MAX = 104
TOKEN_TILE = 256
WIDE_TILE_MAX = 1280
COMBINE_TILE_MAX = 640
PROMPT_TILE = 512
EXPERT_TILE = 512
EXPERT_SUB = 256
SB_TQ = 256
SB_BK = 256
SB_GROUP = 256
DEC_TK = 512
GLA_TC = 256
VMEM_LIMIT = 48 * 1024 * 1024
SB_DEAD = -100.0


def _dot(a, b):
    return jnp.dot(a, b, preferred_element_type=F32)


def _dot_nt(a, b):
    return lax.dot_general(a, b, (((1,), (1,)), ((), ())), preferred_element_type=F32)


def _split_bf16(x):
    hi = x.astype(BF16)
    lo = (x - hi.astype(F32)).astype(BF16)
    return hi, lo


def _dot_exact_rhs(x, m):
    hi, lo = _split_bf16(x)
    return _dot(hi, m) + _dot(lo, m)


def _dot_exact_lhs(m, x):
    hi, lo = _split_bf16(x)
    return _dot(m, hi) + _dot(m, lo)


def _pack_bf16_pairs(x):
    c = x.shape[1] // 2
    bits = lax.bitcast_convert_type(x.astype(BF16).astype(F32), U32)
    return (bits[:, :c] >> 16) | (bits[:, c:] & jnp.uint32(0xFFFF0000))


def _unpack_bf16_pairs(w):
    lo = lax.bitcast_convert_type(w << 16, F32)
    hi = lax.bitcast_convert_type(w & jnp.uint32(0xFFFF0000), F32)
    return jnp.concatenate([lo, hi], axis=1).astype(BF16)


def _softplus(z):
    return jnp.maximum(z, 0.0) + jnp.log(1.0 + jnp.exp(-jnp.abs(z)))


def _layer_norm(y, g, b):
    mu = jnp.mean(y, axis=-1, keepdims=True)
    yc = y - mu
    var = jnp.mean(yc * yc, axis=-1, keepdims=True)
    return yc * lax.rsqrt(var + NORM_EPS) * g + b


def _strict_upper(n, copies=1):
    r = lax.broadcasted_iota(I32, (copies * n, n), 0)
    c = lax.broadcasted_iota(I32, (copies * n, n), 1)
    for k in range(1, copies):
        r = r - jnp.where(r >= n, n, 0)
    return jnp.where(r > c, 1.0, 0.0).astype(BF16)


def _in_proj_kernel(*refs, transposed_kv):
    if transposed_kv:
        (_, _, x_ref, wq_ref, wk_ref, wv_ref, wr_ref, wal_ref, wg_ref, bg_ref,
         q_ref, k_ref, v_ref, r_ref, la_ref) = refs
    else:
        (x_ref, wq_ref, wk_ref, wv_ref, wr_ref, wal_ref, wg_ref, bg_ref,
         q_ref, k_ref, v_ref, r_ref, la_ref) = refs
    xb = x_ref[...].astype(BF16)
    for c in range(0, SB_DIM, 256):
        q_ref[:, c:c + 256] = (_dot(xb, wq_ref[:, c:c + 256]) * (HEAD_DIM ** -0.5)).astype(BF16)
    for c in range(0, r_ref.shape[1], 256):
        r_ref[:, c:c + 256] = _dot(xb, wr_ref[:, c:c + 256])
    al = _dot(xb, wal_ref[...])
    g = _dot(al.astype(BF16), wg_ref[...]) + bg_ref[...]
    la_ref[...] = -_softplus(-g) * (1.0 / GLA_TAU)
    for c in range(0, SB_DIM, 256):
        if transposed_kv:
            k_ref[c:c + 256, :] = _dot_nt(wk_ref[c:c + 256, :], xb)
            v_ref[c:c + 256, :] = _dot_nt(wv_ref[c:c + 256, :], xb)
        else:
            k_ref[:, c:c + 256] = _dot(xb, wk_ref[:, c:c + 256])
            v_ref[:, c:c + 256] = _dot(xb, wv_ref[:, c:c + 256])


def _in_proj(x, row0, rows, tm, weights, kv_prompt=None):
    d = x.shape[1]
    wq, wk, wv, wr, wal, wg, bg = weights
    n_r = wr.shape[1]
    rb = row0 // tm
    full = lambda a: pl.BlockSpec(a.shape, lambda i: (0,) * a.ndim)
    row = lambda w: pl.BlockSpec((tm, w), lambda i: (i, 0))
    x_spec = pl.BlockSpec((tm, d), lambda i: (rb + i, 0))
    sds = lambda w, dt: jax.ShapeDtypeStruct((rows, w), dt)
    if kv_prompt is not None:
        layer, seq, kt, vt = kv_prompt
        tps = seq // tm
        kv_spec = pl.BlockSpec((None, None, SB_DIM, tm), lambda i: (layer, i // tps, 0, i % tps))
        kv_shape = jax.ShapeDtypeStruct(kt.shape, kt.dtype)
        anywhere = pl.BlockSpec(memory_space=pl.ANY)
        extra_in, extra_args, aliases = [anywhere, anywhere], (kt, vt), {0: 1, 1: 2}
    else:
        kv_spec, kv_shape = row(SB_DIM), sds(SB_DIM, F32)
        extra_in, extra_args, aliases = [], (), {}
    return pl.pallas_call(
        functools.partial(_in_proj_kernel, transposed_kv=kv_prompt is not None),
        grid=(rows // tm,),
        in_specs=extra_in + [x_spec] + [full(w) for w in weights],
        out_specs=[row(SB_DIM), kv_spec, kv_spec, row(n_r), row(GLA_DIM)],
        out_shape=[sds(SB_DIM, BF16), kv_shape, kv_shape, sds(n_r, F32), sds(GLA_DIM, F32)],
        input_output_aliases=aliases,
        compiler_params=pltpu.CompilerParams(dimension_semantics=("arbitrary",),
                                             vmem_limit_bytes=VMEM_LIMIT),
        name="in_proj",
    )(*extra_args, x, *weights)


def _sb_weights(z, tri, run, mask):
    sp = _softplus(z)
    l1m = -sp
    lsig = z - sp
    if mask is not None:
        l1m = jnp.where(mask, l1m, 0.0)
    if tri.shape[0] == 2 * z.shape[1]:
        hi = lax.bitcast_convert_type(lax.bitcast_convert_type(l1m, U32) & jnp.uint32(0xFFFF0000), F32)
        parts = jnp.concatenate([hi.astype(BF16), (l1m - hi).astype(BF16)], axis=1)
        rest = _dot(parts, tri) + run
    else:
        rest = _dot_exact_rhs(l1m, tri) + run
    a = jnp.exp(lsig + rest)
    if mask is not None:
        a = jnp.where(mask, a, 0.0)
    return a.astype(BF16), run + jnp.sum(l1m, axis=1, keepdims=True)


def _sb_prompt_kernel(oa_ref, q_ref, kt_ref, vt_ref, o_ref, *, tq, bk):
    del oa_ref
    qi = pl.program_id(2)
    q = q_ref[...]
    n_hh = SB_GROUP // HEAD_DIM
    lane = lax.broadcasted_iota(I32, (1, SB_GROUP), 1)
    in_head = [(lane // HEAD_DIM) == h for h in range(n_hh)]
    qh = [jnp.where(m, q, jnp.zeros_like(q)) for m in in_head]
    tri = _strict_upper(bk, copies=2)
    n_full = (qi * tq) // bk
    qpos = qi * tq + lax.broadcasted_iota(I32, (tq, bk), 0)
    kpos = n_full * bk + lax.broadcasted_iota(I32, (tq, bk), 1)
    diag_mask = kpos < qpos

    def tile(jb, runs, mask):
        ks = pl.multiple_of(jb * bk, bk)
        kt = kt_ref[:, pl.ds(ks, bk)].astype(BF16)
        vt = vt_ref[:, pl.ds(ks, bk)].astype(BF16)
        out = jnp.zeros((tq, SB_GROUP), F32)
        new_runs = []
        for h in range(n_hh):
            a, run = _sb_weights(_dot(qh[h], kt), tri, runs[h], mask)
            out = jnp.where(in_head[h], _dot_nt(a, vt), out)
            new_runs.append(run)
        return out, tuple(new_runs)

    def alive_of(runs):
        m = jnp.max(runs[0])
        for r in runs[1:]:
            m = jnp.maximum(m, jnp.max(r))
        return m > SB_DEAD

    acc, runs = tile(n_full, tuple(jnp.zeros((tq, 1), F32) for _ in range(n_hh)), diag_mask)

    def cond(carry):
        j, alive, _, _ = carry
        return (j >= 0) & alive

    def body(carry):
        j, _, acc, runs = carry
        pv, runs = tile(j, runs, None)
        return j - 1, alive_of(runs), acc + pv, runs

    _, _, acc, _ = lax.while_loop(cond, body, (n_full - 1, alive_of(runs), acc, runs))
    o_ref[...] = acc.astype(o_ref.dtype)


def _sb_prompt(oa, q, kt, vt, layer, batch, seq):
    tq, bk = min(SB_TQ, seq), min(SB_BK, seq)
    nq = seq // tq
    hp = SB_DIM // SB_GROUP
    kv_spec = pl.BlockSpec((None, None, SB_GROUP, seq), lambda b, p, i: (layer, b, p, 0))
    return pl.pallas_call(
        functools.partial(_sb_prompt_kernel, tq=tq, bk=bk),
        grid=(batch, hp, nq),
        in_specs=[pl.BlockSpec(memory_space=pl.ANY),
                  pl.BlockSpec((tq, SB_GROUP), lambda b, p, i: (b * nq + i, p)), kv_spec, kv_spec],
        out_specs=pl.BlockSpec((tq, SB_GROUP), lambda b, p, i: (b * nq + i, p)),
        out_shape=jax.ShapeDtypeStruct(oa.shape, oa.dtype),
        input_output_aliases={0: 0},
        compiler_params=pltpu.CompilerParams(
            dimension_semantics=("arbitrary", "arbitrary", "arbitrary"),
            vmem_limit_bytes=VMEM_LIMIT),
        name="sb_prompt",
    )(oa, q, kt, vt)


def _sb_decode_kernel(oa_ref, q_ref, kn_ref, vn_ref, kc_hbm, vc_hbm, o_ref, kbuf, vbuf, sem, acc_ref, run_ref,
                      alive_ref, *, t, tk, bk, layer, nkb):
    del oa_ref
    b = pl.program_id(0)
    q = q_ref[...]

    def block_copies(j, slot):
        cols = pl.ds(pl.multiple_of((nkb - 1 - j) * tk, tk), tk)
        return (pltpu.make_async_copy(kc_hbm.at[layer, b, :, cols], kbuf.at[slot], sem.at[0, slot]),
                pltpu.make_async_copy(vc_hbm.at[layer, b, :, cols], vbuf.at[slot], sem.at[1, slot]))

    for cp in block_copies(0, 0):
        cp.start()

    kn = kn_ref[...].astype(BF16)
    vn = vn_ref[...].astype(BF16)
    lane = lax.broadcasted_iota(I32, (1, SB_DIM), 1)
    r = lax.broadcasted_iota(I32, (t, t), 0)
    c = lax.broadcasted_iota(I32, (t, t), 1)
    mask = c < r
    tri_new = _strict_upper(t)
    for h in range(SB_HEADS):
        qh = jnp.where((lane // HEAD_DIM) == h, q, jnp.zeros_like(q))
        a, run = _sb_weights(_dot_nt(qh, kn), tri_new, jnp.zeros((t, 1), F32), mask)
        acc_ref[h] = _dot(a, vn)[:, h * HEAD_DIM:(h + 1) * HEAD_DIM]
        run_ref[h * t:(h + 1) * t, :] = run
    alive_ref[0] = (jnp.max(run_ref[...]) > SB_DEAD).astype(I32)

    tri = _strict_upper(bk, copies=2)

    def cond(carry):
        j, alive = carry
        return (j < nkb) & (alive > 0)

    def body(carry):
        j, _ = carry
        slot = j % 2

        @pl.when(j + 1 < nkb)
        def _():
            for cp in block_copies(j + 1, 1 - slot):
                cp.start()

        for cp in block_copies(j, slot):
            cp.wait()
        for c in range(tk // bk - 1, -1, -1):
            cols = slice(c * bk, (c + 1) * bk)

            @pl.when(alive_ref[0] > 0)
            def _():
                z = jnp.concatenate(
                    [_dot(q[:, h * HEAD_DIM:(h + 1) * HEAD_DIM],
                          kbuf[slot, h * HEAD_DIM:(h + 1) * HEAD_DIM, cols].astype(BF16))
                     for h in range(SB_HEADS)], axis=0)
                a, run = _sb_weights(z, tri, run_ref[...], None)
                for h in range(SB_HEADS):
                    vt = vbuf[slot, h * HEAD_DIM:(h + 1) * HEAD_DIM, cols].astype(BF16)
                    acc_ref[h] = acc_ref[h] + _dot_nt(a[h * t:(h + 1) * t, :], vt)
                run_ref[...] = run
                alive_ref[0] = (jnp.max(run) > SB_DEAD).astype(I32)

        return j + 1, alive_ref[0]

    j_end, _ = lax.while_loop(cond, body, (jnp.int32(0), alive_ref[0]))

    @pl.when(j_end < nkb)
    def _():
        for cp in block_copies(j_end, j_end % 2):
            cp.wait()

    for h in range(SB_HEADS):
        o_ref[:, h * HEAD_DIM:(h + 1) * HEAD_DIM] = acc_ref[h].astype(o_ref.dtype)


def _sb_decode(oa, q, ks, vs, cache_kt, cache_vt, layer, row0, batch, t):
    past = cache_kt.shape[3]
    tk = min(DEC_TK, past)
    bk = min(SB_BK, tk)
    nkb = past // tk
    rb = row0 // t
    new = pl.BlockSpec((t, SB_DIM), lambda b: (b, 0))
    anywhere = pl.BlockSpec(memory_space=pl.ANY)
    return pl.pallas_call(
        functools.partial(_sb_decode_kernel, t=t, tk=tk, bk=bk, layer=layer, nkb=nkb),
        grid=(batch,),
        in_specs=[anywhere, new, new, new, anywhere, anywhere],
        out_specs=pl.BlockSpec((t, SB_DIM), lambda b: (rb + b, 0)),
        out_shape=jax.ShapeDtypeStruct(oa.shape, oa.dtype),
        input_output_aliases={0: 0},
        scratch_shapes=[pltpu.VMEM((2, SB_DIM, tk), F32),
                        pltpu.VMEM((2, SB_DIM, tk), F32),
                        pltpu.SemaphoreType.DMA((2, 2)),
                        pltpu.VMEM((SB_HEADS, t, HEAD_DIM), F32),
                        pltpu.VMEM((SB_HEADS * t, 1), F32),
                        pltpu.SMEM((1,), I32)],
        compiler_params=pltpu.CompilerParams(dimension_semantics=("arbitrary",),
                                             vmem_limit_bytes=VMEM_LIMIT),
        name="sb_decode",
    )(oa, q, ks, vs, cache_kt, cache_vt)


def _conv_gla_kernel(*refs, tc, chunk):
    (_, bg_ref, cg_ref, u_ref, qc_ref, kc_ref, vc_ref, gc_ref, la_ref, cw_ref, ng_ref, cprev_ref, sprev_ref,
     o_ref, cnew_ref, snew_ref, ctail_ref, st_ref) = refs
    ti = pl.program_id(1)
    nt = pl.num_programs(1)
    n_pair = GLA_DIM // LANES
    n_hh = LANES // GLA_DK

    @pl.when(ti == 0)
    def _():
        ctail_ref[...] = cprev_ref[...]
        for p in range(n_pair):
            st_ref[p] = jnp.zeros((LANES, LANES), F32)
            for hh in range(n_hh):
                st_ref[p, hh * GLA_DK:(hh + 1) * GLA_DK, hh * GLA_DK:(hh + 1) * GLA_DK] = sprev_ref[p * n_hh + hh].T

    z = cg_ref[...] * u_ref[...]
    tail = ctail_ref[...]
    row = lax.broadcasted_iota(I32, z.shape, 0)
    z1 = jnp.where(row < 1, tail[1:2, :], pltpu.roll(z, 1, 0))
    z2 = jnp.where(row < 2, jnp.where(row < 1, tail[0:1, :], tail[1:2, :]), pltpu.roll(z, 2, 0))
    cw = cw_ref[...]
    y = z2 * cw[0:1, :] + z1 * cw[1:2, :] + z * cw[2:3, :]
    o_ref[:, 0:CONV_DIM] = (bg_ref[...] * y).astype(o_ref.dtype)
    ctail_ref[...] = z[tc - 2:tc, :]

    r = lax.broadcasted_iota(I32, (tc, tc), 0)
    c = lax.broadcasted_iota(I32, (tc, tc), 1)
    same_chunk = (r // chunk) == (c // chunk)
    chunk_sum = jnp.where(same_chunk, 1.0, 0.0).astype(BF16)
    chunk_cumsum = jnp.where(same_chunk & (c <= r), 1.0, 0.0).astype(BF16)
    lane = lax.broadcasted_iota(I32, (1, LANES), 1)
    lr = lax.broadcasted_iota(I32, (LANES, LANES), 0) // GLA_DK
    lc = lax.broadcasted_iota(I32, (LANES, LANES), 1) // GLA_DK
    same_head = lr == lc
    head_mean = jnp.where(same_head, 1.0 / GLA_DK, 0.0).astype(BF16)
    r2 = lax.broadcasted_iota(I32, (tc, n_hh * tc), 0)
    c2 = lax.broadcasted_iota(I32, (tc, n_hh * tc), 1)
    c2 = c2 - jnp.where(c2 >= tc, tc, 0)
    causal2 = ((r2 // chunk) == (c2 // chunk)) & (c2 <= r2)
    for p in range(n_pair):
        cols = slice(p * LANES, (p + 1) * LANES)
        la = la_ref[:, cols]
        b = _dot_exact_lhs(chunk_cumsum, la)
        tot = _dot_exact_lhs(chunk_sum, la)
        k = kc_ref[:, cols]
        qe = (qc_ref[:, cols] * (GLA_DK ** -0.5) * jnp.exp(b)).astype(BF16)
        ke = (k * jnp.exp(-b)).astype(BF16)
        kd = (k * jnp.exp(tot - b)).astype(BF16)
        decay = jnp.exp(tot)
        vb = vc_ref[:, cols].astype(BF16)
        zeros = jnp.zeros_like(ke)
        ke_st = jnp.concatenate([jnp.where((lane // GLA_DK) == hh, ke, zeros)
                                 for hh in range(n_hh)], axis=0)
        v_st = jnp.concatenate([jnp.where((lane // GLA_DK) == hh, vb, zeros)
                                for hh in range(n_hh)], axis=0)
        a = jnp.where(causal2, _dot_nt(qe, ke_st), 0.0)
        o = _dot(a.astype(BF16), v_st)
        st = st_ref[p]
        from_state = []
        for ci in range(tc // chunk):
            rows = slice(ci * chunk, (ci + 1) * chunk)
            from_state.append(_dot_nt(qe[rows], st.astype(BF16)))
            upd = _dot(vb[rows].T, kd[rows])
            st = jnp.where(same_head, st * decay[ci * chunk:ci * chunk + 1, :] + upd, 0.0)
        st_ref[p] = st
        o = o + jnp.concatenate(from_state, axis=0)
        ms = _dot_exact_rhs(o * o, head_mean)
        o = o * lax.rsqrt(ms + NORM_EPS) * ng_ref[:, cols]
        g = gc_ref[:, cols]
        o = o * (g * (1.0 / (1.0 + jnp.exp(-g))))
        o_ref[:, CONV_DIM + p * LANES:CONV_DIM + (p + 1) * LANES] = o.astype(o_ref.dtype)

    @pl.when(ti == nt - 1)
    def _():
        cnew_ref[...] = ctail_ref[...]
        for p in range(n_pair):
            for hh in range(n_hh):
                blk = st_ref[p, hh * GLA_DK:(hh + 1) * GLA_DK, :]
                snew_ref[p * n_hh + hh] = blk.T[hh * GLA_DK:(hh + 1) * GLA_DK, :]


def _conv_gla(obc_prev, n, rest, la, conv_w, norm_g, conv_prev, gla_prev, row0, batch, t):
    tc = min(GLA_TC, t)
    chunk = min(GLA_CHUNK, t)
    nt = t // tc
    rb = row0 // tc
    col = lambda j: pl.BlockSpec((tc, CONV_DIM), lambda b, i: (b * nt + i, j))
    const2 = lambda a: pl.BlockSpec(a.shape, lambda b, i: (0, 0))
    ng = norm_g.reshape(1, GLA_DIM)
    args = (rest,) * 7 + (la, conv_w, ng, conv_prev, gla_prev)
    return pl.pallas_call(
        functools.partial(_conv_gla_kernel, tc=tc, chunk=chunk),
        grid=(batch, nt),
        in_specs=[
            pl.BlockSpec(memory_space=pl.ANY), col(0), col(1), col(2), col(3), col(4), col(5), col(6),
            pl.BlockSpec((tc, GLA_DIM), lambda b, i: (b * nt + i, 0)),
            const2(conv_w), const2(ng),
            pl.BlockSpec((None, CONV_W - 1, CONV_DIM), lambda b, i: (b, 0, 0)),
            pl.BlockSpec((None, GLA_HEADS, GLA_DK, GLA_DK), lambda b, i: (b, 0, 0, 0))],
        out_specs=[pl.BlockSpec((tc, CONV_DIM + GLA_DIM), lambda b, i: (rb + b * nt + i, 0)),
                   pl.BlockSpec((None, CONV_W - 1, CONV_DIM), lambda b, i: (b, 0, 0)),
                   pl.BlockSpec((None, GLA_HEADS, GLA_DK, GLA_DK), lambda b, i: (b, 0, 0, 0))],
        out_shape=[jax.ShapeDtypeStruct((n, CONV_DIM + GLA_DIM), BF16),
                   jax.ShapeDtypeStruct((batch, CONV_W - 1, CONV_DIM), F32),
                   jax.ShapeDtypeStruct((batch, GLA_HEADS, GLA_DK, GLA_DK), F32)],
        input_output_aliases={0: 0},
        scratch_shapes=[pltpu.VMEM((CONV_W - 1, CONV_DIM), F32),
                        pltpu.VMEM((GLA_DIM // LANES, LANES, LANES), F32)],
        compiler_params=pltpu.CompilerParams(dimension_semantics=("arbitrary", "arbitrary"),
                                             vmem_limit_bytes=VMEM_LIMIT),
        name="conv_gla",
    )(obc_prev, *args)


def _out_router_kernel(oa_ref, obc_ref, x_ref, wa_ref, wb_ref, g_ref, b_ref, wr_ref, br_ref,
                       h_ref, hp_ref, idx_ref, gate_ref, rank_ref, cnt_ref, carry_ref, *, alpha):
    i = pl.program_id(0)
    tm = x_ref.shape[0]

    @pl.when(i == 0)
    def _():
        carry_ref[...] = jnp.zeros_like(carry_ref)

    m = _dot(oa_ref[...], wa_ref[...]) + _dot(obc_ref[...], wb_ref[...])
    h = _layer_norm(alpha * x_ref[...] + m, g_ref[...], b_ref[...])
    h_ref[...] = h
    hp_ref[...] = lax.bitcast_convert_type(_pack_bf16_pairs(h), F32)

    h_hi, h_lo = _split_bf16(h)
    w_hi, w_lo = _split_bf16(wr_ref[...])
    logit = _dot_nt(w_hi, h_hi) + _dot_nt(w_hi, h_lo) + _dot_nt(w_lo, h_hi) + br_ref[...]
    eid = lax.broadcasted_iota(I32, (N_EXPERTS, tm), 0)
    r = lax.broadcasted_iota(I32, (tm, tm), 0)
    c = lax.broadcasted_iota(I32, (tm, tm), 1)
    before = jnp.where(r < c, 1.0, 0.0).astype(BF16)
    base = carry_ref[...]
    vals, idxs, ranks = [], [], []
    for _ in range(TOP_K):
        mx = jnp.max(logit, axis=0, keepdims=True)
        sel = jnp.min(jnp.where(logit == mx, eid, N_EXPERTS), axis=0, keepdims=True)
        hit = eid == sel
        logit = jnp.where(hit, -jnp.inf, logit)
        onehot = jnp.where(hit, 1.0, 0.0)
        prior = _dot(onehot.astype(BF16), before) + base
        ranks.append(jnp.sum(onehot * prior, axis=0, keepdims=True))
        base = base + jnp.sum(onehot, axis=1, keepdims=True)
        vals.append(mx)
        idxs.append(sel)
    carry_ref[...] = base
    e = [jnp.exp(v - vals[0]) for v in vals]
    inv = 1.0 / (e[0] + e[1] + e[2] + e[3])
    idx_ref[...] = jnp.concatenate(idxs, axis=0)
    gate_ref[...] = jnp.concatenate([ek * inv for ek in e], axis=0)
    rank_ref[...] = jnp.concatenate(ranks, axis=0).astype(I32)
    cnt_ref[...] = jnp.broadcast_to(base, cnt_ref.shape).astype(I32)


def _out_router(oa, obc, x, wa, wb, ln_g, ln_b, w_router, b_router, alpha):
    n, d = x.shape
    tm = _wide_tile(n, WIDE_TILE_MAX)
    wr = w_router.T
    br = b_router.reshape(N_EXPERTS, 1)
    g, b = ln_g.reshape(1, d), ln_b.reshape(1, d)
    full = lambda a: pl.BlockSpec(a.shape, lambda i: (0, 0))
    row = lambda w: pl.BlockSpec((tm, w), lambda i: (i, 0))
    colb = pl.BlockSpec((TOP_K, tm), lambda i: (0, i))
    return pl.pallas_call(
        functools.partial(_out_router_kernel, alpha=alpha),
        grid=(n // tm,),
        in_specs=[row(SB_DIM), row(CONV_DIM + GLA_DIM), row(d), full(wa), full(wb), full(g), full(b),
                  full(wr), full(br)],
        out_specs=[row(d), row(d // 2), colb, colb, colb,
                   pl.BlockSpec((N_EXPERTS, LANES), lambda i: (0, 0))],
        out_shape=[jax.ShapeDtypeStruct((n, d), F32),
                   jax.ShapeDtypeStruct((n, d // 2), F32),
                   jax.ShapeDtypeStruct((TOP_K, n), I32),
                   jax.ShapeDtypeStruct((TOP_K, n), F32),
                   jax.ShapeDtypeStruct((TOP_K, n), I32),
                   jax.ShapeDtypeStruct((N_EXPERTS, LANES), I32)],
        scratch_shapes=[pltpu.VMEM((N_EXPERTS, 1), F32)],
        compiler_params=pltpu.CompilerParams(dimension_semantics=("arbitrary",),
                                             vmem_limit_bytes=VMEM_LIMIT),
        name="out_router",
    )(oa, obc, x, wa, wb, g, b, wr, br)


def _sc_gather(table, idx):
    m = idx.shape[0]
    d = table.shape[1]
    per_worker = m // (SC_CORES * SC_SUBCORES)
    half = SC_CHUNK // 2
    n_groups = per_worker // SC_CHUNK
    mesh = plsc.VectorSubcoreMesh(core_axis_name="c", subcore_axis_name="s")

    @functools.partial(
        pl.kernel, mesh=mesh,
        out_type=jax.ShapeDtypeStruct((m, d), table.dtype),
        scratch_types=[pltpu.VMEM((per_worker,), I32)] + [pltpu.VMEM((half, d), table.dtype)] * 2
                      + [pltpu.SemaphoreType.DMA] * 4,
        name="sc_gather",
    )
    def gather(table_hbm, idx_hbm, out_hbm, idx_v, buf0, buf1, g0, g1, w0, w1):
        wid = lax.axis_index("s") * SC_CORES + lax.axis_index("c")
        base = wid * per_worker
        pltpu.sync_copy(idx_hbm.at[pl.ds(base, per_worker)], idx_v)

        @pl.loop(0, n_groups)
        def _(g):
            o0 = pl.multiple_of(g * SC_CHUNK, 8)
            o1 = pl.multiple_of(g * SC_CHUNK + half, 8)
            ga = pltpu.async_copy(table_hbm.at[idx_v.at[pl.ds(o0, half)]], buf0, g0)
            gb = pltpu.async_copy(table_hbm.at[idx_v.at[pl.ds(o1, half)]], buf1, g1)
            ga.wait()
            wa = pltpu.async_copy(buf0, out_hbm.at[pl.ds(base + o0, half)], w0)
            gb.wait()
            wb = pltpu.async_copy(buf1, out_hbm.at[pl.ds(base + o1, half)], w1)
            wa.wait()
            wb.wait()

    return gather(table, idx)


def _sc_chunk(per_worker):
    return max(c for c in range(8, SC_SCATTER_MAX + 1, 8) if per_worker % c == 0)


def _sc_scatter_rows(h, pos):
    n, d = h.shape
    per_worker = n // (SC_CORES * SC_SUBCORES)
    ch = _sc_chunk(per_worker)
    n_chunks = per_worker // ch
    mesh = plsc.VectorSubcoreMesh(core_axis_name="c", subcore_axis_name="s")

    @functools.partial(
        pl.kernel, mesh=mesh,
        out_type=jax.ShapeDtypeStruct((TOP_K * n, d), h.dtype),
        scratch_types=[pltpu.VMEM((ch,), I32)] * TOP_K + [pltpu.VMEM((ch, d), h.dtype)]
                      + [pltpu.SemaphoreType.DMA] * TOP_K,
        name="sc_scatter",
    )
    def scatter(h_hbm, pos_hbm, out_hbm, *scratch):
        idx_v, rows_v, sems = scratch[:TOP_K], scratch[TOP_K], scratch[TOP_K + 1:]
        wid = lax.axis_index("s") * SC_CORES + lax.axis_index("c")
        base = wid * per_worker

        @pl.loop(0, n_chunks)
        def _(c):
            t0 = pl.multiple_of(base + c * ch, 8)
            pltpu.sync_copy(h_hbm.at[pl.ds(t0, ch)], rows_v)
            for k in range(TOP_K):
                pltpu.sync_copy(pos_hbm.at[pl.ds(k * n + t0, ch)], idx_v[k])
            copies = [pltpu.async_copy(rows_v, out_hbm.at[idx_v[k]], sems[k]) for k in range(TOP_K)]
            for cp in copies:
                cp.wait()

    return scatter(h, pos)


def _expert_kernel(vt_ref, ve_ref, lo_ref, hi_ref, x_ref, wu_ref, bu_ref, wd_ref, bd_ref, y_ref,
                   wu16_ref, wd16_ref):
    i = pl.program_id(0)
    ip = jnp.maximum(i - 1, 0)
    e = ve_ref[i]
    prev = ve_ref[ip]
    tile = vt_ref[i]
    first_visit = (i == 0) | (tile != vt_ref[ip])
    lo = lo_ref[i]
    hi = hi_ref[i]
    tm = x_ref.shape[0]
    dff = wd_ref.shape[0]

    @pl.when((i == 0) | (e != prev))
    def _():
        step = 128

        def cast(s, _):
            rows = pl.ds(pl.multiple_of(s * step, step), step)
            wu16_ref[rows, :] = wu_ref[rows, :].astype(BF16)
            return 0

        lax.fori_loop(0, wu_ref.shape[0] // step, cast, 0)

        def cast_d(s, _):
            rows = pl.ds(pl.multiple_of(s * step, step), step)
            wd16_ref[rows, :] = wd_ref[rows, :].astype(BF16)
            return 0

        lax.fori_loop(0, dff // step, cast_d, 0)

    @pl.when(first_visit)
    def _():
        y_ref[...] = jnp.zeros_like(y_ref)

    for s in range(tm // EXPERT_SUB):
        r0 = tile * tm + s * EXPERT_SUB
        rows = slice(s * EXPERT_SUB, (s + 1) * EXPERT_SUB)

        @pl.when((hi > r0) & (lo < r0 + EXPERT_SUB))
        def _():
            x = _unpack_bf16_pairs(lax.bitcast_convert_type(x_ref[rows, :], U32))
            glu = jnp.minimum(_dot(x, wu16_ref[:, :dff]) + bu_ref[:, :dff], SWIGLU_LIMIT)
            lin = jnp.clip(_dot(x, wu16_ref[:, dff:]) + bu_ref[:, dff:], -SWIGLU_LIMIT, SWIGLU_LIMIT)
            act = glu * (1.0 / (1.0 + jnp.exp(-SWIGLU_ALPHA * glu))) * (lin + 1.0)
            y = _dot(act.astype(BF16), wd16_ref[...]) + bd_ref[...]
            y = lax.bitcast_convert_type(_pack_bf16_pairs(y), F32)
            row = r0 + lax.broadcasted_iota(I32, (EXPERT_SUB, 1), 0)
            y_ref[rows, :] = jnp.where((row >= lo) & (row < hi), y, y_ref[rows, :])


def _experts(x_sorted, visits, w_up, b_up, w_down, b_down, layer):
    ns = x_sorted.shape[0]
    tm = EXPERT_TILE
    d, dff = w_down.shape[3], w_down.shape[2]
    bu = b_up.reshape(b_up.shape[0], N_EXPERTS, 1, 2 * dff)
    bd = b_down.reshape(b_down.shape[0], N_EXPERTS, 1, d)
    wmap = lambda i, vt, ve, lo, hi: (layer, ve[i], 0, 0)
    xmap = lambda i, vt, ve, lo, hi: (vt[i], 0)
    grid_spec = pltpu.PrefetchScalarGridSpec(
        num_scalar_prefetch=4,
        grid=(visits[0].shape[0],),
        in_specs=[pl.BlockSpec((tm, x_sorted.shape[1]), xmap),
                  pl.BlockSpec((None, None, d, 2 * dff), wmap),
                  pl.BlockSpec((None, None, 1, 2 * dff), wmap),
                  pl.BlockSpec((None, None, dff, d), wmap),
                  pl.BlockSpec((None, None, 1, d), wmap)],
        out_specs=pl.BlockSpec((tm, d // 2), xmap),
        scratch_shapes=[pltpu.VMEM((d, 2 * dff), BF16), pltpu.VMEM((dff, d), BF16)],
    )
    return pl.pallas_call(
        _expert_kernel,
        grid_spec=grid_spec,
        out_shape=jax.ShapeDtypeStruct((ns, d // 2), F32),
        compiler_params=pltpu.CompilerParams(dimension_semantics=("arbitrary",),
                                             vmem_limit_bytes=VMEM_LIMIT),
        name="experts",
    )(*visits, x_sorted, w_up, bu, w_down, bd)


def _expert_visits(cnt, n_rows):
    tm = EXPERT_TILE
    n_steps = n_rows // tm + N_EXPERTS
    ends = jnp.cumsum(cnt)
    starts = ends - cnt
    first_tile = starts // tm
    n_vis = jnp.where(cnt > 0, (ends - 1) // tm - first_tile + 1, 0)
    vis_end = jnp.cumsum(n_vis)
    vis_start = vis_end - n_vis
    v = jnp.arange(n_steps, dtype=I32)
    vc = jnp.minimum(v, vis_end[-1] - 1)
    onehot = ((vis_start[None, :] <= vc[:, None]) & (vc[:, None] < vis_end[None, :])).astype(I32)
    pick = lambda a: jnp.sum(onehot * a[None, :], axis=1).astype(I32)
    expert = pick(jnp.arange(N_EXPERTS, dtype=I32))
    tile = pick(first_tile) + vc - pick(vis_start)
    real = v < vis_end[-1]
    lo = jnp.where(real, pick(starts), 0).astype(I32)
    hi = jnp.where(real, pick(ends), 0).astype(I32)
    return tile.astype(I32), expert, lo, hi


def _combine_kernel(y0_ref, y1_ref, y2_ref, y3_ref, gate_ref, h_ref, g_ref, b_ref, o_ref, *, alpha):
    gate = gate_ref[...]
    lo = jnp.zeros(y0_ref.shape, F32)
    hi = jnp.zeros(y0_ref.shape, F32)
    for k, y_ref in enumerate((y0_ref, y1_ref, y2_ref, y3_ref)):
        w = lax.bitcast_convert_type(y_ref[...], U32)
        lo = lo + gate[:, k:k + 1] * lax.bitcast_convert_type(w << 16, F32)
        hi = hi + gate[:, k:k + 1] * lax.bitcast_convert_type(w & jnp.uint32(0xFFFF0000), F32)
    acc = jnp.concatenate([lo, hi], axis=1)
    o_ref[...] = _layer_norm(alpha * h_ref[...] + acc, g_ref[...], b_ref[...])


def _combine(y_tok, gates, h, ln_g, ln_b, alpha):
    n, d = h.shape
    tm = _wide_tile(n, COMBINE_TILE_MAX)
    nt = n // tm
    g, b = ln_g.reshape(1, d), ln_b.reshape(1, d)
    full = lambda a: pl.BlockSpec(a.shape, lambda i: (0, 0))
    ysp = lambda k: pl.BlockSpec((tm, d // 2), lambda i: (k * nt + i, 0))
    return pl.pallas_call(
        functools.partial(_combine_kernel, alpha=alpha),
        grid=(nt,),
        in_specs=[ysp(0), ysp(1), ysp(2), ysp(3),
                  pl.BlockSpec((tm, TOP_K), lambda i: (i, 0)),
                  pl.BlockSpec((tm, d), lambda i: (i, 0)), full(g), full(b)],
        out_specs=pl.BlockSpec((tm, d), lambda i: (i, 0)),
        out_shape=jax.ShapeDtypeStruct((n, d), F32),
        compiler_params=pltpu.CompilerParams(dimension_semantics=("arbitrary",),
                                             vmem_limit_bytes=VMEM_LIMIT),
        name="combine",
    )(y_tok, y_tok, y_tok, y_tok, gates, h, g, b)


def _round_up(a, m):
    return (a + m - 1) // m * m


def _wide_tile(n, cap):
    return max(t for t in range(TOKEN_TILE, cap + 1, TOKEN_TILE) if n % t == 0)


def _moe(h, hp, idx, gates, rank, counts, w_up, b_up, w_down, b_down, ln_g, ln_b, alpha, layer):
    n, d = h.shape
    ns = TOP_K * n
    cnt = counts[:, 0]
    starts = jnp.cumsum(cnt) - cnt
    experts = jnp.arange(N_EXPERTS, dtype=I32)
    offs = jnp.sum(jnp.where(idx[:, :, None] == experts, starts, 0), axis=-1)
    pos = (offs + rank).reshape(-1).astype(I32)
    x_sorted = _sc_scatter_rows(hp, pos)
    y_sorted = _experts(x_sorted, _expert_visits(cnt, ns), w_up, b_up, w_down, b_down, layer)
    m2 = _round_up(ns, SC_ROW_ALIGN)
    fill = jnp.arange(ns, m2, dtype=I32) - ns
    y_tok = _sc_gather(y_sorted, jnp.concatenate([pos, fill]))
    return _combine(y_tok, gates.T, h, ln_g, ln_b, alpha)


def kernel(x_prompt, x_sample, cache_k, cache_v, state_conv, state_gla, w_in, conv_w, w_gate, b_gate,
           gla_norm_g, w_out, ln1_g, ln1_b, w_router, b_router, w_up, b_up, w_down, b_down, ln2_g, ln2_b):
    depth = w_in.shape[0]
    bp, seq, d = x_prompt.shape
    bs, ts, _ = x_sample.shape
    past = cache_k.shape[2]
    n_p = bp * seq
    n = n_p + bs * ts
    alpha = float((2 * depth) ** 0.25)
    x = jnp.concatenate([x_prompt.reshape(n_p, d), x_sample.reshape(bs * ts, d)], axis=0)
    ckt = cache_k.transpose(0, 1, 3, 4, 2).reshape(depth, bs, SB_DIM, past)
    cvt = cache_v.transpose(0, 1, 3, 4, 2).reshape(depth, bs, SB_DIM, past)
    zero_conv = jnp.zeros((bp, CONV_W - 1, CONV_DIM), F32)
    zero_gla = jnp.zeros((bp, GLA_HEADS, GLA_DK, GLA_DK), F32)
    wb = w_in.astype(BF16)
    o_r = 3 * SB_DIM
    n_r = 3 * CONV_DIM + 4 * GLA_DIM
    wq, wk, wv = wb[:, :, :SB_DIM], wb[:, :, SB_DIM:2 * SB_DIM], wb[:, :, 2 * SB_DIM:o_r]
    wkt, wvt = wk.transpose(0, 2, 1), wv.transpose(0, 2, 1)
    wr, wal = wb[:, :, o_r:o_r + n_r], wb[:, :, o_r + n_r:]
    wg = w_gate.astype(BF16)
    wo = w_out.astype(BF16)
    kt = jnp.zeros((depth, bp, SB_DIM, seq), F32)
    vt = jnp.zeros((depth, bp, SB_DIM, seq), F32)
    tm_p = min(PROMPT_TILE, seq)
    outs = [[] for _ in range(6)]
    for l in range(depth):
        shared = (wr[l], wal[l], wg[l], b_gate[l].reshape(1, -1))
        q_p, kt, vt, rest_p, la_p = _in_proj(x, 0, n_p, tm_p, (wq[l], wkt[l], wvt[l]) + shared,
                                             kv_prompt=(l, seq, kt, vt))
        q_s, ks, vs, rest_s, la_s = _in_proj(x, n_p, n - n_p, TOKEN_TILE, (wq[l], wk[l], wv[l]) + shared)
        oa = _sb_prompt(jnp.zeros((n, SB_DIM), BF16), q_p, kt, vt, l, bp, seq)
        oa = _sb_decode(oa, q_s, ks, vs, ckt, cvt, l, n_p, bs, ts)
        obc, conv_p, gla_p = _conv_gla(jnp.zeros((n, CONV_DIM + GLA_DIM), BF16), n, rest_p, la_p, conv_w[l],
                                       gla_norm_g[l], zero_conv, zero_gla, 0, bp, seq)
        obc, conv_s, gla_s = _conv_gla(obc, n, rest_s, la_s, conv_w[l], gla_norm_g[l], state_conv[l],
                                       state_gla[l], n_p, bs, ts)
        h, hp, idx, gates, rank, counts = _out_router(oa, obc, x, wo[l, :SB_DIM], wo[l, SB_DIM:], ln1_g[l],
                                                      ln1_b[l], w_router[l], b_router[l], alpha)
        x = _moe(h, hp, idx, gates, rank, counts, w_up, b_up, w_down, b_down, ln2_g[l], ln2_b[l],
                 alpha, l)
        outs[0].append(conv_p)
        outs[1].append(gla_p)
        outs[2].append(ks.reshape(bs, ts, SB_HEADS, HEAD_DIM))
        outs[3].append(vs.reshape(bs, ts, SB_HEADS, HEAD_DIM))
        outs[4].append(conv_s)
        outs[5].append(gla_s)
    k_prompt = kt.reshape(depth, bp, SB_HEADS, HEAD_DIM, seq).transpose(0, 1, 4, 2, 3)
    v_prompt = vt.reshape(depth, bp, SB_HEADS, HEAD_DIM, seq).transpose(0, 1, 4, 2, 3)
    st = [jnp.stack(o) for o in outs]
    return (x[:n_p].reshape(bp, seq, d), x[n_p:].reshape(bs, ts, d), k_prompt, v_prompt,
            st[0], st[1], st[2], st[3], st[4], st[5])
```

```python
import functools

import jax
import jax.numpy as jnp
from jax import lax
from jax.experimental import pallas as pl
from jax.experimental.pallas import tpu as pltpu
from jax.experimental.pallas import tpu_sc as plsc

F32 = jnp.float32
BF16 = jnp.bfloat16
I32 = jnp.int32
U32 = jnp.uint32

HEAD_DIM = 64
SB_HEADS = 8
SB_DIM = SB_HEADS * HEAD_DIM
CONV_DIM = 256
CONV_W = 3
GLA_HEADS = 4
GLA_DK = 64
GLA_DIM = GLA_HEADS * GLA_DK
GLA_RANK = 16
GLA_TAU = 16.0
GLA_CHUNK = 64
N_EXPERTS = 32
TOP_K = 4
SWIGLU_LIMIT = 7.0
SWIGLU_ALPHA = 1.702
NORM_EPS = 1e-5

LANES = 128
SC_CORES = 2
SC_SUBCORES = 16
SC_CHUNK = 128
SC_ROW_ALIGN = SC_CORES * SC_SUBCORES * SC_CHUNK
SC_SCATTER_MAX = 104
TOKEN_TILE = 256
WIDE_TILE_MAX = 1280
COMBINE_TILE_MAX = 640
PROMPT_TILE = 512
EXPERT_TILE = 512
EXPERT_SUB = 256
SB_TQ = 256
SB_BK = 256
SB_GROUP = 256
DEC_TK = 512
GLA_TC = 256
VMEM_LIMIT = 48 * 1024 * 1024
SB_DEAD = -100.0


def _dot(a, b):
    return jnp.dot(a, b, preferred_element_type=F32)


def _dot_nt(a, b):
    return lax.dot_general(a, b, (((1,), (1,)), ((), ())), preferred_element_type=F32)


def _split_bf16(x):
    hi = x.astype(BF16)
    lo = (x - hi.astype(F32)).astype(BF16)
    return hi, lo


def _dot_exact_rhs(x, m):
    hi, lo = _split_bf16(x)
    return _dot(hi, m) + _dot(lo, m)


def _dot_exact_lhs(m, x):
    hi, lo = _split_bf16(x)
    return _dot(m, hi) + _dot(m, lo)


def _pack_bf16_pairs(x):
    c = x.shape[1] // 2
    bits = lax.bitcast_convert_type(x.astype(BF16).astype(F32), U32)
    return (bits[:, :c] >> 16) | (bits[:, c:] & jnp.uint32(0xFFFF0000))


def _unpack_bf16_pairs(w):
    lo = lax.bitcast_convert_type(w << 16, F32)
    hi = lax.bitcast_convert_type(w & jnp.uint32(0xFFFF0000), F32)
    return jnp.concatenate([lo, hi], axis=1).astype(BF16)


def _softplus(z):
    return jnp.maximum(z, 0.0) + jnp.log(1.0 + jnp.exp(-jnp.abs(z)))


def _layer_norm(y, g, b):
    mu = jnp.mean(y, axis=-1, keepdims=True)
    yc = y - mu
    var = jnp.mean(yc * yc, axis=-1, keepdims=True)
    return yc * lax.rsqrt(var + NORM_EPS) * g + b


def _strict_upper(n, copies=1):
    r = lax.broadcasted_iota(I32, (copies * n, n), 0)
    c = lax.broadcasted_iota(I32, (copies * n, n), 1)
    for k in range(1, copies):
        r = r - jnp.where(r >= n, n, 0)
    return jnp.where(r > c, 1.0, 0.0).astype(BF16)


def _project(x_ref, wq_ref, wk_ref, wv_ref, wr_ref, wal_ref, wg_ref, bg_ref,
             q_ref, k_ref, v_ref, r_ref, la_ref, *, transposed_kv):
    xb = x_ref[...].astype(BF16)
    for c in range(0, SB_DIM, 256):
        q_ref[:, c:c + 256] = (_dot(xb, wq_ref[:, c:c + 256]) * (HEAD_DIM ** -0.5)).astype(BF16)
    for c in range(0, r_ref.shape[1], 256):
        r_ref[:, c:c + 256] = _dot(xb, wr_ref[:, c:c + 256])
    al = _dot(xb, wal_ref[...])
    g = _dot(al.astype(BF16), wg_ref[...]) + bg_ref[...]
    la_ref[...] = -_softplus(-g) * (1.0 / GLA_TAU)
    for c in range(0, SB_DIM, 256):
        if transposed_kv:
            k_ref[c:c + 256, :] = _dot_nt(wk_ref[c:c + 256, :], xb)
            v_ref[c:c + 256, :] = _dot_nt(wv_ref[c:c + 256, :], xb)
        else:
            k_ref[:, c:c + 256] = _dot(xb, wk_ref[:, c:c + 256])
            v_ref[:, c:c + 256] = _dot(xb, wv_ref[:, c:c + 256])


def _in_proj(x, row0, rows, tm, weights):
    d = x.shape[1]
    n_r = weights[3].shape[1]
    rb = row0 // tm
    full = lambda a: pl.BlockSpec(a.shape, lambda i: (0,) * a.ndim)
    row = lambda w: pl.BlockSpec((tm, w), lambda i: (i, 0))
    sds = lambda w, dt: jax.ShapeDtypeStruct((rows, w), dt)
    return pl.pallas_call(
        functools.partial(_project, transposed_kv=False),
        grid=(rows // tm,),
        in_specs=[pl.BlockSpec((tm, d), lambda i: (rb + i, 0))] + [full(w) for w in weights],
        out_specs=[row(SB_DIM), row(SB_DIM), row(SB_DIM), row(n_r), row(GLA_DIM)],
        out_shape=[sds(SB_DIM, BF16), sds(SB_DIM, F32), sds(SB_DIM, F32), sds(n_r, F32), sds(GLA_DIM, F32)],
        compiler_params=pltpu.CompilerParams(dimension_semantics=("arbitrary",),
                                             vmem_limit_bytes=VMEM_LIMIT),
        name="in_proj",
    )(x, *weights)


def _sb_weights(z, tri, run, mask):
    sp = _softplus(z)
    l1m = -sp
    lsig = z - sp
    if mask is not None:
        l1m = jnp.where(mask, l1m, 0.0)
    if tri.shape[0] == 2 * z.shape[1]:
        hi = lax.bitcast_convert_type(lax.bitcast_convert_type(l1m, U32) & jnp.uint32(0xFFFF0000), F32)
        parts = jnp.concatenate([hi.astype(BF16), (l1m - hi).astype(BF16)], axis=1)
        rest = _dot(parts, tri) + run
    else:
        rest = _dot_exact_rhs(l1m, tri) + run
    a = jnp.exp(lsig + rest)
    if mask is not None:
        a = jnp.where(mask, a, 0.0)
    return a.astype(BF16), run + jnp.sum(l1m, axis=1, keepdims=True)


def _sb_prompt_kernel(oa_ref, q_ref, kt_ref, vt_ref, o_ref, *, tq, bk):
    del oa_ref
    qi = pl.program_id(2)
    q = q_ref[...]
    n_hh = SB_GROUP // HEAD_DIM
    lane = lax.broadcasted_iota(I32, (1, SB_GROUP), 1)
    in_head = [(lane // HEAD_DIM) == h for h in range(n_hh)]
    qh = [jnp.where(m, q, jnp.zeros_like(q)) for m in in_head]
    tri = _strict_upper(bk, copies=2)
    n_full = (qi * tq) // bk
    qpos = qi * tq + lax.broadcasted_iota(I32, (tq, bk), 0)
    kpos = n_full * bk + lax.broadcasted_iota(I32, (tq, bk), 1)
    diag_mask = kpos < qpos

    def tile(jb, runs, mask):
        ks = pl.multiple_of(jb * bk, bk)
        kt = kt_ref[:, pl.ds(ks, bk)].astype(BF16)
        vt = vt_ref[:, pl.ds(ks, bk)].astype(BF16)
        out = jnp.zeros((tq, SB_GROUP), F32)
        new_runs = []
        for h in range(n_hh):
            a, run = _sb_weights(_dot(qh[h], kt), tri, runs[h], mask)
            out = jnp.where(in_head[h], _dot_nt(a, vt), out)
            new_runs.append(run)
        return out, tuple(new_runs)

    def alive_of(runs):
        m = jnp.max(runs[0])
        for r in runs[1:]:
            m = jnp.maximum(m, jnp.max(r))
        return m > SB_DEAD

    acc, runs = tile(n_full, tuple(jnp.zeros((tq, 1), F32) for _ in range(n_hh)), diag_mask)

    def cond(carry):
        j, alive, _, _ = carry
        return (j >= 0) & alive

    def body(carry):
        j, _, acc, runs = carry
        pv, runs = tile(j, runs, None)
        return j - 1, alive_of(runs), acc + pv, runs

    _, _, acc, _ = lax.while_loop(cond, body, (n_full - 1, alive_of(runs), acc, runs))
    o_ref[...] = acc.astype(o_ref.dtype)


def _sb_prompt(oa, q, kt, vt, layer, batch, seq):
    tq, bk = min(SB_TQ, seq), min(SB_BK, seq)
    nq = seq // tq
    hp = SB_DIM // SB_GROUP
    kv_spec = pl.BlockSpec((None, None, SB_GROUP, seq), lambda b, p, i: (layer, b, p, 0))
    return pl.pallas_call(
        functools.partial(_sb_prompt_kernel, tq=tq, bk=bk),
        grid=(batch, hp, nq),
        in_specs=[pl.BlockSpec(memory_space=pl.ANY),
                  pl.BlockSpec((tq, SB_GROUP), lambda b, p, i: (b * nq + i, p)), kv_spec, kv_spec],
        out_specs=pl.BlockSpec((tq, SB_GROUP), lambda b, p, i: (b * nq + i, p)),
        out_shape=jax.ShapeDtypeStruct(oa.shape, oa.dtype),
        input_output_aliases={0: 0},
        compiler_params=pltpu.CompilerParams(
            dimension_semantics=("arbitrary", "arbitrary", "arbitrary"),
            vmem_limit_bytes=VMEM_LIMIT),
        name="sb_prompt",
    )(oa, q, kt, vt)


def _sb_decode_kernel(oa_ref, q_ref, kn_ref, vn_ref, kc_hbm, vc_hbm, o_ref, kbuf, vbuf, sem, acc_ref, run_ref,
                      alive_ref, *, t, tk, bk, layer, nkb):
    del oa_ref
    b = pl.program_id(0)
    q = q_ref[...]

    def block_copies(j, slot):
        cols = pl.ds(pl.multiple_of((nkb - 1 - j) * tk, tk), tk)
        return (pltpu.make_async_copy(kc_hbm.at[layer, b, :, cols], kbuf.at[slot], sem.at[0, slot]),
                pltpu.make_async_copy(vc_hbm.at[layer, b, :, cols], vbuf.at[slot], sem.at[1, slot]))

    for cp in block_copies(0, 0):
        cp.start()

    kn = kn_ref[...].astype(BF16)
    vn = vn_ref[...].astype(BF16)
    lane = lax.broadcasted_iota(I32, (1, SB_DIM), 1)
    r = lax.broadcasted_iota(I32, (t, t), 0)
    c = lax.broadcasted_iota(I32, (t, t), 1)
    mask = c < r
    tri_new = _strict_upper(t)
    for h in range(SB_HEADS):
        qh = jnp.where((lane // HEAD_DIM) == h, q, jnp.zeros_like(q))
        a, run = _sb_weights(_dot_nt(qh, kn), tri_new, jnp.zeros((t, 1), F32), mask)
        acc_ref[h] = _dot(a, vn)[:, h * HEAD_DIM:(h + 1) * HEAD_DIM]
        run_ref[h * t:(h + 1) * t, :] = run
    alive_ref[0] = (jnp.max(run_ref[...]) > SB_DEAD).astype(I32)

    tri = _strict_upper(bk, copies=2)

    def cond(carry):
        j, alive = carry
        return (j < nkb) & (alive > 0)

    def body(carry):
        j, _ = carry
        slot = j % 2

        @pl.when(j + 1 < nkb)
        def _():
            for cp in block_copies(j + 1, 1 - slot):
                cp.start()

        for cp in block_copies(j, slot):
            cp.wait()
        for c in range(tk // bk - 1, -1, -1):
            cols = slice(c * bk, (c + 1) * bk)

            @pl.when(alive_ref[0] > 0)
            def _():
                z = jnp.concatenate(
                    [_dot(q[:, h * HEAD_DIM:(h + 1) * HEAD_DIM],
                          kbuf[slot, h * HEAD_DIM:(h + 1) * HEAD_DIM, cols].astype(BF16))
                     for h in range(SB_HEADS)], axis=0)
                a, run = _sb_weights(z, tri, run_ref[...], None)
                for h in range(SB_HEADS):
                    vt = vbuf[slot, h * HEAD_DIM:(h + 1) * HEAD_DIM, cols].astype(BF16)
                    acc_ref[h] = acc_ref[h] + _dot_nt(a[h * t:(h + 1) * t, :], vt)
                run_ref[...] = run
                alive_ref[0] = (jnp.max(run) > SB_DEAD).astype(I32)

        return j + 1, alive_ref[0]

    j_end, _ = lax.while_loop(cond, body, (jnp.int32(0), alive_ref[0]))

    @pl.when(j_end < nkb)
    def _():
        for cp in block_copies(j_end, j_end % 2):
            cp.wait()

    for h in range(SB_HEADS):
        o_ref[:, h * HEAD_DIM:(h + 1) * HEAD_DIM] = acc_ref[h].astype(o_ref.dtype)


def _sb_decode(oa, q, ks, vs, cache_kt, cache_vt, layer, row0, batch, t):
    past = cache_kt.shape[3]
    tk = min(DEC_TK, past)
    bk = min(SB_BK, tk)
    nkb = past // tk
    rb = row0 // t
    new = pl.BlockSpec((t, SB_DIM), lambda b: (b, 0))
    anywhere = pl.BlockSpec(memory_space=pl.ANY)
    return pl.pallas_call(
        functools.partial(_sb_decode_kernel, t=t, tk=tk, bk=bk, layer=layer, nkb=nkb),
        grid=(batch,),
        in_specs=[anywhere, new, new, new, anywhere, anywhere],
        out_specs=pl.BlockSpec((t, SB_DIM), lambda b: (rb + b, 0)),
        out_shape=jax.ShapeDtypeStruct(oa.shape, oa.dtype),
        input_output_aliases={0: 0},
        scratch_shapes=[pltpu.VMEM((2, SB_DIM, tk), F32),
                        pltpu.VMEM((2, SB_DIM, tk), F32),
                        pltpu.SemaphoreType.DMA((2, 2)),
                        pltpu.VMEM((SB_HEADS, t, HEAD_DIM), F32),
                        pltpu.VMEM((SB_HEADS * t, 1), F32),
                        pltpu.SMEM((1,), I32)],
        compiler_params=pltpu.CompilerParams(dimension_semantics=("arbitrary",),
                                             vmem_limit_bytes=VMEM_LIMIT),
        name="sb_decode",
    )(oa, q, ks, vs, cache_kt, cache_vt)


def _mix_state_load(cprev_ref, sprev_ref, ctail_ref, st_ref):
    n_hh = LANES // GLA_DK
    ctail_ref[...] = cprev_ref[...]
    for p in range(GLA_DIM // LANES):
        st_ref[p] = jnp.zeros((LANES, LANES), F32)
        for hh in range(n_hh):
            st_ref[p, hh * GLA_DK:(hh + 1) * GLA_DK, hh * GLA_DK:(hh + 1) * GLA_DK] = sprev_ref[p * n_hh + hh].T


def _mix_state_store(ctail_ref, st_ref, cnew_ref, snew_ref):
    n_hh = LANES // GLA_DK
    cnew_ref[...] = ctail_ref[...]
    for p in range(GLA_DIM // LANES):
        for hh in range(n_hh):
            blk = st_ref[p, hh * GLA_DK:(hh + 1) * GLA_DK, :]
            snew_ref[p * n_hh + hh] = blk.T[hh * GLA_DK:(hh + 1) * GLA_DK, :]


def _conv_gla_block(bg_ref, cg_ref, u_ref, qc_ref, kc_ref, vc_ref, gc_ref, la_ref, cw_ref, ng_ref,
                    o_ref, ctail_ref, st_ref, tc, chunk):
    n_pair = GLA_DIM // LANES
    n_hh = LANES // GLA_DK

    z = cg_ref[...] * u_ref[...]
    tail = ctail_ref[...]
    row = lax.broadcasted_iota(I32, z.shape, 0)
    z1 = jnp.where(row < 1, tail[1:2, :], pltpu.roll(z, 1, 0))
    z2 = jnp.where(row < 2, jnp.where(row < 1, tail[0:1, :], tail[1:2, :]), pltpu.roll(z, 2, 0))
    cw = cw_ref[...]
    y = z2 * cw[0:1, :] + z1 * cw[1:2, :] + z * cw[2:3, :]
    o_ref[:, 0:CONV_DIM] = (bg_ref[...] * y).astype(o_ref.dtype)
    ctail_ref[...] = z[tc - 2:tc, :]

    r = lax.broadcasted_iota(I32, (tc, tc), 0)
    c = lax.broadcasted_iota(I32, (tc, tc), 1)
    same_chunk = (r // chunk) == (c // chunk)
    chunk_sum = jnp.where(same_chunk, 1.0, 0.0).astype(BF16)
    chunk_cumsum = jnp.where(same_chunk & (c <= r), 1.0, 0.0).astype(BF16)
    lane = lax.broadcasted_iota(I32, (1, LANES), 1)
    lr = lax.broadcasted_iota(I32, (LANES, LANES), 0) // GLA_DK
    lc = lax.broadcasted_iota(I32, (LANES, LANES), 1) // GLA_DK
    same_head = lr == lc
    head_mean = jnp.where(same_head, 1.0 / GLA_DK, 0.0).astype(BF16)
    r2 = lax.broadcasted_iota(I32, (tc, n_hh * tc), 0)
    c2 = lax.broadcasted_iota(I32, (tc, n_hh * tc), 1)
    c2 = c2 - jnp.where(c2 >= tc, tc, 0)
    causal2 = ((r2 // chunk) == (c2 // chunk)) & (c2 <= r2)
    for p in range(n_pair):
        cols = slice(p * LANES, (p + 1) * LANES)
        la = la_ref[:, cols]
        b = _dot_exact_lhs(chunk_cumsum, la)
        tot = _dot_exact_lhs(chunk_sum, la)
        k = kc_ref[:, cols]
        qe = (qc_ref[:, cols] * (GLA_DK ** -0.5) * jnp.exp(b)).astype(BF16)
        ke = (k * jnp.exp(-b)).astype(BF16)
        kd = (k * jnp.exp(tot - b)).astype(BF16)
        decay = jnp.exp(tot)
        vb = vc_ref[:, cols].astype(BF16)
        zeros = jnp.zeros_like(ke)
        ke_st = jnp.concatenate([jnp.where((lane // GLA_DK) == hh, ke, zeros)
                                 for hh in range(n_hh)], axis=0)
        v_st = jnp.concatenate([jnp.where((lane // GLA_DK) == hh, vb, zeros)
                                for hh in range(n_hh)], axis=0)
        a = jnp.where(causal2, _dot_nt(qe, ke_st), 0.0)
        o = _dot(a.astype(BF16), v_st)
        st = st_ref[p]
        from_state = []
        for ci in range(tc // chunk):
            rows = slice(ci * chunk, (ci + 1) * chunk)
            from_state.append(_dot_nt(qe[rows], st.astype(BF16)))
            upd = _dot(vb[rows].T, kd[rows])
            st = jnp.where(same_head, st * decay[ci * chunk:ci * chunk + 1, :] + upd, 0.0)
        st_ref[p] = st
        o = o + jnp.concatenate(from_state, axis=0)
        ms = _dot_exact_rhs(o * o, head_mean)
        o = o * lax.rsqrt(ms + NORM_EPS) * ng_ref[:, cols]
        g = gc_ref[:, cols]
        o = o * (g * (1.0 / (1.0 + jnp.exp(-g))))
        o_ref[:, CONV_DIM + p * LANES:CONV_DIM + (p + 1) * LANES] = o.astype(o_ref.dtype)


def _conv_gla_kernel(*refs, tc, chunk):
    (_, bg_ref, cg_ref, u_ref, qc_ref, kc_ref, vc_ref, gc_ref, la_ref, cw_ref, ng_ref, cprev_ref, sprev_ref,
     o_ref, cnew_ref, snew_ref, ctail_ref, st_ref) = refs
    ti = pl.program_id(1)

    @pl.when(ti == 0)
    def _():
        _mix_state_load(cprev_ref, sprev_ref, ctail_ref, st_ref)

    _conv_gla_block(bg_ref, cg_ref, u_ref, qc_ref, kc_ref, vc_ref, gc_ref, la_ref, cw_ref, ng_ref,
                    o_ref, ctail_ref, st_ref, tc, chunk)

    @pl.when(ti == pl.num_programs(1) - 1)
    def _():
        _mix_state_store(ctail_ref, st_ref, cnew_ref, snew_ref)


def _conv_gla(obc_prev, n, rest, la, conv_w, norm_g, conv_prev, gla_prev, row0, batch, t):
    tc = min(GLA_TC, t)
    chunk = min(GLA_CHUNK, t)
    nt = t // tc
    rb = row0 // tc
    col = lambda j: pl.BlockSpec((tc, CONV_DIM), lambda b, i: (b * nt + i, j))
    const2 = lambda a: pl.BlockSpec(a.shape, lambda b, i: (0, 0))
    ng = norm_g.reshape(1, GLA_DIM)
    args = (rest,) * 7 + (la, conv_w, ng, conv_prev, gla_prev)
    return pl.pallas_call(
        functools.partial(_conv_gla_kernel, tc=tc, chunk=chunk),
        grid=(batch, nt),
        in_specs=[
            pl.BlockSpec(memory_space=pl.ANY), col(0), col(1), col(2), col(3), col(4), col(5), col(6),
            pl.BlockSpec((tc, GLA_DIM), lambda b, i: (b * nt + i, 0)),
            const2(conv_w), const2(ng),
            pl.BlockSpec((None, CONV_W - 1, CONV_DIM), lambda b, i: (b, 0, 0)),
            pl.BlockSpec((None, GLA_HEADS, GLA_DK, GLA_DK), lambda b, i: (b, 0, 0, 0))],
        out_specs=[pl.BlockSpec((tc, CONV_DIM + GLA_DIM), lambda b, i: (rb + b * nt + i, 0)),
                   pl.BlockSpec((None, CONV_W - 1, CONV_DIM), lambda b, i: (b, 0, 0)),
                   pl.BlockSpec((None, GLA_HEADS, GLA_DK, GLA_DK), lambda b, i: (b, 0, 0, 0))],
        out_shape=[jax.ShapeDtypeStruct((n, CONV_DIM + GLA_DIM), BF16),
                   jax.ShapeDtypeStruct((batch, CONV_W - 1, CONV_DIM), F32),
                   jax.ShapeDtypeStruct((batch, GLA_HEADS, GLA_DK, GLA_DK), F32)],
        input_output_aliases={0: 0},
        scratch_shapes=[pltpu.VMEM((CONV_W - 1, CONV_DIM), F32),
                        pltpu.VMEM((GLA_DIM // LANES, LANES, LANES), F32)],
        compiler_params=pltpu.CompilerParams(dimension_semantics=("arbitrary", "arbitrary"),
                                             vmem_limit_bytes=VMEM_LIMIT),
        name="conv_gla",
    )(obc_prev, *args)


def _mix_prompt_kernel(*refs, tc, chunk):
    (_, _, _, x_ref, wq_ref, wk_ref, wv_ref, wr_ref, wal_ref, wg_ref, bg_ref, cw_ref, ng_ref, cprev_ref,
     sprev_ref, q_ref, kt_ref, vt_ref, o_ref, cnew_ref, snew_ref, r_scr, la_scr, ctail_ref, st_ref) = refs
    ti = pl.program_id(1)
    _project(x_ref, wq_ref, wk_ref, wv_ref, wr_ref, wal_ref, wg_ref, bg_ref,
             q_ref, kt_ref, vt_ref, r_scr, la_scr, transposed_kv=True)

    @pl.when(ti == 0)
    def _():
        _mix_state_load(cprev_ref, sprev_ref, ctail_ref, st_ref)

    for s in range(x_ref.shape[0] // tc):
        rows = pl.ds(s * tc, tc)
        part = [r_scr.at[rows, pl.ds(j * CONV_DIM, CONV_DIM)] for j in range(7)]
        _conv_gla_block(*part, la_scr.at[rows], cw_ref, ng_ref, o_ref.at[rows], ctail_ref, st_ref, tc, chunk)

    @pl.when(ti == pl.num_programs(1) - 1)
    def _():
        _mix_state_store(ctail_ref, st_ref, cnew_ref, snew_ref)


def _mix_prompt(x, obc_prev, kt, vt, layer, weights, conv_w, norm_g, conv_prev, gla_prev, batch, seq, tm):
    d = x.shape[1]
    n_p = batch * seq
    nt = seq // tm
    tc = min(GLA_TC, tm)
    wq, wk, wv, wr, wal, wg, bg = weights
    ng = norm_g.reshape(1, GLA_DIM)
    full = lambda a: pl.BlockSpec(a.shape, lambda b, i: (0,) * a.ndim)
    row = lambda w: pl.BlockSpec((tm, w), lambda b, i: (b * nt + i, 0))
    kv_spec = pl.BlockSpec((None, None, SB_DIM, tm), lambda b, i: (layer, b, 0, i))
    conv_spec = pl.BlockSpec((None, CONV_W - 1, CONV_DIM), lambda b, i: (b, 0, 0))
    gla_spec = pl.BlockSpec((None, GLA_HEADS, GLA_DK, GLA_DK), lambda b, i: (b, 0, 0, 0))
    anywhere = pl.BlockSpec(memory_space=pl.ANY)
    consts = (wq, wk, wv, wr, wal, wg, bg, conv_w, ng)
    return pl.pallas_call(
        functools.partial(_mix_prompt_kernel, tc=tc, chunk=min(GLA_CHUNK, tc)),
        grid=(batch, nt),
        in_specs=[anywhere, anywhere, anywhere, row(d)] + [full(w) for w in consts] + [conv_spec, gla_spec],
        out_specs=[row(SB_DIM), kv_spec, kv_spec, row(CONV_DIM + GLA_DIM), conv_spec, gla_spec],
        out_shape=[jax.ShapeDtypeStruct((n_p, SB_DIM), BF16),
                   jax.ShapeDtypeStruct(kt.shape, kt.dtype), jax.ShapeDtypeStruct(vt.shape, vt.dtype),
                   jax.ShapeDtypeStruct(obc_prev.shape, obc_prev.dtype),
                   jax.ShapeDtypeStruct((batch, CONV_W - 1, CONV_DIM), F32),
                   jax.ShapeDtypeStruct((batch, GLA_HEADS, GLA_DK, GLA_DK), F32)],
        input_output_aliases={0: 1, 1: 2, 2: 3},
        scratch_shapes=[pltpu.VMEM((tm, wr.shape[1]), F32),
                        pltpu.VMEM((tm, GLA_DIM), F32),
                        pltpu.VMEM((CONV_W - 1, CONV_DIM), F32),
                        pltpu.VMEM((GLA_DIM // LANES, LANES, LANES), F32)],
        compiler_params=pltpu.CompilerParams(dimension_semantics=("arbitrary", "arbitrary"),
                                             vmem_limit_bytes=VMEM_LIMIT),
        name="mix_prompt",
    )(kt, vt, obc_prev, x, *consts, conv_prev, gla_prev)


def _out_router_kernel(oa_ref, obc_ref, x_ref, wa_ref, wb_ref, g_ref, b_ref, wr_ref, br_ref,
                       h_ref, hp_ref, idx_ref, gate_ref, rank_ref, cnt_ref, carry_ref, *, alpha):
    i = pl.program_id(0)
    tm = x_ref.shape[0]

    @pl.when(i == 0)
    def _():
        carry_ref[...] = jnp.zeros_like(carry_ref)

    m = _dot(oa_ref[...], wa_ref[...]) + _dot(obc_ref[...], wb_ref[...])
    h = _layer_norm(alpha * x_ref[...] + m, g_ref[...], b_ref[...])
    h_ref[...] = h
    hp_ref[...] = lax.bitcast_convert_type(_pack_bf16_pairs(h), F32)

    h_hi, h_lo = _split_bf16(h)
    w_hi, w_lo = _split_bf16(wr_ref[...])
    logit = _dot_nt(w_hi, h_hi) + _dot_nt(w_hi, h_lo) + _dot_nt(w_lo, h_hi) + br_ref[...]
    eid = lax.broadcasted_iota(I32, (N_EXPERTS, tm), 0)
    r = lax.broadcasted_iota(I32, (tm, tm), 0)
    c = lax.broadcasted_iota(I32, (tm, tm), 1)
    before = jnp.where(r < c, 1.0, 0.0).astype(BF16)
    base = carry_ref[...]
    vals, idxs, ranks = [], [], []
    for _ in range(TOP_K):
        mx = jnp.max(logit, axis=0, keepdims=True)
        sel = jnp.min(jnp.where(logit == mx, eid, N_EXPERTS), axis=0, keepdims=True)
        hit = eid == sel
        logit = jnp.where(hit, -jnp.inf, logit)
        onehot = jnp.where(hit, 1.0, 0.0)
        prior = _dot(onehot.astype(BF16), before) + base
        ranks.append(jnp.sum(onehot * prior, axis=0, keepdims=True))
        base = base + jnp.sum(onehot, axis=1, keepdims=True)
        vals.append(mx)
        idxs.append(sel)
    carry_ref[...] = base
    e = [jnp.exp(v - vals[0]) for v in vals]
    inv = 1.0 / (e[0] + e[1] + e[2] + e[3])
    idx_ref[...] = jnp.concatenate(idxs, axis=0)
    gate_ref[...] = jnp.concatenate([ek * inv for ek in e], axis=0)
    rank_ref[...] = jnp.concatenate(ranks, axis=0).astype(I32)
    cnt_ref[...] = jnp.broadcast_to(base, cnt_ref.shape).astype(I32)


def _out_router(oa, obc, x, wa, wb, ln_g, ln_b, w_router, b_router, alpha):
    n, d = x.shape
    tm = _wide_tile(n, WIDE_TILE_MAX)
    wr = w_router.T
    br = b_router.reshape(N_EXPERTS, 1)
    g, b = ln_g.reshape(1, d), ln_b.reshape(1, d)
    full = lambda a: pl.BlockSpec(a.shape, lambda i: (0, 0))
    row = lambda w: pl.BlockSpec((tm, w), lambda i: (i, 0))
    colb = pl.BlockSpec((TOP_K, tm), lambda i: (0, i))
    return pl.pallas_call(
        functools.partial(_out_router_kernel, alpha=alpha),
        grid=(n // tm,),
        in_specs=[row(SB_DIM), row(CONV_DIM + GLA_DIM), row(d), full(wa), full(wb), full(g), full(b),
                  full(wr), full(br)],
        out_specs=[row(d), row(d // 2), colb, colb, colb,
                   pl.BlockSpec((N_EXPERTS, LANES), lambda i: (0, 0))],
        out_shape=[jax.ShapeDtypeStruct((n, d), F32),
                   jax.ShapeDtypeStruct((n, d // 2), F32),
                   jax.ShapeDtypeStruct((TOP_K, n), I32),
                   jax.ShapeDtypeStruct((TOP_K, n), F32),
                   jax.ShapeDtypeStruct((TOP_K, n), I32),
                   jax.ShapeDtypeStruct((N_EXPERTS, LANES), I32)],
        scratch_shapes=[pltpu.VMEM((N_EXPERTS, 1), F32)],
        compiler_params=pltpu.CompilerParams(dimension_semantics=("arbitrary",),
                                             vmem_limit_bytes=VMEM_LIMIT),
        name="out_router",
    )(oa, obc, x, wa, wb, g, b, wr, br)


def _sc_gather(table, idx):
    m = idx.shape[0]
    d = table.shape[1]
    per_worker = m // (SC_CORES * SC_SUBCORES)
    half = SC_CHUNK // 2
    n_groups = per_worker // SC_CHUNK
    mesh = plsc.VectorSubcoreMesh(core_axis_name="c", subcore_axis_name="s")

    @functools.partial(
        pl.kernel, mesh=mesh,
        out_type=jax.ShapeDtypeStruct((m, d), table.dtype),
        scratch_types=[pltpu.VMEM((per_worker,), I32)] + [pltpu.VMEM((half, d), table.dtype)] * 2
                      + [pltpu.SemaphoreType.DMA] * 4,
        name="sc_gather",
    )
    def gather(table_hbm, idx_hbm, out_hbm, idx_v, buf0, buf1, g0, g1, w0, w1):
        wid = lax.axis_index("s") * SC_CORES + lax.axis_index("c")
        base = wid * per_worker
        pltpu.sync_copy(idx_hbm.at[pl.ds(base, per_worker)], idx_v)

        @pl.loop(0, n_groups)
        def _(g):
            o0 = pl.multiple_of(g * SC_CHUNK, 8)
            o1 = pl.multiple_of(g * SC_CHUNK + half, 8)
            ga = pltpu.async_copy(table_hbm.at[idx_v.at[pl.ds(o0, half)]], buf0, g0)
            gb = pltpu.async_copy(table_hbm.at[idx_v.at[pl.ds(o1, half)]], buf1, g1)
            ga.wait()
            wa = pltpu.async_copy(buf0, out_hbm.at[pl.ds(base + o0, half)], w0)
            gb.wait()
            wb = pltpu.async_copy(buf1, out_hbm.at[pl.ds(base + o1, half)], w1)
            wa.wait()
            wb.wait()

    return gather(table, idx)


def _sc_chunk(per_worker):
    return max(c for c in range(8, SC_SCATTER_MAX + 1, 8) if per_worker % c == 0)


def _sc_scatter_rows(h, pos):
    n, d = h.shape
    per_worker = n // (SC_CORES * SC_SUBCORES)
    ch = _sc_chunk(per_worker)
    n_chunks = per_worker // ch
    mesh = plsc.VectorSubcoreMesh(core_axis_name="c", subcore_axis_name="s")

    @functools.partial(
        pl.kernel, mesh=mesh,
        out_type=jax.ShapeDtypeStruct((TOP_K * n, d), h.dtype),
        scratch_types=[pltpu.VMEM((ch,), I32)] * TOP_K + [pltpu.VMEM((ch, d), h.dtype)]
                      + [pltpu.SemaphoreType.DMA] * TOP_K,
        name="sc_scatter",
    )
    def scatter(h_hbm, pos_hbm, out_hbm, *scratch):
        idx_v, rows_v, sems = scratch[:TOP_K], scratch[TOP_K], scratch[TOP_K + 1:]
        wid = lax.axis_index("s") * SC_CORES + lax.axis_index("c")
        base = wid * per_worker

        @pl.loop(0, n_chunks)
        def _(c):
            t0 = pl.multiple_of(base + c * ch, 8)
            pltpu.sync_copy(h_hbm.at[pl.ds(t0, ch)], rows_v)
            for k in range(TOP_K):
                pltpu.sync_copy(pos_hbm.at[pl.ds(k * n + t0, ch)], idx_v[k])
            copies = [pltpu.async_copy(rows_v, out_hbm.at[idx_v[k]], sems[k]) for k in range(TOP_K)]
            for cp in copies:
                cp.wait()

    return scatter(h, pos)


def _expert_kernel(vt_ref, ve_ref, lo_ref, hi_ref, x_ref, wu_ref, bu_ref, wd_ref, bd_ref, y_ref,
                   wu16_ref, wd16_ref):
    i = pl.program_id(0)
    ip = jnp.maximum(i - 1, 0)
    e = ve_ref[i]
    prev = ve_ref[ip]
    tile = vt_ref[i]
    first_visit = (i == 0) | (tile != vt_ref[ip])
    lo = lo_ref[i]
    hi = hi_ref[i]
    tm = x_ref.shape[0]
    dff = wd_ref.shape[0]

    @pl.when((i == 0) | (e != prev))
    def _():
        step = 128

        def cast(s, _):
            rows = pl.ds(pl.multiple_of(s * step, step), step)
            wu16_ref[rows, :] = wu_ref[rows, :].astype(BF16)
            return 0

        lax.fori_loop(0, wu_ref.shape[0] // step, cast, 0)

        def cast_d(s, _):
            rows = pl.ds(pl.multiple_of(s * step, step), step)
            wd16_ref[rows, :] = wd_ref[rows, :].astype(BF16)
            return 0

        lax.fori_loop(0, dff // step, cast_d, 0)

    @pl.when(first_visit)
    def _():
        y_ref[...] = jnp.zeros_like(y_ref)

    for s in range(tm // EXPERT_SUB):
        r0 = tile * tm + s * EXPERT_SUB
        rows = slice(s * EXPERT_SUB, (s + 1) * EXPERT_SUB)

        @pl.when((hi > r0) & (lo < r0 + EXPERT_SUB))
        def _():
            x = _unpack_bf16_pairs(lax.bitcast_convert_type(x_ref[rows, :], U32))
            glu = jnp.minimum(_dot(x, wu16_ref[:, :dff]) + bu_ref[:, :dff], SWIGLU_LIMIT)
            lin = jnp.clip(_dot(x, wu16_ref[:, dff:]) + bu_ref[:, dff:], -SWIGLU_LIMIT, SWIGLU_LIMIT)
            act = glu * (1.0 / (1.0 + jnp.exp(-SWIGLU_ALPHA * glu))) * (lin + 1.0)
            y = _dot(act.astype(BF16), wd16_ref[...]) + bd_ref[...]
            y = lax.bitcast_convert_type(_pack_bf16_pairs(y), F32)
            row = r0 + lax.broadcasted_iota(I32, (EXPERT_SUB, 1), 0)
            y_ref[rows, :] = jnp.where((row >= lo) & (row < hi), y, y_ref[rows, :])


def _experts(x_sorted, visits, w_up, b_up, w_down, b_down, layer):
    ns = x_sorted.shape[0]
    tm = EXPERT_TILE
    d, dff = w_down.shape[3], w_down.shape[2]
    bu = b_up.reshape(b_up.shape[0], N_EXPERTS, 1, 2 * dff)
    bd = b_down.reshape(b_down.shape[0], N_EXPERTS, 1, d)
    wmap = lambda i, vt, ve, lo, hi: (layer, ve[i], 0, 0)
    xmap = lambda i, vt, ve, lo, hi: (vt[i], 0)
    grid_spec = pltpu.PrefetchScalarGridSpec(
        num_scalar_prefetch=4,
        grid=(visits[0].shape[0],),
        in_specs=[pl.BlockSpec((tm, x_sorted.shape[1]), xmap),
                  pl.BlockSpec((None, None, d, 2 * dff), wmap),
                  pl.BlockSpec((None, None, 1, 2 * dff), wmap),
                  pl.BlockSpec((None, None, dff, d), wmap),
                  pl.BlockSpec((None, None, 1, d), wmap)],
        out_specs=pl.BlockSpec((tm, d // 2), xmap),
        scratch_shapes=[pltpu.VMEM((d, 2 * dff), BF16), pltpu.VMEM((dff, d), BF16)],
    )
    return pl.pallas_call(
        _expert_kernel,
        grid_spec=grid_spec,
        out_shape=jax.ShapeDtypeStruct((ns, d // 2), F32),
        compiler_params=pltpu.CompilerParams(dimension_semantics=("arbitrary",),
                                             vmem_limit_bytes=VMEM_LIMIT),
        name="experts",
    )(*visits, x_sorted, w_up, bu, w_down, bd)


def _expert_visits(cnt, n_rows):
    tm = EXPERT_TILE
    n_steps = n_rows // tm + N_EXPERTS
    ends = jnp.cumsum(cnt)
    starts = ends - cnt
    first_tile = starts // tm
    n_vis = jnp.where(cnt > 0, (ends - 1) // tm - first_tile + 1, 0)
    vis_end = jnp.cumsum(n_vis)
    vis_start = vis_end - n_vis
    v = jnp.arange(n_steps, dtype=I32)
    vc = jnp.minimum(v, vis_end[-1] - 1)
    onehot = ((vis_start[None, :] <= vc[:, None]) & (vc[:, None] < vis_end[None, :])).astype(I32)
    pick = lambda a: jnp.sum(onehot * a[None, :], axis=1).astype(I32)
    expert = pick(jnp.arange(N_EXPERTS, dtype=I32))
    tile = pick(first_tile) + vc - pick(vis_start)
    real = v < vis_end[-1]
    lo = jnp.where(real, pick(starts), 0).astype(I32)
    hi = jnp.where(real, pick(ends), 0).astype(I32)
    return tile.astype(I32), expert, lo, hi


def _combine_kernel(y0_ref, y1_ref, y2_ref, y3_ref, gate_ref, h_ref, g_ref, b_ref, o_ref, *, alpha):
    gate = gate_ref[...]
    lo = jnp.zeros(y0_ref.shape, F32)
    hi = jnp.zeros(y0_ref.shape, F32)
    for k, y_ref in enumerate((y0_ref, y1_ref, y2_ref, y3_ref)):
        w = lax.bitcast_convert_type(y_ref[...], U32)
        lo = lo + gate[:, k:k + 1] * lax.bitcast_convert_type(w << 16, F32)
        hi = hi + gate[:, k:k + 1] * lax.bitcast_convert_type(w & jnp.uint32(0xFFFF0000), F32)
    acc = jnp.concatenate([lo, hi], axis=1)
    o_ref[...] = _layer_norm(alpha * h_ref[...] + acc, g_ref[...], b_ref[...])


def _combine(y_tok, gates, h, ln_g, ln_b, alpha):
    n, d = h.shape
    tm = _wide_tile(n, COMBINE_TILE_MAX)
    nt = n // tm
    g, b = ln_g.reshape(1, d), ln_b.reshape(1, d)
    full = lambda a: pl.BlockSpec(a.shape, lambda i: (0, 0))
    ysp = lambda k: pl.BlockSpec((tm, d // 2), lambda i: (k * nt + i, 0))
    return pl.pallas_call(
        functools.partial(_combine_kernel, alpha=alpha),
        grid=(nt,),
        in_specs=[ysp(0), ysp(1), ysp(2), ysp(3),
                  pl.BlockSpec((tm, TOP_K), lambda i: (i, 0)),
                  pl.BlockSpec((tm, d), lambda i: (i, 0)), full(g), full(b)],
        out_specs=pl.BlockSpec((tm, d), lambda i: (i, 0)),
        out_shape=jax.ShapeDtypeStruct((n, d), F32),
        compiler_params=pltpu.CompilerParams(dimension_semantics=("arbitrary",),
                                             vmem_limit_bytes=VMEM_LIMIT),
        name="combine",
    )(y_tok, y_tok, y_tok, y_tok, gates, h, g, b)


def _round_up(a, m):
    return (a + m - 1) // m * m


def _wide_tile(n, cap):
    return max(t for t in range(TOKEN_TILE, cap + 1, TOKEN_TILE) if n % t == 0)


def _moe(h, hp, idx, gates, rank, counts, w_up, b_up, w_down, b_down, ln_g, ln_b, alpha, layer):
    n, d = h.shape
    ns = TOP_K * n
    cnt = counts[:, 0]
    starts = jnp.cumsum(cnt) - cnt
    experts = jnp.arange(N_EXPERTS, dtype=I32)
    offs = jnp.sum(jnp.where(idx[:, :, None] == experts, starts, 0), axis=-1)
    pos = (offs + rank).reshape(-1).astype(I32)
    x_sorted = _sc_scatter_rows(hp, pos)
    y_sorted = _experts(x_sorted, _expert_visits(cnt, ns), w_up, b_up, w_down, b_down, layer)
    m2 = _round_up(ns, SC_ROW_ALIGN)
    fill = jnp.arange(ns, m2, dtype=I32) - ns
    y_tok = _sc_gather(y_sorted, jnp.concatenate([pos, fill]))
    return _combine(y_tok, gates.T, h, ln_g, ln_b, alpha)


def kernel(x_prompt, x_sample, cache_k, cache_v, state_conv, state_gla, w_in, conv_w, w_gate, b_gate,
           gla_norm_g, w_out, ln1_g, ln1_b, w_router, b_router, w_up, b_up, w_down, b_down, ln2_g, ln2_b):
    depth = w_in.shape[0]
    bp, seq, d = x_prompt.shape
    bs, ts, _ = x_sample.shape
    past = cache_k.shape[2]
    n_p = bp * seq
    n = n_p + bs * ts
    alpha = float((2 * depth) ** 0.25)
    x = jnp.concatenate([x_prompt.reshape(n_p, d), x_sample.reshape(bs * ts, d)], axis=0)
    ckt = cache_k.transpose(0, 1, 3, 4, 2).reshape(depth, bs, SB_DIM, past)
    cvt = cache_v.transpose(0, 1, 3, 4, 2).reshape(depth, bs, SB_DIM, past)
    zero_conv = jnp.zeros((bp, CONV_W - 1, CONV_DIM), F32)
    zero_gla = jnp.zeros((bp, GLA_HEADS, GLA_DK, GLA_DK), F32)
    wb = w_in.astype(BF16)
    o_r = 3 * SB_DIM
    n_r = 3 * CONV_DIM + 4 * GLA_DIM
    wq, wk, wv = wb[:, :, :SB_DIM], wb[:, :, SB_DIM:2 * SB_DIM], wb[:, :, 2 * SB_DIM:o_r]
    wkt, wvt = wk.transpose(0, 2, 1), wv.transpose(0, 2, 1)
    wr, wal = wb[:, :, o_r:o_r + n_r], wb[:, :, o_r + n_r:]
    wg = w_gate.astype(BF16)
    wo = w_out.astype(BF16)
    kt = jnp.zeros((depth, bp, SB_DIM, seq), F32)
    vt = jnp.zeros((depth, bp, SB_DIM, seq), F32)
    tm_p = min(PROMPT_TILE, seq)
    outs = [[] for _ in range(6)]
    for l in range(depth):
        shared = (wr[l], wal[l], wg[l], b_gate[l].reshape(1, -1))
        q_p, kt, vt, obc, conv_p, gla_p = _mix_prompt(
            x, jnp.zeros((n, CONV_DIM + GLA_DIM), BF16), kt, vt, l, (wq[l], wkt[l], wvt[l]) + shared,
            conv_w[l], gla_norm_g[l], zero_conv, zero_gla, bp, seq, tm_p)
        q_s, ks, vs, rest_s, la_s = _in_proj(x, n_p, n - n_p, TOKEN_TILE, (wq[l], wk[l], wv[l]) + shared)
        oa = _sb_prompt(jnp.zeros((n, SB_DIM), BF16), q_p, kt, vt, l, bp, seq)
        oa = _sb_decode(oa, q_s, ks, vs, ckt, cvt, l, n_p, bs, ts)
        obc, conv_s, gla_s = _conv_gla(obc, n, rest_s, la_s, conv_w[l], gla_norm_g[l], state_conv[l],
                                       state_gla[l], n_p, bs, ts)
        h, hp, idx, gates, rank, counts = _out_router(oa, obc, x, wo[l, :SB_DIM], wo[l, SB_DIM:], ln1_g[l],
                                                      ln1_b[l], w_router[l], b_router[l], alpha)
        x = _moe(h, hp, idx, gates, rank, counts, w_up, b_up, w_down, b_down, ln2_g[l], ln2_b[l],
                 alpha, l)
        outs[0].append(conv_p)
        outs[1].append(gla_p)
        outs[2].append(ks.reshape(bs, ts, SB_HEADS, HEAD_DIM))
        outs[3].append(vs.reshape(bs, ts, SB_HEADS, HEAD_DIM))
        outs[4].append(conv_s)
        outs[5].append(gla_s)
    k_prompt = kt.reshape(depth, bp, SB_HEADS, HEAD_DIM, seq).transpose(0, 1, 4, 2, 3)
    v_prompt = vt.reshape(depth, bp, SB_HEADS, HEAD_DIM, seq).transpose(0, 1, 4, 2, 3)
    st = [jnp.stack(o) for o in outs]
    return (x[:n_p].reshape(bp, seq, d), x[n_p:].reshape(bs, ts, d), k_prompt, v_prompt,
            st[0], st[1], st[2], st[3], st[4], st[5])
```

```python
import functools

import jax
import jax.numpy as jnp
from jax import lax
from jax.experimental import pallas as pl
from jax.experimental.pallas import tpu as pltpu
from jax.experimental.pallas import tpu_sc as plsc

F32 = jnp.float32
BF16 = jnp.bfloat16
I32 = jnp.int32
U32 = jnp.uint32

HEAD_DIM = 64
SB_HEADS = 8
SB_DIM = SB_HEADS * HEAD_DIM
CONV_DIM = 256
CONV_W = 3
GLA_HEADS = 4
GLA_DK = 64
GLA_DIM = GLA_HEADS * GLA_DK
GLA_RANK = 16
GLA_TAU = 16.0
GLA_CHUNK = 64
N_EXPERTS = 32
TOP_K = 4
SWIGLU_LIMIT = 7.0
SWIGLU_ALPHA = 1.702
NORM_EPS = 1e-5

LANES = 128
SC_CORES = 2
SC_SUBCORES = 16
SC_CHUNK = 128
SC_ROW_ALIGN = SC_CORES * SC_SUBCORES * SC_CHUNK
SC_SCATTER_MAX = 104
TOKEN_TILE = 256
WIDE_TILE_MAX = 1280
COMBINE_TILE_MAX = 640
PROMPT_TILE = 512
EXPERT_TILE = 512
EXPERT_SUB = 256
SB_TQ = 256
SB_BK = 256
SB_GROUP = 256
DEC_TK = 512
GLA_TC = 256
VMEM_LIMIT = 48 * 1024 * 1024
SB_DEAD = -100.0


def _dot(a, b):
    return jnp.dot(a, b, preferred_element_type=F32)


def _dot_nt(a, b):
    return lax.dot_general(a, b, (((1,), (1,)), ((), ())), preferred_element_type=F32)


def _split_bf16(x):
    hi = x.astype(BF16)
    lo = (x - hi.astype(F32)).astype(BF16)
    return hi, lo


def _dot_exact_rhs(x, m):
    hi, lo = _split_bf16(x)
    return _dot(hi, m) + _dot(lo, m)


def _dot_exact_lhs(m, x):
    hi, lo = _split_bf16(x)
    return _dot(m, hi) + _dot(m, lo)


def _pack_bf16_pairs(x):
    c = x.shape[1] // 2
    bits = lax.bitcast_convert_type(x.astype(BF16).astype(F32), U32)
    return (bits[:, :c] >> 16) | (bits[:, c:] & jnp.uint32(0xFFFF0000))


def _unpack_bf16_pairs(w):
    lo = lax.bitcast_convert_type(w << 16, F32)
    hi = lax.bitcast_convert_type(w & jnp.uint32(0xFFFF0000), F32)
    return jnp.concatenate([lo, hi], axis=1).astype(BF16)


def _softplus(z):
    return jnp.maximum(z, 0.0) + jnp.log(1.0 + jnp.exp(-jnp.abs(z)))


def _layer_norm(y, g, b):
    mu = jnp.mean(y, axis=-1, keepdims=True)
    yc = y - mu
    var = jnp.mean(yc * yc, axis=-1, keepdims=True)
    return yc * lax.rsqrt(var + NORM_EPS) * g + b


def _strict_upper(n, copies=1):
    r = lax.broadcasted_iota(I32, (copies * n, n), 0)
    c = lax.broadcasted_iota(I32, (copies * n, n), 1)
    for k in range(1, copies):
        r = r - jnp.where(r >= n, n, 0)
    return jnp.where(r > c, 1.0, 0.0).astype(BF16)


def _project(x_ref, wq_ref, wk_ref, wv_ref, wr_ref, wal_ref, wg_ref, bg_ref,
             q_ref, k_ref, v_ref, r_ref, la_ref, *, transposed_kv):
    xb = x_ref[...].astype(BF16)
    for c in range(0, SB_DIM, 256):
        q_ref[:, c:c + 256] = (_dot(xb, wq_ref[:, c:c + 256]) * (HEAD_DIM ** -0.5)).astype(BF16)
    for c in range(0, r_ref.shape[1], 256):
        r_ref[:, c:c + 256] = _dot(xb, wr_ref[:, c:c + 256])
    al = _dot(xb, wal_ref[...])
    g = _dot(al.astype(BF16), wg_ref[...]) + bg_ref[...]
    la_ref[...] = -_softplus(-g) * (1.0 / GLA_TAU)
    for c in range(0, SB_DIM, 256):
        if transposed_kv:
            k_ref[c:c + 256, :] = _dot_nt(wk_ref[c:c + 256, :], xb)
            v_ref[c:c + 256, :] = _dot_nt(wv_ref[c:c + 256, :], xb)
        else:
            k_ref[:, c:c + 256] = _dot(xb, wk_ref[:, c:c + 256])
            v_ref[:, c:c + 256] = _dot(xb, wv_ref[:, c:c + 256])


def _in_proj(x, row0, rows, tm, weights):
    d = x.shape[1]
    n_r = weights[3].shape[1]
    rb = row0 // tm
    full = lambda a: pl.BlockSpec(a.shape, lambda i: (0,) * a.ndim)
    row = lambda w: pl.BlockSpec((tm, w), lambda i: (i, 0))
    sds = lambda w, dt: jax.ShapeDtypeStruct((rows, w), dt)
    return pl.pallas_call(
        functools.partial(_project, transposed_kv=False),
        grid=(rows // tm,),
        in_specs=[pl.BlockSpec((tm, d), lambda i: (rb + i, 0))] + [full(w) for w in weights],
        out_specs=[row(SB_DIM), row(SB_DIM), row(SB_DIM), row(n_r), row(GLA_DIM)],
        out_shape=[sds(SB_DIM, BF16), sds(SB_DIM, F32), sds(SB_DIM, F32), sds(n_r, F32), sds(GLA_DIM, F32)],
        compiler_params=pltpu.CompilerParams(dimension_semantics=("arbitrary",),
                                             vmem_limit_bytes=VMEM_LIMIT),
        name="in_proj",
    )(x, *weights)


def _sb_weights(z, tri, run, mask):
    sp = _softplus(z)
    l1m = -sp
    lsig = z - sp
    if mask is not None:
        l1m = jnp.where(mask, l1m, 0.0)
    if tri.shape[0] == 2 * z.shape[1]:
        hi = lax.bitcast_convert_type(lax.bitcast_convert_type(l1m, U32) & jnp.uint32(0xFFFF0000), F32)
        parts = jnp.concatenate([hi.astype(BF16), (l1m - hi).astype(BF16)], axis=1)
        rest = _dot(parts, tri) + run
    else:
        rest = _dot_exact_rhs(l1m, tri) + run
    a = jnp.exp(lsig + rest)
    if mask is not None:
        a = jnp.where(mask, a, 0.0)
    return a.astype(BF16), run + jnp.sum(l1m, axis=1, keepdims=True)


def _sb_prompt_kernel(oa_ref, q_ref, kt_ref, vt_ref, o_ref, *, tq, bk):
    del oa_ref
    qi = pl.program_id(2)
    q = q_ref[...]
    n_hh = SB_GROUP // HEAD_DIM
    lane = lax.broadcasted_iota(I32, (1, SB_GROUP), 1)
    in_head = [(lane // HEAD_DIM) == h for h in range(n_hh)]
    qh = [jnp.where(m, q, jnp.zeros_like(q)) for m in in_head]
    tri = _strict_upper(bk, copies=2)
    n_full = (qi * tq) // bk
    qpos = qi * tq + lax.broadcasted_iota(I32, (tq, bk), 0)
    kpos = n_full * bk + lax.broadcasted_iota(I32, (tq, bk), 1)
    diag_mask = kpos < qpos

    def tile(jb, runs, mask):
        ks = pl.multiple_of(jb * bk, bk)
        kt = kt_ref[:, pl.ds(ks, bk)].astype(BF16)
        vt = vt_ref[:, pl.ds(ks, bk)].astype(BF16)
        out = jnp.zeros((tq, SB_GROUP), F32)
        new_runs = []
        for h in range(n_hh):
            a, run = _sb_weights(_dot(qh[h], kt), tri, runs[h], mask)
            out = jnp.where(in_head[h], _dot_nt(a, vt), out)
            new_runs.append(run)
        return out, tuple(new_runs)

    def alive_of(runs):
        m = jnp.max(runs[0])
        for r in runs[1:]:
            m = jnp.maximum(m, jnp.max(r))
        return m > SB_DEAD

    acc, runs = tile(n_full, tuple(jnp.zeros((tq, 1), F32) for _ in range(n_hh)), diag_mask)

    def cond(carry):
        j, alive, _, _ = carry
        return (j >= 0) & alive

    def body(carry):
        j, _, acc, runs = carry
        pv, runs = tile(j, runs, None)
        return j - 1, alive_of(runs), acc + pv, runs

    _, _, acc, _ = lax.while_loop(cond, body, (n_full - 1, alive_of(runs), acc, runs))
    o_ref[...] = acc.astype(o_ref.dtype)


def _sb_prompt(oa, q, kt, vt, layer, batch, seq):
    tq, bk = min(SB_TQ, seq), min(SB_BK, seq)
    nq = seq // tq
    hp = SB_DIM // SB_GROUP
    kv_spec = pl.BlockSpec((None, None, SB_GROUP, seq), lambda b, p, i: (layer, b, p, 0))
    return pl.pallas_call(
        functools.partial(_sb_prompt_kernel, tq=tq, bk=bk),
        grid=(batch, hp, nq),
        in_specs=[pl.BlockSpec(memory_space=pl.ANY),
                  pl.BlockSpec((tq, SB_GROUP), lambda b, p, i: (b * nq + i, p)), kv_spec, kv_spec],
        out_specs=pl.BlockSpec((tq, SB_GROUP), lambda b, p, i: (b * nq + i, p)),
        out_shape=jax.ShapeDtypeStruct(oa.shape, oa.dtype),
        input_output_aliases={0: 0},
        compiler_params=pltpu.CompilerParams(
            dimension_semantics=("arbitrary", "arbitrary", "arbitrary"),
            vmem_limit_bytes=VMEM_LIMIT),
        name="sb_prompt",
    )(oa, q, kt, vt)


def _sb_decode_kernel(oa_ref, q_ref, kn_ref, vn_ref, kc_hbm, vc_hbm, o_ref, kbuf, vbuf, sem, acc_ref, run_ref,
                      alive_ref, *, t, tk, bk, layer, nkb):
    del oa_ref
    b = pl.program_id(0)
    q = q_ref[...]

    def block_copies(j, slot):
        cols = pl.ds(pl.multiple_of((nkb - 1 - j) * tk, tk), tk)
        return (pltpu.make_async_copy(kc_hbm.at[layer, b, :, cols], kbuf.at[slot], sem.at[0, slot]),
                pltpu.make_async_copy(vc_hbm.at[layer, b, :, cols], vbuf.at[slot], sem.at[1, slot]))

    for cp in block_copies(0, 0):
        cp.start()

    kn = kn_ref[...].astype(BF16)
    vn = vn_ref[...].astype(BF16)
    lane = lax.broadcasted_iota(I32, (1, SB_DIM), 1)
    r = lax.broadcasted_iota(I32, (t, t), 0)
    c = lax.broadcasted_iota(I32, (t, t), 1)
    mask = c < r
    tri_new = _strict_upper(t)
    for h in range(SB_HEADS):
        qh = jnp.where((lane // HEAD_DIM) == h, q, jnp.zeros_like(q))
        a, run = _sb_weights(_dot_nt(qh, kn), tri_new, jnp.zeros((t, 1), F32), mask)
        acc_ref[h] = _dot(a, vn)[:, h * HEAD_DIM:(h + 1) * HEAD_DIM]
        run_ref[h * t:(h + 1) * t, :] = run
    alive_ref[0] = (jnp.max(run_ref[...]) > SB_DEAD).astype(I32)

    tri = _strict_upper(bk, copies=2)

    def cond(carry):
        j, alive = carry
        return (j < nkb) & (alive > 0)

    def body(carry):
        j, _ = carry
        slot = j % 2

        @pl.when(j + 1 < nkb)
        def _():
            for cp in block_copies(j + 1, 1 - slot):
                cp.start()

        for cp in block_copies(j, slot):
            cp.wait()
        for c in range(tk // bk - 1, -1, -1):
            cols = slice(c * bk, (c + 1) * bk)

            @pl.when(alive_ref[0] > 0)
            def _():
                z = jnp.concatenate(
                    [_dot(q[:, h * HEAD_DIM:(h + 1) * HEAD_DIM],
                          kbuf[slot, h * HEAD_DIM:(h + 1) * HEAD_DIM, cols].astype(BF16))
                     for h in range(SB_HEADS)], axis=0)
                a, run = _sb_weights(z, tri, run_ref[...], None)
                for h in range(SB_HEADS):
                    vt = vbuf[slot, h * HEAD_DIM:(h + 1) * HEAD_DIM, cols].astype(BF16)
                    acc_ref[h] = acc_ref[h] + _dot_nt(a[h * t:(h + 1) * t, :], vt)
                run_ref[...] = run
                alive_ref[0] = (jnp.max(run) > SB_DEAD).astype(I32)

        return j + 1, alive_ref[0]

    j_end, _ = lax.while_loop(cond, body, (jnp.int32(0), alive_ref[0]))

    @pl.when(j_end < nkb)
    def _():
        for cp in block_copies(j_end, j_end % 2):
            cp.wait()

    for h in range(SB_HEADS):
        o_ref[:, h * HEAD_DIM:(h + 1) * HEAD_DIM] = acc_ref[h].astype(o_ref.dtype)


def _sb_decode(oa, q, ks, vs, cache_kt, cache_vt, layer, row0, batch, t):
    past = cache_kt.shape[3]
    tk = min(DEC_TK, past)
    bk = min(SB_BK, tk)
    nkb = past // tk
    rb = row0 // t
    new = pl.BlockSpec((t, SB_DIM), lambda b: (b, 0))
    anywhere = pl.BlockSpec(memory_space=pl.ANY)
    return pl.pallas_call(
        functools.partial(_sb_decode_kernel, t=t, tk=tk, bk=bk, layer=layer, nkb=nkb),
        grid=(batch,),
        in_specs=[anywhere, new, new, new, anywhere, anywhere],
        out_specs=pl.BlockSpec((t, SB_DIM), lambda b: (rb + b, 0)),
        out_shape=jax.ShapeDtypeStruct(oa.shape, oa.dtype),
        input_output_aliases={0: 0},
        scratch_shapes=[pltpu.VMEM((2, SB_DIM, tk), F32),
                        pltpu.VMEM((2, SB_DIM, tk), F32),
                        pltpu.SemaphoreType.DMA((2, 2)),
                        pltpu.VMEM((SB_HEADS, t, HEAD_DIM), F32),
                        pltpu.VMEM((SB_HEADS * t, 1), F32),
                        pltpu.SMEM((1,), I32)],
        compiler_params=pltpu.CompilerParams(dimension_semantics=("arbitrary",),
                                             vmem_limit_bytes=VMEM_LIMIT),
        name="sb_decode",
    )(oa, q, ks, vs, cache_kt, cache_vt)


def _mix_state_load(cprev_ref, sprev_ref, ctail_ref, st_ref):
    n_hh = LANES // GLA_DK
    ctail_ref[...] = cprev_ref[...]
    for p in range(GLA_DIM // LANES):
        st_ref[p] = jnp.zeros((LANES, LANES), F32)
        for hh in range(n_hh):
            st_ref[p, hh * GLA_DK:(hh + 1) * GLA_DK, hh * GLA_DK:(hh + 1) * GLA_DK] = sprev_ref[p * n_hh + hh].T


def _mix_state_store(ctail_ref, st_ref, cnew_ref, snew_ref):
    n_hh = LANES // GLA_DK
    cnew_ref[...] = ctail_ref[...]
    for p in range(GLA_DIM // LANES):
        for hh in range(n_hh):
            blk = st_ref[p, hh * GLA_DK:(hh + 1) * GLA_DK, :]
            snew_ref[p * n_hh + hh] = blk.T[hh * GLA_DK:(hh + 1) * GLA_DK, :]


def _conv_gla_block(bg_ref, cg_ref, u_ref, qc_ref, kc_ref, vc_ref, gc_ref, la_ref, cw_ref, ng_ref,
                    o_ref, ctail_ref, st_ref, tc, chunk):
    n_pair = GLA_DIM // LANES
    n_hh = LANES // GLA_DK

    z = cg_ref[...] * u_ref[...]
    tail = ctail_ref[...]
    row = lax.broadcasted_iota(I32, z.shape, 0)
    z1 = jnp.where(row < 1, tail[1:2, :], pltpu.roll(z, 1, 0))
    z2 = jnp.where(row < 2, jnp.where(row < 1, tail[0:1, :], tail[1:2, :]), pltpu.roll(z, 2, 0))
    cw = cw_ref[...]
    y = z2 * cw[0:1, :] + z1 * cw[1:2, :] + z * cw[2:3, :]
    o_ref[:, 0:CONV_DIM] = (bg_ref[...] * y).astype(o_ref.dtype)
    ctail_ref[...] = z[tc - 2:tc, :]

    r = lax.broadcasted_iota(I32, (tc, tc), 0)
    c = lax.broadcasted_iota(I32, (tc, tc), 1)
    same_chunk = (r // chunk) == (c // chunk)
    chunk_sum = jnp.where(same_chunk, 1.0, 0.0).astype(BF16)
    chunk_cumsum = jnp.where(same_chunk & (c <= r), 1.0, 0.0).astype(BF16)
    lane = lax.broadcasted_iota(I32, (1, LANES), 1)
    lr = lax.broadcasted_iota(I32, (LANES, LANES), 0) // GLA_DK
    lc = lax.broadcasted_iota(I32, (LANES, LANES), 1) // GLA_DK
    same_head = lr == lc
    head_mean = jnp.where(same_head, 1.0 / GLA_DK, 0.0).astype(BF16)
    r2 = lax.broadcasted_iota(I32, (tc, n_hh * tc), 0)
    c2 = lax.broadcasted_iota(I32, (tc, n_hh * tc), 1)
    c2 = c2 - jnp.where(c2 >= tc, tc, 0)
    causal2 = ((r2 // chunk) == (c2 // chunk)) & (c2 <= r2)
    for p in range(n_pair):
        cols = slice(p * LANES, (p + 1) * LANES)
        la = la_ref[:, cols]
        b = _dot_exact_lhs(chunk_cumsum, la)
        tot = _dot_exact_lhs(chunk_sum, la)
        k = kc_ref[:, cols]
        qe = (qc_ref[:, cols] * (GLA_DK ** -0.5) * jnp.exp(b)).astype(BF16)
        ke = (k * jnp.exp(-b)).astype(BF16)
        kd = (k * jnp.exp(tot - b)).astype(BF16)
        decay = jnp.exp(tot)
        vb = vc_ref[:, cols].astype(BF16)
        zeros = jnp.zeros_like(ke)
        ke_st = jnp.concatenate([jnp.where((lane // GLA_DK) == hh, ke, zeros)
                                 for hh in range(n_hh)], axis=0)
        v_st = jnp.concatenate([jnp.where((lane // GLA_DK) == hh, vb, zeros)
                                for hh in range(n_hh)], axis=0)
        a = jnp.where(causal2, _dot_nt(qe, ke_st), 0.0)
        o = _dot(a.astype(BF16), v_st)
        st = st_ref[p]
        from_state = []
        for ci in range(tc // chunk):
            rows = slice(ci * chunk, (ci + 1) * chunk)
            from_state.append(_dot_nt(qe[rows], st.astype(BF16)))
            upd = _dot(vb[rows].T, kd[rows])
            st = jnp.where(same_head, st * decay[ci * chunk:ci * chunk + 1, :] + upd, 0.0)
        st_ref[p] = st
        o = o + jnp.concatenate(from_state, axis=0)
        ms = _dot_exact_rhs(o * o, head_mean)
        o = o * lax.rsqrt(ms + NORM_EPS) * ng_ref[:, cols]
        g = gc_ref[:, cols]
        o = o * (g * (1.0 / (1.0 + jnp.exp(-g))))
        o_ref[:, CONV_DIM + p * LANES:CONV_DIM + (p + 1) * LANES] = o.astype(o_ref.dtype)


def _conv_gla_kernel(*refs, tc, chunk):
    (_, bg_ref, cg_ref, u_ref, qc_ref, kc_ref, vc_ref, gc_ref, la_ref, cw_ref, ng_ref, cprev_ref, sprev_ref,
     o_ref, cnew_ref, snew_ref, ctail_ref, st_ref) = refs
    ti = pl.program_id(1)

    @pl.when(ti == 0)
    def _():
        _mix_state_load(cprev_ref, sprev_ref, ctail_ref, st_ref)

    _conv_gla_block(bg_ref, cg_ref, u_ref, qc_ref, kc_ref, vc_ref, gc_ref, la_ref, cw_ref, ng_ref,
                    o_ref, ctail_ref, st_ref, tc, chunk)

    @pl.when(ti == pl.num_programs(1) - 1)
    def _():
        _mix_state_store(ctail_ref, st_ref, cnew_ref, snew_ref)


def _conv_gla(obc_prev, n, rest, la, conv_w, norm_g, conv_prev, gla_prev, row0, batch, t):
    tc = min(GLA_TC, t)
    chunk = min(GLA_CHUNK, t)
    nt = t // tc
    rb = row0 // tc
    col = lambda j: pl.BlockSpec((tc, CONV_DIM), lambda b, i: (b * nt + i, j))
    const2 = lambda a: pl.BlockSpec(a.shape, lambda b, i: (0, 0))
    ng = norm_g.reshape(1, GLA_DIM)
    args = (rest,) * 7 + (la, conv_w, ng, conv_prev, gla_prev)
    return pl.pallas_call(
        functools.partial(_conv_gla_kernel, tc=tc, chunk=chunk),
        grid=(batch, nt),
        in_specs=[
            pl.BlockSpec(memory_space=pl.ANY), col(0), col(1), col(2), col(3), col(4), col(5), col(6),
            pl.BlockSpec((tc, GLA_DIM), lambda b, i: (b * nt + i, 0)),
            const2(conv_w), const2(ng),
            pl.BlockSpec((None, CONV_W - 1, CONV_DIM), lambda b, i: (b, 0, 0)),
            pl.BlockSpec((None, GLA_HEADS, GLA_DK, GLA_DK), lambda b, i: (b, 0, 0, 0))],
        out_specs=[pl.BlockSpec((tc, CONV_DIM + GLA_DIM), lambda b, i: (rb + b * nt + i, 0)),
                   pl.BlockSpec((None, CONV_W - 1, CONV_DIM), lambda b, i: (b, 0, 0)),
                   pl.BlockSpec((None, GLA_HEADS, GLA_DK, GLA_DK), lambda b, i: (b, 0, 0, 0))],
        out_shape=[jax.ShapeDtypeStruct((n, CONV_DIM + GLA_DIM), BF16),
                   jax.ShapeDtypeStruct((batch, CONV_W - 1, CONV_DIM), F32),
                   jax.ShapeDtypeStruct((batch, GLA_HEADS, GLA_DK, GLA_DK), F32)],
        input_output_aliases={0: 0},
        scratch_shapes=[pltpu.VMEM((CONV_W - 1, CONV_DIM), F32),
                        pltpu.VMEM((GLA_DIM // LANES, LANES, LANES), F32)],
        compiler_params=pltpu.CompilerParams(dimension_semantics=("arbitrary", "arbitrary"),
                                             vmem_limit_bytes=VMEM_LIMIT),
        name="conv_gla",
    )(obc_prev, *args)


def _mix_prompt_kernel(*refs, tc, chunk):
    (_, _, _, x_ref, wq_ref, wk_ref, wv_ref, wr_ref, wal_ref, wg_ref, bg_ref, cw_ref, ng_ref, cprev_ref,
     sprev_ref, q_ref, kt_ref, vt_ref, o_ref, cnew_ref, snew_ref, r_scr, la_scr, ctail_ref, st_ref) = refs
    ti = pl.program_id(1)
    _project(x_ref, wq_ref, wk_ref, wv_ref, wr_ref, wal_ref, wg_ref, bg_ref,
             q_ref, kt_ref, vt_ref, r_scr, la_scr, transposed_kv=True)

    @pl.when(ti == 0)
    def _():
        _mix_state_load(cprev_ref, sprev_ref, ctail_ref, st_ref)

    for s in range(x_ref.shape[0] // tc):
        rows = pl.ds(s * tc, tc)
        part = [r_scr.at[rows, pl.ds(j * CONV_DIM, CONV_DIM)] for j in range(7)]
        _conv_gla_block(*part, la_scr.at[rows], cw_ref, ng_ref, o_ref.at[rows], ctail_ref, st_ref, tc, chunk)

    @pl.when(ti == pl.num_programs(1) - 1)
    def _():
        _mix_state_store(ctail_ref, st_ref, cnew_ref, snew_ref)


def _mix_prompt(x, obc_prev, kt, vt, layer, weights, conv_w, norm_g, conv_prev, gla_prev, batch, seq, tm):
    d = x.shape[1]
    n_p = batch * seq
    nt = seq // tm
    tc = min(GLA_TC, tm)
    wq, wk, wv, wr, wal, wg, bg = weights
    ng = norm_g.reshape(1, GLA_DIM)
    full = lambda a: pl.BlockSpec(a.shape, lambda b, i: (0,) * a.ndim)
    row = lambda w: pl.BlockSpec((tm, w), lambda b, i: (b * nt + i, 0))
    kv_spec = pl.BlockSpec((None, None, SB_DIM, tm), lambda b, i: (layer, b, 0, i))
    conv_spec = pl.BlockSpec((None, CONV_W - 1, CONV_DIM), lambda b, i: (b, 0, 0))
    gla_spec = pl.BlockSpec((None, GLA_HEADS, GLA_DK, GLA_DK), lambda b, i: (b, 0, 0, 0))
    anywhere = pl.BlockSpec(memory_space=pl.ANY)
    consts = (wq, wk, wv, wr, wal, wg, bg, conv_w, ng)
    return pl.pallas_call(
        functools.partial(_mix_prompt_kernel, tc=tc, chunk=min(GLA_CHUNK, tc)),
        grid=(batch, nt),
        in_specs=[anywhere, anywhere, anywhere, row(d)] + [full(w) for w in consts] + [conv_spec, gla_spec],
        out_specs=[row(SB_DIM), kv_spec, kv_spec, row(CONV_DIM + GLA_DIM), conv_spec, gla_spec],
        out_shape=[jax.ShapeDtypeStruct((n_p, SB_DIM), BF16),
                   jax.ShapeDtypeStruct(kt.shape, kt.dtype), jax.ShapeDtypeStruct(vt.shape, vt.dtype),
                   jax.ShapeDtypeStruct(obc_prev.shape, obc_prev.dtype),
                   jax.ShapeDtypeStruct((batch, CONV_W - 1, CONV_DIM), F32),
                   jax.ShapeDtypeStruct((batch, GLA_HEADS, GLA_DK, GLA_DK), F32)],
        input_output_aliases={0: 1, 1: 2, 2: 3},
        scratch_shapes=[pltpu.VMEM((tm, wr.shape[1]), F32),
                        pltpu.VMEM((tm, GLA_DIM), F32),
                        pltpu.VMEM((CONV_W - 1, CONV_DIM), F32),
                        pltpu.VMEM((GLA_DIM // LANES, LANES, LANES), F32)],
        compiler_params=pltpu.CompilerParams(dimension_semantics=("arbitrary", "arbitrary"),
                                             vmem_limit_bytes=VMEM_LIMIT),
        name="mix_prompt",
    )(kt, vt, obc_prev, x, *consts, conv_prev, gla_prev)


def _out_router_kernel(oa_ref, obc_ref, x_ref, wa_ref, wb_ref, g_ref, b_ref, wr_ref, br_ref,
                       h_ref, hp_ref, idx_ref, gate_ref, rank_ref, cnt_ref, carry_ref, *, alpha):
    i = pl.program_id(0)
    tm = x_ref.shape[0]

    @pl.when(i == 0)
    def _():
        carry_ref[...] = jnp.zeros_like(carry_ref)

    m = _dot(oa_ref[...], wa_ref[...]) + _dot(obc_ref[...], wb_ref[...])
    h = _layer_norm(alpha * x_ref[...] + m, g_ref[...], b_ref[...])
    h_ref[...] = h
    hp_ref[...] = lax.bitcast_convert_type(_pack_bf16_pairs(h), F32)

    h_hi, h_lo = _split_bf16(h)
    w_hi, w_lo = _split_bf16(wr_ref[...])
    logit = _dot_nt(w_hi, h_hi) + _dot_nt(w_hi, h_lo) + _dot_nt(w_lo, h_hi) + br_ref[...]
    eid = lax.broadcasted_iota(I32, (N_EXPERTS, tm), 0)
    r = lax.broadcasted_iota(I32, (tm, tm), 0)
    c = lax.broadcasted_iota(I32, (tm, tm), 1)
    before = jnp.where(r < c, 1.0, 0.0).astype(BF16)
    base = carry_ref[...]
    vals, idxs, onehots, bases = [], [], [], []
    for _ in range(TOP_K):
        mx = jnp.max(logit, axis=0, keepdims=True)
        sel = jnp.min(jnp.where(logit == mx, eid, N_EXPERTS), axis=0, keepdims=True)
        hit = eid == sel
        logit = jnp.where(hit, -jnp.inf, logit)
        onehot = jnp.where(hit, 1.0, 0.0)
        onehots.append(onehot)
        bases.append(base)
        base = base + jnp.sum(onehot, axis=1, keepdims=True)
        vals.append(mx)
        idxs.append(sel)
    carry_ref[...] = base
    earlier = _dot(jnp.concatenate(onehots, axis=0).astype(BF16), before)
    ranks = [jnp.sum(onehots[k] * (earlier[k * N_EXPERTS:(k + 1) * N_EXPERTS] + bases[k]),
                     axis=0, keepdims=True) for k in range(TOP_K)]
    e = [jnp.exp(v - vals[0]) for v in vals]
    inv = 1.0 / (e[0] + e[1] + e[2] + e[3])
    idx_ref[...] = jnp.concatenate(idxs, axis=0)
    gate_ref[...] = jnp.concatenate([ek * inv for ek in e], axis=0)
    rank_ref[...] = jnp.concatenate(ranks, axis=0).astype(I32)
    cnt_ref[...] = jnp.broadcast_to(base, cnt_ref.shape).astype(I32)


def _out_router(oa, obc, x, wa, wb, ln_g, ln_b, w_router, b_router, alpha):
    n, d = x.shape
    tm = _wide_tile(n, WIDE_TILE_MAX)
    wr = w_router.T
    br = b_router.reshape(N_EXPERTS, 1)
    g, b = ln_g.reshape(1, d), ln_b.reshape(1, d)
    full = lambda a: pl.BlockSpec(a.shape, lambda i: (0, 0))
    row = lambda w: pl.BlockSpec((tm, w), lambda i: (i, 0))
    colb = pl.BlockSpec((TOP_K, tm), lambda i: (0, i))
    return pl.pallas_call(
        functools.partial(_out_router_kernel, alpha=alpha),
        grid=(n // tm,),
        in_specs=[row(SB_DIM), row(CONV_DIM + GLA_DIM), row(d), full(wa), full(wb), full(g), full(b),
                  full(wr), full(br)],
        out_specs=[row(d), row(d // 2), colb, colb, colb,
                   pl.BlockSpec((N_EXPERTS, LANES), lambda i: (0, 0))],
        out_shape=[jax.ShapeDtypeStruct((n, d), F32),
                   jax.ShapeDtypeStruct((n, d // 2), F32),
                   jax.ShapeDtypeStruct((TOP_K, n), I32),
                   jax.ShapeDtypeStruct((TOP_K, n), F32),
                   jax.ShapeDtypeStruct((TOP_K, n), I32),
                   jax.ShapeDtypeStruct((N_EXPERTS, LANES), I32)],
        scratch_shapes=[pltpu.VMEM((N_EXPERTS, 1), F32)],
        compiler_params=pltpu.CompilerParams(dimension_semantics=("arbitrary",),
                                             vmem_limit_bytes=VMEM_LIMIT),
        name="out_router",
    )(oa, obc, x, wa, wb, g, b, wr, br)


def _sc_gather(table, idx):
    m = idx.shape[0]
    d = table.shape[1]
    per_worker = m // (SC_CORES * SC_SUBCORES)
    half = SC_CHUNK // 2
    n_groups = per_worker // SC_CHUNK
    mesh = plsc.VectorSubcoreMesh(core_axis_name="c", subcore_axis_name="s")

    @functools.partial(
        pl.kernel, mesh=mesh,
        out_type=jax.ShapeDtypeStruct((m, d), table.dtype),
        scratch_types=[pltpu.VMEM((per_worker,), I32)] + [pltpu.VMEM((half, d), table.dtype)] * 2
                      + [pltpu.SemaphoreType.DMA] * 4,
        name="sc_gather",
    )
    def gather(table_hbm, idx_hbm, out_hbm, idx_v, buf0, buf1, g0, g1, w0, w1):
        wid = lax.axis_index("s") * SC_CORES + lax.axis_index("c")
        base = wid * per_worker
        pltpu.sync_copy(idx_hbm.at[pl.ds(base, per_worker)], idx_v)

        @pl.loop(0, n_groups)
        def _(g):
            o0 = pl.multiple_of(g * SC_CHUNK, 8)
            o1 = pl.multiple_of(g * SC_CHUNK + half, 8)
            ga = pltpu.async_copy(table_hbm.at[idx_v.at[pl.ds(o0, half)]], buf0, g0)
            gb = pltpu.async_copy(table_hbm.at[idx_v.at[pl.ds(o1, half)]], buf1, g1)
            ga.wait()
            wa = pltpu.async_copy(buf0, out_hbm.at[pl.ds(base + o0, half)], w0)
            gb.wait()
            wb = pltpu.async_copy(buf1, out_hbm.at[pl.ds(base + o1, half)], w1)
            wa.wait()
            wb.wait()

    return gather(table, idx)


def _sc_chunk(per_worker):
    return max(c for c in range(8, SC_SCATTER_MAX + 1, 8) if per_worker % c == 0)


def _sc_scatter_rows(h, pos):
    n, d = h.shape
    per_worker = n // (SC_CORES * SC_SUBCORES)
    ch = _sc_chunk(per_worker)
    n_chunks = per_worker // ch
    mesh = plsc.VectorSubcoreMesh(core_axis_name="c", subcore_axis_name="s")

    @functools.partial(
        pl.kernel, mesh=mesh,
        out_type=jax.ShapeDtypeStruct((TOP_K * n, d), h.dtype),
        scratch_types=[pltpu.VMEM((ch,), I32)] * TOP_K + [pltpu.VMEM((ch, d), h.dtype)]
                      + [pltpu.SemaphoreType.DMA] * TOP_K,
        name="sc_scatter",
    )
    def scatter(h_hbm, pos_hbm, out_hbm, *scratch):
        idx_v, rows_v, sems = scratch[:TOP_K], scratch[TOP_K], scratch[TOP_K + 1:]
        wid = lax.axis_index("s") * SC_CORES + lax.axis_index("c")
        base = wid * per_worker

        @pl.loop(0, n_chunks)
        def _(c):
            t0 = pl.multiple_of(base + c * ch, 8)
            pltpu.sync_copy(h_hbm.at[pl.ds(t0, ch)], rows_v)
            for k in range(TOP_K):
                pltpu.sync_copy(pos_hbm.at[pl.ds(k * n + t0, ch)], idx_v[k])
            copies = [pltpu.async_copy(rows_v, out_hbm.at[idx_v[k]], sems[k]) for k in range(TOP_K)]
            for cp in copies:
                cp.wait()

    return scatter(h, pos)


def _expert_kernel(vt_ref, ve_ref, lo_ref, hi_ref, x_ref, wu_ref, bu_ref, wd_ref, bd_ref, y_ref,
                   wu16_ref, wd16_ref):
    i = pl.program_id(0)
    ip = jnp.maximum(i - 1, 0)
    e = ve_ref[i]
    prev = ve_ref[ip]
    tile = vt_ref[i]
    first_visit = (i == 0) | (tile != vt_ref[ip])
    lo = lo_ref[i]
    hi = hi_ref[i]
    tm = x_ref.shape[0]
    dff = wd_ref.shape[0]

    @pl.when((i == 0) | (e != prev))
    def _():
        step = 128

        def cast(s, _):
            rows = pl.ds(pl.multiple_of(s * step, step), step)
            wu16_ref[rows, :] = wu_ref[rows, :].astype(BF16)
            return 0

        lax.fori_loop(0, wu_ref.shape[0] // step, cast, 0)

        def cast_d(s, _):
            rows = pl.ds(pl.multiple_of(s * step, step), step)
            wd16_ref[rows, :] = wd_ref[rows, :].astype(BF16)
            return 0

        lax.fori_loop(0, dff // step, cast_d, 0)

    @pl.when(first_visit)
    def _():
        y_ref[...] = jnp.zeros_like(y_ref)

    def ffn(rows):
        x = _unpack_bf16_pairs(lax.bitcast_convert_type(x_ref[rows, :], U32))
        glu = jnp.minimum(_dot(x, wu16_ref[:, :dff]) + bu_ref[:, :dff], SWIGLU_LIMIT)
        lin = jnp.clip(_dot(x, wu16_ref[:, dff:]) + bu_ref[:, dff:], -SWIGLU_LIMIT, SWIGLU_LIMIT)
        act = glu * (1.0 / (1.0 + jnp.exp(-SWIGLU_ALPHA * glu))) * (lin + 1.0)
        y = _dot(act.astype(BF16), wd16_ref[...]) + bd_ref[...]
        return lax.bitcast_convert_type(_pack_bf16_pairs(y), F32)

    subs = [slice(s * EXPERT_SUB, (s + 1) * EXPERT_SUB) for s in range(tm // EXPERT_SUB)]
    whole = (lo <= tile * tm) & (hi >= tile * tm + tm)

    @pl.when(whole)
    def _():
        for rows in subs:
            y_ref[rows, :] = ffn(rows)

    for s, rows in enumerate(subs):
        r0 = tile * tm + s * EXPERT_SUB

        @pl.when(jnp.logical_not(whole) & (hi > r0) & (lo < r0 + EXPERT_SUB))
        def _():
            row = r0 + lax.broadcasted_iota(I32, (EXPERT_SUB, 1), 0)
            y_ref[rows, :] = jnp.where((row >= lo) & (row < hi), ffn(rows), y_ref[rows, :])


def _experts(x_sorted, visits, w_up, b_up, w_down, b_down, layer):
    ns = x_sorted.shape[0]
    tm = EXPERT_TILE
    d, dff = w_down.shape[3], w_down.shape[2]
    bu = b_up.reshape(b_up.shape[0], N_EXPERTS, 1, 2 * dff)
    bd = b_down.reshape(b_down.shape[0], N_EXPERTS, 1, d)
    wmap = lambda i, vt, ve, lo, hi: (layer, ve[i], 0, 0)
    xmap = lambda i, vt, ve, lo, hi: (vt[i], 0)
    grid_spec = pltpu.PrefetchScalarGridSpec(
        num_scalar_prefetch=4,
        grid=(visits[0].shape[0],),
        in_specs=[pl.BlockSpec((tm, x_sorted.shape[1]), xmap),
                  pl.BlockSpec((None, None, d, 2 * dff), wmap),
                  pl.BlockSpec((None, None, 1, 2 * dff), wmap),
                  pl.BlockSpec((None, None, dff, d), wmap),
                  pl.BlockSpec((None, None, 1, d), wmap)],
        out_specs=pl.BlockSpec((tm, d // 2), xmap),
        scratch_shapes=[pltpu.VMEM((d, 2 * dff), BF16), pltpu.VMEM((dff, d), BF16)],
    )
    return pl.pallas_call(
        _expert_kernel,
        grid_spec=grid_spec,
        out_shape=jax.ShapeDtypeStruct((ns, d // 2), F32),
        compiler_params=pltpu.CompilerParams(dimension_semantics=("arbitrary",),
                                             vmem_limit_bytes=VMEM_LIMIT),
        name="experts",
    )(*visits, x_sorted, w_up, bu, w_down, bd)


def _expert_visits(cnt, n_rows):
    tm = EXPERT_TILE
    n_steps = n_rows // tm + N_EXPERTS
    ends = jnp.cumsum(cnt)
    starts = ends - cnt
    first_tile = starts // tm
    n_vis = jnp.where(cnt > 0, (ends - 1) // tm - first_tile + 1, 0)
    vis_end = jnp.cumsum(n_vis)
    vis_start = vis_end - n_vis
    v = jnp.arange(n_steps, dtype=I32)
    vc = jnp.minimum(v, vis_end[-1] - 1)
    onehot = ((vis_start[None, :] <= vc[:, None]) & (vc[:, None] < vis_end[None, :])).astype(I32)
    pick = lambda a: jnp.sum(onehot * a[None, :], axis=1).astype(I32)
    expert = pick(jnp.arange(N_EXPERTS, dtype=I32))
    tile = pick(first_tile) + vc - pick(vis_start)
    real = v < vis_end[-1]
    lo = jnp.where(real, pick(starts), 0).astype(I32)
    hi = jnp.where(real, pick(ends), 0).astype(I32)
    return tile.astype(I32), expert, lo, hi


def _combine_kernel(y0_ref, y1_ref, y2_ref, y3_ref, gate_ref, h_ref, g_ref, b_ref, *o_refs, alpha, first_tiles):
    gate = gate_ref[...]
    lo = jnp.zeros(y0_ref.shape, F32)
    hi = jnp.zeros(y0_ref.shape, F32)
    for k, y_ref in enumerate((y0_ref, y1_ref, y2_ref, y3_ref)):
        w = lax.bitcast_convert_type(y_ref[...], U32)
        lo = lo + gate[:, k:k + 1] * lax.bitcast_convert_type(w << 16, F32)
        hi = hi + gate[:, k:k + 1] * lax.bitcast_convert_type(w & jnp.uint32(0xFFFF0000), F32)
    acc = jnp.concatenate([lo, hi], axis=1)
    out = _layer_norm(alpha * h_ref[...] + acc, g_ref[...], b_ref[...])
    if first_tiles is None:
        o_refs[0][...] = out
    else:
        i = pl.program_id(0)

        @pl.when(i < first_tiles)
        def _():
            o_refs[0][...] = out

        @pl.when(i >= first_tiles)
        def _():
            o_refs[1][...] = out


def _combine(y_tok, gates, h, ln_g, ln_b, alpha, split=None):
    n, d = h.shape
    tm = _wide_tile(n, COMBINE_TILE_MAX) if split is None else TOKEN_TILE
    nt = n // tm
    g, b = ln_g.reshape(1, d), ln_b.reshape(1, d)
    full = lambda a: pl.BlockSpec(a.shape, lambda i: (0, 0))
    ysp = lambda k: pl.BlockSpec((tm, d // 2), lambda i: (k * nt + i, 0))
    if split is None:
        first_tiles = None
        out_specs = pl.BlockSpec((tm, d), lambda i: (i, 0))
        out_shape = jax.ShapeDtypeStruct((n, d), F32)
    else:
        first_tiles = split // tm
        out_specs = [pl.BlockSpec((tm, d), lambda i: (jnp.minimum(i, first_tiles - 1), 0)),
                     pl.BlockSpec((tm, d), lambda i: (jnp.maximum(i - first_tiles, 0), 0))]
        out_shape = [jax.ShapeDtypeStruct((split, d), F32), jax.ShapeDtypeStruct((n - split, d), F32)]
    return pl.pallas_call(
        functools.partial(_combine_kernel, alpha=alpha, first_tiles=first_tiles),
        grid=(nt,),
        in_specs=[ysp(0), ysp(1), ysp(2), ysp(3),
                  pl.BlockSpec((tm, TOP_K), lambda i: (i, 0)),
                  pl.BlockSpec((tm, d), lambda i: (i, 0)), full(g), full(b)],
        out_specs=out_specs,
        out_shape=out_shape,
        compiler_params=pltpu.CompilerParams(dimension_semantics=("arbitrary",),
                                             vmem_limit_bytes=VMEM_LIMIT),
        name="combine",
    )(y_tok, y_tok, y_tok, y_tok, gates, h, g, b)


def _round_up(a, m):
    return (a + m - 1) // m * m


def _wide_tile(n, cap):
    return max(t for t in range(TOKEN_TILE, cap + 1, TOKEN_TILE) if n % t == 0)


def _moe(h, hp, idx, gates, rank, counts, w_up, b_up, w_down, b_down, ln_g, ln_b, alpha, layer, split=None):
    n, d = h.shape
    ns = TOP_K * n
    cnt = counts[:, 0]
    starts = jnp.cumsum(cnt) - cnt
    experts = jnp.arange(N_EXPERTS, dtype=I32)
    offs = jnp.sum(jnp.where(idx[:, :, None] == experts, starts, 0), axis=-1)
    pos = (offs + rank).reshape(-1).astype(I32)
    x_sorted = _sc_scatter_rows(hp, pos)
    y_sorted = _experts(x_sorted, _expert_visits(cnt, ns), w_up, b_up, w_down, b_down, layer)
    m2 = _round_up(ns, SC_ROW_ALIGN)
    fill = jnp.arange(ns, m2, dtype=I32) - ns
    y_tok = _sc_gather(y_sorted, jnp.concatenate([pos, fill]))
    return _combine(y_tok, gates.T, h, ln_g, ln_b, alpha, split)


def kernel(x_prompt, x_sample, cache_k, cache_v, state_conv, state_gla, w_in, conv_w, w_gate, b_gate,
           gla_norm_g, w_out, ln1_g, ln1_b, w_router, b_router, w_up, b_up, w_down, b_down, ln2_g, ln2_b):
    depth = w_in.shape[0]
    bp, seq, d = x_prompt.shape
    bs, ts, _ = x_sample.shape
    past = cache_k.shape[2]
    n_p = bp * seq
    n = n_p + bs * ts
    alpha = float((2 * depth) ** 0.25)
    x = jnp.concatenate([x_prompt.reshape(n_p, d), x_sample.reshape(bs * ts, d)], axis=0)
    ckt = cache_k.transpose(0, 1, 3, 4, 2).reshape(depth, bs, SB_DIM, past)
    cvt = cache_v.transpose(0, 1, 3, 4, 2).reshape(depth, bs, SB_DIM, past)
    zero_conv = jnp.zeros((bp, CONV_W - 1, CONV_DIM), F32)
    zero_gla = jnp.zeros((bp, GLA_HEADS, GLA_DK, GLA_DK), F32)
    wb = w_in.astype(BF16)
    o_r = 3 * SB_DIM
    n_r = 3 * CONV_DIM + 4 * GLA_DIM
    wq, wk, wv = wb[:, :, :SB_DIM], wb[:, :, SB_DIM:2 * SB_DIM], wb[:, :, 2 * SB_DIM:o_r]
    wkt, wvt = wk.transpose(0, 2, 1), wv.transpose(0, 2, 1)
    wr, wal = wb[:, :, o_r:o_r + n_r], wb[:, :, o_r + n_r:]
    wg = w_gate.astype(BF16)
    wo = w_out.astype(BF16)
    kt = jnp.zeros((depth, bp, SB_DIM, seq), F32)
    vt = jnp.zeros((depth, bp, SB_DIM, seq), F32)
    tm_p = min(PROMPT_TILE, seq)
    outs = [[] for _ in range(6)]
    for l in range(depth):
        shared = (wr[l], wal[l], wg[l], b_gate[l].reshape(1, -1))
        q_p, kt, vt, obc, conv_p, gla_p = _mix_prompt(
            x, jnp.zeros((n, CONV_DIM + GLA_DIM), BF16), kt, vt, l, (wq[l], wkt[l], wvt[l]) + shared,
            conv_w[l], gla_norm_g[l], zero_conv, zero_gla, bp, seq, tm_p)
        q_s, ks, vs, rest_s, la_s = _in_proj(x, n_p, n - n_p, TOKEN_TILE, (wq[l], wk[l], wv[l]) + shared)
        oa = _sb_prompt(jnp.zeros((n, SB_DIM), BF16), q_p, kt, vt, l, bp, seq)
        oa = _sb_decode(oa, q_s, ks, vs, ckt, cvt, l, n_p, bs, ts)
        obc, conv_s, gla_s = _conv_gla(obc, n, rest_s, la_s, conv_w[l], gla_norm_g[l], state_conv[l],
                                       state_gla[l], n_p, bs, ts)
        h, hp, idx, gates, rank, counts = _out_router(oa, obc, x, wo[l, :SB_DIM], wo[l, SB_DIM:], ln1_g[l],
                                                      ln1_b[l], w_router[l], b_router[l], alpha)
        x = _moe(h, hp, idx, gates, rank, counts, w_up, b_up, w_down, b_down, ln2_g[l], ln2_b[l],
                 alpha, l, split=n_p if l == depth - 1 else None)
        outs[0].append(conv_p)
        outs[1].append(gla_p)
        outs[2].append(ks.reshape(bs, ts, SB_HEADS, HEAD_DIM))
        outs[3].append(vs.reshape(bs, ts, SB_HEADS, HEAD_DIM))
        outs[4].append(conv_s)
        outs[5].append(gla_s)
    k_prompt = kt.reshape(depth, bp, SB_HEADS, HEAD_DIM, seq).transpose(0, 1, 4, 2, 3)
    v_prompt = vt.reshape(depth, bp, SB_HEADS, HEAD_DIM, seq).transpose(0, 1, 4, 2, 3)
    st = [jnp.stack(o) for o in outs]
    y_prompt, y_sample = x
    return (y_prompt.reshape(bp, seq, d), y_sample.reshape(bs, ts, d), k_prompt, v_prompt,
            st[0], st[1], st[2], st[3], st[4], st[5])
```

```python
import functools

import jax
import jax.numpy as jnp
from jax import lax
from jax.experimental import pallas as pl
from jax.experimental.pallas import tpu as pltpu
from jax.experimental.pallas import tpu_sc as plsc

F32 = jnp.float32
BF16 = jnp.bfloat16
I32 = jnp.int32
U32 = jnp.uint32

HEAD_DIM = 64
SB_HEADS = 8
SB_DIM = SB_HEADS * HEAD_DIM
CONV_DIM = 256
CONV_W = 3
GLA_HEADS = 4
GLA_DK = 64
GLA_DIM = GLA_HEADS * GLA_DK
GLA_RANK = 16
GLA_TAU = 16.0
GLA_CHUNK = 64
N_EXPERTS = 32
TOP_K = 4
SWIGLU_LIMIT = 7.0
SWIGLU_ALPHA = 1.702
NORM_EPS = 1e-5

LANES = 128
SC_CORES = 2
SC_SUBCORES = 16
SC_CHUNK = 128
SC_ROW_ALIGN = SC_CORES * SC_SUBCORES * SC_CHUNK
SC_SCATTER_MAX = 104
TOKEN_TILE = 256
WIDE_TILE_MAX = 1280
COMBINE_TILE_MAX = 640
PROMPT_TILE = 512
EXPERT_TILE = 512
EXPERT_SUB = 256
SB_TQ = 256
SB_BK = 256
SB_GROUP = 256
DEC_TK = 512
GLA_TC = 256
VMEM_LIMIT = 48 * 1024 * 1024
SB_DEAD = -100.0


def _dot(a, b):
    return jnp.dot(a, b, preferred_element_type=F32)


def _dot_nt(a, b):
    return lax.dot_general(a, b, (((1,), (1,)), ((), ())), preferred_element_type=F32)


def _split_bf16(x):
    hi = x.astype(BF16)
    lo = (x - hi.astype(F32)).astype(BF16)
    return hi, lo


def _dot_exact_rhs(x, m):
    hi, lo = _split_bf16(x)
    return _dot(hi, m) + _dot(lo, m)


def _dot_exact_lhs(m, x):
    hi, lo = _split_bf16(x)
    return _dot(m, hi) + _dot(m, lo)


def _pack_bf16_pairs(x):
    c = x.shape[1] // 2
    bits = lax.bitcast_convert_type(x.astype(BF16).astype(F32), U32)
    return (bits[:, :c] >> 16) | (bits[:, c:] & jnp.uint32(0xFFFF0000))


def _unpack_bf16_pairs(w):
    lo = lax.bitcast_convert_type(w << 16, F32)
    hi = lax.bitcast_convert_type(w & jnp.uint32(0xFFFF0000), F32)
    return jnp.concatenate([lo, hi], axis=1).astype(BF16)


def _softplus(z):
    return jnp.maximum(z, 0.0) + jnp.log(1.0 + jnp.exp(-jnp.abs(z)))


def _layer_norm(y, g, b):
    mu = jnp.mean(y, axis=-1, keepdims=True)
    yc = y - mu
    var = jnp.mean(yc * yc, axis=-1, keepdims=True)
    return yc * lax.rsqrt(var + NORM_EPS) * g + b


def _strict_upper(n, copies=1):
    r = lax.broadcasted_iota(I32, (copies * n, n), 0)
    c = lax.broadcasted_iota(I32, (copies * n, n), 1)
    for k in range(1, copies):
        r = r - jnp.where(r >= n, n, 0)
    return jnp.where(r > c, 1.0, 0.0).astype(BF16)


def _project(x_ref, wq_ref, wk_ref, wv_ref, wr_ref, wal_ref, wg_ref, bg_ref,
             q_ref, k_ref, v_ref, r_ref, la_ref, *, transposed_kv):
    xb = x_ref[...].astype(BF16)
    for c in range(0, SB_DIM, 256):
        q_ref[:, c:c + 256] = (_dot(xb, wq_ref[:, c:c + 256]) * (HEAD_DIM ** -0.5)).astype(BF16)
    for c in range(0, r_ref.shape[1], 256):
        r_ref[:, c:c + 256] = _dot(xb, wr_ref[:, c:c + 256])
    al = _dot(xb, wal_ref[...])
    g = _dot(al.astype(BF16), wg_ref[...]) + bg_ref[...]
    la_ref[...] = -_softplus(-g) * (1.0 / GLA_TAU)
    for c in range(0, SB_DIM, 256):
        if transposed_kv:
            k_ref[c:c + 256, :] = _dot_nt(wk_ref[c:c + 256, :], xb)
            v_ref[c:c + 256, :] = _dot_nt(wv_ref[c:c + 256, :], xb)
        else:
            k_ref[:, c:c + 256] = _dot(xb, wk_ref[:, c:c + 256])
            v_ref[:, c:c + 256] = _dot(xb, wv_ref[:, c:c + 256])


def _in_proj(x, row0, rows, tm, weights):
    d = x.shape[1]
    n_r = weights[3].shape[1]
    rb = row0 // tm
    full = lambda a: pl.BlockSpec(a.shape, lambda i: (0,) * a.ndim)
    row = lambda w: pl.BlockSpec((tm, w), lambda i: (i, 0))
    sds = lambda w, dt: jax.ShapeDtypeStruct((rows, w), dt)
    return pl.pallas_call(
        functools.partial(_project, transposed_kv=False),
        grid=(rows // tm,),
        in_specs=[pl.BlockSpec((tm, d), lambda i: (rb + i, 0))] + [full(w) for w in weights],
        out_specs=[row(SB_DIM), row(SB_DIM), row(SB_DIM), row(n_r), row(GLA_DIM)],
        out_shape=[sds(SB_DIM, BF16), sds(SB_DIM, F32), sds(SB_DIM, F32), sds(n_r, F32), sds(GLA_DIM, F32)],
        compiler_params=pltpu.CompilerParams(dimension_semantics=("arbitrary",),
                                             vmem_limit_bytes=VMEM_LIMIT),
        name="in_proj",
    )(x, *weights)


def _sb_weights(z, tri, run, mask):
    sp = _softplus(z)
    l1m = -sp
    lsig = z - sp
    if mask is not None:
        l1m = jnp.where(mask, l1m, 0.0)
    if tri.shape[0] == 2 * z.shape[1]:
        hi = lax.bitcast_convert_type(lax.bitcast_convert_type(l1m, U32) & jnp.uint32(0xFFFF0000), F32)
        parts = jnp.concatenate([hi.astype(BF16), (l1m - hi).astype(BF16)], axis=1)
        rest = _dot(parts, tri) + run
    else:
        rest = _dot_exact_rhs(l1m, tri) + run
    a = jnp.exp(lsig + rest)
    if mask is not None:
        a = jnp.where(mask, a, 0.0)
    return a.astype(BF16), run + jnp.sum(l1m, axis=1, keepdims=True)


def _sb_prompt_kernel(oa_ref, q_ref, kt_ref, vt_ref, o_ref, *, tq, bk):
    del oa_ref
    qi = pl.program_id(2)
    q = q_ref[...]
    n_hh = SB_GROUP // HEAD_DIM
    lane = lax.broadcasted_iota(I32, (1, SB_GROUP), 1)
    in_head = [(lane // HEAD_DIM) == h for h in range(n_hh)]
    qh = [jnp.where(m, q, jnp.zeros_like(q)) for m in in_head]
    tri = _strict_upper(bk, copies=2)
    n_full = (qi * tq) // bk
    qpos = qi * tq + lax.broadcasted_iota(I32, (tq, bk), 0)
    kpos = n_full * bk + lax.broadcasted_iota(I32, (tq, bk), 1)
    diag_mask = kpos < qpos

    def tile(jb, runs, mask):
        ks = pl.multiple_of(jb * bk, bk)
        kt = kt_ref[:, pl.ds(ks, bk)].astype(BF16)
        vt = vt_ref[:, pl.ds(ks, bk)].astype(BF16)
        out = jnp.zeros((tq, SB_GROUP), F32)
        new_runs = []
        for h in range(n_hh):
            a, run = _sb_weights(_dot(qh[h], kt), tri, runs[h], mask)
            out = jnp.where(in_head[h], _dot_nt(a, vt), out)
            new_runs.append(run)
        return out, tuple(new_runs)

    def alive_of(runs):
        m = jnp.max(runs[0])
        for r in runs[1:]:
            m = jnp.maximum(m, jnp.max(r))
        return m > SB_DEAD

    acc, runs = tile(n_full, tuple(jnp.zeros((tq, 1), F32) for _ in range(n_hh)), diag_mask)

    def cond(carry):
        j, alive, _, _ = carry
        return (j >= 0) & alive

    def body(carry):
        j, _, acc, runs = carry
        pv, runs = tile(j, runs, None)
        return j - 1, alive_of(runs), acc + pv, runs

    _, _, acc, _ = lax.while_loop(cond, body, (n_full - 1, alive_of(runs), acc, runs))
    o_ref[...] = acc.astype(o_ref.dtype)


def _sb_prompt(oa, q, kt, vt, layer, batch, seq):
    tq, bk = min(SB_TQ, seq), min(SB_BK, seq)
    nq = seq // tq
    hp = SB_DIM // SB_GROUP
    kv_spec = pl.BlockSpec((None, None, SB_GROUP, seq), lambda b, p, i: (layer, b, p, 0))
    return pl.pallas_call(
        functools.partial(_sb_prompt_kernel, tq=tq, bk=bk),
        grid=(batch, hp, nq),
        in_specs=[pl.BlockSpec(memory_space=pl.ANY),
                  pl.BlockSpec((tq, SB_GROUP), lambda b, p, i: (b * nq + i, p)), kv_spec, kv_spec],
        out_specs=pl.BlockSpec((tq, SB_GROUP), lambda b, p, i: (b * nq + i, p)),
        out_shape=jax.ShapeDtypeStruct(oa.shape, oa.dtype),
        input_output_aliases={0: 0},
        compiler_params=pltpu.CompilerParams(
            dimension_semantics=("arbitrary", "arbitrary", "arbitrary"),
            vmem_limit_bytes=VMEM_LIMIT),
        name="sb_prompt",
    )(oa, q, kt, vt)


def _sb_decode_kernel(oa_ref, q_ref, kn_ref, vn_ref, kc_hbm, vc_hbm, o_ref, kbuf, vbuf, sem, acc_ref, run_ref,
                      alive_ref, *, t, tk, bk, layer, nkb):
    del oa_ref
    b = pl.program_id(0)
    q = q_ref[...]

    def block_copies(j, slot):
        cols = pl.ds(pl.multiple_of((nkb - 1 - j) * tk, tk), tk)
        return (pltpu.make_async_copy(kc_hbm.at[layer, b, :, cols], kbuf.at[slot], sem.at[0, slot]),
                pltpu.make_async_copy(vc_hbm.at[layer, b, :, cols], vbuf.at[slot], sem.at[1, slot]))

    for cp in block_copies(0, 0):
        cp.start()

    kn = kn_ref[...].astype(BF16)
    vn = vn_ref[...].astype(BF16)
    lane = lax.broadcasted_iota(I32, (1, SB_DIM), 1)
    r = lax.broadcasted_iota(I32, (t, t), 0)
    c = lax.broadcasted_iota(I32, (t, t), 1)
    mask = c < r
    tri_new = _strict_upper(t)
    for h in range(SB_HEADS):
        qh = jnp.where((lane // HEAD_DIM) == h, q, jnp.zeros_like(q))
        a, run = _sb_weights(_dot_nt(qh, kn), tri_new, jnp.zeros((t, 1), F32), mask)
        acc_ref[h] = _dot(a, vn)[:, h * HEAD_DIM:(h + 1) * HEAD_DIM]
        run_ref[h * t:(h + 1) * t, :] = run
    alive_ref[0] = (jnp.max(run_ref[...]) > SB_DEAD).astype(I32)

    tri = _strict_upper(bk, copies=2)

    def cond(carry):
        j, alive = carry
        return (j < nkb) & (alive > 0)

    def body(carry):
        j, _ = carry
        slot = j % 2

        @pl.when(j + 1 < nkb)
        def _():
            for cp in block_copies(j + 1, 1 - slot):
                cp.start()

        for cp in block_copies(j, slot):
            cp.wait()
        for c in range(tk // bk - 1, -1, -1):
            cols = slice(c * bk, (c + 1) * bk)

            @pl.when(alive_ref[0] > 0)
            def _():
                z = jnp.concatenate(
                    [_dot(q[:, h * HEAD_DIM:(h + 1) * HEAD_DIM],
                          kbuf[slot, h * HEAD_DIM:(h + 1) * HEAD_DIM, cols].astype(BF16))
                     for h in range(SB_HEADS)], axis=0)
                a, run = _sb_weights(z, tri, run_ref[...], None)
                for h in range(SB_HEADS):
                    vt = vbuf[slot, h * HEAD_DIM:(h + 1) * HEAD_DIM, cols].astype(BF16)
                    acc_ref[h] = acc_ref[h] + _dot_nt(a[h * t:(h + 1) * t, :], vt)
                run_ref[...] = run
                alive_ref[0] = (jnp.max(run) > SB_DEAD).astype(I32)

        return j + 1, alive_ref[0]

    j_end, _ = lax.while_loop(cond, body, (jnp.int32(0), alive_ref[0]))

    @pl.when(j_end < nkb)
    def _():
        for cp in block_copies(j_end, j_end % 2):
            cp.wait()

    for h in range(SB_HEADS):
        o_ref[:, h * HEAD_DIM:(h + 1) * HEAD_DIM] = acc_ref[h].astype(o_ref.dtype)


def _sb_decode(oa, q, ks, vs, cache_kt, cache_vt, layer, row0, batch, t):
    past = cache_kt.shape[3]
    tk = min(DEC_TK, past)
    bk = min(SB_BK, tk)
    nkb = past // tk
    rb = row0 // t
    new = pl.BlockSpec((t, SB_DIM), lambda b: (b, 0))
    anywhere = pl.BlockSpec(memory_space=pl.ANY)
    return pl.pallas_call(
        functools.partial(_sb_decode_kernel, t=t, tk=tk, bk=bk, layer=layer, nkb=nkb),
        grid=(batch,),
        in_specs=[anywhere, new, new, new, anywhere, anywhere],
        out_specs=pl.BlockSpec((t, SB_DIM), lambda b: (rb + b, 0)),
        out_shape=jax.ShapeDtypeStruct(oa.shape, oa.dtype),
        input_output_aliases={0: 0},
        scratch_shapes=[pltpu.VMEM((2, SB_DIM, tk), F32),
                        pltpu.VMEM((2, SB_DIM, tk), F32),
                        pltpu.SemaphoreType.DMA((2, 2)),
                        pltpu.VMEM((SB_HEADS, t, HEAD_DIM), F32),
                        pltpu.VMEM((SB_HEADS * t, 1), F32),
                        pltpu.SMEM((1,), I32)],
        compiler_params=pltpu.CompilerParams(dimension_semantics=("arbitrary",),
                                             vmem_limit_bytes=VMEM_LIMIT),
        name="sb_decode",
    )(oa, q, ks, vs, cache_kt, cache_vt)


def _mix_state_load(cprev_ref, sprev_ref, ctail_ref, st_ref):
    n_hh = LANES // GLA_DK
    ctail_ref[...] = cprev_ref[...]
    for p in range(GLA_DIM // LANES):
        st_ref[p] = jnp.zeros((LANES, LANES), F32)
        for hh in range(n_hh):
            st_ref[p, hh * GLA_DK:(hh + 1) * GLA_DK, hh * GLA_DK:(hh + 1) * GLA_DK] = sprev_ref[p * n_hh + hh].T


def _mix_state_store(ctail_ref, st_ref, cnew_ref, snew_ref):
    n_hh = LANES // GLA_DK
    cnew_ref[...] = ctail_ref[...]
    for p in range(GLA_DIM // LANES):
        for hh in range(n_hh):
            blk = st_ref[p, hh * GLA_DK:(hh + 1) * GLA_DK, :]
            snew_ref[p * n_hh + hh] = blk.T[hh * GLA_DK:(hh + 1) * GLA_DK, :]


def _conv_gla_block(bg_ref, cg_ref, u_ref, qc_ref, kc_ref, vc_ref, gc_ref, la_ref, cw_ref, ng_ref,
                    o_ref, ctail_ref, st_ref, tc, chunk):
    n_pair = GLA_DIM // LANES
    n_hh = LANES // GLA_DK

    z = cg_ref[...] * u_ref[...]
    tail = ctail_ref[...]
    row = lax.broadcasted_iota(I32, z.shape, 0)
    z1 = jnp.where(row < 1, tail[1:2, :], pltpu.roll(z, 1, 0))
    z2 = jnp.where(row < 2, jnp.where(row < 1, tail[0:1, :], tail[1:2, :]), pltpu.roll(z, 2, 0))
    cw = cw_ref[...]
    y = z2 * cw[0:1, :] + z1 * cw[1:2, :] + z * cw[2:3, :]
    o_ref[:, 0:CONV_DIM] = (bg_ref[...] * y).astype(o_ref.dtype)
    ctail_ref[...] = z[tc - 2:tc, :]

    r = lax.broadcasted_iota(I32, (tc, tc), 0)
    c = lax.broadcasted_iota(I32, (tc, tc), 1)
    same_chunk = (r // chunk) == (c // chunk)
    chunk_sum = jnp.where(same_chunk, 1.0, 0.0).astype(BF16)
    chunk_cumsum = jnp.where(same_chunk & (c <= r), 1.0, 0.0).astype(BF16)
    lane = lax.broadcasted_iota(I32, (1, LANES), 1)
    lr = lax.broadcasted_iota(I32, (LANES, LANES), 0) // GLA_DK
    lc = lax.broadcasted_iota(I32, (LANES, LANES), 1) // GLA_DK
    same_head = lr == lc
    head_mean = jnp.where(same_head, 1.0 / GLA_DK, 0.0).astype(BF16)
    r2 = lax.broadcasted_iota(I32, (tc, n_hh * tc), 0)
    c2 = lax.broadcasted_iota(I32, (tc, n_hh * tc), 1)
    c2 = c2 - jnp.where(c2 >= tc, tc, 0)
    causal2 = ((r2 // chunk) == (c2 // chunk)) & (c2 <= r2)
    for p in range(n_pair):
        cols = slice(p * LANES, (p + 1) * LANES)
        la = la_ref[:, cols]
        b = _dot_exact_lhs(chunk_cumsum, la)
        tot = _dot_exact_lhs(chunk_sum, la)
        k = kc_ref[:, cols]
        qe = (qc_ref[:, cols] * (GLA_DK ** -0.5) * jnp.exp(b)).astype(BF16)
        ke = (k * jnp.exp(-b)).astype(BF16)
        kd = (k * jnp.exp(tot - b)).astype(BF16)
        decay = jnp.exp(tot)
        vb = vc_ref[:, cols].astype(BF16)
        zeros = jnp.zeros_like(ke)
        ke_st = jnp.concatenate([jnp.where((lane // GLA_DK) == hh, ke, zeros)
                                 for hh in range(n_hh)], axis=0)
        v_st = jnp.concatenate([jnp.where((lane // GLA_DK) == hh, vb, zeros)
                                for hh in range(n_hh)], axis=0)
        a = jnp.where(causal2, _dot_nt(qe, ke_st), 0.0)
        o = _dot(a.astype(BF16), v_st)
        st = st_ref[p]
        from_state = []
        for ci in range(tc // chunk):
            rows = slice(ci * chunk, (ci + 1) * chunk)
            from_state.append(_dot_nt(qe[rows], st.astype(BF16)))
            upd = _dot(vb[rows].T, kd[rows])
            st = jnp.where(same_head, st * decay[ci * chunk:ci * chunk + 1, :] + upd, 0.0)
        st_ref[p] = st
        o = o + jnp.concatenate(from_state, axis=0)
        ms = _dot_exact_rhs(o * o, head_mean)
        o = o * lax.rsqrt(ms + NORM_EPS) * ng_ref[:, cols]
        g = gc_ref[:, cols]
        o = o * (g * (1.0 / (1.0 + jnp.exp(-g))))
        o_ref[:, CONV_DIM + p * LANES:CONV_DIM + (p + 1) * LANES] = o.astype(o_ref.dtype)


def _conv_gla_kernel(*refs, tc, chunk):
    (_, bg_ref, cg_ref, u_ref, qc_ref, kc_ref, vc_ref, gc_ref, la_ref, cw_ref, ng_ref, cprev_ref, sprev_ref,
     o_ref, cnew_ref, snew_ref, ctail_ref, st_ref) = refs
    ti = pl.program_id(1)

    @pl.when(ti == 0)
    def _():
        _mix_state_load(cprev_ref, sprev_ref, ctail_ref, st_ref)

    _conv_gla_block(bg_ref, cg_ref, u_ref, qc_ref, kc_ref, vc_ref, gc_ref, la_ref, cw_ref, ng_ref,
                    o_ref, ctail_ref, st_ref, tc, chunk)

    @pl.when(ti == pl.num_programs(1) - 1)
    def _():
        _mix_state_store(ctail_ref, st_ref, cnew_ref, snew_ref)


def _conv_gla(obc_prev, n, rest, la, conv_w, norm_g, conv_prev, gla_prev, row0, batch, t):
    tc = min(GLA_TC, t)
    chunk = min(GLA_CHUNK, t)
    nt = t // tc
    rb = row0 // tc
    col = lambda j: pl.BlockSpec((tc, CONV_DIM), lambda b, i: (b * nt + i, j))
    const2 = lambda a: pl.BlockSpec(a.shape, lambda b, i: (0, 0))
    ng = norm_g.reshape(1, GLA_DIM)
    args = (rest,) * 7 + (la, conv_w, ng, conv_prev, gla_prev)
    return pl.pallas_call(
        functools.partial(_conv_gla_kernel, tc=tc, chunk=chunk),
        grid=(batch, nt),
        in_specs=[
            pl.BlockSpec(memory_space=pl.ANY), col(0), col(1), col(2), col(3), col(4), col(5), col(6),
            pl.BlockSpec((tc, GLA_DIM), lambda b, i: (b * nt + i, 0)),
            const2(conv_w), const2(ng),
            pl.BlockSpec((None, CONV_W - 1, CONV_DIM), lambda b, i: (b, 0, 0)),
            pl.BlockSpec((None, GLA_HEADS, GLA_DK, GLA_DK), lambda b, i: (b, 0, 0, 0))],
        out_specs=[pl.BlockSpec((tc, CONV_DIM + GLA_DIM), lambda b, i: (rb + b * nt + i, 0)),
                   pl.BlockSpec((None, CONV_W - 1, CONV_DIM), lambda b, i: (b, 0, 0)),
                   pl.BlockSpec((None, GLA_HEADS, GLA_DK, GLA_DK), lambda b, i: (b, 0, 0, 0))],
        out_shape=[jax.ShapeDtypeStruct((n, CONV_DIM + GLA_DIM), BF16),
                   jax.ShapeDtypeStruct((batch, CONV_W - 1, CONV_DIM), F32),
                   jax.ShapeDtypeStruct((batch, GLA_HEADS, GLA_DK, GLA_DK), F32)],
        input_output_aliases={0: 0},
        scratch_shapes=[pltpu.VMEM((CONV_W - 1, CONV_DIM), F32),
                        pltpu.VMEM((GLA_DIM // LANES, LANES, LANES), F32)],
        compiler_params=pltpu.CompilerParams(dimension_semantics=("arbitrary", "arbitrary"),
                                             vmem_limit_bytes=VMEM_LIMIT),
        name="conv_gla",
    )(obc_prev, *args)


def _mix_prompt_kernel(*refs, tc, chunk):
    (_, _, _, x_ref, wq_ref, wk_ref, wv_ref, wr_ref, wal_ref, wg_ref, bg_ref, cw_ref, ng_ref, cprev_ref,
     sprev_ref, q_ref, kt_ref, vt_ref, o_ref, cnew_ref, snew_ref, r_scr, la_scr, ctail_ref, st_ref) = refs
    ti = pl.program_id(1)
    _project(x_ref, wq_ref, wk_ref, wv_ref, wr_ref, wal_ref, wg_ref, bg_ref,
             q_ref, kt_ref, vt_ref, r_scr, la_scr, transposed_kv=True)

    @pl.when(ti == 0)
    def _():
        _mix_state_load(cprev_ref, sprev_ref, ctail_ref, st_ref)

    for s in range(x_ref.shape[0] // tc):
        rows = pl.ds(s * tc, tc)
        part = [r_scr.at[rows, pl.ds(j * CONV_DIM, CONV_DIM)] for j in range(7)]
        _conv_gla_block(*part, la_scr.at[rows], cw_ref, ng_ref, o_ref.at[rows], ctail_ref, st_ref, tc, chunk)

    @pl.when(ti == pl.num_programs(1) - 1)
    def _():
        _mix_state_store(ctail_ref, st_ref, cnew_ref, snew_ref)


def _mix_prompt(x, obc_prev, kt, vt, layer, weights, conv_w, norm_g, conv_prev, gla_prev, batch, seq, tm):
    d = x.shape[1]
    n_p = batch * seq
    nt = seq // tm
    tc = min(GLA_TC, tm)
    wq, wk, wv, wr, wal, wg, bg = weights
    ng = norm_g.reshape(1, GLA_DIM)
    full = lambda a: pl.BlockSpec(a.shape, lambda b, i: (0,) * a.ndim)
    row = lambda w: pl.BlockSpec((tm, w), lambda b, i: (b * nt + i, 0))
    kv_spec = pl.BlockSpec((None, None, SB_DIM, tm), lambda b, i: (layer, b, 0, i))
    conv_spec = pl.BlockSpec((None, CONV_W - 1, CONV_DIM), lambda b, i: (b, 0, 0))
    gla_spec = pl.BlockSpec((None, GLA_HEADS, GLA_DK, GLA_DK), lambda b, i: (b, 0, 0, 0))
    anywhere = pl.BlockSpec(memory_space=pl.ANY)
    consts = (wq, wk, wv, wr, wal, wg, bg, conv_w, ng)
    return pl.pallas_call(
        functools.partial(_mix_prompt_kernel, tc=tc, chunk=min(GLA_CHUNK, tc)),
        grid=(batch, nt),
        in_specs=[anywhere, anywhere, anywhere, row(d)] + [full(w) for w in consts] + [conv_spec, gla_spec],
        out_specs=[row(SB_DIM), kv_spec, kv_spec, row(CONV_DIM + GLA_DIM), conv_spec, gla_spec],
        out_shape=[jax.ShapeDtypeStruct((n_p, SB_DIM), BF16),
                   jax.ShapeDtypeStruct(kt.shape, kt.dtype), jax.ShapeDtypeStruct(vt.shape, vt.dtype),
                   jax.ShapeDtypeStruct(obc_prev.shape, obc_prev.dtype),
                   jax.ShapeDtypeStruct((batch, CONV_W - 1, CONV_DIM), F32),
                   jax.ShapeDtypeStruct((batch, GLA_HEADS, GLA_DK, GLA_DK), F32)],
        input_output_aliases={0: 1, 1: 2, 2: 3},
        scratch_shapes=[pltpu.VMEM((tm, wr.shape[1]), F32),
                        pltpu.VMEM((tm, GLA_DIM), F32),
                        pltpu.VMEM((CONV_W - 1, CONV_DIM), F32),
                        pltpu.VMEM((GLA_DIM // LANES, LANES, LANES), F32)],
        compiler_params=pltpu.CompilerParams(dimension_semantics=("arbitrary", "arbitrary"),
                                             vmem_limit_bytes=VMEM_LIMIT),
        name="mix_prompt",
    )(kt, vt, obc_prev, x, *consts, conv_prev, gla_prev)


def _out_router_kernel(oa_ref, obc_ref, x_ref, wa_ref, wb_ref, g_ref, b_ref, wr_ref, br_ref,
                       h_ref, hp_ref, idx_ref, gate_ref, rank_ref, cnt_ref, carry_ref, *, alpha):
    i = pl.program_id(0)
    tm = x_ref.shape[0]

    @pl.when(i == 0)
    def _():
        carry_ref[...] = jnp.zeros_like(carry_ref)

    m = _dot(oa_ref[...], wa_ref[...]) + _dot(obc_ref[...], wb_ref[...])
    h = _layer_norm(alpha * x_ref[...] + m, g_ref[...], b_ref[...])
    h_ref[...] = h
    hp_ref[...] = lax.bitcast_convert_type(_pack_bf16_pairs(h), F32)

    h_hi, h_lo = _split_bf16(h)
    w_hi, w_lo = _split_bf16(wr_ref[...])
    logit = _dot_nt(w_hi, h_hi) + _dot_nt(w_hi, h_lo) + _dot_nt(w_lo, h_hi) + br_ref[...]
    eid = lax.broadcasted_iota(I32, (N_EXPERTS, tm), 0)
    r = lax.broadcasted_iota(I32, (tm, tm), 0)
    c = lax.broadcasted_iota(I32, (tm, tm), 1)
    before = jnp.where(r < c, 1.0, 0.0).astype(BF16)
    base = carry_ref[...]
    vals, idxs, onehots, bases = [], [], [], []
    for _ in range(TOP_K):
        mx = jnp.max(logit, axis=0, keepdims=True)
        sel = jnp.min(jnp.where(logit == mx, eid, N_EXPERTS), axis=0, keepdims=True)
        hit = eid == sel
        logit = jnp.where(hit, -jnp.inf, logit)
        onehot = jnp.where(hit, 1.0, 0.0)
        onehots.append(onehot)
        bases.append(base)
        base = base + jnp.sum(onehot, axis=1, keepdims=True)
        vals.append(mx)
        idxs.append(sel)
    carry_ref[...] = base
    earlier = _dot(jnp.concatenate(onehots, axis=0).astype(BF16), before)
    ranks = [jnp.sum(onehots[k] * (earlier[k * N_EXPERTS:(k + 1) * N_EXPERTS] + bases[k]),
                     axis=0, keepdims=True) for k in range(TOP_K)]
    e = [jnp.exp(v - vals[0]) for v in vals]
    inv = 1.0 / (e[0] + e[1] + e[2] + e[3])
    idx_ref[...] = jnp.concatenate(idxs, axis=0)
    gate_ref[...] = jnp.concatenate([ek * inv for ek in e], axis=0)
    rank_ref[...] = jnp.concatenate(ranks, axis=0).astype(I32)
    cnt_ref[...] = jnp.broadcast_to(base, cnt_ref.shape).astype(I32)


def _out_router(oa, obc, x, wa, wb, ln_g, ln_b, w_router, b_router, alpha):
    n, d = x.shape
    tm = _wide_tile(n, WIDE_TILE_MAX)
    wr = w_router.T
    br = b_router.reshape(N_EXPERTS, 1)
    g, b = ln_g.reshape(1, d), ln_b.reshape(1, d)
    full = lambda a: pl.BlockSpec(a.shape, lambda i: (0, 0))
    row = lambda w: pl.BlockSpec((tm, w), lambda i: (i, 0))
    colb = pl.BlockSpec((TOP_K, tm), lambda i: (0, i))
    return pl.pallas_call(
        functools.partial(_out_router_kernel, alpha=alpha),
        grid=(n // tm,),
        in_specs=[row(SB_DIM), row(CONV_DIM + GLA_DIM), row(d), full(wa), full(wb), full(g), full(b),
                  full(wr), full(br)],
        out_specs=[row(d), row(d // 2), colb, colb, colb,
                   pl.BlockSpec((N_EXPERTS, LANES), lambda i: (0, 0))],
        out_shape=[jax.ShapeDtypeStruct((n, d), F32),
                   jax.ShapeDtypeStruct((n, d // 2), F32),
                   jax.ShapeDtypeStruct((TOP_K, n), I32),
                   jax.ShapeDtypeStruct((TOP_K, n), F32),
                   jax.ShapeDtypeStruct((TOP_K, n), I32),
                   jax.ShapeDtypeStruct((N_EXPERTS, LANES), I32)],
        scratch_shapes=[pltpu.VMEM((N_EXPERTS, 1), F32)],
        compiler_params=pltpu.CompilerParams(dimension_semantics=("arbitrary",),
                                             vmem_limit_bytes=VMEM_LIMIT),
        name="out_router",
    )(oa, obc, x, wa, wb, g, b, wr, br)


def _sc_gather(table, idx):
    m = idx.shape[0]
    d = table.shape[1]
    per_worker = m // (SC_CORES * SC_SUBCORES)
    half = SC_CHUNK // 2
    n_groups = per_worker // SC_CHUNK
    mesh = plsc.VectorSubcoreMesh(core_axis_name="c", subcore_axis_name="s")

    @functools.partial(
        pl.kernel, mesh=mesh,
        out_type=jax.ShapeDtypeStruct((m, d), table.dtype),
        scratch_types=[pltpu.VMEM((per_worker,), I32)] + [pltpu.VMEM((half, d), table.dtype)] * 2
                      + [pltpu.SemaphoreType.DMA] * 4,
        name="sc_gather",
    )
    def gather(table_hbm, idx_hbm, out_hbm, idx_v, buf0, buf1, g0, g1, w0, w1):
        wid = lax.axis_index("s") * SC_CORES + lax.axis_index("c")
        base = wid * per_worker
        pltpu.sync_copy(idx_hbm.at[pl.ds(base, per_worker)], idx_v)

        @pl.loop(0, n_groups)
        def _(g):
            o0 = pl.multiple_of(g * SC_CHUNK, 8)
            o1 = pl.multiple_of(g * SC_CHUNK + half, 8)
            ga = pltpu.async_copy(table_hbm.at[idx_v.at[pl.ds(o0, half)]], buf0, g0)
            gb = pltpu.async_copy(table_hbm.at[idx_v.at[pl.ds(o1, half)]], buf1, g1)
            ga.wait()
            wa = pltpu.async_copy(buf0, out_hbm.at[pl.ds(base + o0, half)], w0)
            gb.wait()
            wb = pltpu.async_copy(buf1, out_hbm.at[pl.ds(base + o1, half)], w1)
            wa.wait()
            wb.wait()

    return gather(table, idx)


def _sc_chunk(per_worker):
    return max(c for c in range(8, SC_SCATTER_MAX + 1, 8) if per_worker % c == 0)


def _sc_scatter_rows(h, pos):
    n, d = h.shape
    per_worker = n // (SC_CORES * SC_SUBCORES)
    ch = _sc_chunk(per_worker)
    n_chunks = per_worker // ch
    mesh = plsc.VectorSubcoreMesh(core_axis_name="c", subcore_axis_name="s")

    @functools.partial(
        pl.kernel, mesh=mesh,
        out_type=jax.ShapeDtypeStruct((TOP_K * n, d), h.dtype),
        scratch_types=[pltpu.VMEM((ch,), I32)] * TOP_K + [pltpu.VMEM((ch, d), h.dtype)]
                      + [pltpu.SemaphoreType.DMA] * TOP_K,
        name="sc_scatter",
    )
    def scatter(h_hbm, pos_hbm, out_hbm, *scratch):
        idx_v, rows_v, sems = scratch[:TOP_K], scratch[TOP_K], scratch[TOP_K + 1:]
        wid = lax.axis_index("s") * SC_CORES + lax.axis_index("c")
        base = wid * per_worker

        @pl.loop(0, n_chunks)
        def _(c):
            t0 = pl.multiple_of(base + c * ch, 8)
            pltpu.sync_copy(h_hbm.at[pl.ds(t0, ch)], rows_v)
            for k in range(TOP_K):
                pltpu.sync_copy(pos_hbm.at[pl.ds(k * n + t0, ch)], idx_v[k])
            copies = [pltpu.async_copy(rows_v, out_hbm.at[idx_v[k]], sems[k]) for k in range(TOP_K)]
            for cp in copies:
                cp.wait()

    return scatter(h, pos)


def _expert_kernel(vt_ref, ve_ref, lo_ref, hi_ref, ord_ref, nxt_ref, x_ref, wu_hbm, bu_ref, wd_hbm, bd_ref,
                   y_ref, wu32_ref, wd32_ref, sem, wu16_ref, wd16_ref, *, layer):
    i = pl.program_id(0)
    ip = jnp.maximum(i - 1, 0)
    e = ve_ref[i]
    tile = vt_ref[i]
    first_visit = (i == 0) | (tile != vt_ref[ip])
    lo = lo_ref[i]
    hi = hi_ref[i]
    tm = x_ref.shape[0]
    dff = wd16_ref.shape[0]

    def weight_copies(expert, slot):
        return (pltpu.make_async_copy(wu_hbm.at[layer, expert], wu32_ref.at[slot], sem.at[0, slot]),
                pltpu.make_async_copy(wd_hbm.at[layer, expert], wd32_ref.at[slot], sem.at[1, slot]))

    @pl.when(i == 0)
    def _():
        for cp in weight_copies(e, 0):
            cp.start()

    @pl.when((i == 0) | (e != ve_ref[ip]))
    def _():
        slot = ord_ref[i] % 2
        for cp in weight_copies(e, slot):
            cp.wait()

        @pl.when(nxt_ref[i] >= 0)
        def _():
            for cp in weight_copies(nxt_ref[i], 1 - slot):
                cp.start()

        step = 128

        def cast(s, _):
            rows = pl.ds(pl.multiple_of(s * step, step), step)
            wu16_ref[rows, :] = wu32_ref[slot, rows, :].astype(BF16)
            return 0

        lax.fori_loop(0, wu16_ref.shape[0] // step, cast, 0)

        def cast_d(s, _):
            rows = pl.ds(pl.multiple_of(s * step, step), step)
            wd16_ref[rows, :] = wd32_ref[slot, rows, :].astype(BF16)
            return 0

        lax.fori_loop(0, dff // step, cast_d, 0)

    @pl.when(first_visit)
    def _():
        y_ref[...] = jnp.zeros_like(y_ref)

    def ffn(rows):
        x = _unpack_bf16_pairs(lax.bitcast_convert_type(x_ref[rows, :], U32))
        glu = jnp.minimum(_dot(x, wu16_ref[:, :dff]) + bu_ref[:, :dff], SWIGLU_LIMIT)
        lin = jnp.clip(_dot(x, wu16_ref[:, dff:]) + bu_ref[:, dff:], -SWIGLU_LIMIT, SWIGLU_LIMIT)
        act = glu * (1.0 / (1.0 + jnp.exp(-SWIGLU_ALPHA * glu))) * (lin + 1.0)
        y = _dot(act.astype(BF16), wd16_ref[...]) + bd_ref[...]
        return lax.bitcast_convert_type(_pack_bf16_pairs(y), F32)

    subs = [slice(s * EXPERT_SUB, (s + 1) * EXPERT_SUB) for s in range(tm // EXPERT_SUB)]
    whole = (lo <= tile * tm) & (hi >= tile * tm + tm)

    @pl.when(whole)
    def _():
        for rows in subs:
            y_ref[rows, :] = ffn(rows)

    for s, rows in enumerate(subs):
        r0 = tile * tm + s * EXPERT_SUB

        @pl.when(jnp.logical_not(whole) & (hi > r0) & (lo < r0 + EXPERT_SUB))
        def _():
            row = r0 + lax.broadcasted_iota(I32, (EXPERT_SUB, 1), 0)
            y_ref[rows, :] = jnp.where((row >= lo) & (row < hi), ffn(rows), y_ref[rows, :])


def _experts(x_sorted, visits, w_up, b_up, w_down, b_down, layer):
    ns = x_sorted.shape[0]
    tm = EXPERT_TILE
    d, dff = w_down.shape[3], w_down.shape[2]
    bu = b_up.reshape(b_up.shape[0], N_EXPERTS, 1, 2 * dff)
    bd = b_down.reshape(b_down.shape[0], N_EXPERTS, 1, d)
    wmap = lambda i, vt, ve, *_: (layer, ve[i], 0, 0)
    xmap = lambda i, vt, *_: (vt[i], 0)
    anywhere = pl.BlockSpec(memory_space=pl.ANY)
    grid_spec = pltpu.PrefetchScalarGridSpec(
        num_scalar_prefetch=len(visits),
        grid=(visits[0].shape[0],),
        in_specs=[pl.BlockSpec((tm, x_sorted.shape[1]), xmap),
                  anywhere,
                  pl.BlockSpec((None, None, 1, 2 * dff), wmap),
                  anywhere,
                  pl.BlockSpec((None, None, 1, d), wmap)],
        out_specs=pl.BlockSpec((tm, d // 2), xmap),
        scratch_shapes=[pltpu.VMEM((2, d, 2 * dff), F32), pltpu.VMEM((2, dff, d), F32),
                        pltpu.SemaphoreType.DMA((2, 2)),
                        pltpu.VMEM((d, 2 * dff), BF16), pltpu.VMEM((dff, d), BF16)],
    )
    return pl.pallas_call(
        functools.partial(_expert_kernel, layer=layer),
        grid_spec=grid_spec,
        out_shape=jax.ShapeDtypeStruct((ns, d // 2), F32),
        compiler_params=pltpu.CompilerParams(dimension_semantics=("arbitrary",),
                                             vmem_limit_bytes=VMEM_LIMIT),
        name="experts",
    )(*visits, x_sorted, w_up, bu, w_down, bd)


def _expert_visits(cnt, n_rows):
    tm = EXPERT_TILE
    n_steps = n_rows // tm + N_EXPERTS
    ends = jnp.cumsum(cnt)
    starts = ends - cnt
    first_tile = starts // tm
    n_vis = jnp.where(cnt > 0, (ends - 1) // tm - first_tile + 1, 0)
    vis_end = jnp.cumsum(n_vis)
    vis_start = vis_end - n_vis
    v = jnp.arange(n_steps, dtype=I32)
    vc = jnp.minimum(v, vis_end[-1] - 1)
    onehot = ((vis_start[None, :] <= vc[:, None]) & (vc[:, None] < vis_end[None, :])).astype(I32)
    pick = lambda a: jnp.sum(onehot * a[None, :], axis=1).astype(I32)
    expert = pick(jnp.arange(N_EXPERTS, dtype=I32))
    tile = pick(first_tile) + vc - pick(vis_start)
    real = v < vis_end[-1]
    lo = jnp.where(real, pick(starts), 0).astype(I32)
    hi = jnp.where(real, pick(ends), 0).astype(I32)
    ids = jnp.arange(N_EXPERTS, dtype=I32)
    used = cnt > 0
    ordinal = jnp.cumsum(used.astype(I32)) - used.astype(I32)
    later = jnp.min(jnp.where((ids[None, :] > ids[:, None]) & used[None, :], ids[None, :], N_EXPERTS), axis=1)
    following = jnp.where(later < N_EXPERTS, later, -1).astype(I32)
    return tile.astype(I32), expert, lo, hi, pick(ordinal), pick(following)


def _combine_kernel(y0_ref, y1_ref, y2_ref, y3_ref, gate_ref, h_ref, g_ref, b_ref, *o_refs, alpha, first_tiles):
    gate = gate_ref[...]
    lo = jnp.zeros(y0_ref.shape, F32)
    hi = jnp.zeros(y0_ref.shape, F32)
    for k, y_ref in enumerate((y0_ref, y1_ref, y2_ref, y3_ref)):
        w = lax.bitcast_convert_type(y_ref[...], U32)
        lo = lo + gate[:, k:k + 1] * lax.bitcast_convert_type(w << 16, F32)
        hi = hi + gate[:, k:k + 1] * lax.bitcast_convert_type(w & jnp.uint32(0xFFFF0000), F32)
    acc = jnp.concatenate([lo, hi], axis=1)
    out = _layer_norm(alpha * h_ref[...] + acc, g_ref[...], b_ref[...])
    if first_tiles is None:
        o_refs[0][...] = out
    else:
        i = pl.program_id(0)

        @pl.when(i < first_tiles)
        def _():
            o_refs[0][...] = out

        @pl.when(i >= first_tiles)
        def _():
            o_refs[1][...] = out


def _combine(y_tok, gates, h, ln_g, ln_b, alpha, split=None):
    n, d = h.shape
    tm = _wide_tile(n, COMBINE_TILE_MAX) if split is None else TOKEN_TILE
    nt = n // tm
    g, b = ln_g.reshape(1, d), ln_b.reshape(1, d)
    full = lambda a: pl.BlockSpec(a.shape, lambda i: (0, 0))
    ysp = lambda k: pl.BlockSpec((tm, d // 2), lambda i: (k * nt + i, 0))
    if split is None:
        first_tiles = None
        out_specs = pl.BlockSpec((tm, d), lambda i: (i, 0))
        out_shape = jax.ShapeDtypeStruct((n, d), F32)
    else:
        first_tiles = split // tm
        out_specs = [pl.BlockSpec((tm, d), lambda i: (jnp.minimum(i, first_tiles - 1), 0)),
                     pl.BlockSpec((tm, d), lambda i: (jnp.maximum(i - first_tiles, 0), 0))]
        out_shape = [jax.ShapeDtypeStruct((split, d), F32), jax.ShapeDtypeStruct((n - split, d), F32)]
    return pl.pallas_call(
        functools.partial(_combine_kernel, alpha=alpha, first_tiles=first_tiles),
        grid=(nt,),
        in_specs=[ysp(0), ysp(1), ysp(2), ysp(3),
                  pl.BlockSpec((tm, TOP_K), lambda i: (i, 0)),
                  pl.BlockSpec((tm, d), lambda i: (i, 0)), full(g), full(b)],
        out_specs=out_specs,
        out_shape=out_shape,
        compiler_params=pltpu.CompilerParams(dimension_semantics=("arbitrary",),
                                             vmem_limit_bytes=VMEM_LIMIT),
        name="combine",
    )(y_tok, y_tok, y_tok, y_tok, gates, h, g, b)


def _round_up(a, m):
    return (a + m - 1) // m * m


def _wide_tile(n, cap):
    return max(t for t in range(TOKEN_TILE, cap + 1, TOKEN_TILE) if n % t == 0)


def _moe(h, hp, idx, gates, rank, counts, w_up, b_up, w_down, b_down, ln_g, ln_b, alpha, layer, split=None):
    n, d = h.shape
    ns = TOP_K * n
    cnt = counts[:, 0]
    starts = jnp.cumsum(cnt) - cnt
    experts = jnp.arange(N_EXPERTS, dtype=I32)
    offs = jnp.sum(jnp.where(idx[:, :, None] == experts, starts, 0), axis=-1)
    pos = (offs + rank).reshape(-1).astype(I32)
    x_sorted = _sc_scatter_rows(hp, pos)
    y_sorted = _experts(x_sorted, _expert_visits(cnt, ns), w_up, b_up, w_down, b_down, layer)
    m2 = _round_up(ns, SC_ROW_ALIGN)
    fill = jnp.arange(ns, m2, dtype=I32) - ns
    y_tok = _sc_gather(y_sorted, jnp.concatenate([pos, fill]))
    return _combine(y_tok, gates.T, h, ln_g, ln_b, alpha, split)


def kernel(x_prompt, x_sample, cache_k, cache_v, state_conv, state_gla, w_in, conv_w, w_gate, b_gate,
           gla_norm_g, w_out, ln1_g, ln1_b, w_router, b_router, w_up, b_up, w_down, b_down, ln2_g, ln2_b):
    depth = w_in.shape[0]
    bp, seq, d = x_prompt.shape
    bs, ts, _ = x_sample.shape
    past = cache_k.shape[2]
    n_p = bp * seq
    n = n_p + bs * ts
    alpha = float((2 * depth) ** 0.25)
    x = jnp.concatenate([x_prompt.reshape(n_p, d), x_sample.reshape(bs * ts, d)], axis=0)
    ckt = cache_k.transpose(0, 1, 3, 4, 2).reshape(depth, bs, SB_DIM, past)
    cvt = cache_v.transpose(0, 1, 3, 4, 2).reshape(depth, bs, SB_DIM, past)
    zero_conv = jnp.zeros((bp, CONV_W - 1, CONV_DIM), F32)
    zero_gla = jnp.zeros((bp, GLA_HEADS, GLA_DK, GLA_DK), F32)
    wb = w_in.astype(BF16)
    o_r = 3 * SB_DIM
    n_r = 3 * CONV_DIM + 4 * GLA_DIM
    wq, wk, wv = wb[:, :, :SB_DIM], wb[:, :, SB_DIM:2 * SB_DIM], wb[:, :, 2 * SB_DIM:o_r]
    wkt, wvt = wk.transpose(0, 2, 1), wv.transpose(0, 2, 1)
    wr, wal = wb[:, :, o_r:o_r + n_r], wb[:, :, o_r + n_r:]
    wg = w_gate.astype(BF16)
    wo = w_out.astype(BF16)
    kt = jnp.zeros((depth, bp, SB_DIM, seq), F32)
    vt = jnp.zeros((depth, bp, SB_DIM, seq), F32)
    tm_p = min(PROMPT_TILE, seq)
    outs = [[] for _ in range(6)]
    for l in range(depth):
        shared = (wr[l], wal[l], wg[l], b_gate[l].reshape(1, -1))
        q_p, kt, vt, obc, conv_p, gla_p = _mix_prompt(
            x, jnp.zeros((n, CONV_DIM + GLA_DIM), BF16), kt, vt, l, (wq[l], wkt[l], wvt[l]) + shared,
            conv_w[l], gla_norm_g[l], zero_conv, zero_gla, bp, seq, tm_p)
        q_s, ks, vs, rest_s, la_s = _in_proj(x, n_p, n - n_p, TOKEN_TILE, (wq[l], wk[l], wv[l]) + shared)
        oa = _sb_prompt(jnp.zeros((n, SB_DIM), BF16), q_p, kt, vt, l, bp, seq)
        oa = _sb_decode(oa, q_s, ks, vs, ckt, cvt, l, n_p, bs, ts)
        obc, conv_s, gla_s = _conv_gla(obc, n, rest_s, la_s, conv_w[l], gla_norm_g[l], state_conv[l],
                                       state_gla[l], n_p, bs, ts)
        h, hp, idx, gates, rank, counts = _out_router(oa, obc, x, wo[l, :SB_DIM], wo[l, SB_DIM:], ln1_g[l],
                                                      ln1_b[l], w_router[l], b_router[l], alpha)
        x = _moe(h, hp, idx, gates, rank, counts, w_up, b_up, w_down, b_down, ln2_g[l], ln2_b[l],
                 alpha, l, split=n_p if l == depth - 1 else None)
        outs[0].append(conv_p)
        outs[1].append(gla_p)
        outs[2].append(ks.reshape(bs, ts, SB_HEADS, HEAD_DIM))
        outs[3].append(vs.reshape(bs, ts, SB_HEADS, HEAD_DIM))
        outs[4].append(conv_s)
        outs[5].append(gla_s)
    k_prompt = kt.reshape(depth, bp, SB_HEADS, HEAD_DIM, seq).transpose(0, 1, 4, 2, 3)
    v_prompt = vt.reshape(depth, bp, SB_HEADS, HEAD_DIM, seq).transpose(0, 1, 4, 2, 3)
    st = [jnp.stack(o) for o in outs]
    y_prompt, y_sample = x
    return (y_prompt.reshape(bp, seq, d), y_sample.reshape(bs, ts, d), k_prompt, v_prompt,
            st[0], st[1], st[2], st[3], st[4], st[5])
```

```python
import functools

import jax
import jax.numpy as jnp
from jax import lax
from jax.experimental import pallas as pl
from jax.experimental.pallas import tpu as pltpu
from jax.experimental.pallas import tpu_sc as plsc

F32 = jnp.float32
BF16 = jnp.bfloat16
I32 = jnp.int32
U32 = jnp.uint32

HEAD_DIM = 64
SB_HEADS = 8
SB_DIM = SB_HEADS * HEAD_DIM
CONV_DIM = 256
CONV_W = 3
GLA_HEADS = 4
GLA_DK = 64
GLA_DIM = GLA_HEADS * GLA_DK
GLA_RANK = 16
GLA_TAU = 16.0
GLA_CHUNK = 64
N_EXPERTS = 32
TOP_K = 4
SWIGLU_LIMIT = 7.0
SWIGLU_ALPHA = 1.702
NORM_EPS = 1e-5

LANES = 128
SC_CORES = 2
SC_SUBCORES = 16
SC_CHUNK = 128
SC_ROW_ALIGN = SC_CORES * SC_SUBCORES * SC_CHUNK
SC_SCATTER_MAX = 104
TOKEN_TILE = 256
WIDE_TILE_MAX = 1280
COMBINE_TILE_MAX = 640
PROMPT_TILE = 512
EXPERT_TILE = 512
EXPERT_SUB = 256
SB_TQ = 256
SB_BK = 256
SB_GROUP = 256
DEC_TK = 512
GLA_TC = 128
VMEM_LIMIT = 48 * 1024 * 1024
SB_DEAD = -100.0


def _dot(a, b):
    return jnp.dot(a, b, preferred_element_type=F32)


def _dot_nt(a, b):
    return lax.dot_general(a, b, (((1,), (1,)), ((), ())), preferred_element_type=F32)


def _split_bf16(x):
    hi = x.astype(BF16)
    lo = (x - hi.astype(F32)).astype(BF16)
    return hi, lo


def _dot_exact_rhs(x, m):
    hi, lo = _split_bf16(x)
    return _dot(hi, m) + _dot(lo, m)


def _dot_exact_lhs(m, x):
    hi, lo = _split_bf16(x)
    return _dot(m, hi) + _dot(m, lo)


def _pack_bf16_pairs(x):
    c = x.shape[1] // 2
    bits = lax.bitcast_convert_type(x.astype(BF16).astype(F32), U32)
    return (bits[:, :c] >> 16) | (bits[:, c:] & jnp.uint32(0xFFFF0000))


def _unpack_bf16_pairs(w):
    lo = lax.bitcast_convert_type(w << 16, F32)
    hi = lax.bitcast_convert_type(w & jnp.uint32(0xFFFF0000), F32)
    return jnp.concatenate([lo, hi], axis=1).astype(BF16)


def _softplus(z):
    return jnp.maximum(z, 0.0) + jnp.log(1.0 + jnp.exp(-jnp.abs(z)))


def _layer_norm(y, g, b):
    mu = jnp.mean(y, axis=-1, keepdims=True)
    yc = y - mu
    var = jnp.mean(yc * yc, axis=-1, keepdims=True)
    return yc * lax.rsqrt(var + NORM_EPS) * g + b


def _strict_upper(n, copies=1):
    r = lax.broadcasted_iota(I32, (copies * n, n), 0)
    c = lax.broadcasted_iota(I32, (copies * n, n), 1)
    for k in range(1, copies):
        r = r - jnp.where(r >= n, n, 0)
    return jnp.where(r > c, 1.0, 0.0).astype(BF16)


def _project(x_ref, wq_ref, wk_ref, wv_ref, wr_ref, wal_ref, wg_ref, bg_ref,
             q_ref, k_ref, v_ref, r_ref, la_ref, *, transposed_kv):
    xb = x_ref[...].astype(BF16)
    for c in range(0, SB_DIM, 256):
        q_ref[:, c:c + 256] = (_dot(xb, wq_ref[:, c:c + 256]) * (HEAD_DIM ** -0.5)).astype(BF16)
    for c in range(0, r_ref.shape[1], 256):
        r_ref[:, c:c + 256] = _dot(xb, wr_ref[:, c:c + 256])
    al = _dot(xb, wal_ref[...])
    g = _dot(al.astype(BF16), wg_ref[...]) + bg_ref[...]
    la_ref[...] = -_softplus(-g) * (1.0 / GLA_TAU)
    for c in range(0, SB_DIM, 256):
        if transposed_kv:
            k_ref[c:c + 256, :] = _dot_nt(wk_ref[c:c + 256, :], xb)
            v_ref[c:c + 256, :] = _dot_nt(wv_ref[c:c + 256, :], xb)
        else:
            k_ref[:, c:c + 256] = _dot(xb, wk_ref[:, c:c + 256])
            v_ref[:, c:c + 256] = _dot(xb, wv_ref[:, c:c + 256])


def _in_proj(x, row0, rows, tm, weights):
    d = x.shape[1]
    n_r = weights[3].shape[1]
    rb = row0 // tm
    full = lambda a: pl.BlockSpec(a.shape, lambda i: (0,) * a.ndim)
    row = lambda w: pl.BlockSpec((tm, w), lambda i: (i, 0))
    sds = lambda w, dt: jax.ShapeDtypeStruct((rows, w), dt)
    return pl.pallas_call(
        functools.partial(_project, transposed_kv=False),
        grid=(rows // tm,),
        in_specs=[pl.BlockSpec((tm, d), lambda i: (rb + i, 0))] + [full(w) for w in weights],
        out_specs=[row(SB_DIM), row(SB_DIM), row(SB_DIM), row(n_r), row(GLA_DIM)],
        out_shape=[sds(SB_DIM, BF16), sds(SB_DIM, F32), sds(SB_DIM, F32), sds(n_r, F32), sds(GLA_DIM, F32)],
        compiler_params=pltpu.CompilerParams(dimension_semantics=("arbitrary",),
                                             vmem_limit_bytes=VMEM_LIMIT),
        name="in_proj",
    )(x, *weights)


def _sb_weights(z, tri, run, mask):
    sp = _softplus(z)
    l1m = -sp
    lsig = z - sp
    if mask is not None:
        l1m = jnp.where(mask, l1m, 0.0)
    if tri.shape[0] == 2 * z.shape[1]:
        hi = lax.bitcast_convert_type(lax.bitcast_convert_type(l1m, U32) & jnp.uint32(0xFFFF0000), F32)
        parts = jnp.concatenate([hi.astype(BF16), (l1m - hi).astype(BF16)], axis=1)
        rest = _dot(parts, tri) + run
    else:
        rest = _dot_exact_rhs(l1m, tri) + run
    a = jnp.exp(lsig + rest)
    if mask is not None:
        a = jnp.where(mask, a, 0.0)
    return a.astype(BF16), run + jnp.sum(l1m, axis=1, keepdims=True)


def _sb_prompt_kernel(oa_ref, q_ref, kt_ref, vt_ref, o_ref, *, tq, bk):
    del oa_ref
    qi = pl.program_id(2)
    q = q_ref[...]
    n_hh = SB_GROUP // HEAD_DIM
    lane = lax.broadcasted_iota(I32, (1, SB_GROUP), 1)
    in_head = [(lane // HEAD_DIM) == h for h in range(n_hh)]
    qh = [jnp.where(m, q, jnp.zeros_like(q)) for m in in_head]
    tri = _strict_upper(bk, copies=2)
    n_full = (qi * tq) // bk
    qpos = qi * tq + lax.broadcasted_iota(I32, (tq, bk), 0)
    kpos = n_full * bk + lax.broadcasted_iota(I32, (tq, bk), 1)
    diag_mask = kpos < qpos

    def tile(jb, runs, mask):
        ks = pl.multiple_of(jb * bk, bk)
        kt = kt_ref[:, pl.ds(ks, bk)].astype(BF16)
        vt = vt_ref[:, pl.ds(ks, bk)].astype(BF16)
        out = jnp.zeros((tq, SB_GROUP), F32)
        new_runs = []
        for h in range(n_hh):
            a, run = _sb_weights(_dot(qh[h], kt), tri, runs[h], mask)
            out = jnp.where(in_head[h], _dot_nt(a, vt), out)
            new_runs.append(run)
        return out, tuple(new_runs)

    def alive_of(runs):
        m = jnp.max(runs[0])
        for r in runs[1:]:
            m = jnp.maximum(m, jnp.max(r))
        return m > SB_DEAD

    acc, runs = tile(n_full, tuple(jnp.zeros((tq, 1), F32) for _ in range(n_hh)), diag_mask)

    def cond(carry):
        j, alive, _, _ = carry
        return (j >= 0) & alive

    def body(carry):
        j, _, acc, runs = carry
        pv, runs = tile(j, runs, None)
        return j - 1, alive_of(runs), acc + pv, runs

    _, _, acc, _ = lax.while_loop(cond, body, (n_full - 1, jnp.bool_(True), acc, runs))
    o_ref[...] = acc.astype(o_ref.dtype)


def _sb_prompt(oa, q, kt, vt, layer, batch, seq):
    tq, bk = min(SB_TQ, seq), min(SB_BK, seq)
    nq = seq // tq
    hp = SB_DIM // SB_GROUP
    kv_spec = pl.BlockSpec((None, None, SB_GROUP, seq), lambda b, p, i: (layer, b, p, 0))
    return pl.pallas_call(
        functools.partial(_sb_prompt_kernel, tq=tq, bk=bk),
        grid=(batch, hp, nq),
        in_specs=[pl.BlockSpec(memory_space=pl.ANY),
                  pl.BlockSpec((tq, SB_GROUP), lambda b, p, i: (b * nq + i, p)), kv_spec, kv_spec],
        out_specs=pl.BlockSpec((tq, SB_GROUP), lambda b, p, i: (b * nq + i, p)),
        out_shape=jax.ShapeDtypeStruct(oa.shape, oa.dtype),
        input_output_aliases={0: 0},
        compiler_params=pltpu.CompilerParams(
            dimension_semantics=("arbitrary", "arbitrary", "arbitrary"),
            vmem_limit_bytes=VMEM_LIMIT),
        name="sb_prompt",
    )(oa, q, kt, vt)


def _sb_decode_kernel(oa_ref, q_ref, kn_ref, vn_ref, kc_hbm, vc_hbm, o_ref, kbuf, vbuf, sem, acc_ref, run_ref,
                      alive_ref, *, t, tk, bk, layer, nkb):
    del oa_ref
    b = pl.program_id(0)
    q = q_ref[...]

    def block_copies(j, slot):
        cols = pl.ds(pl.multiple_of((nkb - 1 - j) * tk, tk), tk)
        return (pltpu.make_async_copy(kc_hbm.at[layer, b, :, cols], kbuf.at[slot], sem.at[0, slot]),
                pltpu.make_async_copy(vc_hbm.at[layer, b, :, cols], vbuf.at[slot], sem.at[1, slot]))

    for cp in block_copies(0, 0):
        cp.start()

    kn = kn_ref[...].astype(BF16)
    vn = vn_ref[...].astype(BF16)
    lane = lax.broadcasted_iota(I32, (1, SB_DIM), 1)
    r = lax.broadcasted_iota(I32, (t, t), 0)
    c = lax.broadcasted_iota(I32, (t, t), 1)
    mask = c < r
    tri_new = _strict_upper(t)
    for h in range(SB_HEADS):
        qh = jnp.where((lane // HEAD_DIM) == h, q, jnp.zeros_like(q))
        a, run = _sb_weights(_dot_nt(qh, kn), tri_new, jnp.zeros((t, 1), F32), mask)
        acc_ref[h] = _dot(a, vn)[:, h * HEAD_DIM:(h + 1) * HEAD_DIM]
        run_ref[h * t:(h + 1) * t, :] = run
    alive_ref[0] = (jnp.max(run_ref[...]) > SB_DEAD).astype(I32)

    tri = _strict_upper(bk, copies=2)

    def cond(carry):
        j, alive = carry
        return (j < nkb) & (alive > 0)

    def body(carry):
        j, _ = carry
        slot = j % 2

        @pl.when(j + 1 < nkb)
        def _():
            for cp in block_copies(j + 1, 1 - slot):
                cp.start()

        for cp in block_copies(j, slot):
            cp.wait()
        for c in range(tk // bk - 1, -1, -1):
            cols = slice(c * bk, (c + 1) * bk)

            @pl.when(alive_ref[0] > 0)
            def _():
                z = jnp.concatenate(
                    [_dot(q[:, h * HEAD_DIM:(h + 1) * HEAD_DIM],
                          kbuf[slot, h * HEAD_DIM:(h + 1) * HEAD_DIM, cols].astype(BF16))
                     for h in range(SB_HEADS)], axis=0)
                a, run = _sb_weights(z, tri, run_ref[...], None)
                for h in range(SB_HEADS):
                    vt = vbuf[slot, h * HEAD_DIM:(h + 1) * HEAD_DIM, cols].astype(BF16)
                    acc_ref[h] = acc_ref[h] + _dot_nt(a[h * t:(h + 1) * t, :], vt)
                run_ref[...] = run
                alive_ref[0] = (jnp.max(run) > SB_DEAD).astype(I32)

        return j + 1, alive_ref[0]

    j_end, _ = lax.while_loop(cond, body, (jnp.int32(0), alive_ref[0]))

    @pl.when(j_end < nkb)
    def _():
        for cp in block_copies(j_end, j_end % 2):
            cp.wait()

    for h in range(SB_HEADS):
        o_ref[:, h * HEAD_DIM:(h + 1) * HEAD_DIM] = acc_ref[h].astype(o_ref.dtype)


def _sb_decode(oa, q, ks, vs, cache_kt, cache_vt, layer, row0, batch, t):
    past = cache_kt.shape[3]
    tk = min(DEC_TK, past)
    bk = min(SB_BK, tk)
    nkb = past // tk
    rb = row0 // t
    new = pl.BlockSpec((t, SB_DIM), lambda b: (b, 0))
    anywhere = pl.BlockSpec(memory_space=pl.ANY)
    return pl.pallas_call(
        functools.partial(_sb_decode_kernel, t=t, tk=tk, bk=bk, layer=layer, nkb=nkb),
        grid=(batch,),
        in_specs=[anywhere, new, new, new, anywhere, anywhere],
        out_specs=pl.BlockSpec((t, SB_DIM), lambda b: (rb + b, 0)),
        out_shape=jax.ShapeDtypeStruct(oa.shape, oa.dtype),
        input_output_aliases={0: 0},
        scratch_shapes=[pltpu.VMEM((2, SB_DIM, tk), F32),
                        pltpu.VMEM((2, SB_DIM, tk), F32),
                        pltpu.SemaphoreType.DMA((2, 2)),
                        pltpu.VMEM((SB_HEADS, t, HEAD_DIM), F32),
                        pltpu.VMEM((SB_HEADS * t, 1), F32),
                        pltpu.SMEM((1,), I32)],
        compiler_params=pltpu.CompilerParams(dimension_semantics=("arbitrary",),
                                             vmem_limit_bytes=VMEM_LIMIT),
        name="sb_decode",
    )(oa, q, ks, vs, cache_kt, cache_vt)


def _mix_state_load(cprev_ref, sprev_ref, ctail_ref, st_ref):
    n_hh = LANES // GLA_DK
    ctail_ref[...] = cprev_ref[...]
    for p in range(GLA_DIM // LANES):
        st_ref[p] = jnp.zeros((LANES, LANES), F32)
        for hh in range(n_hh):
            st_ref[p, hh * GLA_DK:(hh + 1) * GLA_DK, hh * GLA_DK:(hh + 1) * GLA_DK] = sprev_ref[p * n_hh + hh].T


def _mix_state_store(ctail_ref, st_ref, cnew_ref, snew_ref):
    n_hh = LANES // GLA_DK
    cnew_ref[...] = ctail_ref[...]
    for p in range(GLA_DIM // LANES):
        for hh in range(n_hh):
            blk = st_ref[p, hh * GLA_DK:(hh + 1) * GLA_DK, :]
            snew_ref[p * n_hh + hh] = blk.T[hh * GLA_DK:(hh + 1) * GLA_DK, :]


def _conv_gla_block(bg_ref, cg_ref, u_ref, qc_ref, kc_ref, vc_ref, gc_ref, la_ref, cw_ref, ng_ref,
                    o_ref, ctail_ref, st_ref, tc, chunk):
    n_pair = GLA_DIM // LANES
    n_hh = LANES // GLA_DK

    z = cg_ref[...] * u_ref[...]
    tail = ctail_ref[...]
    row = lax.broadcasted_iota(I32, z.shape, 0)
    z1 = jnp.where(row < 1, tail[1:2, :], pltpu.roll(z, 1, 0))
    z2 = jnp.where(row < 2, jnp.where(row < 1, tail[0:1, :], tail[1:2, :]), pltpu.roll(z, 2, 0))
    cw = cw_ref[...]
    y = z2 * cw[0:1, :] + z1 * cw[1:2, :] + z * cw[2:3, :]
    o_ref[:, 0:CONV_DIM] = (bg_ref[...] * y).astype(o_ref.dtype)
    ctail_ref[...] = z[tc - 2:tc, :]

    r = lax.broadcasted_iota(I32, (tc, tc), 0)
    c = lax.broadcasted_iota(I32, (tc, tc), 1)
    same_chunk = (r // chunk) == (c // chunk)
    chunk_sum = jnp.where(same_chunk, 1.0, 0.0).astype(BF16)
    chunk_cumsum = jnp.where(same_chunk & (c <= r), 1.0, 0.0).astype(BF16)
    lane = lax.broadcasted_iota(I32, (1, LANES), 1)
    lr = lax.broadcasted_iota(I32, (LANES, LANES), 0) // GLA_DK
    lc = lax.broadcasted_iota(I32, (LANES, LANES), 1) // GLA_DK
    same_head = lr == lc
    head_mean = jnp.where(same_head, 1.0 / GLA_DK, 0.0).astype(BF16)
    r2 = lax.broadcasted_iota(I32, (tc, n_hh * tc), 0)
    c2 = lax.broadcasted_iota(I32, (tc, n_hh * tc), 1)
    c2 = c2 - jnp.where(c2 >= tc, tc, 0)
    causal2 = ((r2 // chunk) == (c2 // chunk)) & (c2 <= r2)
    for p in range(n_pair):
        cols = slice(p * LANES, (p + 1) * LANES)
        la = la_ref[:, cols]
        b = _dot_exact_lhs(chunk_cumsum, la)
        tot = _dot_exact_lhs(chunk_sum, la)
        k = kc_ref[:, cols]
        qe = (qc_ref[:, cols] * (GLA_DK ** -0.5) * jnp.exp(b)).astype(BF16)
        ke = (k * jnp.exp(-b)).astype(BF16)
        kd = (k * jnp.exp(tot - b)).astype(BF16)
        decay = jnp.exp(tot)
        vb = vc_ref[:, cols].astype(BF16)
        zeros = jnp.zeros_like(ke)
        ke_st = jnp.concatenate([jnp.where((lane // GLA_DK) == hh, ke, zeros)
                                 for hh in range(n_hh)], axis=0)
        v_st = jnp.concatenate([jnp.where((lane // GLA_DK) == hh, vb, zeros)
                                for hh in range(n_hh)], axis=0)
        a = jnp.where(causal2, _dot_nt(qe, ke_st), 0.0)
        o = _dot(a.astype(BF16), v_st)
        st = st_ref[p]
        from_state = []
        for ci in range(tc // chunk):
            rows = slice(ci * chunk, (ci + 1) * chunk)
            from_state.append(_dot_nt(qe[rows], st.astype(BF16)))
            upd = _dot(vb[rows].T, kd[rows])
            st = jnp.where(same_head, st * decay[ci * chunk:ci * chunk + 1, :] + upd, 0.0)
        st_ref[p] = st
        o = o + jnp.concatenate(from_state, axis=0)
        ms = _dot_exact_rhs(o * o, head_mean)
        o = o * lax.rsqrt(ms + NORM_EPS) * ng_ref[:, cols]
        g = gc_ref[:, cols]
        o = o * (g * (1.0 / (1.0 + jnp.exp(-g))))
        o_ref[:, CONV_DIM + p * LANES:CONV_DIM + (p + 1) * LANES] = o.astype(o_ref.dtype)


def _conv_gla_kernel(*refs, tc, chunk):
    (_, bg_ref, cg_ref, u_ref, qc_ref, kc_ref, vc_ref, gc_ref, la_ref, cw_ref, ng_ref, cprev_ref, sprev_ref,
     o_ref, cnew_ref, snew_ref, ctail_ref, st_ref) = refs
    ti = pl.program_id(1)

    @pl.when(ti == 0)
    def _():
        _mix_state_load(cprev_ref, sprev_ref, ctail_ref, st_ref)

    _conv_gla_block(bg_ref, cg_ref, u_ref, qc_ref, kc_ref, vc_ref, gc_ref, la_ref, cw_ref, ng_ref,
                    o_ref, ctail_ref, st_ref, tc, chunk)

    @pl.when(ti == pl.num_programs(1) - 1)
    def _():
        _mix_state_store(ctail_ref, st_ref, cnew_ref, snew_ref)


def _conv_gla(obc_prev, n, rest, la, conv_w, norm_g, conv_prev, gla_prev, row0, batch, t):
    tc = min(GLA_TC, t)
    chunk = min(GLA_CHUNK, t)
    nt = t // tc
    rb = row0 // tc
    col = lambda j: pl.BlockSpec((tc, CONV_DIM), lambda b, i: (b * nt + i, j))
    const2 = lambda a: pl.BlockSpec(a.shape, lambda b, i: (0, 0))
    ng = norm_g.reshape(1, GLA_DIM)
    args = (rest,) * 7 + (la, conv_w, ng, conv_prev, gla_prev)
    return pl.pallas_call(
        functools.partial(_conv_gla_kernel, tc=tc, chunk=chunk),
        grid=(batch, nt),
        in_specs=[
            pl.BlockSpec(memory_space=pl.ANY), col(0), col(1), col(2), col(3), col(4), col(5), col(6),
            pl.BlockSpec((tc, GLA_DIM), lambda b, i: (b * nt + i, 0)),
            const2(conv_w), const2(ng),
            pl.BlockSpec((None, CONV_W - 1, CONV_DIM), lambda b, i: (b, 0, 0)),
            pl.BlockSpec((None, GLA_HEADS, GLA_DK, GLA_DK), lambda b, i: (b, 0, 0, 0))],
        out_specs=[pl.BlockSpec((tc, CONV_DIM + GLA_DIM), lambda b, i: (rb + b * nt + i, 0)),
                   pl.BlockSpec((None, CONV_W - 1, CONV_DIM), lambda b, i: (b, 0, 0)),
                   pl.BlockSpec((None, GLA_HEADS, GLA_DK, GLA_DK), lambda b, i: (b, 0, 0, 0))],
        out_shape=[jax.ShapeDtypeStruct((n, CONV_DIM + GLA_DIM), BF16),
                   jax.ShapeDtypeStruct((batch, CONV_W - 1, CONV_DIM), F32),
                   jax.ShapeDtypeStruct((batch, GLA_HEADS, GLA_DK, GLA_DK), F32)],
        input_output_aliases={0: 0},
        scratch_shapes=[pltpu.VMEM((CONV_W - 1, CONV_DIM), F32),
                        pltpu.VMEM((GLA_DIM // LANES, LANES, LANES), F32)],
        compiler_params=pltpu.CompilerParams(dimension_semantics=("arbitrary", "arbitrary"),
                                             vmem_limit_bytes=VMEM_LIMIT),
        name="conv_gla",
    )(obc_prev, *args)


def _mix_prompt_kernel(*refs, tc, chunk):
    (_, _, _, x_ref, wq_ref, wk_ref, wv_ref, wr_ref, wal_ref, wg_ref, bg_ref, cw_ref, ng_ref, cprev_ref,
     sprev_ref, q_ref, kt_ref, vt_ref, o_ref, cnew_ref, snew_ref, r_scr, la_scr, ctail_ref, st_ref) = refs
    ti = pl.program_id(1)
    _project(x_ref, wq_ref, wk_ref, wv_ref, wr_ref, wal_ref, wg_ref, bg_ref,
             q_ref, kt_ref, vt_ref, r_scr, la_scr, transposed_kv=True)

    @pl.when(ti == 0)
    def _():
        _mix_state_load(cprev_ref, sprev_ref, ctail_ref, st_ref)

    for s in range(x_ref.shape[0] // tc):
        rows = pl.ds(s * tc, tc)
        part = [r_scr.at[rows, pl.ds(j * CONV_DIM, CONV_DIM)] for j in range(7)]
        _conv_gla_block(*part, la_scr.at[rows], cw_ref, ng_ref, o_ref.at[rows], ctail_ref, st_ref, tc, chunk)

    @pl.when(ti == pl.num_programs(1) - 1)
    def _():
        _mix_state_store(ctail_ref, st_ref, cnew_ref, snew_ref)


def _mix_prompt(x, obc_prev, kt, vt, layer, weights, conv_w, norm_g, conv_prev, gla_prev, batch, seq, tm):
    d = x.shape[1]
    n_p = batch * seq
    nt = seq // tm
    tc = min(GLA_TC, tm)
    wq, wk, wv, wr, wal, wg, bg = weights
    ng = norm_g.reshape(1, GLA_DIM)
    full = lambda a: pl.BlockSpec(a.shape, lambda b, i: (0,) * a.ndim)
    row = lambda w: pl.BlockSpec((tm, w), lambda b, i: (b * nt + i, 0))
    kv_spec = pl.BlockSpec((None, None, SB_DIM, tm), lambda b, i: (layer, b, 0, i))
    conv_spec = pl.BlockSpec((None, CONV_W - 1, CONV_DIM), lambda b, i: (b, 0, 0))
    gla_spec = pl.BlockSpec((None, GLA_HEADS, GLA_DK, GLA_DK), lambda b, i: (b, 0, 0, 0))
    anywhere = pl.BlockSpec(memory_space=pl.ANY)
    consts = (wq, wk, wv, wr, wal, wg, bg, conv_w, ng)
    return pl.pallas_call(
        functools.partial(_mix_prompt_kernel, tc=tc, chunk=min(GLA_CHUNK, tc)),
        grid=(batch, nt),
        in_specs=[anywhere, anywhere, anywhere, row(d)] + [full(w) for w in consts] + [conv_spec, gla_spec],
        out_specs=[row(SB_DIM), kv_spec, kv_spec, row(CONV_DIM + GLA_DIM), conv_spec, gla_spec],
        out_shape=[jax.ShapeDtypeStruct((n_p, SB_DIM), BF16),
                   jax.ShapeDtypeStruct(kt.shape, kt.dtype), jax.ShapeDtypeStruct(vt.shape, vt.dtype),
                   jax.ShapeDtypeStruct(obc_prev.shape, obc_prev.dtype),
                   jax.ShapeDtypeStruct((batch, CONV_W - 1, CONV_DIM), F32),
                   jax.ShapeDtypeStruct((batch, GLA_HEADS, GLA_DK, GLA_DK), F32)],
        input_output_aliases={0: 1, 1: 2, 2: 3},
        scratch_shapes=[pltpu.VMEM((tm, wr.shape[1]), F32),
                        pltpu.VMEM((tm, GLA_DIM), F32),
                        pltpu.VMEM((CONV_W - 1, CONV_DIM), F32),
                        pltpu.VMEM((GLA_DIM // LANES, LANES, LANES), F32)],
        compiler_params=pltpu.CompilerParams(dimension_semantics=("arbitrary", "arbitrary"),
                                             vmem_limit_bytes=VMEM_LIMIT),
        name="mix_prompt",
    )(kt, vt, obc_prev, x, *consts, conv_prev, gla_prev)


def _out_router_kernel(oa_ref, obc_ref, x_ref, wa_ref, wb_ref, g_ref, b_ref, wr_ref, br_ref,
                       h_ref, hp_ref, idx_ref, gate_ref, rank_ref, cnt_ref, carry_ref, *, alpha):
    i = pl.program_id(0)
    tm = x_ref.shape[0]

    @pl.when(i == 0)
    def _():
        carry_ref[...] = jnp.zeros_like(carry_ref)

    m = _dot(oa_ref[...], wa_ref[...]) + _dot(obc_ref[...], wb_ref[...])
    h = _layer_norm(alpha * x_ref[...] + m, g_ref[...], b_ref[...])
    h_ref[...] = h
    hp_ref[...] = lax.bitcast_convert_type(_pack_bf16_pairs(h), F32)

    h_hi, h_lo = _split_bf16(h)
    w_hi, w_lo = _split_bf16(wr_ref[...])
    logit = _dot_nt(w_hi, h_hi) + _dot_nt(w_hi, h_lo) + _dot_nt(w_lo, h_hi) + br_ref[...]
    eid = lax.broadcasted_iota(I32, (N_EXPERTS, tm), 0)
    r = lax.broadcasted_iota(I32, (tm, tm), 0)
    c = lax.broadcasted_iota(I32, (tm, tm), 1)
    before = jnp.where(r < c, 1.0, 0.0).astype(BF16)
    base = carry_ref[...]
    vals, idxs, onehots, bases = [], [], [], []
    for _ in range(TOP_K):
        mx = jnp.max(logit, axis=0, keepdims=True)
        sel = jnp.min(jnp.where(logit == mx, eid, N_EXPERTS), axis=0, keepdims=True)
        hit = eid == sel
        logit = jnp.where(hit, -jnp.inf, logit)
        onehot = jnp.where(hit, 1.0, 0.0)
        onehots.append(onehot)
        bases.append(base)
        base = base + jnp.sum(onehot, axis=1, keepdims=True)
        vals.append(mx)
        idxs.append(sel)
    carry_ref[...] = base
    earlier = _dot(jnp.concatenate(onehots, axis=0).astype(BF16), before)
    ranks = [jnp.sum(onehots[k] * (earlier[k * N_EXPERTS:(k + 1) * N_EXPERTS] + bases[k]),
                     axis=0, keepdims=True) for k in range(TOP_K)]
    e = [jnp.exp(v - vals[0]) for v in vals]
    inv = 1.0 / (e[0] + e[1] + e[2] + e[3])
    idx_ref[...] = jnp.concatenate(idxs, axis=0)
    gate_ref[...] = jnp.concatenate([ek * inv for ek in e], axis=0)
    rank_ref[...] = jnp.concatenate(ranks, axis=0).astype(I32)
    cnt_ref[...] = jnp.broadcast_to(base, cnt_ref.shape).astype(I32)


def _out_router(oa, obc, x, wa, wb, ln_g, ln_b, w_router, b_router, alpha):
    n, d = x.shape
    tm = _wide_tile(n, WIDE_TILE_MAX)
    wr = w_router.T
    br = b_router.reshape(N_EXPERTS, 1)
    g, b = ln_g.reshape(1, d), ln_b.reshape(1, d)
    full = lambda a: pl.BlockSpec(a.shape, lambda i: (0, 0))
    row = lambda w: pl.BlockSpec((tm, w), lambda i: (i, 0))
    colb = pl.BlockSpec((TOP_K, tm), lambda i: (0, i))
    return pl.pallas_call(
        functools.partial(_out_router_kernel, alpha=alpha),
        grid=(n // tm,),
        in_specs=[row(SB_DIM), row(CONV_DIM + GLA_DIM), row(d), full(wa), full(wb), full(g), full(b),
                  full(wr), full(br)],
        out_specs=[row(d), row(d // 2), colb, colb, colb,
                   pl.BlockSpec((N_EXPERTS, LANES), lambda i: (0, 0))],
        out_shape=[jax.ShapeDtypeStruct((n, d), F32),
                   jax.ShapeDtypeStruct((n, d // 2), F32),
                   jax.ShapeDtypeStruct((TOP_K, n), I32),
                   jax.ShapeDtypeStruct((TOP_K, n), F32),
                   jax.ShapeDtypeStruct((TOP_K, n), I32),
                   jax.ShapeDtypeStruct((N_EXPERTS, LANES), I32)],
        scratch_shapes=[pltpu.VMEM((N_EXPERTS, 1), F32)],
        compiler_params=pltpu.CompilerParams(dimension_semantics=("arbitrary",),
                                             vmem_limit_bytes=VMEM_LIMIT),
        name="out_router",
    )(oa, obc, x, wa, wb, g, b, wr, br)


def _sc_gather(table, idx):
    m = idx.shape[0]
    d = table.shape[1]
    per_worker = m // (SC_CORES * SC_SUBCORES)
    half = SC_CHUNK // 2
    n_groups = per_worker // SC_CHUNK
    mesh = plsc.VectorSubcoreMesh(core_axis_name="c", subcore_axis_name="s")

    @functools.partial(
        pl.kernel, mesh=mesh,
        out_type=jax.ShapeDtypeStruct((m, d), table.dtype),
        scratch_types=[pltpu.VMEM((per_worker,), I32)] + [pltpu.VMEM((half, d), table.dtype)] * 2
                      + [pltpu.SemaphoreType.DMA] * 4,
        name="sc_gather",
    )
    def gather(table_hbm, idx_hbm, out_hbm, idx_v, buf0, buf1, g0, g1, w0, w1):
        wid = lax.axis_index("s") * SC_CORES + lax.axis_index("c")
        base = wid * per_worker
        pltpu.sync_copy(idx_hbm.at[pl.ds(base, per_worker)], idx_v)

        @pl.loop(0, n_groups)
        def _(g):
            o0 = pl.multiple_of(g * SC_CHUNK, 8)
            o1 = pl.multiple_of(g * SC_CHUNK + half, 8)
            ga = pltpu.async_copy(table_hbm.at[idx_v.at[pl.ds(o0, half)]], buf0, g0)
            gb = pltpu.async_copy(table_hbm.at[idx_v.at[pl.ds(o1, half)]], buf1, g1)
            ga.wait()
            wa = pltpu.async_copy(buf0, out_hbm.at[pl.ds(base + o0, half)], w0)
            gb.wait()
            wb = pltpu.async_copy(buf1, out_hbm.at[pl.ds(base + o1, half)], w1)
            wa.wait()
            wb.wait()

    return gather(table, idx)


def _sc_chunk(per_worker):
    return max(c for c in range(8, SC_SCATTER_MAX + 1, 8) if per_worker % c == 0)


def _sc_scatter_rows(h, pos):
    n, d = h.shape
    per_worker = n // (SC_CORES * SC_SUBCORES)
    ch = _sc_chunk(per_worker)
    n_chunks = per_worker // ch
    mesh = plsc.VectorSubcoreMesh(core_axis_name="c", subcore_axis_name="s")

    @functools.partial(
        pl.kernel, mesh=mesh,
        out_type=jax.ShapeDtypeStruct((TOP_K * n, d), h.dtype),
        scratch_types=[pltpu.VMEM((TOP_K * n_chunks, ch), I32), pltpu.VMEM((ch, d), h.dtype),
                       pltpu.VMEM((ch, d), h.dtype)] + [pltpu.SemaphoreType.DMA] * 5,
        name="sc_scatter",
    )
    def scatter(h_hbm, pos_hbm, out_hbm, idx_v, rows0, rows1, isem, r0, r1, w0, w1):
        bufs, rsem, wsem = (rows0, rows1), (r0, r1), (w0, w1)
        wid = lax.axis_index("s") * SC_CORES + lax.axis_index("c")
        base = pl.multiple_of(wid * per_worker, 8)

        def rows_load(c):
            return pltpu.async_copy(h_hbm.at[pl.ds(base + c * ch, ch)], bufs[c % 2], rsem[c % 2])

        idx_loads = [pltpu.async_copy(pos_hbm.at[pl.ds(k * n + base + c * ch, ch)],
                                      idx_v.at[k * n_chunks + c], isem)
                     for k in range(TOP_K) for c in range(n_chunks)]
        load = rows_load(0)
        for cp in idx_loads:
            cp.wait()
        for c in range(n_chunks):
            nxt = rows_load(c + 1) if c + 1 < n_chunks else None
            load.wait()
            writes = [pltpu.async_copy(bufs[c % 2], out_hbm.at[idx_v.at[k * n_chunks + c]], wsem[c % 2])
                      for k in range(TOP_K)]
            for cp in writes:
                cp.wait()
            load = nxt

    return scatter(h, pos)


def _expert_kernel(vt_ref, ve_ref, lo_ref, hi_ref, ord_ref, nxt_ref, x_ref, wu_hbm, bu_ref, wd_hbm, bd_ref,
                   y_ref, wu32_ref, wd32_ref, sem, wu16_ref, wd16_ref, *, layer):
    i = pl.program_id(0)
    ip = jnp.maximum(i - 1, 0)
    e = ve_ref[i]
    tile = vt_ref[i]
    first_visit = (i == 0) | (tile != vt_ref[ip])
    lo = lo_ref[i]
    hi = hi_ref[i]
    tm = x_ref.shape[0]
    dff = wd16_ref.shape[0]

    def weight_copies(expert, slot):
        return (pltpu.make_async_copy(wu_hbm.at[layer, expert], wu32_ref.at[slot], sem.at[0, slot]),
                pltpu.make_async_copy(wd_hbm.at[layer, expert], wd32_ref.at[slot], sem.at[1, slot]))

    @pl.when(i == 0)
    def _():
        for cp in weight_copies(e, 0):
            cp.start()

    @pl.when((i == 0) | (e != ve_ref[ip]))
    def _():
        slot = ord_ref[i] % 2
        for cp in weight_copies(e, slot):
            cp.wait()

        @pl.when(nxt_ref[i] >= 0)
        def _():
            for cp in weight_copies(nxt_ref[i], 1 - slot):
                cp.start()

        step = 128

        def cast(s, _):
            rows = pl.ds(pl.multiple_of(s * step, step), step)
            wu16_ref[rows, :] = wu32_ref[slot, rows, :].astype(BF16)
            return 0

        lax.fori_loop(0, wu16_ref.shape[0] // step, cast, 0)

        def cast_d(s, _):
            rows = pl.ds(pl.multiple_of(s * step, step), step)
            wd16_ref[rows, :] = wd32_ref[slot, rows, :].astype(BF16)
            return 0

        lax.fori_loop(0, dff // step, cast_d, 0)

    @pl.when(first_visit)
    def _():
        y_ref[...] = jnp.zeros_like(y_ref)

    def ffn(rows):
        x = _unpack_bf16_pairs(lax.bitcast_convert_type(x_ref[rows, :], U32))
        glu = jnp.minimum(_dot(x, wu16_ref[:, :dff]) + bu_ref[:, :dff], SWIGLU_LIMIT)
        lin = jnp.clip(_dot(x, wu16_ref[:, dff:]) + bu_ref[:, dff:], -SWIGLU_LIMIT, SWIGLU_LIMIT)
        act = glu * (1.0 / (1.0 + jnp.exp(-SWIGLU_ALPHA * glu))) * (lin + 1.0)
        y = _dot(act.astype(BF16), wd16_ref[...]) + bd_ref[...]
        return lax.bitcast_convert_type(_pack_bf16_pairs(y), F32)

    subs = [slice(s * EXPERT_SUB, (s + 1) * EXPERT_SUB) for s in range(tm // EXPERT_SUB)]
    whole = (lo <= tile * tm) & (hi >= tile * tm + tm)

    @pl.when(whole)
    def _():
        for rows in subs:
            y_ref[rows, :] = ffn(rows)

    for s, rows in enumerate(subs):
        r0 = tile * tm + s * EXPERT_SUB

        @pl.when(jnp.logical_not(whole) & (hi > r0) & (lo < r0 + EXPERT_SUB))
        def _():
            row = r0 + lax.broadcasted_iota(I32, (EXPERT_SUB, 1), 0)
            y_ref[rows, :] = jnp.where((row >= lo) & (row < hi), ffn(rows), y_ref[rows, :])


def _experts(x_sorted, visits, w_up, b_up, w_down, b_down, layer):
    ns = x_sorted.shape[0]
    tm = EXPERT_TILE
    d, dff = w_down.shape[3], w_down.shape[2]
    bu = b_up.reshape(b_up.shape[0], N_EXPERTS, 1, 2 * dff)
    bd = b_down.reshape(b_down.shape[0], N_EXPERTS, 1, d)
    wmap = lambda i, vt, ve, *_: (layer, ve[i], 0, 0)
    xmap = lambda i, vt, *_: (vt[i], 0)
    anywhere = pl.BlockSpec(memory_space=pl.ANY)
    grid_spec = pltpu.PrefetchScalarGridSpec(
        num_scalar_prefetch=len(visits),
        grid=(visits[0].shape[0],),
        in_specs=[pl.BlockSpec((tm, x_sorted.shape[1]), xmap),
                  anywhere,
                  pl.BlockSpec((None, None, 1, 2 * dff), wmap),
                  anywhere,
                  pl.BlockSpec((None, None, 1, d), wmap)],
        out_specs=pl.BlockSpec((tm, d // 2), xmap),
        scratch_shapes=[pltpu.VMEM((2, d, 2 * dff), F32), pltpu.VMEM((2, dff, d), F32),
                        pltpu.SemaphoreType.DMA((2, 2)),
                        pltpu.VMEM((d, 2 * dff), BF16), pltpu.VMEM((dff, d), BF16)],
    )
    return pl.pallas_call(
        functools.partial(_expert_kernel, layer=layer),
        grid_spec=grid_spec,
        out_shape=jax.ShapeDtypeStruct((ns, d // 2), F32),
        compiler_params=pltpu.CompilerParams(dimension_semantics=("arbitrary",),
                                             vmem_limit_bytes=VMEM_LIMIT),
        name="experts",
    )(*visits, x_sorted, w_up, bu, w_down, bd)


def _expert_visits(cnt, n_rows):
    tm = EXPERT_TILE
    n_steps = n_rows // tm + N_EXPERTS
    ends = jnp.cumsum(cnt)
    starts = ends - cnt
    first_tile = starts // tm
    n_vis = jnp.where(cnt > 0, (ends - 1) // tm - first_tile + 1, 0)
    vis_end = jnp.cumsum(n_vis)
    vis_start = vis_end - n_vis
    v = jnp.arange(n_steps, dtype=I32)
    vc = jnp.minimum(v, vis_end[-1] - 1)
    onehot = ((vis_start[None, :] <= vc[:, None]) & (vc[:, None] < vis_end[None, :])).astype(I32)
    pick = lambda a: jnp.sum(onehot * a[None, :], axis=1).astype(I32)
    expert = pick(jnp.arange(N_EXPERTS, dtype=I32))
    tile = pick(first_tile) + vc - pick(vis_start)
    real = v < vis_end[-1]
    lo = jnp.where(real, pick(starts), 0).astype(I32)
    hi = jnp.where(real, pick(ends), 0).astype(I32)
    ids = jnp.arange(N_EXPERTS, dtype=I32)
    used = cnt > 0
    ordinal = jnp.cumsum(used.astype(I32)) - used.astype(I32)
    later = jnp.min(jnp.where((ids[None, :] > ids[:, None]) & used[None, :], ids[None, :], N_EXPERTS), axis=1)
    following = jnp.where(later < N_EXPERTS, later, -1).astype(I32)
    return tile.astype(I32), expert, lo, hi, pick(ordinal), pick(following)


def _combine_kernel(y0_ref, y1_ref, y2_ref, y3_ref, gate_ref, h_ref, g_ref, b_ref, *o_refs, alpha, first_tiles):
    gate = gate_ref[...]
    lo = jnp.zeros(y0_ref.shape, F32)
    hi = jnp.zeros(y0_ref.shape, F32)
    for k, y_ref in enumerate((y0_ref, y1_ref, y2_ref, y3_ref)):
        w = lax.bitcast_convert_type(y_ref[...], U32)
        lo = lo + gate[:, k:k + 1] * lax.bitcast_convert_type(w << 16, F32)
        hi = hi + gate[:, k:k + 1] * lax.bitcast_convert_type(w & jnp.uint32(0xFFFF0000), F32)
    acc = jnp.concatenate([lo, hi], axis=1)
    out = _layer_norm(alpha * h_ref[...] + acc, g_ref[...], b_ref[...])
    if first_tiles is None:
        o_refs[0][...] = out
    else:
        i = pl.program_id(0)

        @pl.when(i < first_tiles)
        def _():
            o_refs[0][...] = out

        @pl.when(i >= first_tiles)
        def _():
            o_refs[1][...] = out


def _combine(y_tok, gates, h, ln_g, ln_b, alpha, split=None):
    n, d = h.shape
    tm = _wide_tile(n, COMBINE_TILE_MAX) if split is None else TOKEN_TILE
    nt = n // tm
    g, b = ln_g.reshape(1, d), ln_b.reshape(1, d)
    full = lambda a: pl.BlockSpec(a.shape, lambda i: (0, 0))
    ysp = lambda k: pl.BlockSpec((tm, d // 2), lambda i: (k * nt + i, 0))
    if split is None:
        first_tiles = None
        out_specs = pl.BlockSpec((tm, d), lambda i: (i, 0))
        out_shape = jax.ShapeDtypeStruct((n, d), F32)
    else:
        first_tiles = split // tm
        out_specs = [pl.BlockSpec((tm, d), lambda i: (jnp.minimum(i, first_tiles - 1), 0)),
                     pl.BlockSpec((tm, d), lambda i: (jnp.maximum(i - first_tiles, 0), 0))]
        out_shape = [jax.ShapeDtypeStruct((split, d), F32), jax.ShapeDtypeStruct((n - split, d), F32)]
    return pl.pallas_call(
        functools.partial(_combine_kernel, alpha=alpha, first_tiles=first_tiles),
        grid=(nt,),
        in_specs=[ysp(0), ysp(1), ysp(2), ysp(3),
                  pl.BlockSpec((tm, TOP_K), lambda i: (i, 0)),
                  pl.BlockSpec((tm, d), lambda i: (i, 0)), full(g), full(b)],
        out_specs=out_specs,
        out_shape=out_shape,
        compiler_params=pltpu.CompilerParams(dimension_semantics=("arbitrary",),
                                             vmem_limit_bytes=VMEM_LIMIT),
        name="combine",
    )(y_tok, y_tok, y_tok, y_tok, gates, h, g, b)


def _round_up(a, m):
    return (a + m - 1) // m * m


def _wide_tile(n, cap):
    return max(t for t in range(TOKEN_TILE, cap + 1, TOKEN_TILE) if n % t == 0)


def _moe(h, hp, idx, gates, rank, counts, w_up, b_up, w_down, b_down, ln_g, ln_b, alpha, layer, split=None):
    n, d = h.shape
    ns = TOP_K * n
    cnt = counts[:, 0]
    starts = jnp.cumsum(cnt) - cnt
    experts = jnp.arange(N_EXPERTS, dtype=I32)
    offs = jnp.sum(jnp.where(idx[:, :, None] == experts, starts, 0), axis=-1)
    pos = (offs + rank).reshape(-1).astype(I32)
    x_sorted = _sc_scatter_rows(hp, pos)
    y_sorted = _experts(x_sorted, _expert_visits(cnt, ns), w_up, b_up, w_down, b_down, layer)
    m2 = _round_up(ns, SC_ROW_ALIGN)
    fill = jnp.arange(ns, m2, dtype=I32) - ns
    y_tok = _sc_gather(y_sorted, jnp.concatenate([pos, fill]))
    return _combine(y_tok, gates.T, h, ln_g, ln_b, alpha, split)


def kernel(x_prompt, x_sample, cache_k, cache_v, state_conv, state_gla, w_in, conv_w, w_gate, b_gate,
           gla_norm_g, w_out, ln1_g, ln1_b, w_router, b_router, w_up, b_up, w_down, b_down, ln2_g, ln2_b):
    depth = w_in.shape[0]
    bp, seq, d = x_prompt.shape
    bs, ts, _ = x_sample.shape
    past = cache_k.shape[2]
    n_p = bp * seq
    n = n_p + bs * ts
    alpha = float((2 * depth) ** 0.25)
    x = jnp.concatenate([x_prompt.reshape(n_p, d), x_sample.reshape(bs * ts, d)], axis=0)
    ckt = cache_k.transpose(0, 1, 3, 4, 2).reshape(depth, bs, SB_DIM, past)
    cvt = cache_v.transpose(0, 1, 3, 4, 2).reshape(depth, bs, SB_DIM, past)
    zero_conv = jnp.zeros((bp, CONV_W - 1, CONV_DIM), F32)
    zero_gla = jnp.zeros((bp, GLA_HEADS, GLA_DK, GLA_DK), F32)
    wb = w_in.astype(BF16)
    o_r = 3 * SB_DIM
    n_r = 3 * CONV_DIM + 4 * GLA_DIM
    wq, wk, wv = wb[:, :, :SB_DIM], wb[:, :, SB_DIM:2 * SB_DIM], wb[:, :, 2 * SB_DIM:o_r]
    wkt, wvt = wk.transpose(0, 2, 1), wv.transpose(0, 2, 1)
    wr, wal = wb[:, :, o_r:o_r + n_r], wb[:, :, o_r + n_r:]
    wg = w_gate.astype(BF16)
    wo = w_out.astype(BF16)
    kt = jnp.zeros((depth, bp, SB_DIM, seq), F32)
    vt = jnp.zeros((depth, bp, SB_DIM, seq), F32)
    tm_p = min(PROMPT_TILE, seq)
    outs = [[] for _ in range(6)]
    for l in range(depth):
        shared = (wr[l], wal[l], wg[l], b_gate[l].reshape(1, -1))
        q_p, kt, vt, obc, conv_p, gla_p = _mix_prompt(
            x, jnp.zeros((n, CONV_DIM + GLA_DIM), BF16), kt, vt, l, (wq[l], wkt[l], wvt[l]) + shared,
            conv_w[l], gla_norm_g[l], zero_conv, zero_gla, bp, seq, tm_p)
        q_s, ks, vs, rest_s, la_s = _in_proj(x, n_p, n - n_p, TOKEN_TILE, (wq[l], wk[l], wv[l]) + shared)
        oa = _sb_prompt(jnp.zeros((n, SB_DIM), BF16), q_p, kt, vt, l, bp, seq)
        oa = _sb_decode(oa, q_s, ks, vs, ckt, cvt, l, n_p, bs, ts)
        obc, conv_s, gla_s = _conv_gla(obc, n, rest_s, la_s, conv_w[l], gla_norm_g[l], state_conv[l],
                                       state_gla[l], n_p, bs, ts)
        h, hp, idx, gates, rank, counts = _out_router(oa, obc, x, wo[l, :SB_DIM], wo[l, SB_DIM:], ln1_g[l],
                                                      ln1_b[l], w_router[l], b_router[l], alpha)
        x = _moe(h, hp, idx, gates, rank, counts, w_up, b_up, w_down, b_down, ln2_g[l], ln2_b[l],
                 alpha, l, split=n_p if l == depth - 1 else None)
        outs[0].append(conv_p)
        outs[1].append(gla_p)
        outs[2].append(ks.reshape(bs, ts, SB_HEADS, HEAD_DIM))
        outs[3].append(vs.reshape(bs, ts, SB_HEADS, HEAD_DIM))
        outs[4].append(conv_s)
        outs[5].append(gla_s)
    k_prompt = kt.reshape(depth, bp, SB_HEADS, HEAD_DIM, seq).transpose(0, 1, 4, 2, 3)
    v_prompt = vt.reshape(depth, bp, SB_HEADS, HEAD_DIM, seq).transpose(0, 1, 4, 2, 3)
    st = [jnp.stack(o) for o in outs]
    y_prompt, y_sample = x
    return (y_prompt.reshape(bp, seq, d), y_sample.reshape(bs, ts, d), k_prompt, v_prompt,
            st[0], st[1], st[2], st[3], st[4], st[5])
```

```python
import functools

import jax
import jax.numpy as jnp
from jax import lax
from jax.experimental import pallas as pl
from jax.experimental.pallas import tpu as pltpu
from jax.experimental.pallas import tpu_sc as plsc

F32 = jnp.float32
BF16 = jnp.bfloat16
I32 = jnp.int32
U32 = jnp.uint32

HEAD_DIM = 64
SB_HEADS = 8
SB_DIM = SB_HEADS * HEAD_DIM
CONV_DIM = 256
CONV_W = 3
GLA_HEADS = 4
GLA_DK = 64
GLA_DIM = GLA_HEADS * GLA_DK
GLA_RANK = 16
GLA_TAU = 16.0
GLA_CHUNK = 64
N_EXPERTS = 32
TOP_K = 4
SWIGLU_LIMIT = 7.0
SWIGLU_ALPHA = 1.702
NORM_EPS = 1e-5

LANES = 128
SC_CORES = 2
SC_SUBCORES = 16
SC_CHUNK = 128
SC_ROW_ALIGN = SC_CORES * SC_SUBCORES * SC_CHUNK
SC_SCATTER_MAX = 104
TOKEN_TILE = 256
WIDE_TILE_MAX = 1280
COMBINE_TILE_MAX = 640
PROMPT_TILE = 512
EXPERT_TILE = 512
EXPERT_SUB = 256
SB_TQ = 256
SB_BK = 256
SB_GROUP = 256
DEC_TK = 512
GLA_TC = 128
VMEM_LIMIT = 48 * 1024 * 1024
SB_DEAD = -100.0


def _dot(a, b):
    return jnp.dot(a, b, preferred_element_type=F32)


def _dot_nt(a, b):
    return lax.dot_general(a, b, (((1,), (1,)), ((), ())), preferred_element_type=F32)


def _split_bf16(x):
    hi = x.astype(BF16)
    lo = (x - hi.astype(F32)).astype(BF16)
    return hi, lo


def _dot_exact_rhs(x, m):
    hi, lo = _split_bf16(x)
    return _dot(hi, m) + _dot(lo, m)


def _dot_exact_lhs(m, x):
    hi, lo = _split_bf16(x)
    return _dot(m, hi) + _dot(m, lo)


def _pack_bf16_pairs(x):
    c = x.shape[1] // 2
    bits = lax.bitcast_convert_type(x.astype(BF16).astype(F32), U32)
    return (bits[:, :c] >> 16) | (bits[:, c:] & jnp.uint32(0xFFFF0000))


def _unpack_bf16_pairs(w):
    lo = lax.bitcast_convert_type(w << 16, F32)
    hi = lax.bitcast_convert_type(w & jnp.uint32(0xFFFF0000), F32)
    return jnp.concatenate([lo, hi], axis=1).astype(BF16)


def _softplus(z):
    return jnp.maximum(z, 0.0) + jnp.log(1.0 + jnp.exp(-jnp.abs(z)))


def _layer_norm(y, g, b):
    mu = jnp.mean(y, axis=-1, keepdims=True)
    yc = y - mu
    var = jnp.mean(yc * yc, axis=-1, keepdims=True)
    return yc * lax.rsqrt(var + NORM_EPS) * g + b


def _strict_upper(n, copies=1):
    r = lax.broadcasted_iota(I32, (copies * n, n), 0)
    c = lax.broadcasted_iota(I32, (copies * n, n), 1)
    for k in range(1, copies):
        r = r - jnp.where(r >= n, n, 0)
    return jnp.where(r > c, 1.0, 0.0).astype(BF16)


def _project(x_ref, wq_ref, wk_ref, wv_ref, wr_ref, wal_ref, wg_ref, bg_ref,
             q_ref, k_ref, v_ref, r_ref, la_ref, *, transposed_kv):
    xb = x_ref[...].astype(BF16)
    for c in range(0, SB_DIM, 256):
        q_ref[:, c:c + 256] = (_dot(xb, wq_ref[:, c:c + 256]) * (HEAD_DIM ** -0.5)).astype(BF16)
    for c in range(0, r_ref.shape[1], 256):
        r_ref[:, c:c + 256] = _dot(xb, wr_ref[:, c:c + 256])
    al = _dot(xb, wal_ref[...])
    g = _dot(al.astype(BF16), wg_ref[...]) + bg_ref[...]
    la_ref[...] = -_softplus(-g) * (1.0 / GLA_TAU)
    for c in range(0, SB_DIM, 256):
        if transposed_kv:
            k_ref[c:c + 256, :] = _dot_nt(wk_ref[c:c + 256, :], xb)
            v_ref[c:c + 256, :] = _dot_nt(wv_ref[c:c + 256, :], xb)
        else:
            k_ref[:, c:c + 256] = _dot(xb, wk_ref[:, c:c + 256])
            v_ref[:, c:c + 256] = _dot(xb, wv_ref[:, c:c + 256])


def _in_proj(x, row0, rows, tm, weights):
    d = x.shape[1]
    n_r = weights[3].shape[1]
    rb = row0 // tm
    full = lambda a: pl.BlockSpec(a.shape, lambda i: (0,) * a.ndim)
    row = lambda w: pl.BlockSpec((tm, w), lambda i: (i, 0))
    sds = lambda w, dt: jax.ShapeDtypeStruct((rows, w), dt)
    return pl.pallas_call(
        functools.partial(_project, transposed_kv=False),
        grid=(rows // tm,),
        in_specs=[pl.BlockSpec((tm, d), lambda i: (rb + i, 0))] + [full(w) for w in weights],
        out_specs=[row(SB_DIM), row(SB_DIM), row(SB_DIM), row(n_r), row(GLA_DIM)],
        out_shape=[sds(SB_DIM, BF16), sds(SB_DIM, F32), sds(SB_DIM, F32), sds(n_r, F32), sds(GLA_DIM, F32)],
        compiler_params=pltpu.CompilerParams(dimension_semantics=("arbitrary",),
                                             vmem_limit_bytes=VMEM_LIMIT),
        name="in_proj",
    )(x, *weights)


def _sb_weights(z, tri, run, mask):
    sp = _softplus(z)
    l1m = -sp
    lsig = z - sp
    if mask is not None:
        l1m = jnp.where(mask, l1m, 0.0)
    if tri.shape[0] == 2 * z.shape[1]:
        hi = lax.bitcast_convert_type(lax.bitcast_convert_type(l1m, U32) & jnp.uint32(0xFFFF0000), F32)
        parts = jnp.concatenate([hi.astype(BF16), (l1m - hi).astype(BF16)], axis=1)
        rest = _dot(parts, tri) + run
    else:
        rest = _dot_exact_rhs(l1m, tri) + run
    a = jnp.exp(lsig + rest)
    if mask is not None:
        a = jnp.where(mask, a, 0.0)
    return a.astype(BF16), run + jnp.sum(l1m, axis=1, keepdims=True)


def _sb_prompt_kernel(oa_ref, q_ref, kt_ref, vt_ref, o_ref, *, tq, bk):
    del oa_ref
    qi = pl.program_id(2)
    q = q_ref[...]
    n_hh = SB_GROUP // HEAD_DIM
    lane = lax.broadcasted_iota(I32, (1, SB_GROUP), 1)
    in_head = [(lane // HEAD_DIM) == h for h in range(n_hh)]
    qh = [jnp.where(m, q, jnp.zeros_like(q)) for m in in_head]
    tri = _strict_upper(bk, copies=2)
    n_full = (qi * tq) // bk
    qpos = qi * tq + lax.broadcasted_iota(I32, (tq, bk), 0)
    kpos = n_full * bk + lax.broadcasted_iota(I32, (tq, bk), 1)
    diag_mask = kpos < qpos

    def tile(jb, runs, mask):
        ks = pl.multiple_of(jb * bk, bk)
        kt = kt_ref[:, pl.ds(ks, bk)].astype(BF16)
        vt = vt_ref[:, pl.ds(ks, bk)].astype(BF16)
        out = jnp.zeros((tq, SB_GROUP), F32)
        new_runs = []
        for h in range(n_hh):
            a, run = _sb_weights(_dot(qh[h], kt), tri, runs[h], mask)
            out = jnp.where(in_head[h], _dot_nt(a, vt), out)
            new_runs.append(run)
        return out, tuple(new_runs)

    def alive_of(runs):
        m = jnp.max(runs[0])
        for r in runs[1:]:
            m = jnp.maximum(m, jnp.max(r))
        return m > SB_DEAD

    acc, runs = tile(n_full, tuple(jnp.zeros((tq, 1), F32) for _ in range(n_hh)), diag_mask)

    def cond(carry):
        j, alive, _, _ = carry
        return (j >= 0) & alive

    def body(carry):
        j, _, acc, runs = carry
        pv, runs = tile(j, runs, None)
        return j - 1, alive_of(runs), acc + pv, runs

    _, _, acc, _ = lax.while_loop(cond, body, (n_full - 1, jnp.bool_(True), acc, runs))
    o_ref[...] = acc.astype(o_ref.dtype)


def _sb_prompt(oa, q, kt, vt, layer, batch, seq):
    tq, bk = min(SB_TQ, seq), min(SB_BK, seq)
    nq = seq // tq
    hp = SB_DIM // SB_GROUP
    kv_spec = pl.BlockSpec((None, None, SB_GROUP, seq), lambda b, p, i: (layer, b, p, 0))
    return pl.pallas_call(
        functools.partial(_sb_prompt_kernel, tq=tq, bk=bk),
        grid=(batch, hp, nq),
        in_specs=[pl.BlockSpec(memory_space=pl.ANY),
                  pl.BlockSpec((tq, SB_GROUP), lambda b, p, i: (b * nq + i, p)), kv_spec, kv_spec],
        out_specs=pl.BlockSpec((tq, SB_GROUP), lambda b, p, i: (b * nq + i, p)),
        out_shape=jax.ShapeDtypeStruct(oa.shape, oa.dtype),
        input_output_aliases={0: 0},
        compiler_params=pltpu.CompilerParams(
            dimension_semantics=("arbitrary", "arbitrary", "arbitrary"),
            vmem_limit_bytes=VMEM_LIMIT),
        name="sb_prompt",
    )(oa, q, kt, vt)


def _sb_decode_kernel(oa_ref, q_ref, kn_ref, vn_ref, kc_hbm, vc_hbm, o_ref, kbuf, vbuf, sem, acc_ref, run_ref,
                      alive_ref, *, t, tk, bk, layer, nkb):
    del oa_ref
    b = pl.program_id(0)
    q = q_ref[...]

    def block_copies(j, slot):
        cols = pl.ds(pl.multiple_of((nkb - 1 - j) * tk, tk), tk)
        return (pltpu.make_async_copy(kc_hbm.at[layer, b, :, cols], kbuf.at[slot], sem.at[0, slot]),
                pltpu.make_async_copy(vc_hbm.at[layer, b, :, cols], vbuf.at[slot], sem.at[1, slot]))

    for cp in block_copies(0, 0):
        cp.start()

    kn = kn_ref[...].astype(BF16)
    vn = vn_ref[...].astype(BF16)
    lane = lax.broadcasted_iota(I32, (1, SB_DIM), 1)
    r = lax.broadcasted_iota(I32, (t, t), 0)
    c = lax.broadcasted_iota(I32, (t, t), 1)
    mask = c < r
    tri_new = _strict_upper(t)
    for h in range(SB_HEADS):
        qh = jnp.where((lane // HEAD_DIM) == h, q, jnp.zeros_like(q))
        a, run = _sb_weights(_dot_nt(qh, kn), tri_new, jnp.zeros((t, 1), F32), mask)
        acc_ref[h] = _dot(a, vn)[:, h * HEAD_DIM:(h + 1) * HEAD_DIM]
        run_ref[h * t:(h + 1) * t, :] = run
    alive_ref[0] = (jnp.max(run_ref[...]) > SB_DEAD).astype(I32)

    tri = _strict_upper(bk, copies=2)

    def cond(carry):
        j, alive = carry
        return (j < nkb) & (alive > 0)

    def body(carry):
        j, _ = carry
        slot = j % 2

        @pl.when(j + 1 < nkb)
        def _():
            for cp in block_copies(j + 1, 1 - slot):
                cp.start()

        for cp in block_copies(j, slot):
            cp.wait()
        for c in range(tk // bk - 1, -1, -1):
            cols = slice(c * bk, (c + 1) * bk)

            @pl.when(alive_ref[0] > 0)
            def _():
                z = jnp.concatenate(
                    [_dot(q[:, h * HEAD_DIM:(h + 1) * HEAD_DIM],
                          kbuf[slot, h * HEAD_DIM:(h + 1) * HEAD_DIM, cols].astype(BF16))
                     for h in range(SB_HEADS)], axis=0)
                a, run = _sb_weights(z, tri, run_ref[...], None)
                for h in range(SB_HEADS):
                    vt = vbuf[slot, h * HEAD_DIM:(h + 1) * HEAD_DIM, cols].astype(BF16)
                    acc_ref[h] = acc_ref[h] + _dot_nt(a[h * t:(h + 1) * t, :], vt)
                run_ref[...] = run
                alive_ref[0] = (jnp.max(run) > SB_DEAD).astype(I32)

        return j + 1, alive_ref[0]

    j_end, _ = lax.while_loop(cond, body, (jnp.int32(0), alive_ref[0]))

    @pl.when(j_end < nkb)
    def _():
        for cp in block_copies(j_end, j_end % 2):
            cp.wait()

    for h in range(SB_HEADS):
        o_ref[:, h * HEAD_DIM:(h + 1) * HEAD_DIM] = acc_ref[h].astype(o_ref.dtype)


def _sb_decode(oa, q, ks, vs, cache_kt, cache_vt, layer, row0, batch, t):
    past = cache_kt.shape[3]
    tk = min(DEC_TK, past)
    bk = min(SB_BK, tk)
    nkb = past // tk
    rb = row0 // t
    new = pl.BlockSpec((t, SB_DIM), lambda b: (b, 0))
    anywhere = pl.BlockSpec(memory_space=pl.ANY)
    return pl.pallas_call(
        functools.partial(_sb_decode_kernel, t=t, tk=tk, bk=bk, layer=layer, nkb=nkb),
        grid=(batch,),
        in_specs=[anywhere, new, new, new, anywhere, anywhere],
        out_specs=pl.BlockSpec((t, SB_DIM), lambda b: (rb + b, 0)),
        out_shape=jax.ShapeDtypeStruct(oa.shape, oa.dtype),
        input_output_aliases={0: 0},
        scratch_shapes=[pltpu.VMEM((2, SB_DIM, tk), F32),
                        pltpu.VMEM((2, SB_DIM, tk), F32),
                        pltpu.SemaphoreType.DMA((2, 2)),
                        pltpu.VMEM((SB_HEADS, t, HEAD_DIM), F32),
                        pltpu.VMEM((SB_HEADS * t, 1), F32),
                        pltpu.SMEM((1,), I32)],
        compiler_params=pltpu.CompilerParams(dimension_semantics=("arbitrary",),
                                             vmem_limit_bytes=VMEM_LIMIT),
        name="sb_decode",
    )(oa, q, ks, vs, cache_kt, cache_vt)


def _mix_state_load(cprev_ref, sprev_ref, ctail_ref, st_ref):
    n_hh = LANES // GLA_DK
    ctail_ref[...] = cprev_ref[...]
    for p in range(GLA_DIM // LANES):
        st_ref[p] = jnp.zeros((LANES, LANES), F32)
        for hh in range(n_hh):
            st_ref[p, hh * GLA_DK:(hh + 1) * GLA_DK, hh * GLA_DK:(hh + 1) * GLA_DK] = sprev_ref[p * n_hh + hh].T


def _mix_state_store(ctail_ref, st_ref, cnew_ref, snew_ref):
    n_hh = LANES // GLA_DK
    cnew_ref[...] = ctail_ref[...]
    for p in range(GLA_DIM // LANES):
        for hh in range(n_hh):
            blk = st_ref[p, hh * GLA_DK:(hh + 1) * GLA_DK, :]
            snew_ref[p * n_hh + hh] = blk.T[hh * GLA_DK:(hh + 1) * GLA_DK, :]


def _conv_gla_block(bg_ref, cg_ref, u_ref, qc_ref, kc_ref, vc_ref, gc_ref, la_ref, cw_ref, ng_ref,
                    o_ref, ctail_ref, st_ref, tc, chunk):
    n_pair = GLA_DIM // LANES
    n_hh = LANES // GLA_DK

    z = cg_ref[...] * u_ref[...]
    tail = ctail_ref[...]
    row = lax.broadcasted_iota(I32, z.shape, 0)
    z1 = jnp.where(row < 1, tail[1:2, :], pltpu.roll(z, 1, 0))
    z2 = jnp.where(row < 2, jnp.where(row < 1, tail[0:1, :], tail[1:2, :]), pltpu.roll(z, 2, 0))
    cw = cw_ref[...]
    y = z2 * cw[0:1, :] + z1 * cw[1:2, :] + z * cw[2:3, :]
    o_ref[:, 0:CONV_DIM] = (bg_ref[...] * y).astype(o_ref.dtype)
    ctail_ref[...] = z[tc - 2:tc, :]

    r = lax.broadcasted_iota(I32, (tc, tc), 0)
    c = lax.broadcasted_iota(I32, (tc, tc), 1)
    same_chunk = (r // chunk) == (c // chunk)
    chunk_sum = jnp.where(same_chunk, 1.0, 0.0).astype(BF16)
    chunk_cumsum = jnp.where(same_chunk & (c <= r), 1.0, 0.0).astype(BF16)
    lane = lax.broadcasted_iota(I32, (1, LANES), 1)
    lr = lax.broadcasted_iota(I32, (LANES, LANES), 0) // GLA_DK
    lc = lax.broadcasted_iota(I32, (LANES, LANES), 1) // GLA_DK
    same_head = lr == lc
    head_mean = jnp.where(same_head, 1.0 / GLA_DK, 0.0).astype(BF16)
    r2 = lax.broadcasted_iota(I32, (tc, n_hh * tc), 0)
    c2 = lax.broadcasted_iota(I32, (tc, n_hh * tc), 1)
    c2 = c2 - jnp.where(c2 >= tc, tc, 0)
    causal2 = ((r2 // chunk) == (c2 // chunk)) & (c2 <= r2)
    for p in range(n_pair):
        cols = slice(p * LANES, (p + 1) * LANES)
        la = la_ref[:, cols]
        b = _dot_exact_lhs(chunk_cumsum, la)
        tot = _dot_exact_lhs(chunk_sum, la)
        k = kc_ref[:, cols]
        qe = (qc_ref[:, cols] * (GLA_DK ** -0.5) * jnp.exp(b)).astype(BF16)
        ke = (k * jnp.exp(-b)).astype(BF16)
        kd = (k * jnp.exp(tot - b)).astype(BF16)
        decay = jnp.exp(tot)
        vb = vc_ref[:, cols].astype(BF16)
        zeros = jnp.zeros_like(ke)
        ke_st = jnp.concatenate([jnp.where((lane // GLA_DK) == hh, ke, zeros)
                                 for hh in range(n_hh)], axis=0)
        v_st = jnp.concatenate([jnp.where((lane // GLA_DK) == hh, vb, zeros)
                                for hh in range(n_hh)], axis=0)
        a = jnp.where(causal2, _dot_nt(qe, ke_st), 0.0)
        o = _dot(a.astype(BF16), v_st)
        st = st_ref[p]
        from_state = []
        for ci in range(tc // chunk):
            rows = slice(ci * chunk, (ci + 1) * chunk)
            from_state.append(_dot_nt(qe[rows], st.astype(BF16)))
            upd = _dot(vb[rows].T, kd[rows])
            st = jnp.where(same_head, st * decay[ci * chunk:ci * chunk + 1, :] + upd, 0.0)
        st_ref[p] = st
        o = o + jnp.concatenate(from_state, axis=0)
        ms = _dot_exact_rhs(o * o, head_mean)
        o = o * lax.rsqrt(ms + NORM_EPS) * ng_ref[:, cols]
        g = gc_ref[:, cols]
        o = o * (g * (1.0 / (1.0 + jnp.exp(-g))))
        o_ref[:, CONV_DIM + p * LANES:CONV_DIM + (p + 1) * LANES] = o.astype(o_ref.dtype)


def _conv_gla_kernel(*refs, tc, chunk):
    (_, bg_ref, cg_ref, u_ref, qc_ref, kc_ref, vc_ref, gc_ref, la_ref, cw_ref, ng_ref, cprev_ref, sprev_ref,
     o_ref, cnew_ref, snew_ref, ctail_ref, st_ref) = refs
    ti = pl.program_id(1)

    @pl.when(ti == 0)
    def _():
        _mix_state_load(cprev_ref, sprev_ref, ctail_ref, st_ref)

    _conv_gla_block(bg_ref, cg_ref, u_ref, qc_ref, kc_ref, vc_ref, gc_ref, la_ref, cw_ref, ng_ref,
                    o_ref, ctail_ref, st_ref, tc, chunk)

    @pl.when(ti == pl.num_programs(1) - 1)
    def _():
        _mix_state_store(ctail_ref, st_ref, cnew_ref, snew_ref)


def _conv_gla(obc_prev, n, rest, la, conv_w, norm_g, conv_prev, gla_prev, row0, batch, t):
    tc = min(GLA_TC, t)
    chunk = min(GLA_CHUNK, t)
    nt = t // tc
    rb = row0 // tc
    col = lambda j: pl.BlockSpec((tc, CONV_DIM), lambda b, i: (b * nt + i, j))
    const2 = lambda a: pl.BlockSpec(a.shape, lambda b, i: (0, 0))
    ng = norm_g.reshape(1, GLA_DIM)
    args = (rest,) * 7 + (la, conv_w, ng, conv_prev, gla_prev)
    return pl.pallas_call(
        functools.partial(_conv_gla_kernel, tc=tc, chunk=chunk),
        grid=(batch, nt),
        in_specs=[
            pl.BlockSpec(memory_space=pl.ANY), col(0), col(1), col(2), col(3), col(4), col(5), col(6),
            pl.BlockSpec((tc, GLA_DIM), lambda b, i: (b * nt + i, 0)),
            const2(conv_w), const2(ng),
            pl.BlockSpec((None, CONV_W - 1, CONV_DIM), lambda b, i: (b, 0, 0)),
            pl.BlockSpec((None, GLA_HEADS, GLA_DK, GLA_DK), lambda b, i: (b, 0, 0, 0))],
        out_specs=[pl.BlockSpec((tc, CONV_DIM + GLA_DIM), lambda b, i: (rb + b * nt + i, 0)),
                   pl.BlockSpec((None, CONV_W - 1, CONV_DIM), lambda b, i: (b, 0, 0)),
                   pl.BlockSpec((None, GLA_HEADS, GLA_DK, GLA_DK), lambda b, i: (b, 0, 0, 0))],
        out_shape=[jax.ShapeDtypeStruct((n, CONV_DIM + GLA_DIM), BF16),
                   jax.ShapeDtypeStruct((batch, CONV_W - 1, CONV_DIM), F32),
                   jax.ShapeDtypeStruct((batch, GLA_HEADS, GLA_DK, GLA_DK), F32)],
        input_output_aliases={0: 0},
        scratch_shapes=[pltpu.VMEM((CONV_W - 1, CONV_DIM), F32),
                        pltpu.VMEM((GLA_DIM // LANES, LANES, LANES), F32)],
        compiler_params=pltpu.CompilerParams(dimension_semantics=("arbitrary", "arbitrary"),
                                             vmem_limit_bytes=VMEM_LIMIT),
        name="conv_gla",
    )(obc_prev, *args)


def _mix_prompt_kernel(*refs, tc, chunk):
    (_, _, _, x_ref, wq_ref, wk_ref, wv_ref, wr_ref, wal_ref, wg_ref, bg_ref, cw_ref, ng_ref, cprev_ref,
     sprev_ref, q_ref, kt_ref, vt_ref, o_ref, cnew_ref, snew_ref, r_scr, la_scr, ctail_ref, st_ref) = refs
    ti = pl.program_id(1)
    _project(x_ref, wq_ref, wk_ref, wv_ref, wr_ref, wal_ref, wg_ref, bg_ref,
             q_ref, kt_ref, vt_ref, r_scr, la_scr, transposed_kv=True)

    @pl.when(ti == 0)
    def _():
        _mix_state_load(cprev_ref, sprev_ref, ctail_ref, st_ref)

    for s in range(x_ref.shape[0] // tc):
        rows = pl.ds(s * tc, tc)
        part = [r_scr.at[rows, pl.ds(j * CONV_DIM, CONV_DIM)] for j in range(7)]
        _conv_gla_block(*part, la_scr.at[rows], cw_ref, ng_ref, o_ref.at[rows], ctail_ref, st_ref, tc, chunk)

    @pl.when(ti == pl.num_programs(1) - 1)
    def _():
        _mix_state_store(ctail_ref, st_ref, cnew_ref, snew_ref)


def _mix_prompt(x, obc_prev, kt, vt, layer, weights, conv_w, norm_g, conv_prev, gla_prev, batch, seq, tm):
    d = x.shape[1]
    n_p = batch * seq
    nt = seq // tm
    tc = min(GLA_TC, tm)
    wq, wk, wv, wr, wal, wg, bg = weights
    ng = norm_g.reshape(1, GLA_DIM)
    full = lambda a: pl.BlockSpec(a.shape, lambda b, i: (0,) * a.ndim)
    row = lambda w: pl.BlockSpec((tm, w), lambda b, i: (b * nt + i, 0))
    kv_spec = pl.BlockSpec((None, None, SB_DIM, tm), lambda b, i: (layer, b, 0, i))
    conv_spec = pl.BlockSpec((None, CONV_W - 1, CONV_DIM), lambda b, i: (b, 0, 0))
    gla_spec = pl.BlockSpec((None, GLA_HEADS, GLA_DK, GLA_DK), lambda b, i: (b, 0, 0, 0))
    anywhere = pl.BlockSpec(memory_space=pl.ANY)
    consts = (wq, wk, wv, wr, wal, wg, bg, conv_w, ng)
    return pl.pallas_call(
        functools.partial(_mix_prompt_kernel, tc=tc, chunk=min(GLA_CHUNK, tc)),
        grid=(batch, nt),
        in_specs=[anywhere, anywhere, anywhere, row(d)] + [full(w) for w in consts] + [conv_spec, gla_spec],
        out_specs=[row(SB_DIM), kv_spec, kv_spec, row(CONV_DIM + GLA_DIM), conv_spec, gla_spec],
        out_shape=[jax.ShapeDtypeStruct((n_p, SB_DIM), BF16),
                   jax.ShapeDtypeStruct(kt.shape, kt.dtype), jax.ShapeDtypeStruct(vt.shape, vt.dtype),
                   jax.ShapeDtypeStruct(obc_prev.shape, obc_prev.dtype),
                   jax.ShapeDtypeStruct((batch, CONV_W - 1, CONV_DIM), F32),
                   jax.ShapeDtypeStruct((batch, GLA_HEADS, GLA_DK, GLA_DK), F32)],
        input_output_aliases={0: 1, 1: 2, 2: 3},
        scratch_shapes=[pltpu.VMEM((tm, wr.shape[1]), F32),
                        pltpu.VMEM((tm, GLA_DIM), F32),
                        pltpu.VMEM((CONV_W - 1, CONV_DIM), F32),
                        pltpu.VMEM((GLA_DIM // LANES, LANES, LANES), F32)],
        compiler_params=pltpu.CompilerParams(dimension_semantics=("arbitrary", "arbitrary"),
                                             vmem_limit_bytes=VMEM_LIMIT),
        name="mix_prompt",
    )(kt, vt, obc_prev, x, *consts, conv_prev, gla_prev)


def _out_router_kernel(oa_ref, obc_ref, x_ref, wa_ref, wb_ref, g_ref, b_ref, wr_ref, br_ref,
                       h_ref, hp_ref, idx_ref, gate_ref, rank_ref, cnt_ref, carry_ref, *, alpha):
    i = pl.program_id(0)
    tm = x_ref.shape[0]

    @pl.when(i == 0)
    def _():
        carry_ref[...] = jnp.zeros_like(carry_ref)

    m = _dot(oa_ref[...], wa_ref[...]) + _dot(obc_ref[...], wb_ref[...])
    h = _layer_norm(alpha * x_ref[...] + m, g_ref[...], b_ref[...])
    h_ref[...] = h
    hp_ref[...] = lax.bitcast_convert_type(_pack_bf16_pairs(h), F32)

    h_hi, h_lo = _split_bf16(h)
    w_hi, w_lo = _split_bf16(wr_ref[...])
    logit = _dot_nt(w_hi, h_hi) + _dot_nt(w_hi, h_lo) + _dot_nt(w_lo, h_hi) + br_ref[...]
    eid = lax.broadcasted_iota(I32, (N_EXPERTS, tm), 0)
    r = lax.broadcasted_iota(I32, (tm, tm), 0)
    c = lax.broadcasted_iota(I32, (tm, tm), 1)
    before = jnp.where(r < c, 1.0, 0.0).astype(BF16)
    base = carry_ref[...]
    vals, idxs, onehots, bases = [], [], [], []
    for _ in range(TOP_K):
        mx = jnp.max(logit, axis=0, keepdims=True)
        sel = jnp.min(jnp.where(logit == mx, eid, N_EXPERTS), axis=0, keepdims=True)
        hit = eid == sel
        logit = jnp.where(hit, -jnp.inf, logit)
        onehot = jnp.where(hit, 1.0, 0.0)
        onehots.append(onehot)
        bases.append(base)
        base = base + jnp.sum(onehot, axis=1, keepdims=True)
        vals.append(mx)
        idxs.append(sel)
    carry_ref[...] = base
    earlier = _dot(jnp.concatenate(onehots, axis=0).astype(BF16), before)
    ranks = [jnp.sum(onehots[k] * (earlier[k * N_EXPERTS:(k + 1) * N_EXPERTS] + bases[k]),
                     axis=0, keepdims=True) for k in range(TOP_K)]
    e = [jnp.exp(v - vals[0]) for v in vals]
    inv = 1.0 / (e[0] + e[1] + e[2] + e[3])
    idx_ref[...] = jnp.concatenate(idxs, axis=0)
    gate_ref[...] = jnp.concatenate([ek * inv for ek in e], axis=0)
    rank_ref[...] = jnp.concatenate(ranks, axis=0).astype(I32)
    cnt_ref[...] = jnp.broadcast_to(base, cnt_ref.shape).astype(I32)


def _out_router(oa, obc, x, wa, wb, ln_g, ln_b, w_router, b_router, alpha):
    n, d = x.shape
    tm = _wide_tile(n, WIDE_TILE_MAX)
    wr = w_router.T
    br = b_router.reshape(N_EXPERTS, 1)
    g, b = ln_g.reshape(1, d), ln_b.reshape(1, d)
    full = lambda a: pl.BlockSpec(a.shape, lambda i: (0, 0))
    row = lambda w: pl.BlockSpec((tm, w), lambda i: (i, 0))
    colb = pl.BlockSpec((TOP_K, tm), lambda i: (0, i))
    return pl.pallas_call(
        functools.partial(_out_router_kernel, alpha=alpha),
        grid=(n // tm,),
        in_specs=[row(SB_DIM), row(CONV_DIM + GLA_DIM), row(d), full(wa), full(wb), full(g), full(b),
                  full(wr), full(br)],
        out_specs=[row(d), row(d // 2), colb, colb, colb,
                   pl.BlockSpec((N_EXPERTS, LANES), lambda i: (0, 0))],
        out_shape=[jax.ShapeDtypeStruct((n, d), F32),
                   jax.ShapeDtypeStruct((n, d // 2), F32),
                   jax.ShapeDtypeStruct((TOP_K, n), I32),
                   jax.ShapeDtypeStruct((TOP_K, n), F32),
                   jax.ShapeDtypeStruct((TOP_K, n), I32),
                   jax.ShapeDtypeStruct((N_EXPERTS, LANES), I32)],
        scratch_shapes=[pltpu.VMEM((N_EXPERTS, 1), F32)],
        compiler_params=pltpu.CompilerParams(dimension_semantics=("arbitrary",),
                                             vmem_limit_bytes=VMEM_LIMIT),
        name="out_router",
    )(oa, obc, x, wa, wb, g, b, wr, br)


def _sc_gather(table, idx):
    m = idx.shape[0]
    d = table.shape[1]
    per_worker = m // (SC_CORES * SC_SUBCORES)
    half = SC_CHUNK // 2
    n_groups = per_worker // SC_CHUNK
    mesh = plsc.VectorSubcoreMesh(core_axis_name="c", subcore_axis_name="s")

    @functools.partial(
        pl.kernel, mesh=mesh,
        out_type=jax.ShapeDtypeStruct((m, d), table.dtype),
        scratch_types=[pltpu.VMEM((per_worker,), I32)] + [pltpu.VMEM((half, d), table.dtype)] * 2
                      + [pltpu.SemaphoreType.DMA] * 4,
        name="sc_gather",
    )
    def gather(table_hbm, idx_hbm, out_hbm, idx_v, buf0, buf1, g0, g1, w0, w1):
        wid = lax.axis_index("s") * SC_CORES + lax.axis_index("c")
        base = wid * per_worker
        pltpu.sync_copy(idx_hbm.at[pl.ds(base, per_worker)], idx_v)

        def fetch(c, buf, sem):
            off = pl.multiple_of(c * half, 8)
            return pltpu.make_async_copy(table_hbm.at[idx_v.at[pl.ds(off, half)]], buf, sem)

        def put(c, buf, sem):
            off = pl.multiple_of(c * half, 8)
            return pltpu.make_async_copy(buf, out_hbm.at[pl.ds(base + off, half)], sem)

        fetch(0, buf0, g0).start()

        @pl.loop(0, n_groups)
        def _(g):
            c0 = 2 * g
            c1 = c0 + 1

            @pl.when(g > 0)
            def _():
                put(c0 - 1, buf1, w1).wait()

            fetch(c1, buf1, g1).start()
            fetch(c0, buf0, g0).wait()
            put(c0, buf0, w0).start()
            put(c0, buf0, w0).wait()

            @pl.when(g + 1 < n_groups)
            def _():
                fetch(c0 + 2, buf0, g0).start()

            fetch(c1, buf1, g1).wait()
            put(c1, buf1, w1).start()

        put(2 * n_groups - 1, buf1, w1).wait()

    return gather(table, idx)


def _sc_chunk(per_worker):
    return max(c for c in range(8, SC_SCATTER_MAX + 1, 8) if per_worker % c == 0)


def _sc_scatter_rows(h, pos):
    n, d = h.shape
    per_worker = n // (SC_CORES * SC_SUBCORES)
    ch = _sc_chunk(per_worker)
    n_chunks = per_worker // ch
    mesh = plsc.VectorSubcoreMesh(core_axis_name="c", subcore_axis_name="s")

    @functools.partial(
        pl.kernel, mesh=mesh,
        out_type=jax.ShapeDtypeStruct((TOP_K * n, d), h.dtype),
        scratch_types=[pltpu.VMEM((TOP_K * n_chunks, ch), I32), pltpu.VMEM((ch, d), h.dtype),
                       pltpu.VMEM((ch, d), h.dtype)] + [pltpu.SemaphoreType.DMA] * 5,
        name="sc_scatter",
    )
    def scatter(h_hbm, pos_hbm, out_hbm, idx_v, rows0, rows1, isem, r0, r1, w0, w1):
        bufs, rsem, wsem = (rows0, rows1), (r0, r1), (w0, w1)
        wid = lax.axis_index("s") * SC_CORES + lax.axis_index("c")
        base = pl.multiple_of(wid * per_worker, 8)

        def rows_load(c):
            return pltpu.async_copy(h_hbm.at[pl.ds(base + c * ch, ch)], bufs[c % 2], rsem[c % 2])

        idx_loads = [pltpu.async_copy(pos_hbm.at[pl.ds(k * n + base + c * ch, ch)],
                                      idx_v.at[k * n_chunks + c], isem)
                     for k in range(TOP_K) for c in range(n_chunks)]
        load = rows_load(0)
        for cp in idx_loads:
            cp.wait()
        for c in range(n_chunks):
            nxt = rows_load(c + 1) if c + 1 < n_chunks else None
            load.wait()
            writes = [pltpu.async_copy(bufs[c % 2], out_hbm.at[idx_v.at[k * n_chunks + c]], wsem[c % 2])
                      for k in range(TOP_K)]
            for cp in writes:
                cp.wait()
            load = nxt

    return scatter(h, pos)


def _expert_kernel(vt_ref, ve_ref, lo_ref, hi_ref, ord_ref, nxt_ref, x_ref, wu_hbm, bu_ref, wd_hbm, bd_ref,
                   y_ref, wu32_ref, wd32_ref, sem, wu16_ref, wd16_ref, *, layer):
    i = pl.program_id(0)
    ip = jnp.maximum(i - 1, 0)
    e = ve_ref[i]
    tile = vt_ref[i]
    first_visit = (i == 0) | (tile != vt_ref[ip])
    lo = lo_ref[i]
    hi = hi_ref[i]
    tm = x_ref.shape[0]
    dff = wd16_ref.shape[0]

    def weight_copies(expert, slot):
        return (pltpu.make_async_copy(wu_hbm.at[layer, expert], wu32_ref.at[slot], sem.at[0, slot]),
                pltpu.make_async_copy(wd_hbm.at[layer, expert], wd32_ref.at[slot], sem.at[1, slot]))

    @pl.when(i == 0)
    def _():
        for cp in weight_copies(e, 0):
            cp.start()

    @pl.when((i == 0) | (e != ve_ref[ip]))
    def _():
        slot = ord_ref[i] % 2
        for cp in weight_copies(e, slot):
            cp.wait()

        @pl.when(nxt_ref[i] >= 0)
        def _():
            for cp in weight_copies(nxt_ref[i], 1 - slot):
                cp.start()

        step = 128

        def cast(s, _):
            rows = pl.ds(pl.multiple_of(s * step, step), step)
            wu16_ref[rows, :] = wu32_ref[slot, rows, :].astype(BF16)
            return 0

        lax.fori_loop(0, wu16_ref.shape[0] // step, cast, 0)

        def cast_d(s, _):
            rows = pl.ds(pl.multiple_of(s * step, step), step)
            wd16_ref[rows, :] = wd32_ref[slot, rows, :].astype(BF16)
            return 0

        lax.fori_loop(0, dff // step, cast_d, 0)

    @pl.when(first_visit)
    def _():
        y_ref[...] = jnp.zeros_like(y_ref)

    def ffn(rows):
        x = _unpack_bf16_pairs(lax.bitcast_convert_type(x_ref[rows, :], U32))
        glu = jnp.minimum(_dot(x, wu16_ref[:, :dff]) + bu_ref[:, :dff], SWIGLU_LIMIT)
        lin = jnp.clip(_dot(x, wu16_ref[:, dff:]) + bu_ref[:, dff:], -SWIGLU_LIMIT, SWIGLU_LIMIT)
        act = glu * (1.0 / (1.0 + jnp.exp(-SWIGLU_ALPHA * glu))) * (lin + 1.0)
        y = _dot(act.astype(BF16), wd16_ref[...]) + bd_ref[...]
        return lax.bitcast_convert_type(_pack_bf16_pairs(y), F32)

    subs = [slice(s * EXPERT_SUB, (s + 1) * EXPERT_SUB) for s in range(tm // EXPERT_SUB)]
    whole = (lo <= tile * tm) & (hi >= tile * tm + tm)

    @pl.when(whole)
    def _():
        for rows in subs:
            y_ref[rows, :] = ffn(rows)

    for s, rows in enumerate(subs):
        r0 = tile * tm + s * EXPERT_SUB

        @pl.when(jnp.logical_not(whole) & (hi > r0) & (lo < r0 + EXPERT_SUB))
        def _():
            row = r0 + lax.broadcasted_iota(I32, (EXPERT_SUB, 1), 0)
            y_ref[rows, :] = jnp.where((row >= lo) & (row < hi), ffn(rows), y_ref[rows, :])


def _experts(x_sorted, visits, w_up, b_up, w_down, b_down, layer):
    ns = x_sorted.shape[0]
    tm = EXPERT_TILE
    d, dff = w_down.shape[3], w_down.shape[2]
    bu = b_up.reshape(b_up.shape[0], N_EXPERTS, 1, 2 * dff)
    bd = b_down.reshape(b_down.shape[0], N_EXPERTS, 1, d)
    wmap = lambda i, vt, ve, *_: (layer, ve[i], 0, 0)
    xmap = lambda i, vt, *_: (vt[i], 0)
    anywhere = pl.BlockSpec(memory_space=pl.ANY)
    grid_spec = pltpu.PrefetchScalarGridSpec(
        num_scalar_prefetch=len(visits),
        grid=(visits[0].shape[0],),
        in_specs=[pl.BlockSpec((tm, x_sorted.shape[1]), xmap),
                  anywhere,
                  pl.BlockSpec((None, None, 1, 2 * dff), wmap),
                  anywhere,
                  pl.BlockSpec((None, None, 1, d), wmap)],
        out_specs=pl.BlockSpec((tm, d // 2), xmap),
        scratch_shapes=[pltpu.VMEM((2, d, 2 * dff), F32), pltpu.VMEM((2, dff, d), F32),
                        pltpu.SemaphoreType.DMA((2, 2)),
                        pltpu.VMEM((d, 2 * dff), BF16), pltpu.VMEM((dff, d), BF16)],
    )
    return pl.pallas_call(
        functools.partial(_expert_kernel, layer=layer),
        grid_spec=grid_spec,
        out_shape=jax.ShapeDtypeStruct((ns, d // 2), F32),
        compiler_params=pltpu.CompilerParams(dimension_semantics=("arbitrary",),
                                             vmem_limit_bytes=VMEM_LIMIT),
        name="experts",
    )(*visits, x_sorted, w_up, bu, w_down, bd)


def _expert_visits(cnt, n_rows):
    tm = EXPERT_TILE
    n_steps = n_rows // tm + N_EXPERTS
    ends = jnp.cumsum(cnt)
    starts = ends - cnt
    first_tile = starts // tm
    n_vis = jnp.where(cnt > 0, (ends - 1) // tm - first_tile + 1, 0)
    vis_end = jnp.cumsum(n_vis)
    vis_start = vis_end - n_vis
    v = jnp.arange(n_steps, dtype=I32)
    vc = jnp.minimum(v, vis_end[-1] - 1)
    onehot = ((vis_start[None, :] <= vc[:, None]) & (vc[:, None] < vis_end[None, :])).astype(I32)
    pick = lambda a: jnp.sum(onehot * a[None, :], axis=1).astype(I32)
    expert = pick(jnp.arange(N_EXPERTS, dtype=I32))
    tile = pick(first_tile) + vc - pick(vis_start)
    real = v < vis_end[-1]
    lo = jnp.where(real, pick(starts), 0).astype(I32)
    hi = jnp.where(real, pick(ends), 0).astype(I32)
    ids = jnp.arange(N_EXPERTS, dtype=I32)
    used = cnt > 0
    ordinal = jnp.cumsum(used.astype(I32)) - used.astype(I32)
    later = jnp.min(jnp.where((ids[None, :] > ids[:, None]) & used[None, :], ids[None, :], N_EXPERTS), axis=1)
    following = jnp.where(later < N_EXPERTS, later, -1).astype(I32)
    return tile.astype(I32), expert, lo, hi, pick(ordinal), pick(following)


def _combine_kernel(y0_ref, y1_ref, y2_ref, y3_ref, gate_ref, h_ref, g_ref, b_ref, *o_refs, alpha, first_tiles):
    gate = gate_ref[...]
    lo = jnp.zeros(y0_ref.shape, F32)
    hi = jnp.zeros(y0_ref.shape, F32)
    for k, y_ref in enumerate((y0_ref, y1_ref, y2_ref, y3_ref)):
        w = lax.bitcast_convert_type(y_ref[...], U32)
        lo = lo + gate[:, k:k + 1] * lax.bitcast_convert_type(w << 16, F32)
        hi = hi + gate[:, k:k + 1] * lax.bitcast_convert_type(w & jnp.uint32(0xFFFF0000), F32)
    acc = jnp.concatenate([lo, hi], axis=1)
    out = _layer_norm(alpha * h_ref[...] + acc, g_ref[...], b_ref[...])
    if first_tiles is None:
        o_refs[0][...] = out
    else:
        i = pl.program_id(0)

        @pl.when(i < first_tiles)
        def _():
            o_refs[0][...] = out

        @pl.when(i >= first_tiles)
        def _():
            o_refs[1][...] = out


def _combine(y_tok, gates, h, ln_g, ln_b, alpha, split=None):
    n, d = h.shape
    tm = _wide_tile(n, COMBINE_TILE_MAX) if split is None else TOKEN_TILE
    nt = n // tm
    g, b = ln_g.reshape(1, d), ln_b.reshape(1, d)
    full = lambda a: pl.BlockSpec(a.shape, lambda i: (0, 0))
    ysp = lambda k: pl.BlockSpec((tm, d // 2), lambda i: (k * nt + i, 0))
    if split is None:
        first_tiles = None
        out_specs = pl.BlockSpec((tm, d), lambda i: (i, 0))
        out_shape = jax.ShapeDtypeStruct((n, d), F32)
    else:
        first_tiles = split // tm
        out_specs = [pl.BlockSpec((tm, d), lambda i: (jnp.minimum(i, first_tiles - 1), 0)),
                     pl.BlockSpec((tm, d), lambda i: (jnp.maximum(i - first_tiles, 0), 0))]
        out_shape = [jax.ShapeDtypeStruct((split, d), F32), jax.ShapeDtypeStruct((n - split, d), F32)]
    return pl.pallas_call(
        functools.partial(_combine_kernel, alpha=alpha, first_tiles=first_tiles),
        grid=(nt,),
        in_specs=[ysp(0), ysp(1), ysp(2), ysp(3),
                  pl.BlockSpec((tm, TOP_K), lambda i: (i, 0)),
                  pl.BlockSpec((tm, d), lambda i: (i, 0)), full(g), full(b)],
        out_specs=out_specs,
        out_shape=out_shape,
        compiler_params=pltpu.CompilerParams(dimension_semantics=("arbitrary",),
                                             vmem_limit_bytes=VMEM_LIMIT),
        name="combine",
    )(y_tok, y_tok, y_tok, y_tok, gates, h, g, b)


def _round_up(a, m):
    return (a + m - 1) // m * m


def _wide_tile(n, cap):
    return max(t for t in range(TOKEN_TILE, cap + 1, TOKEN_TILE) if n % t == 0)


def _moe(h, hp, idx, gates, rank, counts, w_up, b_up, w_down, b_down, ln_g, ln_b, alpha, layer, split=None):
    n, d = h.shape
    ns = TOP_K * n
    cnt = counts[:, 0]
    starts = jnp.cumsum(cnt) - cnt
    experts = jnp.arange(N_EXPERTS, dtype=I32)
    offs = jnp.sum(jnp.where(idx[:, :, None] == experts, starts, 0), axis=-1)
    pos = (offs + rank).reshape(-1).astype(I32)
    x_sorted = _sc_scatter_rows(hp, pos)
    y_sorted = _experts(x_sorted, _expert_visits(cnt, ns), w_up, b_up, w_down, b_down, layer)
    m2 = _round_up(ns, SC_ROW_ALIGN)
    fill = jnp.arange(ns, m2, dtype=I32) - ns
    y_tok = _sc_gather(y_sorted, jnp.concatenate([pos, fill]))
    return _combine(y_tok, gates.T, h, ln_g, ln_b, alpha, split)


def kernel(x_prompt, x_sample, cache_k, cache_v, state_conv, state_gla, w_in, conv_w, w_gate, b_gate,
           gla_norm_g, w_out, ln1_g, ln1_b, w_router, b_router, w_up, b_up, w_down, b_down, ln2_g, ln2_b):
    depth = w_in.shape[0]
    bp, seq, d = x_prompt.shape
    bs, ts, _ = x_sample.shape
    past = cache_k.shape[2]
    n_p = bp * seq
    n = n_p + bs * ts
    alpha = float((2 * depth) ** 0.25)
    x = jnp.concatenate([x_prompt.reshape(n_p, d), x_sample.reshape(bs * ts, d)], axis=0)
    ckt = cache_k.transpose(0, 1, 3, 4, 2).reshape(depth, bs, SB_DIM, past)
    cvt = cache_v.transpose(0, 1, 3, 4, 2).reshape(depth, bs, SB_DIM, past)
    zero_conv = jnp.zeros((bp, CONV_W - 1, CONV_DIM), F32)
    zero_gla = jnp.zeros((bp, GLA_HEADS, GLA_DK, GLA_DK), F32)
    wb = w_in.astype(BF16)
    o_r = 3 * SB_DIM
    n_r = 3 * CONV_DIM + 4 * GLA_DIM
    wq, wk, wv = wb[:, :, :SB_DIM], wb[:, :, SB_DIM:2 * SB_DIM], wb[:, :, 2 * SB_DIM:o_r]
    wkt, wvt = wk.transpose(0, 2, 1), wv.transpose(0, 2, 1)
    wr, wal = wb[:, :, o_r:o_r + n_r], wb[:, :, o_r + n_r:]
    wg = w_gate.astype(BF16)
    wo = w_out.astype(BF16)
    kt = jnp.zeros((depth, bp, SB_DIM, seq), F32)
    vt = jnp.zeros((depth, bp, SB_DIM, seq), F32)
    tm_p = min(PROMPT_TILE, seq)
    outs = [[] for _ in range(6)]
    for l in range(depth):
        shared = (wr[l], wal[l], wg[l], b_gate[l].reshape(1, -1))
        q_p, kt, vt, obc, conv_p, gla_p = _mix_prompt(
            x, jnp.zeros((n, CONV_DIM + GLA_DIM), BF16), kt, vt, l, (wq[l], wkt[l], wvt[l]) + shared,
            conv_w[l], gla_norm_g[l], zero_conv, zero_gla, bp, seq, tm_p)
        q_s, ks, vs, rest_s, la_s = _in_proj(x, n_p, n - n_p, TOKEN_TILE, (wq[l], wk[l], wv[l]) + shared)
        oa = _sb_prompt(jnp.zeros((n, SB_DIM), BF16), q_p, kt, vt, l, bp, seq)
        oa = _sb_decode(oa, q_s, ks, vs, ckt, cvt, l, n_p, bs, ts)
        obc, conv_s, gla_s = _conv_gla(obc, n, rest_s, la_s, conv_w[l], gla_norm_g[l], state_conv[l],
                                       state_gla[l], n_p, bs, ts)
        h, hp, idx, gates, rank, counts = _out_router(oa, obc, x, wo[l, :SB_DIM], wo[l, SB_DIM:], ln1_g[l],
                                                      ln1_b[l], w_router[l], b_router[l], alpha)
        x = _moe(h, hp, idx, gates, rank, counts, w_up, b_up, w_down, b_down, ln2_g[l], ln2_b[l],
                 alpha, l, split=n_p if l == depth - 1 else None)
        outs[0].append(conv_p)
        outs[1].append(gla_p)
        outs[2].append(ks.reshape(bs, ts, SB_HEADS, HEAD_DIM))
        outs[3].append(vs.reshape(bs, ts, SB_HEADS, HEAD_DIM))
        outs[4].append(conv_s)
        outs[5].append(gla_s)
    k_prompt = kt.reshape(depth, bp, SB_HEADS, HEAD_DIM, seq).transpose(0, 1, 4, 2, 3)
    v_prompt = vt.reshape(depth, bp, SB_HEADS, HEAD_DIM, seq).transpose(0, 1, 4, 2, 3)
    st = [jnp.stack(o) for o in outs]
    y_prompt, y_sample = x
    return (y_prompt.reshape(bp, seq, d), y_sample.reshape(bs, ts, d), k_prompt, v_prompt,
            st[0], st[1], st[2], st[3], st[4], st[5])
```

```python
import functools

import jax
import jax.numpy as jnp
from jax import lax
from jax.experimental import pallas as pl
from jax.experimental.pallas import tpu as pltpu
from jax.experimental.pallas import tpu_sc as plsc

F32 = jnp.float32
BF16 = jnp.bfloat16
I32 = jnp.int32
U32 = jnp.uint32

HEAD_DIM = 64
SB_HEADS = 8
SB_DIM = SB_HEADS * HEAD_DIM
CONV_DIM = 256
CONV_W = 3
GLA_HEADS = 4
GLA_DK = 64
GLA_DIM = GLA_HEADS * GLA_DK
GLA_RANK = 16
GLA_TAU = 16.0
GLA_CHUNK = 64
N_EXPERTS = 32
TOP_K = 4
SWIGLU_LIMIT = 7.0
SWIGLU_ALPHA = 1.702
NORM_EPS = 1e-5

LANES = 128
SC_CORES = 2
SC_SUBCORES = 16
SC_CHUNK = 128
SC_ROW_ALIGN = SC_CORES * SC_SUBCORES * SC_CHUNK
SC_SCATTER_MAX = 104
TOKEN_TILE = 256
WIDE_TILE_MAX = 1280
COMBINE_TILE_MAX = 640
PROMPT_TILE = 512
EXPERT_TILE = 512
EXPERT_SUB = 256
SB_TQ = 256
SB_BK = 256
SB_GROUP = 256
DEC_TK = 256
GLA_TC = 128
VMEM_LIMIT = 48 * 1024 * 1024
SB_DEAD = -100.0


def _dot(a, b):
    return jnp.dot(a, b, preferred_element_type=F32)


def _dot_nt(a, b):
    return lax.dot_general(a, b, (((1,), (1,)), ((), ())), preferred_element_type=F32)


def _split_bf16(x):
    hi = x.astype(BF16)
    lo = (x - hi.astype(F32)).astype(BF16)
    return hi, lo


def _dot_exact_rhs(x, m):
    hi, lo = _split_bf16(x)
    return _dot(hi, m) + _dot(lo, m)


def _dot_exact_lhs(m, x):
    hi, lo = _split_bf16(x)
    return _dot(m, hi) + _dot(m, lo)


def _pack_bf16_pairs(x):
    c = x.shape[1] // 2
    bits = lax.bitcast_convert_type(x.astype(BF16).astype(F32), U32)
    return (bits[:, :c] >> 16) | (bits[:, c:] & jnp.uint32(0xFFFF0000))


def _unpack_bf16_pairs(w):
    lo = lax.bitcast_convert_type(w << 16, F32)
    hi = lax.bitcast_convert_type(w & jnp.uint32(0xFFFF0000), F32)
    return jnp.concatenate([lo, hi], axis=1).astype(BF16)


def _softplus(z):
    return jnp.maximum(z, 0.0) + jnp.log(1.0 + jnp.exp(-jnp.abs(z)))


def _layer_norm(y, g, b):
    mu = jnp.mean(y, axis=-1, keepdims=True)
    yc = y - mu
    var = jnp.mean(yc * yc, axis=-1, keepdims=True)
    return yc * lax.rsqrt(var + NORM_EPS) * g + b


def _strict_upper(n, copies=1):
    r = lax.broadcasted_iota(I32, (copies * n, n), 0)
    c = lax.broadcasted_iota(I32, (copies * n, n), 1)
    for k in range(1, copies):
        r = r - jnp.where(r >= n, n, 0)
    return jnp.where(r > c, 1.0, 0.0).astype(BF16)


def _project(x_ref, wq_ref, wk_ref, wv_ref, wr_ref, wal_ref, wg_ref, bg_ref,
             q_ref, k_ref, v_ref, r_ref, la_ref, *, transposed_kv):
    xb = x_ref[...].astype(BF16)
    for c in range(0, SB_DIM, 256):
        q_ref[:, c:c + 256] = (_dot(xb, wq_ref[:, c:c + 256]) * (HEAD_DIM ** -0.5)).astype(BF16)
    for c in range(0, r_ref.shape[1], 256):
        r_ref[:, c:c + 256] = _dot(xb, wr_ref[:, c:c + 256])
    al = _dot(xb, wal_ref[...])
    g = _dot(al.astype(BF16), wg_ref[...]) + bg_ref[...]
    la_ref[...] = -_softplus(-g) * (1.0 / GLA_TAU)
    for c in range(0, SB_DIM, 256):
        if transposed_kv:
            k_ref[c:c + 256, :] = _dot_nt(wk_ref[c:c + 256, :], xb)
            v_ref[c:c + 256, :] = _dot_nt(wv_ref[c:c + 256, :], xb)
        else:
            k_ref[:, c:c + 256] = _dot(xb, wk_ref[:, c:c + 256])
            v_ref[:, c:c + 256] = _dot(xb, wv_ref[:, c:c + 256])


def _in_proj(x, row0, rows, tm, weights):
    d = x.shape[1]
    n_r = weights[3].shape[1]
    rb = row0 // tm
    full = lambda a: pl.BlockSpec(a.shape, lambda i: (0,) * a.ndim)
    row = lambda w: pl.BlockSpec((tm, w), lambda i: (i, 0))
    sds = lambda w, dt: jax.ShapeDtypeStruct((rows, w), dt)
    return pl.pallas_call(
        functools.partial(_project, transposed_kv=False),
        grid=(rows // tm,),
        in_specs=[pl.BlockSpec((tm, d), lambda i: (rb + i, 0))] + [full(w) for w in weights],
        out_specs=[row(SB_DIM), row(SB_DIM), row(SB_DIM), row(n_r), row(GLA_DIM)],
        out_shape=[sds(SB_DIM, BF16), sds(SB_DIM, F32), sds(SB_DIM, F32), sds(n_r, F32), sds(GLA_DIM, F32)],
        compiler_params=pltpu.CompilerParams(dimension_semantics=("arbitrary",),
                                             vmem_limit_bytes=VMEM_LIMIT),
        name="in_proj",
    )(x, *weights)


def _sb_weights(z, tri, run, mask):
    sp = _softplus(z)
    l1m = -sp
    lsig = z - sp
    if mask is not None:
        l1m = jnp.where(mask, l1m, 0.0)
    if tri.shape[0] == 2 * z.shape[1]:
        hi = lax.bitcast_convert_type(lax.bitcast_convert_type(l1m, U32) & jnp.uint32(0xFFFF0000), F32)
        parts = jnp.concatenate([hi.astype(BF16), (l1m - hi).astype(BF16)], axis=1)
        rest = _dot(parts, tri) + run
    else:
        rest = _dot_exact_rhs(l1m, tri) + run
    a = jnp.exp(lsig + rest)
    if mask is not None:
        a = jnp.where(mask, a, 0.0)
    return a.astype(BF16), run + jnp.sum(l1m, axis=1, keepdims=True)


def _sb_prompt_kernel(oa_ref, q_ref, kt_ref, vt_ref, o_ref, *, tq, bk):
    del oa_ref
    qi = pl.program_id(2)
    q = q_ref[...]
    n_hh = SB_GROUP // HEAD_DIM
    lane = lax.broadcasted_iota(I32, (1, SB_GROUP), 1)
    in_head = [(lane // HEAD_DIM) == h for h in range(n_hh)]
    qh = [jnp.where(m, q, jnp.zeros_like(q)) for m in in_head]
    tri = _strict_upper(bk, copies=2)
    n_full = (qi * tq) // bk
    qpos = qi * tq + lax.broadcasted_iota(I32, (tq, bk), 0)
    kpos = n_full * bk + lax.broadcasted_iota(I32, (tq, bk), 1)
    diag_mask = kpos < qpos

    def tile(jb, runs, mask):
        ks = pl.multiple_of(jb * bk, bk)
        kt = kt_ref[:, pl.ds(ks, bk)].astype(BF16)
        vt = vt_ref[:, pl.ds(ks, bk)].astype(BF16)
        out = jnp.zeros((tq, SB_GROUP), F32)
        new_runs = []
        for h in range(n_hh):
            a, run = _sb_weights(_dot(qh[h], kt), tri, runs[h], mask)
            out = jnp.where(in_head[h], _dot_nt(a, vt), out)
            new_runs.append(run)
        return out, tuple(new_runs)

    def alive_of(runs):
        m = jnp.max(runs[0])
        for r in runs[1:]:
            m = jnp.maximum(m, jnp.max(r))
        return m > SB_DEAD

    acc, runs = tile(n_full, tuple(jnp.zeros((tq, 1), F32) for _ in range(n_hh)), diag_mask)

    def cond(carry):
        j, alive, _, _ = carry
        return (j >= 0) & alive

    def body(carry):
        j, _, acc, runs = carry
        pv, runs = tile(j, runs, None)
        return j - 1, alive_of(runs), acc + pv, runs

    _, _, acc, _ = lax.while_loop(cond, body, (n_full - 1, jnp.bool_(True), acc, runs))
    o_ref[...] = acc.astype(o_ref.dtype)


def _sb_prompt(oa, q, kt, vt, layer, batch, seq):
    tq, bk = min(SB_TQ, seq), min(SB_BK, seq)
    nq = seq // tq
    hp = SB_DIM // SB_GROUP
    kv_spec = pl.BlockSpec((None, None, SB_GROUP, seq), lambda b, p, i: (layer, b, p, 0))
    return pl.pallas_call(
        functools.partial(_sb_prompt_kernel, tq=tq, bk=bk),
        grid=(batch, hp, nq),
        in_specs=[pl.BlockSpec(memory_space=pl.ANY),
                  pl.BlockSpec((tq, SB_GROUP), lambda b, p, i: (b * nq + i, p)), kv_spec, kv_spec],
        out_specs=pl.BlockSpec((tq, SB_GROUP), lambda b, p, i: (b * nq + i, p)),
        out_shape=jax.ShapeDtypeStruct(oa.shape, oa.dtype),
        input_output_aliases={0: 0},
        compiler_params=pltpu.CompilerParams(
            dimension_semantics=("arbitrary", "arbitrary", "arbitrary"),
            vmem_limit_bytes=VMEM_LIMIT),
        name="sb_prompt",
    )(oa, q, kt, vt)


def _sb_decode_kernel(oa_ref, q_ref, kn_ref, vn_ref, kc_hbm, vc_hbm, o_ref, kbuf, vbuf, sem, acc_ref, run_ref,
                      alive_ref, *, t, tk, bk, layer, nkb):
    del oa_ref
    b = pl.program_id(0)
    q = q_ref[...]

    def block_copies(j, slot):
        cols = pl.ds(pl.multiple_of((nkb - 1 - j) * tk, tk), tk)
        return (pltpu.make_async_copy(kc_hbm.at[layer, b, :, cols], kbuf.at[slot], sem.at[0, slot]),
                pltpu.make_async_copy(vc_hbm.at[layer, b, :, cols], vbuf.at[slot], sem.at[1, slot]))

    for cp in block_copies(0, 0):
        cp.start()

    kn = kn_ref[...].astype(BF16)
    vn = vn_ref[...].astype(BF16)
    lane = lax.broadcasted_iota(I32, (1, SB_DIM), 1)
    r = lax.broadcasted_iota(I32, (t, t), 0)
    c = lax.broadcasted_iota(I32, (t, t), 1)
    mask = c < r
    tri_new = _strict_upper(t)
    for h in range(SB_HEADS):
        qh = jnp.where((lane // HEAD_DIM) == h, q, jnp.zeros_like(q))
        a, run = _sb_weights(_dot_nt(qh, kn), tri_new, jnp.zeros((t, 1), F32), mask)
        acc_ref[h] = _dot(a, vn)[:, h * HEAD_DIM:(h + 1) * HEAD_DIM]
        run_ref[h * t:(h + 1) * t, :] = run
    alive_ref[0] = (jnp.max(run_ref[...]) > SB_DEAD).astype(I32)

    tri = _strict_upper(bk, copies=2)

    def cond(carry):
        j, alive = carry
        return (j < nkb) & (alive > 0)

    def body(carry):
        j, _ = carry
        slot = j % 2

        @pl.when(j + 1 < nkb)
        def _():
            for cp in block_copies(j + 1, 1 - slot):
                cp.start()

        for cp in block_copies(j, slot):
            cp.wait()
        for c in range(tk // bk - 1, -1, -1):
            cols = slice(c * bk, (c + 1) * bk)

            @pl.when(alive_ref[0] > 0)
            def _():
                z = jnp.concatenate(
                    [_dot(q[:, h * HEAD_DIM:(h + 1) * HEAD_DIM],
                          kbuf[slot, h * HEAD_DIM:(h + 1) * HEAD_DIM, cols].astype(BF16))
                     for h in range(SB_HEADS)], axis=0)
                a, run = _sb_weights(z, tri, run_ref[...], None)
                for h in range(SB_HEADS):
                    vt = vbuf[slot, h * HEAD_DIM:(h + 1) * HEAD_DIM, cols].astype(BF16)
                    acc_ref[h] = acc_ref[h] + _dot_nt(a[h * t:(h + 1) * t, :], vt)
                run_ref[...] = run
                alive_ref[0] = (jnp.max(run) > SB_DEAD).astype(I32)

        return j + 1, alive_ref[0]

    j_end, _ = lax.while_loop(cond, body, (jnp.int32(0), alive_ref[0]))

    @pl.when(j_end < nkb)
    def _():
        for cp in block_copies(j_end, j_end % 2):
            cp.wait()

    for h in range(SB_HEADS):
        o_ref[:, h * HEAD_DIM:(h + 1) * HEAD_DIM] = acc_ref[h].astype(o_ref.dtype)


def _sb_decode(oa, q, ks, vs, cache_kt, cache_vt, layer, row0, batch, t):
    past = cache_kt.shape[3]
    tk = min(DEC_TK, past)
    bk = min(SB_BK, tk)
    nkb = past // tk
    rb = row0 // t
    new = pl.BlockSpec((t, SB_DIM), lambda b: (b, 0))
    anywhere = pl.BlockSpec(memory_space=pl.ANY)
    return pl.pallas_call(
        functools.partial(_sb_decode_kernel, t=t, tk=tk, bk=bk, layer=layer, nkb=nkb),
        grid=(batch,),
        in_specs=[anywhere, new, new, new, anywhere, anywhere],
        out_specs=pl.BlockSpec((t, SB_DIM), lambda b: (rb + b, 0)),
        out_shape=jax.ShapeDtypeStruct(oa.shape, oa.dtype),
        input_output_aliases={0: 0},
        scratch_shapes=[pltpu.VMEM((2, SB_DIM, tk), F32),
                        pltpu.VMEM((2, SB_DIM, tk), F32),
                        pltpu.SemaphoreType.DMA((2, 2)),
                        pltpu.VMEM((SB_HEADS, t, HEAD_DIM), F32),
                        pltpu.VMEM((SB_HEADS * t, 1), F32),
                        pltpu.SMEM((1,), I32)],
        compiler_params=pltpu.CompilerParams(dimension_semantics=("arbitrary",),
                                             vmem_limit_bytes=VMEM_LIMIT),
        name="sb_decode",
    )(oa, q, ks, vs, cache_kt, cache_vt)


def _mix_state_load(cprev_ref, sprev_ref, ctail_ref, st_ref):
    n_hh = LANES // GLA_DK
    ctail_ref[...] = cprev_ref[...]
    for p in range(GLA_DIM // LANES):
        st_ref[p] = jnp.zeros((LANES, LANES), F32)
        for hh in range(n_hh):
            st_ref[p, hh * GLA_DK:(hh + 1) * GLA_DK, hh * GLA_DK:(hh + 1) * GLA_DK] = sprev_ref[p * n_hh + hh].T


def _mix_state_store(ctail_ref, st_ref, cnew_ref, snew_ref):
    n_hh = LANES // GLA_DK
    cnew_ref[...] = ctail_ref[...]
    for p in range(GLA_DIM // LANES):
        for hh in range(n_hh):
            blk = st_ref[p, hh * GLA_DK:(hh + 1) * GLA_DK, :]
            snew_ref[p * n_hh + hh] = blk.T[hh * GLA_DK:(hh + 1) * GLA_DK, :]


def _conv_gla_block(bg_ref, cg_ref, u_ref, qc_ref, kc_ref, vc_ref, gc_ref, la_ref, cw_ref, ng_ref,
                    o_ref, ctail_ref, st_ref, tc, chunk):
    n_pair = GLA_DIM // LANES
    n_hh = LANES // GLA_DK

    z = cg_ref[...] * u_ref[...]
    tail = ctail_ref[...]
    row = lax.broadcasted_iota(I32, z.shape, 0)
    z1 = jnp.where(row < 1, tail[1:2, :], pltpu.roll(z, 1, 0))
    z2 = jnp.where(row < 2, jnp.where(row < 1, tail[0:1, :], tail[1:2, :]), pltpu.roll(z, 2, 0))
    cw = cw_ref[...]
    y = z2 * cw[0:1, :] + z1 * cw[1:2, :] + z * cw[2:3, :]
    o_ref[:, 0:CONV_DIM] = (bg_ref[...] * y).astype(o_ref.dtype)
    ctail_ref[...] = z[tc - 2:tc, :]

    r = lax.broadcasted_iota(I32, (tc, tc), 0)
    c = lax.broadcasted_iota(I32, (tc, tc), 1)
    same_chunk = (r // chunk) == (c // chunk)
    chunk_sum = jnp.where(same_chunk, 1.0, 0.0).astype(BF16)
    chunk_cumsum = jnp.where(same_chunk & (c <= r), 1.0, 0.0).astype(BF16)
    lane = lax.broadcasted_iota(I32, (1, LANES), 1)
    lr = lax.broadcasted_iota(I32, (LANES, LANES), 0) // GLA_DK
    lc = lax.broadcasted_iota(I32, (LANES, LANES), 1) // GLA_DK
    same_head = lr == lc
    head_mean = jnp.where(same_head, 1.0 / GLA_DK, 0.0).astype(BF16)
    r2 = lax.broadcasted_iota(I32, (tc, n_hh * tc), 0)
    c2 = lax.broadcasted_iota(I32, (tc, n_hh * tc), 1)
    c2 = c2 - jnp.where(c2 >= tc, tc, 0)
    causal2 = ((r2 // chunk) == (c2 // chunk)) & (c2 <= r2)
    for p in range(n_pair):
        cols = slice(p * LANES, (p + 1) * LANES)
        la = la_ref[:, cols]
        b = _dot_exact_lhs(chunk_cumsum, la)
        tot = _dot_exact_lhs(chunk_sum, la)
        k = kc_ref[:, cols]
        qe = (qc_ref[:, cols] * (GLA_DK ** -0.5) * jnp.exp(b)).astype(BF16)
        ke = (k * jnp.exp(-b)).astype(BF16)
        kd = (k * jnp.exp(tot - b)).astype(BF16)
        decay = jnp.exp(tot)
        vb = vc_ref[:, cols].astype(BF16)
        zeros = jnp.zeros_like(ke)
        ke_st = jnp.concatenate([jnp.where((lane // GLA_DK) == hh, ke, zeros)
                                 for hh in range(n_hh)], axis=0)
        v_st = jnp.concatenate([jnp.where((lane // GLA_DK) == hh, vb, zeros)
                                for hh in range(n_hh)], axis=0)
        a = jnp.where(causal2, _dot_nt(qe, ke_st), 0.0)
        o = _dot(a.astype(BF16), v_st)
        st = st_ref[p]
        from_state = []
        for ci in range(tc // chunk):
            rows = slice(ci * chunk, (ci + 1) * chunk)
            from_state.append(_dot_nt(qe[rows], st.astype(BF16)))
            upd = _dot(vb[rows].T, kd[rows])
            st = jnp.where(same_head, st * decay[ci * chunk:ci * chunk + 1, :] + upd, 0.0)
        st_ref[p] = st
        o = o + jnp.concatenate(from_state, axis=0)
        ms = _dot_exact_rhs(o * o, head_mean)
        o = o * lax.rsqrt(ms + NORM_EPS) * ng_ref[:, cols]
        g = gc_ref[:, cols]
        o = o * (g * (1.0 / (1.0 + jnp.exp(-g))))
        o_ref[:, CONV_DIM + p * LANES:CONV_DIM + (p + 1) * LANES] = o.astype(o_ref.dtype)


def _conv_gla_kernel(*refs, tc, chunk):
    (_, bg_ref, cg_ref, u_ref, qc_ref, kc_ref, vc_ref, gc_ref, la_ref, cw_ref, ng_ref, cprev_ref, sprev_ref,
     o_ref, cnew_ref, snew_ref, ctail_ref, st_ref) = refs
    ti = pl.program_id(1)

    @pl.when(ti == 0)
    def _():
        _mix_state_load(cprev_ref, sprev_ref, ctail_ref, st_ref)

    _conv_gla_block(bg_ref, cg_ref, u_ref, qc_ref, kc_ref, vc_ref, gc_ref, la_ref, cw_ref, ng_ref,
                    o_ref, ctail_ref, st_ref, tc, chunk)

    @pl.when(ti == pl.num_programs(1) - 1)
    def _():
        _mix_state_store(ctail_ref, st_ref, cnew_ref, snew_ref)


def _conv_gla(obc_prev, n, rest, la, conv_w, norm_g, conv_prev, gla_prev, row0, batch, t):
    tc = min(GLA_TC, t)
    chunk = min(GLA_CHUNK, t)
    nt = t // tc
    rb = row0 // tc
    col = lambda j: pl.BlockSpec((tc, CONV_DIM), lambda b, i: (b * nt + i, j))
    const2 = lambda a: pl.BlockSpec(a.shape, lambda b, i: (0, 0))
    ng = norm_g.reshape(1, GLA_DIM)
    args = (rest,) * 7 + (la, conv_w, ng, conv_prev, gla_prev)
    return pl.pallas_call(
        functools.partial(_conv_gla_kernel, tc=tc, chunk=chunk),
        grid=(batch, nt),
        in_specs=[
            pl.BlockSpec(memory_space=pl.ANY), col(0), col(1), col(2), col(3), col(4), col(5), col(6),
            pl.BlockSpec((tc, GLA_DIM), lambda b, i: (b * nt + i, 0)),
            const2(conv_w), const2(ng),
            pl.BlockSpec((None, CONV_W - 1, CONV_DIM), lambda b, i: (b, 0, 0)),
            pl.BlockSpec((None, GLA_HEADS, GLA_DK, GLA_DK), lambda b, i: (b, 0, 0, 0))],
        out_specs=[pl.BlockSpec((tc, CONV_DIM + GLA_DIM), lambda b, i: (rb + b * nt + i, 0)),
                   pl.BlockSpec((None, CONV_W - 1, CONV_DIM), lambda b, i: (b, 0, 0)),
                   pl.BlockSpec((None, GLA_HEADS, GLA_DK, GLA_DK), lambda b, i: (b, 0, 0, 0))],
        out_shape=[jax.ShapeDtypeStruct((n, CONV_DIM + GLA_DIM), BF16),
                   jax.ShapeDtypeStruct((batch, CONV_W - 1, CONV_DIM), F32),
                   jax.ShapeDtypeStruct((batch, GLA_HEADS, GLA_DK, GLA_DK), F32)],
        input_output_aliases={0: 0},
        scratch_shapes=[pltpu.VMEM((CONV_W - 1, CONV_DIM), F32),
                        pltpu.VMEM((GLA_DIM // LANES, LANES, LANES), F32)],
        compiler_params=pltpu.CompilerParams(dimension_semantics=("arbitrary", "arbitrary"),
                                             vmem_limit_bytes=VMEM_LIMIT),
        name="conv_gla",
    )(obc_prev, *args)


def _mix_prompt_kernel(*refs, tc, chunk):
    (_, _, _, x_ref, wq_ref, wk_ref, wv_ref, wr_ref, wal_ref, wg_ref, bg_ref, cw_ref, ng_ref, cprev_ref,
     sprev_ref, q_ref, kt_ref, vt_ref, o_ref, cnew_ref, snew_ref, r_scr, la_scr, ctail_ref, st_ref) = refs
    ti = pl.program_id(1)
    _project(x_ref, wq_ref, wk_ref, wv_ref, wr_ref, wal_ref, wg_ref, bg_ref,
             q_ref, kt_ref, vt_ref, r_scr, la_scr, transposed_kv=True)

    @pl.when(ti == 0)
    def _():
        _mix_state_load(cprev_ref, sprev_ref, ctail_ref, st_ref)

    for s in range(x_ref.shape[0] // tc):
        rows = pl.ds(s * tc, tc)
        part = [r_scr.at[rows, pl.ds(j * CONV_DIM, CONV_DIM)] for j in range(7)]
        _conv_gla_block(*part, la_scr.at[rows], cw_ref, ng_ref, o_ref.at[rows], ctail_ref, st_ref, tc, chunk)

    @pl.when(ti == pl.num_programs(1) - 1)
    def _():
        _mix_state_store(ctail_ref, st_ref, cnew_ref, snew_ref)


def _mix_prompt(x, obc_prev, kt, vt, layer, weights, conv_w, norm_g, conv_prev, gla_prev, batch, seq, tm):
    d = x.shape[1]
    n_p = batch * seq
    nt = seq // tm
    tc = min(GLA_TC, tm)
    wq, wk, wv, wr, wal, wg, bg = weights
    ng = norm_g.reshape(1, GLA_DIM)
    full = lambda a: pl.BlockSpec(a.shape, lambda b, i: (0,) * a.ndim)
    row = lambda w: pl.BlockSpec((tm, w), lambda b, i: (b * nt + i, 0))
    kv_spec = pl.BlockSpec((None, None, SB_DIM, tm), lambda b, i: (layer, b, 0, i))
    conv_spec = pl.BlockSpec((None, CONV_W - 1, CONV_DIM), lambda b, i: (b, 0, 0))
    gla_spec = pl.BlockSpec((None, GLA_HEADS, GLA_DK, GLA_DK), lambda b, i: (b, 0, 0, 0))
    anywhere = pl.BlockSpec(memory_space=pl.ANY)
    consts = (wq, wk, wv, wr, wal, wg, bg, conv_w, ng)
    return pl.pallas_call(
        functools.partial(_mix_prompt_kernel, tc=tc, chunk=min(GLA_CHUNK, tc)),
        grid=(batch, nt),
        in_specs=[anywhere, anywhere, anywhere, row(d)] + [full(w) for w in consts] + [conv_spec, gla_spec],
        out_specs=[row(SB_DIM), kv_spec, kv_spec, row(CONV_DIM + GLA_DIM), conv_spec, gla_spec],
        out_shape=[jax.ShapeDtypeStruct((n_p, SB_DIM), BF16),
                   jax.ShapeDtypeStruct(kt.shape, kt.dtype), jax.ShapeDtypeStruct(vt.shape, vt.dtype),
                   jax.ShapeDtypeStruct(obc_prev.shape, obc_prev.dtype),
                   jax.ShapeDtypeStruct((batch, CONV_W - 1, CONV_DIM), F32),
                   jax.ShapeDtypeStruct((batch, GLA_HEADS, GLA_DK, GLA_DK), F32)],
        input_output_aliases={0: 1, 1: 2, 2: 3},
        scratch_shapes=[pltpu.VMEM((tm, wr.shape[1]), F32),
                        pltpu.VMEM((tm, GLA_DIM), F32),
                        pltpu.VMEM((CONV_W - 1, CONV_DIM), F32),
                        pltpu.VMEM((GLA_DIM // LANES, LANES, LANES), F32)],
        compiler_params=pltpu.CompilerParams(dimension_semantics=("arbitrary", "arbitrary"),
                                             vmem_limit_bytes=VMEM_LIMIT),
        name="mix_prompt",
    )(kt, vt, obc_prev, x, *consts, conv_prev, gla_prev)


def _out_router_kernel(oa_ref, obc_ref, x_ref, wa_ref, wb_ref, g_ref, b_ref, wr_ref, br_ref,
                       h_ref, hp_ref, idx_ref, gate_ref, rank_ref, cnt_ref, carry_ref, *, alpha):
    i = pl.program_id(0)
    tm = x_ref.shape[0]

    @pl.when(i == 0)
    def _():
        carry_ref[...] = jnp.zeros_like(carry_ref)

    m = _dot(oa_ref[...], wa_ref[...]) + _dot(obc_ref[...], wb_ref[...])
    h = _layer_norm(alpha * x_ref[...] + m, g_ref[...], b_ref[...])
    h_ref[...] = h
    hp_ref[...] = lax.bitcast_convert_type(_pack_bf16_pairs(h), F32)

    h_hi, h_lo = _split_bf16(h)
    w_hi, w_lo = _split_bf16(wr_ref[...])
    logit = _dot_nt(w_hi, h_hi) + _dot_nt(w_hi, h_lo) + _dot_nt(w_lo, h_hi) + br_ref[...]
    eid = lax.broadcasted_iota(I32, (N_EXPERTS, tm), 0)
    r = lax.broadcasted_iota(I32, (tm, tm), 0)
    c = lax.broadcasted_iota(I32, (tm, tm), 1)
    before = jnp.where(r < c, 1.0, 0.0).astype(BF16)
    base = carry_ref[...]
    vals, idxs, onehots, bases = [], [], [], []
    for _ in range(TOP_K):
        mx = jnp.max(logit, axis=0, keepdims=True)
        sel = jnp.min(jnp.where(logit == mx, eid, N_EXPERTS), axis=0, keepdims=True)
        hit = eid == sel
        logit = jnp.where(hit, -jnp.inf, logit)
        onehot = jnp.where(hit, 1.0, 0.0)
        onehots.append(onehot)
        bases.append(base)
        base = base + jnp.sum(onehot, axis=1, keepdims=True)
        vals.append(mx)
        idxs.append(sel)
    carry_ref[...] = base
    earlier = _dot(jnp.concatenate(onehots, axis=0).astype(BF16), before)
    ranks = [jnp.sum(onehots[k] * (earlier[k * N_EXPERTS:(k + 1) * N_EXPERTS] + bases[k]),
                     axis=0, keepdims=True) for k in range(TOP_K)]
    e = [jnp.exp(v - vals[0]) for v in vals]
    inv = 1.0 / (e[0] + e[1] + e[2] + e[3])
    idx_ref[...] = jnp.concatenate(idxs, axis=0)
    gate_ref[...] = jnp.concatenate([ek * inv for ek in e], axis=0)
    rank_ref[...] = jnp.concatenate(ranks, axis=0).astype(I32)
    cnt_ref[...] = jnp.broadcast_to(base, cnt_ref.shape).astype(I32)


def _out_router(oa, obc, x, wa, wb, ln_g, ln_b, w_router, b_router, alpha):
    n, d = x.shape
    tm = _wide_tile(n, WIDE_TILE_MAX)
    wr = w_router.T
    br = b_router.reshape(N_EXPERTS, 1)
    g, b = ln_g.reshape(1, d), ln_b.reshape(1, d)
    full = lambda a: pl.BlockSpec(a.shape, lambda i: (0, 0))
    row = lambda w: pl.BlockSpec((tm, w), lambda i: (i, 0))
    colb = pl.BlockSpec((TOP_K, tm), lambda i: (0, i))
    return pl.pallas_call(
        functools.partial(_out_router_kernel, alpha=alpha),
        grid=(n // tm,),
        in_specs=[row(SB_DIM), row(CONV_DIM + GLA_DIM), row(d), full(wa), full(wb), full(g), full(b),
                  full(wr), full(br)],
        out_specs=[row(d), row(d // 2), colb, colb, colb,
                   pl.BlockSpec((N_EXPERTS, LANES), lambda i: (0, 0))],
        out_shape=[jax.ShapeDtypeStruct((n, d), F32),
                   jax.ShapeDtypeStruct((n, d // 2), F32),
                   jax.ShapeDtypeStruct((TOP_K, n), I32),
                   jax.ShapeDtypeStruct((TOP_K, n), F32),
                   jax.ShapeDtypeStruct((TOP_K, n), I32),
                   jax.ShapeDtypeStruct((N_EXPERTS, LANES), I32)],
        scratch_shapes=[pltpu.VMEM((N_EXPERTS, 1), F32)],
        compiler_params=pltpu.CompilerParams(dimension_semantics=("arbitrary",),
                                             vmem_limit_bytes=VMEM_LIMIT),
        name="out_router",
    )(oa, obc, x, wa, wb, g, b, wr, br)


def _sc_gather(table, idx):
    m = idx.shape[0]
    d = table.shape[1]
    per_worker = m // (SC_CORES * SC_SUBCORES)
    half = SC_CHUNK // 2
    n_groups = per_worker // SC_CHUNK
    mesh = plsc.VectorSubcoreMesh(core_axis_name="c", subcore_axis_name="s")

    @functools.partial(
        pl.kernel, mesh=mesh,
        out_type=jax.ShapeDtypeStruct((m, d), table.dtype),
        scratch_types=[pltpu.VMEM((per_worker,), I32)] + [pltpu.VMEM((half, d), table.dtype)] * 2
                      + [pltpu.SemaphoreType.DMA] * 4,
        name="sc_gather",
    )
    def gather(table_hbm, idx_hbm, out_hbm, idx_v, buf0, buf1, g0, g1, w0, w1):
        wid = lax.axis_index("s") * SC_CORES + lax.axis_index("c")
        base = wid * per_worker
        pltpu.sync_copy(idx_hbm.at[pl.ds(base, per_worker)], idx_v)

        def fetch(c, buf, sem):
            off = pl.multiple_of(c * half, 8)
            return pltpu.make_async_copy(table_hbm.at[idx_v.at[pl.ds(off, half)]], buf, sem)

        def put(c, buf, sem):
            off = pl.multiple_of(c * half, 8)
            return pltpu.make_async_copy(buf, out_hbm.at[pl.ds(base + off, half)], sem)

        fetch(0, buf0, g0).start()

        @pl.loop(0, n_groups)
        def _(g):
            c0 = 2 * g
            c1 = c0 + 1

            @pl.when(g > 0)
            def _():
                put(c0 - 1, buf1, w1).wait()

            fetch(c1, buf1, g1).start()
            fetch(c0, buf0, g0).wait()
            put(c0, buf0, w0).start()
            put(c0, buf0, w0).wait()

            @pl.when(g + 1 < n_groups)
            def _():
                fetch(c0 + 2, buf0, g0).start()

            fetch(c1, buf1, g1).wait()
            put(c1, buf1, w1).start()

        put(2 * n_groups - 1, buf1, w1).wait()

    return gather(table, idx)


def _sc_chunk(per_worker):
    return max(c for c in range(8, SC_SCATTER_MAX + 1, 8) if per_worker % c == 0)


def _sc_scatter_rows(h, pos):
    n, d = h.shape
    per_worker = n // (SC_CORES * SC_SUBCORES)
    ch = _sc_chunk(per_worker)
    n_chunks = per_worker // ch
    mesh = plsc.VectorSubcoreMesh(core_axis_name="c", subcore_axis_name="s")

    @functools.partial(
        pl.kernel, mesh=mesh,
        out_type=jax.ShapeDtypeStruct((TOP_K * n, d), h.dtype),
        scratch_types=[pltpu.VMEM((TOP_K * n_chunks, ch), I32), pltpu.VMEM((ch, d), h.dtype),
                       pltpu.VMEM((ch, d), h.dtype)] + [pltpu.SemaphoreType.DMA] * 5,
        name="sc_scatter",
    )
    def scatter(h_hbm, pos_hbm, out_hbm, idx_v, rows0, rows1, isem, r0, r1, w0, w1):
        bufs, rsem, wsem = (rows0, rows1), (r0, r1), (w0, w1)
        wid = lax.axis_index("s") * SC_CORES + lax.axis_index("c")
        base = pl.multiple_of(wid * per_worker, 8)

        def rows_load(c):
            return pltpu.async_copy(h_hbm.at[pl.ds(base + c * ch, ch)], bufs[c % 2], rsem[c % 2])

        idx_loads = [pltpu.async_copy(pos_hbm.at[pl.ds(k * n + base + c * ch, ch)],
                                      idx_v.at[k * n_chunks + c], isem)
                     for k in range(TOP_K) for c in range(n_chunks)]
        load = rows_load(0)
        for cp in idx_loads:
            cp.wait()
        for c in range(n_chunks):
            nxt = rows_load(c + 1) if c + 1 < n_chunks else None
            load.wait()
            writes = [pltpu.async_copy(bufs[c % 2], out_hbm.at[idx_v.at[k * n_chunks + c]], wsem[c % 2])
                      for k in range(TOP_K)]
            for cp in writes:
                cp.wait()
            load = nxt

    return scatter(h, pos)


def _expert_kernel(vt_ref, ve_ref, lo_ref, hi_ref, ord_ref, nxt_ref, x_ref, wu_hbm, bu_ref, wd_hbm, bd_ref,
                   y_ref, wu32_ref, wd32_ref, sem, wu16_ref, wd16_ref, *, layer):
    i = pl.program_id(0)
    ip = jnp.maximum(i - 1, 0)
    e = ve_ref[i]
    tile = vt_ref[i]
    first_visit = (i == 0) | (tile != vt_ref[ip])
    lo = lo_ref[i]
    hi = hi_ref[i]
    tm = x_ref.shape[0]
    dff = wd16_ref.shape[0]

    def weight_copies(expert, slot):
        return (pltpu.make_async_copy(wu_hbm.at[layer, expert], wu32_ref.at[slot], sem.at[0, slot]),
                pltpu.make_async_copy(wd_hbm.at[layer, expert], wd32_ref.at[slot], sem.at[1, slot]))

    @pl.when(i == 0)
    def _():
        for cp in weight_copies(e, 0):
            cp.start()

    @pl.when((i == 0) | (e != ve_ref[ip]))
    def _():
        slot = ord_ref[i] % 2
        for cp in weight_copies(e, slot):
            cp.wait()

        @pl.when(nxt_ref[i] >= 0)
        def _():
            for cp in weight_copies(nxt_ref[i], 1 - slot):
                cp.start()

        step = 128

        def cast(s, _):
            rows = pl.ds(pl.multiple_of(s * step, step), step)
            wu16_ref[rows, :] = wu32_ref[slot, rows, :].astype(BF16)
            return 0

        lax.fori_loop(0, wu16_ref.shape[0] // step, cast, 0)

        def cast_d(s, _):
            rows = pl.ds(pl.multiple_of(s * step, step), step)
            wd16_ref[rows, :] = wd32_ref[slot, rows, :].astype(BF16)
            return 0

        lax.fori_loop(0, dff // step, cast_d, 0)

    @pl.when(first_visit)
    def _():
        y_ref[...] = jnp.zeros_like(y_ref)

    def ffn(rows):
        x = _unpack_bf16_pairs(lax.bitcast_convert_type(x_ref[rows, :], U32))
        glu = jnp.minimum(_dot(x, wu16_ref[:, :dff]) + bu_ref[:, :dff], SWIGLU_LIMIT)
        lin = jnp.clip(_dot(x, wu16_ref[:, dff:]) + bu_ref[:, dff:], -SWIGLU_LIMIT, SWIGLU_LIMIT)
        act = glu * (1.0 / (1.0 + jnp.exp(-SWIGLU_ALPHA * glu))) * (lin + 1.0)
        y = _dot(act.astype(BF16), wd16_ref[...]) + bd_ref[...]
        return lax.bitcast_convert_type(_pack_bf16_pairs(y), F32)

    subs = [slice(s * EXPERT_SUB, (s + 1) * EXPERT_SUB) for s in range(tm // EXPERT_SUB)]
    whole = (lo <= tile * tm) & (hi >= tile * tm + tm)

    @pl.when(whole)
    def _():
        for rows in subs:
            y_ref[rows, :] = ffn(rows)

    for s, rows in enumerate(subs):
        r0 = tile * tm + s * EXPERT_SUB

        @pl.when(jnp.logical_not(whole) & (hi > r0) & (lo < r0 + EXPERT_SUB))
        def _():
            row = r0 + lax.broadcasted_iota(I32, (EXPERT_SUB, 1), 0)
            y_ref[rows, :] = jnp.where((row >= lo) & (row < hi), ffn(rows), y_ref[rows, :])


def _experts(x_sorted, visits, w_up, b_up, w_down, b_down, layer):
    ns = x_sorted.shape[0]
    tm = EXPERT_TILE
    d, dff = w_down.shape[3], w_down.shape[2]
    bu = b_up.reshape(b_up.shape[0], N_EXPERTS, 1, 2 * dff)
    bd = b_down.reshape(b_down.shape[0], N_EXPERTS, 1, d)
    wmap = lambda i, vt, ve, *_: (layer, ve[i], 0, 0)
    xmap = lambda i, vt, *_: (vt[i], 0)
    anywhere = pl.BlockSpec(memory_space=pl.ANY)
    grid_spec = pltpu.PrefetchScalarGridSpec(
        num_scalar_prefetch=len(visits),
        grid=(visits[0].shape[0],),
        in_specs=[pl.BlockSpec((tm, x_sorted.shape[1]), xmap),
                  anywhere,
                  pl.BlockSpec((None, None, 1, 2 * dff), wmap),
                  anywhere,
                  pl.BlockSpec((None, None, 1, d), wmap)],
        out_specs=pl.BlockSpec((tm, d // 2), xmap),
        scratch_shapes=[pltpu.VMEM((2, d, 2 * dff), F32), pltpu.VMEM((2, dff, d), F32),
                        pltpu.SemaphoreType.DMA((2, 2)),
                        pltpu.VMEM((d, 2 * dff), BF16), pltpu.VMEM((dff, d), BF16)],
    )
    return pl.pallas_call(
        functools.partial(_expert_kernel, layer=layer),
        grid_spec=grid_spec,
        out_shape=jax.ShapeDtypeStruct((ns, d // 2), F32),
        compiler_params=pltpu.CompilerParams(dimension_semantics=("arbitrary",),
                                             vmem_limit_bytes=VMEM_LIMIT),
        name="experts",
    )(*visits, x_sorted, w_up, bu, w_down, bd)


def _expert_visits(cnt, n_rows):
    tm = EXPERT_TILE
    n_steps = n_rows // tm + N_EXPERTS
    ends = jnp.cumsum(cnt)
    starts = ends - cnt
    first_tile = starts // tm
    n_vis = jnp.where(cnt > 0, (ends - 1) // tm - first_tile + 1, 0)
    vis_end = jnp.cumsum(n_vis)
    vis_start = vis_end - n_vis
    v = jnp.arange(n_steps, dtype=I32)
    vc = jnp.minimum(v, vis_end[-1] - 1)
    onehot = ((vis_start[None, :] <= vc[:, None]) & (vc[:, None] < vis_end[None, :])).astype(I32)
    pick = lambda a: jnp.sum(onehot * a[None, :], axis=1).astype(I32)
    expert = pick(jnp.arange(N_EXPERTS, dtype=I32))
    tile = pick(first_tile) + vc - pick(vis_start)
    real = v < vis_end[-1]
    lo = jnp.where(real, pick(starts), 0).astype(I32)
    hi = jnp.where(real, pick(ends), 0).astype(I32)
    ids = jnp.arange(N_EXPERTS, dtype=I32)
    used = cnt > 0
    ordinal = jnp.cumsum(used.astype(I32)) - used.astype(I32)
    later = jnp.min(jnp.where((ids[None, :] > ids[:, None]) & used[None, :], ids[None, :], N_EXPERTS), axis=1)
    following = jnp.where(later < N_EXPERTS, later, -1).astype(I32)
    return tile.astype(I32), expert, lo, hi, pick(ordinal), pick(following)


def _combine_kernel(y0_ref, y1_ref, y2_ref, y3_ref, gate_ref, h_ref, g_ref, b_ref, *o_refs, alpha, first_tiles):
    gate = gate_ref[...]
    lo = jnp.zeros(y0_ref.shape, F32)
    hi = jnp.zeros(y0_ref.shape, F32)
    for k, y_ref in enumerate((y0_ref, y1_ref, y2_ref, y3_ref)):
        w = lax.bitcast_convert_type(y_ref[...], U32)
        lo = lo + gate[:, k:k + 1] * lax.bitcast_convert_type(w << 16, F32)
        hi = hi + gate[:, k:k + 1] * lax.bitcast_convert_type(w & jnp.uint32(0xFFFF0000), F32)
    acc = jnp.concatenate([lo, hi], axis=1)
    out = _layer_norm(alpha * h_ref[...] + acc, g_ref[...], b_ref[...])
    if first_tiles is None:
        o_refs[0][...] = out
    else:
        i = pl.program_id(0)

        @pl.when(i < first_tiles)
        def _():
            o_refs[0][...] = out

        @pl.when(i >= first_tiles)
        def _():
            o_refs[1][...] = out


def _combine(y_tok, gates, h, ln_g, ln_b, alpha, split=None):
    n, d = h.shape
    tm = _wide_tile(n, COMBINE_TILE_MAX) if split is None else TOKEN_TILE
    nt = n // tm
    g, b = ln_g.reshape(1, d), ln_b.reshape(1, d)
    full = lambda a: pl.BlockSpec(a.shape, lambda i: (0, 0))
    ysp = lambda k: pl.BlockSpec((tm, d // 2), lambda i: (k * nt + i, 0))
    if split is None:
        first_tiles = None
        out_specs = pl.BlockSpec((tm, d), lambda i: (i, 0))
        out_shape = jax.ShapeDtypeStruct((n, d), F32)
    else:
        first_tiles = split // tm
        out_specs = [pl.BlockSpec((tm, d), lambda i: (jnp.minimum(i, first_tiles - 1), 0)),
                     pl.BlockSpec((tm, d), lambda i: (jnp.maximum(i - first_tiles, 0), 0))]
        out_shape = [jax.ShapeDtypeStruct((split, d), F32), jax.ShapeDtypeStruct((n - split, d), F32)]
    return pl.pallas_call(
        functools.partial(_combine_kernel, alpha=alpha, first_tiles=first_tiles),
        grid=(nt,),
        in_specs=[ysp(0), ysp(1), ysp(2), ysp(3),
                  pl.BlockSpec((tm, TOP_K), lambda i: (i, 0)),
                  pl.BlockSpec((tm, d), lambda i: (i, 0)), full(g), full(b)],
        out_specs=out_specs,
        out_shape=out_shape,
        compiler_params=pltpu.CompilerParams(dimension_semantics=("arbitrary",),
                                             vmem_limit_bytes=VMEM_LIMIT),
        name="combine",
    )(y_tok, y_tok, y_tok, y_tok, gates, h, g, b)


def _round_up(a, m):
    return (a + m - 1) // m * m


def _wide_tile(n, cap):
    return max(t for t in range(TOKEN_TILE, cap + 1, TOKEN_TILE) if n % t == 0)


def _moe(h, hp, idx, gates, rank, counts, w_up, b_up, w_down, b_down, ln_g, ln_b, alpha, layer, split=None):
    n, d = h.shape
    ns = TOP_K * n
    cnt = counts[:, 0]
    starts = jnp.cumsum(cnt) - cnt
    experts = jnp.arange(N_EXPERTS, dtype=I32)
    offs = jnp.sum(jnp.where(idx[:, :, None] == experts, starts, 0), axis=-1)
    pos = (offs + rank).reshape(-1).astype(I32)
    x_sorted = _sc_scatter_rows(hp, pos)
    y_sorted = _experts(x_sorted, _expert_visits(cnt, ns), w_up, b_up, w_down, b_down, layer)
    m2 = _round_up(ns, SC_ROW_ALIGN)
    fill = jnp.arange(ns, m2, dtype=I32) - ns
    y_tok = _sc_gather(y_sorted, jnp.concatenate([pos, fill]))
    return _combine(y_tok, gates.T, h, ln_g, ln_b, alpha, split)


def kernel(x_prompt, x_sample, cache_k, cache_v, state_conv, state_gla, w_in, conv_w, w_gate, b_gate,
           gla_norm_g, w_out, ln1_g, ln1_b, w_router, b_router, w_up, b_up, w_down, b_down, ln2_g, ln2_b):
    depth = w_in.shape[0]
    bp, seq, d = x_prompt.shape
    bs, ts, _ = x_sample.shape
    past = cache_k.shape[2]
    n_p = bp * seq
    n = n_p + bs * ts
    alpha = float((2 * depth) ** 0.25)
    x = jnp.concatenate([x_prompt.reshape(n_p, d), x_sample.reshape(bs * ts, d)], axis=0)
    ckt = cache_k.transpose(0, 1, 3, 4, 2).reshape(depth, bs, SB_DIM, past)
    cvt = cache_v.transpose(0, 1, 3, 4, 2).reshape(depth, bs, SB_DIM, past)
    zero_conv = jnp.zeros((bp, CONV_W - 1, CONV_DIM), F32)
    zero_gla = jnp.zeros((bp, GLA_HEADS, GLA_DK, GLA_DK), F32)
    wb = w_in.astype(BF16)
    o_r = 3 * SB_DIM
    n_r = 3 * CONV_DIM + 4 * GLA_DIM
    wq, wk, wv = wb[:, :, :SB_DIM], wb[:, :, SB_DIM:2 * SB_DIM], wb[:, :, 2 * SB_DIM:o_r]
    wkt, wvt = wk.transpose(0, 2, 1), wv.transpose(0, 2, 1)
    wr, wal = wb[:, :, o_r:o_r + n_r], wb[:, :, o_r + n_r:]
    wg = w_gate.astype(BF16)
    wo = w_out.astype(BF16)
    kt = jnp.zeros((depth, bp, SB_DIM, seq), F32)
    vt = jnp.zeros((depth, bp, SB_DIM, seq), F32)
    tm_p = min(PROMPT_TILE, seq)
    oa = jnp.zeros((n, SB_DIM), BF16)
    obc = jnp.zeros((n, CONV_DIM + GLA_DIM), BF16)
    outs = [[] for _ in range(6)]
    for l in range(depth):
        shared = (wr[l], wal[l], wg[l], b_gate[l].reshape(1, -1))
        q_p, kt, vt, obc, conv_p, gla_p = _mix_prompt(
            x, obc, kt, vt, l, (wq[l], wkt[l], wvt[l]) + shared,
            conv_w[l], gla_norm_g[l], zero_conv, zero_gla, bp, seq, tm_p)
        q_s, ks, vs, rest_s, la_s = _in_proj(x, n_p, n - n_p, TOKEN_TILE, (wq[l], wk[l], wv[l]) + shared)
        oa = _sb_prompt(oa, q_p, kt, vt, l, bp, seq)
        oa = _sb_decode(oa, q_s, ks, vs, ckt, cvt, l, n_p, bs, ts)
        obc, conv_s, gla_s = _conv_gla(obc, n, rest_s, la_s, conv_w[l], gla_norm_g[l], state_conv[l],
                                       state_gla[l], n_p, bs, ts)
        h, hp, idx, gates, rank, counts = _out_router(oa, obc, x, wo[l, :SB_DIM], wo[l, SB_DIM:], ln1_g[l],
                                                      ln1_b[l], w_router[l], b_router[l], alpha)
        x = _moe(h, hp, idx, gates, rank, counts, w_up, b_up, w_down, b_down, ln2_g[l], ln2_b[l],
                 alpha, l, split=n_p if l == depth - 1 else None)
        outs[0].append(conv_p)
        outs[1].append(gla_p)
        outs[2].append(ks.reshape(bs, ts, SB_HEADS, HEAD_DIM))
        outs[3].append(vs.reshape(bs, ts, SB_HEADS, HEAD_DIM))
        outs[4].append(conv_s)
        outs[5].append(gla_s)
    k_prompt = kt.reshape(depth, bp, SB_HEADS, HEAD_DIM, seq).transpose(0, 1, 4, 2, 3)
    v_prompt = vt.reshape(depth, bp, SB_HEADS, HEAD_DIM, seq).transpose(0, 1, 4, 2, 3)
    st = [jnp.stack(o) for o in outs]
    y_prompt, y_sample = x
    return (y_prompt.reshape(bp, seq, d), y_sample.reshape(bs, ts, d), k_prompt, v_prompt,
            st[0], st[1], st[2], st[3], st[4], st[5])
```

```python
import functools

import jax
import jax.numpy as jnp
from jax import lax
from jax.experimental import pallas as pl
from jax.experimental.pallas import tpu as pltpu
from jax.experimental.pallas import tpu_sc as plsc

F32 = jnp.float32
BF16 = jnp.bfloat16
I32 = jnp.int32
U32 = jnp.uint32

HEAD_DIM = 64
SB_HEADS = 8
SB_DIM = SB_HEADS * HEAD_DIM
CONV_DIM = 256
CONV_W = 3
GLA_HEADS = 4
GLA_DK = 64
GLA_DIM = GLA_HEADS * GLA_DK
GLA_TAU = 16.0
GLA_CHUNK = 64
N_EXPERTS = 32
TOP_K = 4
SWIGLU_LIMIT = 7.0
SWIGLU_ALPHA = 1.702
NORM_EPS = 1e-5

LANES = 128
SC_CORES = 2
SC_SUBCORES = 16
SC_CHUNK = 128
SC_ROW_ALIGN = SC_CORES * SC_SUBCORES * SC_CHUNK
SC_SCATTER_MAX = 104
TOKEN_TILE = 256
WIDE_TILE_MAX = 1280
COMBINE_TILE_MAX = 640
PROMPT_TILE = 512
EXPERT_TILE = 512
EXPERT_SUB = 256
SB_TQ = 256
SB_BK = 256
SB_GROUP = 256
DEC_TK = 256
GLA_TC = 128
VMEM_LIMIT = 48 * 1024 * 1024
SB_DEAD = -100.0


def _dot(a, b):
    return jnp.dot(a, b, preferred_element_type=F32)


def _dot_nt(a, b):
    return lax.dot_general(a, b, (((1,), (1,)), ((), ())), preferred_element_type=F32)


def _split_bf16(x):
    hi = x.astype(BF16)
    lo = (x - hi.astype(F32)).astype(BF16)
    return hi, lo


def _dot_exact_rhs(x, m):
    hi, lo = _split_bf16(x)
    return _dot(hi, m) + _dot(lo, m)


def _dot_exact_lhs(m, x):
    hi, lo = _split_bf16(x)
    return _dot(m, hi) + _dot(m, lo)


def _pack_bf16_pairs(x):
    c = x.shape[1] // 2
    bits = lax.bitcast_convert_type(x.astype(BF16).astype(F32), U32)
    return (bits[:, :c] >> 16) | (bits[:, c:] & jnp.uint32(0xFFFF0000))


def _unpack_bf16_pairs(w):
    lo = lax.bitcast_convert_type(w << 16, F32)
    hi = lax.bitcast_convert_type(w & jnp.uint32(0xFFFF0000), F32)
    return jnp.concatenate([lo, hi], axis=1).astype(BF16)


def _softplus(z):
    return jnp.maximum(z, 0.0) + jnp.log(1.0 + jnp.exp(-jnp.abs(z)))


def _layer_norm(y, g, b):
    mu = jnp.mean(y, axis=-1, keepdims=True)
    yc = y - mu
    var = jnp.mean(yc * yc, axis=-1, keepdims=True)
    return yc * lax.rsqrt(var + NORM_EPS) * g + b


def _strict_upper(n, copies=1):
    r = lax.broadcasted_iota(I32, (copies * n, n), 0)
    c = lax.broadcasted_iota(I32, (copies * n, n), 1)
    for k in range(1, copies):
        r = r - jnp.where(r >= n, n, 0)
    return jnp.where(r > c, 1.0, 0.0).astype(BF16)


def _project(x_ref, wq_ref, wk_ref, wv_ref, wr_ref, wal_ref, wg_ref, bg_ref,
             q_ref, k_ref, v_ref, r_ref, la_ref, *, transposed_kv):
    xb = x_ref[...].astype(BF16)
    for c in range(0, SB_DIM, 256):
        q_ref[:, c:c + 256] = (_dot(xb, wq_ref[:, c:c + 256]) * (HEAD_DIM ** -0.5)).astype(BF16)
    for c in range(0, r_ref.shape[1], 256):
        r_ref[:, c:c + 256] = _dot(xb, wr_ref[:, c:c + 256])
    al = _dot(xb, wal_ref[...])
    g = _dot(al.astype(BF16), wg_ref[...]) + bg_ref[...]
    la_ref[...] = -_softplus(-g) * (1.0 / GLA_TAU)
    for c in range(0, SB_DIM, 256):
        if transposed_kv:
            k_ref[c:c + 256, :] = _dot_nt(wk_ref[c:c + 256, :], xb)
            v_ref[c:c + 256, :] = _dot_nt(wv_ref[c:c + 256, :], xb)
        else:
            k_ref[:, c:c + 256] = _dot(xb, wk_ref[:, c:c + 256])
            v_ref[:, c:c + 256] = _dot(xb, wv_ref[:, c:c + 256])


def _in_proj(x, row0, rows, tm, weights):
    d = x.shape[1]
    n_r = weights[3].shape[1]
    rb = row0 // tm
    full = lambda a: pl.BlockSpec(a.shape, lambda i: (0,) * a.ndim)
    row = lambda w: pl.BlockSpec((tm, w), lambda i: (i, 0))
    sds = lambda w, dt: jax.ShapeDtypeStruct((rows, w), dt)
    return pl.pallas_call(
        functools.partial(_project, transposed_kv=False),
        grid=(rows // tm,),
        in_specs=[pl.BlockSpec((tm, d), lambda i: (rb + i, 0))] + [full(w) for w in weights],
        out_specs=[row(SB_DIM), row(SB_DIM), row(SB_DIM), row(n_r), row(GLA_DIM)],
        out_shape=[sds(SB_DIM, BF16), sds(SB_DIM, F32), sds(SB_DIM, F32), sds(n_r, F32), sds(GLA_DIM, F32)],
        compiler_params=pltpu.CompilerParams(dimension_semantics=("arbitrary",),
                                             vmem_limit_bytes=VMEM_LIMIT),
        name="in_proj",
    )(x, *weights)


def _sb_weights(z, tri, run, mask):
    sp = _softplus(z)
    l1m = -sp
    lsig = z - sp
    if mask is not None:
        l1m = jnp.where(mask, l1m, 0.0)
    if tri.shape[0] == 2 * z.shape[1]:
        hi = lax.bitcast_convert_type(lax.bitcast_convert_type(l1m, U32) & jnp.uint32(0xFFFF0000), F32)
        parts = jnp.concatenate([hi.astype(BF16), (l1m - hi).astype(BF16)], axis=1)
        rest = _dot(parts, tri) + run
    else:
        rest = _dot_exact_rhs(l1m, tri) + run
    a = jnp.exp(lsig + rest)
    if mask is not None:
        a = jnp.where(mask, a, 0.0)
    return a.astype(BF16), run + jnp.sum(l1m, axis=1, keepdims=True)


def _sb_prompt_kernel(oa_ref, q_ref, kt_ref, vt_ref, o_ref, *, tq, bk):
    del oa_ref
    qi = pl.program_id(2)
    q = q_ref[...]
    n_hh = SB_GROUP // HEAD_DIM
    lane = lax.broadcasted_iota(I32, (1, SB_GROUP), 1)
    in_head = [(lane // HEAD_DIM) == h for h in range(n_hh)]
    qh = [jnp.where(m, q, jnp.zeros_like(q)) for m in in_head]
    tri = _strict_upper(bk, copies=2)
    n_full = (qi * tq) // bk
    qpos = qi * tq + lax.broadcasted_iota(I32, (tq, bk), 0)
    kpos = n_full * bk + lax.broadcasted_iota(I32, (tq, bk), 1)
    diag_mask = kpos < qpos

    def tile(jb, runs, mask):
        ks = pl.multiple_of(jb * bk, bk)
        kt = kt_ref[:, pl.ds(ks, bk)].astype(BF16)
        vt = vt_ref[:, pl.ds(ks, bk)].astype(BF16)
        out = jnp.zeros((tq, SB_GROUP), F32)
        new_runs = []
        for h in range(n_hh):
            a, run = _sb_weights(_dot(qh[h], kt), tri, runs[h], mask)
            out = jnp.where(in_head[h], _dot_nt(a, vt), out)
            new_runs.append(run)
        return out, tuple(new_runs)

    def alive_of(runs):
        m = jnp.max(runs[0])
        for r in runs[1:]:
            m = jnp.maximum(m, jnp.max(r))
        return m > SB_DEAD

    acc, runs = tile(n_full, tuple(jnp.zeros((tq, 1), F32) for _ in range(n_hh)), diag_mask)

    def cond(carry):
        j, alive, _, _ = carry
        return (j >= 0) & alive

    def body(carry):
        j, _, acc, runs = carry
        pv, runs = tile(j, runs, None)
        return j - 1, alive_of(runs), acc + pv, runs

    _, _, acc, _ = lax.while_loop(cond, body, (n_full - 1, jnp.bool_(True), acc, runs))
    o_ref[...] = acc.astype(o_ref.dtype)


def _sb_prompt(oa, q, kt, vt, layer, batch, seq):
    tq, bk = min(SB_TQ, seq), min(SB_BK, seq)
    nq = seq // tq
    hp = SB_DIM // SB_GROUP
    kv_spec = pl.BlockSpec((None, None, SB_GROUP, seq), lambda b, p, i: (layer, b, p, 0))
    return pl.pallas_call(
        functools.partial(_sb_prompt_kernel, tq=tq, bk=bk),
        grid=(batch, hp, nq),
        in_specs=[pl.BlockSpec(memory_space=pl.ANY),
                  pl.BlockSpec((tq, SB_GROUP), lambda b, p, i: (b * nq + i, p)), kv_spec, kv_spec],
        out_specs=pl.BlockSpec((tq, SB_GROUP), lambda b, p, i: (b * nq + i, p)),
        out_shape=jax.ShapeDtypeStruct(oa.shape, oa.dtype),
        input_output_aliases={0: 0},
        compiler_params=pltpu.CompilerParams(
            dimension_semantics=("arbitrary", "arbitrary", "arbitrary"),
            vmem_limit_bytes=VMEM_LIMIT),
        name="sb_prompt",
    )(oa, q, kt, vt)


def _sb_decode_kernel(oa_ref, q_ref, kn_ref, vn_ref, kc_hbm, vc_hbm, o_ref, kbuf, vbuf, sem, acc_ref, run_ref,
                      alive_ref, *, t, tk, bk, layer, nkb):
    del oa_ref
    b = pl.program_id(0)
    q = q_ref[...]

    def block_copies(j, slot):
        cols = pl.ds(pl.multiple_of((nkb - 1 - j) * tk, tk), tk)
        return (pltpu.make_async_copy(kc_hbm.at[layer, b, :, cols], kbuf.at[slot], sem.at[0, slot]),
                pltpu.make_async_copy(vc_hbm.at[layer, b, :, cols], vbuf.at[slot], sem.at[1, slot]))

    for cp in block_copies(0, 0):
        cp.start()

    kn = kn_ref[...].astype(BF16)
    vn = vn_ref[...].astype(BF16)
    lane = lax.broadcasted_iota(I32, (1, SB_DIM), 1)
    r = lax.broadcasted_iota(I32, (t, t), 0)
    c = lax.broadcasted_iota(I32, (t, t), 1)
    mask = c < r
    tri_new = _strict_upper(t)
    for h in range(SB_HEADS):
        qh = jnp.where((lane // HEAD_DIM) == h, q, jnp.zeros_like(q))
        a, run = _sb_weights(_dot_nt(qh, kn), tri_new, jnp.zeros((t, 1), F32), mask)
        acc_ref[h] = _dot(a, vn)[:, h * HEAD_DIM:(h + 1) * HEAD_DIM]
        run_ref[h * t:(h + 1) * t, :] = run
    alive_ref[0] = (jnp.max(run_ref[...]) > SB_DEAD).astype(I32)

    tri = _strict_upper(bk, copies=2)

    def cond(carry):
        j, alive = carry
        return (j < nkb) & (alive > 0)

    def body(carry):
        j, _ = carry
        slot = j % 2

        @pl.when(j + 1 < nkb)
        def _():
            for cp in block_copies(j + 1, 1 - slot):
                cp.start()

        for cp in block_copies(j, slot):
            cp.wait()
        for c in range(tk // bk - 1, -1, -1):
            cols = slice(c * bk, (c + 1) * bk)

            @pl.when(alive_ref[0] > 0)
            def _():
                z = jnp.concatenate(
                    [_dot(q[:, h * HEAD_DIM:(h + 1) * HEAD_DIM],
                          kbuf[slot, h * HEAD_DIM:(h + 1) * HEAD_DIM, cols].astype(BF16))
                     for h in range(SB_HEADS)], axis=0)
                a, run = _sb_weights(z, tri, run_ref[...], None)
                for h in range(SB_HEADS):
                    vt = vbuf[slot, h * HEAD_DIM:(h + 1) * HEAD_DIM, cols].astype(BF16)
                    acc_ref[h] = acc_ref[h] + _dot_nt(a[h * t:(h + 1) * t, :], vt)
                run_ref[...] = run
                alive_ref[0] = (jnp.max(run) > SB_DEAD).astype(I32)

        return j + 1, alive_ref[0]

    j_end, _ = lax.while_loop(cond, body, (jnp.int32(0), alive_ref[0]))

    @pl.when(j_end < nkb)
    def _():
        for cp in block_copies(j_end, j_end % 2):
            cp.wait()

    for h in range(SB_HEADS):
        o_ref[:, h * HEAD_DIM:(h + 1) * HEAD_DIM] = acc_ref[h].astype(o_ref.dtype)


def _sb_decode(oa, q, ks, vs, cache_kt, cache_vt, layer, row0, batch, t):
    past = cache_kt.shape[3]
    tk = min(DEC_TK, past)
    bk = min(SB_BK, tk)
    nkb = past // tk
    rb = row0 // t
    new = pl.BlockSpec((t, SB_DIM), lambda b: (b, 0))
    anywhere = pl.BlockSpec(memory_space=pl.ANY)
    return pl.pallas_call(
        functools.partial(_sb_decode_kernel, t=t, tk=tk, bk=bk, layer=layer, nkb=nkb),
        grid=(batch,),
        in_specs=[anywhere, new, new, new, anywhere, anywhere],
        out_specs=pl.BlockSpec((t, SB_DIM), lambda b: (rb + b, 0)),
        out_shape=jax.ShapeDtypeStruct(oa.shape, oa.dtype),
        input_output_aliases={0: 0},
        scratch_shapes=[pltpu.VMEM((2, SB_DIM, tk), F32),
                        pltpu.VMEM((2, SB_DIM, tk), F32),
                        pltpu.SemaphoreType.DMA((2, 2)),
                        pltpu.VMEM((SB_HEADS, t, HEAD_DIM), F32),
                        pltpu.VMEM((SB_HEADS * t, 1), F32),
                        pltpu.SMEM((1,), I32)],
        compiler_params=pltpu.CompilerParams(dimension_semantics=("arbitrary",),
                                             vmem_limit_bytes=VMEM_LIMIT),
        name="sb_decode",
    )(oa, q, ks, vs, cache_kt, cache_vt)


def _mix_state_load(cprev_ref, sprev_ref, ctail_ref, st_ref):
    n_hh = LANES // GLA_DK
    ctail_ref[...] = cprev_ref[...]
    for p in range(GLA_DIM // LANES):
        st_ref[p] = jnp.zeros((LANES, LANES), F32)
        for hh in range(n_hh):
            st_ref[p, hh * GLA_DK:(hh + 1) * GLA_DK, hh * GLA_DK:(hh + 1) * GLA_DK] = sprev_ref[p * n_hh + hh].T


def _mix_state_store(ctail_ref, st_ref, cnew_ref, snew_ref):
    n_hh = LANES // GLA_DK
    cnew_ref[...] = ctail_ref[...]
    for p in range(GLA_DIM // LANES):
        for hh in range(n_hh):
            blk = st_ref[p, hh * GLA_DK:(hh + 1) * GLA_DK, :]
            snew_ref[p * n_hh + hh] = blk.T[hh * GLA_DK:(hh + 1) * GLA_DK, :]


def _conv_gla_block(bg_ref, cg_ref, u_ref, qc_ref, kc_ref, vc_ref, gc_ref, la_ref, cw_ref, ng_ref,
                    o_ref, ctail_ref, st_ref, tc, chunk):
    n_pair = GLA_DIM // LANES
    n_hh = LANES // GLA_DK

    z = cg_ref[...] * u_ref[...]
    tail = ctail_ref[...]
    row = lax.broadcasted_iota(I32, z.shape, 0)
    z1 = jnp.where(row < 1, tail[1:2, :], pltpu.roll(z, 1, 0))
    z2 = jnp.where(row < 2, jnp.where(row < 1, tail[0:1, :], tail[1:2, :]), pltpu.roll(z, 2, 0))
    cw = cw_ref[...]
    y = z2 * cw[0:1, :] + z1 * cw[1:2, :] + z * cw[2:3, :]
    o_ref[:, 0:CONV_DIM] = (bg_ref[...] * y).astype(o_ref.dtype)
    ctail_ref[...] = z[tc - 2:tc, :]

    r = lax.broadcasted_iota(I32, (tc, tc), 0)
    c = lax.broadcasted_iota(I32, (tc, tc), 1)
    same_chunk = (r // chunk) == (c // chunk)
    chunk_sum = jnp.where(same_chunk, 1.0, 0.0).astype(BF16)
    chunk_cumsum = jnp.where(same_chunk & (c <= r), 1.0, 0.0).astype(BF16)
    lane = lax.broadcasted_iota(I32, (1, LANES), 1)
    lr = lax.broadcasted_iota(I32, (LANES, LANES), 0) // GLA_DK
    lc = lax.broadcasted_iota(I32, (LANES, LANES), 1) // GLA_DK
    same_head = lr == lc
    head_mean = jnp.where(same_head, 1.0 / GLA_DK, 0.0).astype(BF16)
    r2 = lax.broadcasted_iota(I32, (tc, n_hh * tc), 0)
    c2 = lax.broadcasted_iota(I32, (tc, n_hh * tc), 1)
    c2 = c2 - jnp.where(c2 >= tc, tc, 0)
    causal2 = ((r2 // chunk) == (c2 // chunk)) & (c2 <= r2)
    for p in range(n_pair):
        cols = slice(p * LANES, (p + 1) * LANES)
        la = la_ref[:, cols]
        b = _dot_exact_lhs(chunk_cumsum, la)
        tot = _dot_exact_lhs(chunk_sum, la)
        k = kc_ref[:, cols]
        qe = (qc_ref[:, cols] * (GLA_DK ** -0.5) * jnp.exp(b)).astype(BF16)
        ke = (k * jnp.exp(-b)).astype(BF16)
        kd = (k * jnp.exp(tot - b)).astype(BF16)
        decay = jnp.exp(tot)
        vb = vc_ref[:, cols].astype(BF16)
        zeros = jnp.zeros_like(ke)
        ke_st = jnp.concatenate([jnp.where((lane // GLA_DK) == hh, ke, zeros)
                                 for hh in range(n_hh)], axis=0)
        v_st = jnp.concatenate([jnp.where((lane // GLA_DK) == hh, vb, zeros)
                                for hh in range(n_hh)], axis=0)
        a = jnp.where(causal2, _dot_nt(qe, ke_st), 0.0)
        o = _dot(a.astype(BF16), v_st)
        st = st_ref[p]
        from_state = []
        for ci in range(tc // chunk):
            rows = slice(ci * chunk, (ci + 1) * chunk)
            from_state.append(_dot_nt(qe[rows], st.astype(BF16)))
            upd = _dot(vb[rows].T, kd[rows])
            st = jnp.where(same_head, st * decay[ci * chunk:ci * chunk + 1, :] + upd, 0.0)
        st_ref[p] = st
        o = o + jnp.concatenate(from_state, axis=0)
        ms = _dot_exact_rhs(o * o, head_mean)
        o = o * lax.rsqrt(ms + NORM_EPS) * ng_ref[:, cols]
        g = gc_ref[:, cols]
        o = o * (g * (1.0 / (1.0 + jnp.exp(-g))))
        o_ref[:, CONV_DIM + p * LANES:CONV_DIM + (p + 1) * LANES] = o.astype(o_ref.dtype)


def _conv_gla_kernel(*refs, tc, chunk):
    (_, bg_ref, cg_ref, u_ref, qc_ref, kc_ref, vc_ref, gc_ref, la_ref, cw_ref, ng_ref, cprev_ref, sprev_ref,
     o_ref, cnew_ref, snew_ref, ctail_ref, st_ref) = refs
    ti = pl.program_id(1)

    @pl.when(ti == 0)
    def _():
        _mix_state_load(cprev_ref, sprev_ref, ctail_ref, st_ref)

    _conv_gla_block(bg_ref, cg_ref, u_ref, qc_ref, kc_ref, vc_ref, gc_ref, la_ref, cw_ref, ng_ref,
                    o_ref, ctail_ref, st_ref, tc, chunk)

    @pl.when(ti == pl.num_programs(1) - 1)
    def _():
        _mix_state_store(ctail_ref, st_ref, cnew_ref, snew_ref)


def _conv_gla(obc_prev, n, rest, la, conv_w, norm_g, conv_prev, gla_prev, row0, batch, t):
    tc = min(GLA_TC, t)
    chunk = min(GLA_CHUNK, t)
    nt = t // tc
    rb = row0 // tc
    col = lambda j: pl.BlockSpec((tc, CONV_DIM), lambda b, i: (b * nt + i, j))
    const2 = lambda a: pl.BlockSpec(a.shape, lambda b, i: (0, 0))
    ng = norm_g.reshape(1, GLA_DIM)
    args = (rest,) * 7 + (la, conv_w, ng, conv_prev, gla_prev)
    return pl.pallas_call(
        functools.partial(_conv_gla_kernel, tc=tc, chunk=chunk),
        grid=(batch, nt),
        in_specs=[
            pl.BlockSpec(memory_space=pl.ANY), col(0), col(1), col(2), col(3), col(4), col(5), col(6),
            pl.BlockSpec((tc, GLA_DIM), lambda b, i: (b * nt + i, 0)),
            const2(conv_w), const2(ng),
            pl.BlockSpec((None, CONV_W - 1, CONV_DIM), lambda b, i: (b, 0, 0)),
            pl.BlockSpec((None, GLA_HEADS, GLA_DK, GLA_DK), lambda b, i: (b, 0, 0, 0))],
        out_specs=[pl.BlockSpec((tc, CONV_DIM + GLA_DIM), lambda b, i: (rb + b * nt + i, 0)),
                   pl.BlockSpec((None, CONV_W - 1, CONV_DIM), lambda b, i: (b, 0, 0)),
                   pl.BlockSpec((None, GLA_HEADS, GLA_DK, GLA_DK), lambda b, i: (b, 0, 0, 0))],
        out_shape=[jax.ShapeDtypeStruct((n, CONV_DIM + GLA_DIM), BF16),
                   jax.ShapeDtypeStruct((batch, CONV_W - 1, CONV_DIM), F32),
                   jax.ShapeDtypeStruct((batch, GLA_HEADS, GLA_DK, GLA_DK), F32)],
        input_output_aliases={0: 0},
        scratch_shapes=[pltpu.VMEM((CONV_W - 1, CONV_DIM), F32),
                        pltpu.VMEM((GLA_DIM // LANES, LANES, LANES), F32)],
        compiler_params=pltpu.CompilerParams(dimension_semantics=("arbitrary", "arbitrary"),
                                             vmem_limit_bytes=VMEM_LIMIT),
        name="conv_gla",
    )(obc_prev, *args)


def _mix_prompt_kernel(*refs, tc, chunk):
    (_, _, _, x_ref, wq_ref, wk_ref, wv_ref, wr_ref, wal_ref, wg_ref, bg_ref, cw_ref, ng_ref, cprev_ref,
     sprev_ref, q_ref, kt_ref, vt_ref, o_ref, cnew_ref, snew_ref, r_scr, la_scr, ctail_ref, st_ref) = refs
    ti = pl.program_id(1)
    _project(x_ref, wq_ref, wk_ref, wv_ref, wr_ref, wal_ref, wg_ref, bg_ref,
             q_ref, kt_ref, vt_ref, r_scr, la_scr, transposed_kv=True)

    @pl.when(ti == 0)
    def _():
        _mix_state_load(cprev_ref, sprev_ref, ctail_ref, st_ref)

    for s in range(x_ref.shape[0] // tc):
        rows = pl.ds(s * tc, tc)
        part = [r_scr.at[rows, pl.ds(j * CONV_DIM, CONV_DIM)] for j in range(7)]
        _conv_gla_block(*part, la_scr.at[rows], cw_ref, ng_ref, o_ref.at[rows], ctail_ref, st_ref, tc, chunk)

    @pl.when(ti == pl.num_programs(1) - 1)
    def _():
        _mix_state_store(ctail_ref, st_ref, cnew_ref, snew_ref)


def _mix_prompt(x, obc_prev, kt, vt, layer, weights, conv_w, norm_g, conv_prev, gla_prev, batch, seq, tm):
    d = x.shape[1]
    n_p = batch * seq
    nt = seq // tm
    tc = min(GLA_TC, tm)
    wq, wk, wv, wr, wal, wg, bg = weights
    ng = norm_g.reshape(1, GLA_DIM)
    full = lambda a: pl.BlockSpec(a.shape, lambda b, i: (0,) * a.ndim)
    row = lambda w: pl.BlockSpec((tm, w), lambda b, i: (b * nt + i, 0))
    kv_spec = pl.BlockSpec((None, None, SB_DIM, tm), lambda b, i: (layer, b, 0, i))
    conv_spec = pl.BlockSpec((None, CONV_W - 1, CONV_DIM), lambda b, i: (b, 0, 0))
    gla_spec = pl.BlockSpec((None, GLA_HEADS, GLA_DK, GLA_DK), lambda b, i: (b, 0, 0, 0))
    anywhere = pl.BlockSpec(memory_space=pl.ANY)
    consts = (wq, wk, wv, wr, wal, wg, bg, conv_w, ng)
    return pl.pallas_call(
        functools.partial(_mix_prompt_kernel, tc=tc, chunk=min(GLA_CHUNK, tc)),
        grid=(batch, nt),
        in_specs=[anywhere, anywhere, anywhere, row(d)] + [full(w) for w in consts] + [conv_spec, gla_spec],
        out_specs=[row(SB_DIM), kv_spec, kv_spec, row(CONV_DIM + GLA_DIM), conv_spec, gla_spec],
        out_shape=[jax.ShapeDtypeStruct((n_p, SB_DIM), BF16),
                   jax.ShapeDtypeStruct(kt.shape, kt.dtype), jax.ShapeDtypeStruct(vt.shape, vt.dtype),
                   jax.ShapeDtypeStruct(obc_prev.shape, obc_prev.dtype),
                   jax.ShapeDtypeStruct((batch, CONV_W - 1, CONV_DIM), F32),
                   jax.ShapeDtypeStruct((batch, GLA_HEADS, GLA_DK, GLA_DK), F32)],
        input_output_aliases={0: 1, 1: 2, 2: 3},
        scratch_shapes=[pltpu.VMEM((tm, wr.shape[1]), F32),
                        pltpu.VMEM((tm, GLA_DIM), F32),
                        pltpu.VMEM((CONV_W - 1, CONV_DIM), F32),
                        pltpu.VMEM((GLA_DIM // LANES, LANES, LANES), F32)],
        compiler_params=pltpu.CompilerParams(dimension_semantics=("arbitrary", "arbitrary"),
                                             vmem_limit_bytes=VMEM_LIMIT),
        name="mix_prompt",
    )(kt, vt, obc_prev, x, *consts, conv_prev, gla_prev)


def _out_router_kernel(oa_ref, obc_ref, x_ref, wa_ref, wb_ref, g_ref, b_ref, wr_ref, br_ref,
                       h_ref, hp_ref, idx_ref, gate_ref, rank_ref, cnt_ref, carry_ref, *, alpha):
    i = pl.program_id(0)
    tm = x_ref.shape[0]

    @pl.when(i == 0)
    def _():
        carry_ref[...] = jnp.zeros_like(carry_ref)

    m = _dot(oa_ref[...], wa_ref[...]) + _dot(obc_ref[...], wb_ref[...])
    h = _layer_norm(alpha * x_ref[...] + m, g_ref[...], b_ref[...])
    h_ref[...] = h
    hp_ref[...] = lax.bitcast_convert_type(_pack_bf16_pairs(h), F32)

    h_hi, h_lo = _split_bf16(h)
    w_hi, w_lo = _split_bf16(wr_ref[...])
    logit = _dot_nt(w_hi, h_hi) + _dot_nt(w_hi, h_lo) + _dot_nt(w_lo, h_hi) + br_ref[...]
    eid = lax.broadcasted_iota(I32, (N_EXPERTS, tm), 0)
    r = lax.broadcasted_iota(I32, (tm, tm), 0)
    c = lax.broadcasted_iota(I32, (tm, tm), 1)
    before = jnp.where(r < c, 1.0, 0.0).astype(BF16)
    base = carry_ref[...]
    vals, idxs, onehots, bases = [], [], [], []
    for _ in range(TOP_K):
        mx = jnp.max(logit, axis=0, keepdims=True)
        sel = jnp.min(jnp.where(logit == mx, eid, N_EXPERTS), axis=0, keepdims=True)
        hit = eid == sel
        logit = jnp.where(hit, -jnp.inf, logit)
        onehot = jnp.where(hit, 1.0, 0.0)
        onehots.append(onehot)
        bases.append(base)
        base = base + jnp.sum(onehot, axis=1, keepdims=True)
        vals.append(mx)
        idxs.append(sel)
    carry_ref[...] = base
    earlier = _dot(jnp.concatenate(onehots, axis=0).astype(BF16), before)
    ranks = [jnp.sum(onehots[k] * (earlier[k * N_EXPERTS:(k + 1) * N_EXPERTS] + bases[k]),
                     axis=0, keepdims=True) for k in range(TOP_K)]
    e = [jnp.exp(v - vals[0]) for v in vals]
    inv = 1.0 / (e[0] + e[1] + e[2] + e[3])
    idx_ref[...] = jnp.concatenate(idxs, axis=0)
    gate_ref[...] = jnp.concatenate([ek * inv for ek in e], axis=0)
    rank_ref[...] = jnp.concatenate(ranks, axis=0).astype(I32)
    cnt_ref[...] = jnp.broadcast_to(base, cnt_ref.shape).astype(I32)


def _out_router(oa, obc, x, wa, wb, ln_g, ln_b, w_router, b_router, alpha):
    n, d = x.shape
    tm = _wide_tile(n, WIDE_TILE_MAX)
    wr = w_router.T
    br = b_router.reshape(N_EXPERTS, 1)
    g, b = ln_g.reshape(1, d), ln_b.reshape(1, d)
    full = lambda a: pl.BlockSpec(a.shape, lambda i: (0, 0))
    row = lambda w: pl.BlockSpec((tm, w), lambda i: (i, 0))
    colb = pl.BlockSpec((TOP_K, tm), lambda i: (0, i))
    return pl.pallas_call(
        functools.partial(_out_router_kernel, alpha=alpha),
        grid=(n // tm,),
        in_specs=[row(SB_DIM), row(CONV_DIM + GLA_DIM), row(d), full(wa), full(wb), full(g), full(b),
                  full(wr), full(br)],
        out_specs=[row(d), row(d // 2), colb, colb, colb,
                   pl.BlockSpec((N_EXPERTS, LANES), lambda i: (0, 0))],
        out_shape=[jax.ShapeDtypeStruct((n, d), F32),
                   jax.ShapeDtypeStruct((n, d // 2), F32),
                   jax.ShapeDtypeStruct((TOP_K, n), I32),
                   jax.ShapeDtypeStruct((TOP_K, n), F32),
                   jax.ShapeDtypeStruct((TOP_K, n), I32),
                   jax.ShapeDtypeStruct((N_EXPERTS, LANES), I32)],
        scratch_shapes=[pltpu.VMEM((N_EXPERTS, 1), F32)],
        compiler_params=pltpu.CompilerParams(dimension_semantics=("arbitrary",),
                                             vmem_limit_bytes=VMEM_LIMIT),
        name="out_router",
    )(oa, obc, x, wa, wb, g, b, wr, br)


def _sc_gather(table, idx):
    m = idx.shape[0]
    d = table.shape[1]
    per_worker = m // (SC_CORES * SC_SUBCORES)
    half = SC_CHUNK // 2
    n_groups = per_worker // SC_CHUNK
    mesh = plsc.VectorSubcoreMesh(core_axis_name="c", subcore_axis_name="s")

    @functools.partial(
        pl.kernel, mesh=mesh,
        out_type=jax.ShapeDtypeStruct((m, d), table.dtype),
        scratch_types=[pltpu.VMEM((per_worker,), I32)] + [pltpu.VMEM((half, d), table.dtype)] * 2
                      + [pltpu.SemaphoreType.DMA] * 4,
        name="sc_gather",
    )
    def gather(table_hbm, idx_hbm, out_hbm, idx_v, buf0, buf1, g0, g1, w0, w1):
        wid = lax.axis_index("s") * SC_CORES + lax.axis_index("c")
        base = wid * per_worker
        pltpu.sync_copy(idx_hbm.at[pl.ds(base, per_worker)], idx_v)

        def fetch(c, buf, sem):
            off = pl.multiple_of(c * half, 8)
            return pltpu.make_async_copy(table_hbm.at[idx_v.at[pl.ds(off, half)]], buf, sem)

        def put(c, buf, sem):
            off = pl.multiple_of(c * half, 8)
            return pltpu.make_async_copy(buf, out_hbm.at[pl.ds(base + off, half)], sem)

        fetch(0, buf0, g0).start()

        @pl.loop(0, n_groups)
        def _(g):
            c0 = 2 * g
            c1 = c0 + 1

            @pl.when(g > 0)
            def _():
                put(c0 - 1, buf1, w1).wait()

            fetch(c1, buf1, g1).start()
            fetch(c0, buf0, g0).wait()
            put(c0, buf0, w0).start()
            put(c0, buf0, w0).wait()

            @pl.when(g + 1 < n_groups)
            def _():
                fetch(c0 + 2, buf0, g0).start()

            fetch(c1, buf1, g1).wait()
            put(c1, buf1, w1).start()

        put(2 * n_groups - 1, buf1, w1).wait()

    return gather(table, idx)


def _sc_chunk(per_worker):
    return max(c for c in range(8, SC_SCATTER_MAX + 1, 8) if per_worker % c == 0)


def _sc_scatter_rows(h, pos):
    n, d = h.shape
    per_worker = n // (SC_CORES * SC_SUBCORES)
    ch = _sc_chunk(per_worker)
    n_chunks = per_worker // ch
    mesh = plsc.VectorSubcoreMesh(core_axis_name="c", subcore_axis_name="s")

    @functools.partial(
        pl.kernel, mesh=mesh,
        out_type=jax.ShapeDtypeStruct((TOP_K * n, d), h.dtype),
        scratch_types=[pltpu.VMEM((TOP_K * n_chunks, ch), I32), pltpu.VMEM((ch, d), h.dtype),
                       pltpu.VMEM((ch, d), h.dtype)] + [pltpu.SemaphoreType.DMA] * 5,
        name="sc_scatter",
    )
    def scatter(h_hbm, pos_hbm, out_hbm, idx_v, rows0, rows1, isem, r0, r1, w0, w1):
        bufs, rsem, wsem = (rows0, rows1), (r0, r1), (w0, w1)
        wid = lax.axis_index("s") * SC_CORES + lax.axis_index("c")
        base = pl.multiple_of(wid * per_worker, 8)

        def rows_load(c):
            return pltpu.async_copy(h_hbm.at[pl.ds(base + c * ch, ch)], bufs[c % 2], rsem[c % 2])

        idx_loads = [pltpu.async_copy(pos_hbm.at[pl.ds(k * n + base + c * ch, ch)],
                                      idx_v.at[k * n_chunks + c], isem)
                     for k in range(TOP_K) for c in range(n_chunks)]
        load = rows_load(0)
        for cp in idx_loads:
            cp.wait()
        for c in range(n_chunks):
            nxt = rows_load(c + 1) if c + 1 < n_chunks else None
            load.wait()
            writes = [pltpu.async_copy(bufs[c % 2], out_hbm.at[idx_v.at[k * n_chunks + c]], wsem[c % 2])
                      for k in range(TOP_K)]
            for cp in writes:
                cp.wait()
            load = nxt

    return scatter(h, pos)


def _expert_kernel(vt_ref, ve_ref, lo_ref, hi_ref, ord_ref, nxt_ref, x_ref, wu_hbm, bu_ref, wd_hbm, bd_ref,
                   y_ref, wu32_ref, wd32_ref, sem, wu16_ref, wd16_ref, *, layer):
    i = pl.program_id(0)
    ip = jnp.maximum(i - 1, 0)
    e = ve_ref[i]
    tile = vt_ref[i]
    first_visit = (i == 0) | (tile != vt_ref[ip])
    lo = lo_ref[i]
    hi = hi_ref[i]
    tm = x_ref.shape[0]
    dff = wd16_ref.shape[0]

    def weight_copies(expert, slot):
        return (pltpu.make_async_copy(wu_hbm.at[layer, expert], wu32_ref.at[slot], sem.at[0, slot]),
                pltpu.make_async_copy(wd_hbm.at[layer, expert], wd32_ref.at[slot], sem.at[1, slot]))

    @pl.when(i == 0)
    def _():
        for cp in weight_copies(e, 0):
            cp.start()

    @pl.when((i == 0) | (e != ve_ref[ip]))
    def _():
        slot = ord_ref[i] % 2
        for cp in weight_copies(e, slot):
            cp.wait()

        @pl.when(nxt_ref[i] >= 0)
        def _():
            for cp in weight_copies(nxt_ref[i], 1 - slot):
                cp.start()

        step = 128

        def cast(s, _):
            rows = pl.ds(pl.multiple_of(s * step, step), step)
            wu16_ref[rows, :] = wu32_ref[slot, rows, :].astype(BF16)
            return 0

        lax.fori_loop(0, wu16_ref.shape[0] // step, cast, 0)

        def cast_d(s, _):
            rows = pl.ds(pl.multiple_of(s * step, step), step)
            wd16_ref[rows, :] = wd32_ref[slot, rows, :].astype(BF16)
            return 0

        lax.fori_loop(0, dff // step, cast_d, 0)

    @pl.when(first_visit)
    def _():
        y_ref[...] = jnp.zeros_like(y_ref)

    def ffn(rows):
        x = _unpack_bf16_pairs(lax.bitcast_convert_type(x_ref[rows, :], U32))
        glu = jnp.minimum(_dot(x, wu16_ref[:, :dff]) + bu_ref[:, :dff], SWIGLU_LIMIT)
        lin = jnp.clip(_dot(x, wu16_ref[:, dff:]) + bu_ref[:, dff:], -SWIGLU_LIMIT, SWIGLU_LIMIT)
        act = glu * (1.0 / (1.0 + jnp.exp(-SWIGLU_ALPHA * glu))) * (lin + 1.0)
        y = _dot(act.astype(BF16), wd16_ref[...]) + bd_ref[...]
        return lax.bitcast_convert_type(_pack_bf16_pairs(y), F32)

    subs = [slice(s * EXPERT_SUB, (s + 1) * EXPERT_SUB) for s in range(tm // EXPERT_SUB)]
    whole = (lo <= tile * tm) & (hi >= tile * tm + tm)

    @pl.when(whole)
    def _():
        for rows in subs:
            y_ref[rows, :] = ffn(rows)

    for s, rows in enumerate(subs):
        r0 = tile * tm + s * EXPERT_SUB

        @pl.when(jnp.logical_not(whole) & (hi > r0) & (lo < r0 + EXPERT_SUB))
        def _():
            row = r0 + lax.broadcasted_iota(I32, (EXPERT_SUB, 1), 0)
            y_ref[rows, :] = jnp.where((row >= lo) & (row < hi), ffn(rows), y_ref[rows, :])


def _experts(x_sorted, visits, w_up, b_up, w_down, b_down, layer):
    ns = x_sorted.shape[0]
    tm = EXPERT_TILE
    d, dff = w_down.shape[3], w_down.shape[2]
    bu = b_up.reshape(b_up.shape[0], N_EXPERTS, 1, 2 * dff)
    bd = b_down.reshape(b_down.shape[0], N_EXPERTS, 1, d)
    wmap = lambda i, vt, ve, *_: (layer, ve[i], 0, 0)
    xmap = lambda i, vt, *_: (vt[i], 0)
    anywhere = pl.BlockSpec(memory_space=pl.ANY)
    grid_spec = pltpu.PrefetchScalarGridSpec(
        num_scalar_prefetch=len(visits),
        grid=(visits[0].shape[0],),
        in_specs=[pl.BlockSpec((tm, x_sorted.shape[1]), xmap),
                  anywhere,
                  pl.BlockSpec((None, None, 1, 2 * dff), wmap),
                  anywhere,
                  pl.BlockSpec((None, None, 1, d), wmap)],
        out_specs=pl.BlockSpec((tm, d // 2), xmap),
        scratch_shapes=[pltpu.VMEM((2, d, 2 * dff), F32), pltpu.VMEM((2, dff, d), F32),
                        pltpu.SemaphoreType.DMA((2, 2)),
                        pltpu.VMEM((d, 2 * dff), BF16), pltpu.VMEM((dff, d), BF16)],
    )
    return pl.pallas_call(
        functools.partial(_expert_kernel, layer=layer),
        grid_spec=grid_spec,
        out_shape=jax.ShapeDtypeStruct((ns, d // 2), F32),
        compiler_params=pltpu.CompilerParams(dimension_semantics=("arbitrary",),
                                             vmem_limit_bytes=VMEM_LIMIT),
        name="experts",
    )(*visits, x_sorted, w_up, bu, w_down, bd)


def _expert_visits(cnt, n_rows):
    tm = EXPERT_TILE
    n_steps = n_rows // tm + N_EXPERTS
    ends = jnp.cumsum(cnt)
    starts = ends - cnt
    first_tile = starts // tm
    n_vis = jnp.where(cnt > 0, (ends - 1) // tm - first_tile + 1, 0)
    vis_end = jnp.cumsum(n_vis)
    vis_start = vis_end - n_vis
    v = jnp.arange(n_steps, dtype=I32)
    vc = jnp.minimum(v, vis_end[-1] - 1)
    onehot = ((vis_start[None, :] <= vc[:, None]) & (vc[:, None] < vis_end[None, :])).astype(I32)
    pick = lambda a: jnp.sum(onehot * a[None, :], axis=1).astype(I32)
    expert = pick(jnp.arange(N_EXPERTS, dtype=I32))
    tile = pick(first_tile) + vc - pick(vis_start)
    real = v < vis_end[-1]
    lo = jnp.where(real, pick(starts), 0).astype(I32)
    hi = jnp.where(real, pick(ends), 0).astype(I32)
    ids = jnp.arange(N_EXPERTS, dtype=I32)
    used = cnt > 0
    ordinal = jnp.cumsum(used.astype(I32)) - used.astype(I32)
    later = jnp.min(jnp.where((ids[None, :] > ids[:, None]) & used[None, :], ids[None, :], N_EXPERTS), axis=1)
    following = jnp.where(later < N_EXPERTS, later, -1).astype(I32)
    return tile.astype(I32), expert, lo, hi, pick(ordinal), pick(following)


def _combine_kernel(y0_ref, y1_ref, y2_ref, y3_ref, gate_ref, h_ref, g_ref, b_ref, *o_refs, alpha, first_tiles):
    gate = gate_ref[...]
    lo = jnp.zeros(y0_ref.shape, F32)
    hi = jnp.zeros(y0_ref.shape, F32)
    for k, y_ref in enumerate((y0_ref, y1_ref, y2_ref, y3_ref)):
        w = lax.bitcast_convert_type(y_ref[...], U32)
        lo = lo + gate[:, k:k + 1] * lax.bitcast_convert_type(w << 16, F32)
        hi = hi + gate[:, k:k + 1] * lax.bitcast_convert_type(w & jnp.uint32(0xFFFF0000), F32)
    acc = jnp.concatenate([lo, hi], axis=1)
    out = _layer_norm(alpha * h_ref[...] + acc, g_ref[...], b_ref[...])
    if first_tiles is None:
        o_refs[0][...] = out
    else:
        i = pl.program_id(0)

        @pl.when(i < first_tiles)
        def _():
            o_refs[0][...] = out

        @pl.when(i >= first_tiles)
        def _():
            o_refs[1][...] = out


def _combine(y_tok, gates, h, ln_g, ln_b, alpha, split=None):
    n, d = h.shape
    tm = _wide_tile(n, COMBINE_TILE_MAX) if split is None else TOKEN_TILE
    nt = n // tm
    g, b = ln_g.reshape(1, d), ln_b.reshape(1, d)
    full = lambda a: pl.BlockSpec(a.shape, lambda i: (0, 0))
    ysp = lambda k: pl.BlockSpec((tm, d // 2), lambda i: (k * nt + i, 0))
    if split is None:
        first_tiles = None
        out_specs = pl.BlockSpec((tm, d), lambda i: (i, 0))
        out_shape = jax.ShapeDtypeStruct((n, d), F32)
    else:
        first_tiles = split // tm
        out_specs = [pl.BlockSpec((tm, d), lambda i: (jnp.minimum(i, first_tiles - 1), 0)),
                     pl.BlockSpec((tm, d), lambda i: (jnp.maximum(i - first_tiles, 0), 0))]
        out_shape = [jax.ShapeDtypeStruct((split, d), F32), jax.ShapeDtypeStruct((n - split, d), F32)]
    return pl.pallas_call(
        functools.partial(_combine_kernel, alpha=alpha, first_tiles=first_tiles),
        grid=(nt,),
        in_specs=[ysp(0), ysp(1), ysp(2), ysp(3),
                  pl.BlockSpec((tm, TOP_K), lambda i: (i, 0)),
                  pl.BlockSpec((tm, d), lambda i: (i, 0)), full(g), full(b)],
        out_specs=out_specs,
        out_shape=out_shape,
        compiler_params=pltpu.CompilerParams(dimension_semantics=("arbitrary",),
                                             vmem_limit_bytes=VMEM_LIMIT),
        name="combine",
    )(y_tok, y_tok, y_tok, y_tok, gates, h, g, b)


def _round_up(a, m):
    return (a + m - 1) // m * m


def _wide_tile(n, cap):
    return max(t for t in range(TOKEN_TILE, cap + 1, TOKEN_TILE) if n % t == 0)


def _moe(h, hp, idx, gates, rank, counts, w_up, b_up, w_down, b_down, ln_g, ln_b, alpha, layer, split=None):
    n, d = h.shape
    ns = TOP_K * n
    cnt = counts[:, 0]
    starts = jnp.cumsum(cnt) - cnt
    experts = jnp.arange(N_EXPERTS, dtype=I32)
    offs = jnp.sum(jnp.where(idx[:, :, None] == experts, starts, 0), axis=-1)
    pos = (offs + rank).reshape(-1).astype(I32)
    x_sorted = _sc_scatter_rows(hp, pos)
    y_sorted = _experts(x_sorted, _expert_visits(cnt, ns), w_up, b_up, w_down, b_down, layer)
    m2 = _round_up(ns, SC_ROW_ALIGN)
    fill = jnp.arange(ns, m2, dtype=I32) - ns
    y_tok = _sc_gather(y_sorted, jnp.concatenate([pos, fill]))
    return _combine(y_tok, gates.T, h, ln_g, ln_b, alpha, split)


def kernel(x_prompt, x_sample, cache_k, cache_v, state_conv, state_gla, w_in, conv_w, w_gate, b_gate,
           gla_norm_g, w_out, ln1_g, ln1_b, w_router, b_router, w_up, b_up, w_down, b_down, ln2_g, ln2_b):
    depth = w_in.shape[0]
    bp, seq, d = x_prompt.shape
    bs, ts, _ = x_sample.shape
    past = cache_k.shape[2]
    n_p = bp * seq
    n = n_p + bs * ts
    alpha = float((2 * depth) ** 0.25)
    x = jnp.concatenate([x_prompt.reshape(n_p, d), x_sample.reshape(bs * ts, d)], axis=0)
    ckt = cache_k.transpose(0, 1, 3, 4, 2).reshape(depth, bs, SB_DIM, past)
    cvt = cache_v.transpose(0, 1, 3, 4, 2).reshape(depth, bs, SB_DIM, past)
    zero_conv = jnp.zeros((bp, CONV_W - 1, CONV_DIM), F32)
    zero_gla = jnp.zeros((bp, GLA_HEADS, GLA_DK, GLA_DK), F32)
    wb = w_in.astype(BF16)
    o_r = 3 * SB_DIM
    n_r = 3 * CONV_DIM + 4 * GLA_DIM
    wq, wk, wv = wb[:, :, :SB_DIM], wb[:, :, SB_DIM:2 * SB_DIM], wb[:, :, 2 * SB_DIM:o_r]
    wkt, wvt = wk.transpose(0, 2, 1), wv.transpose(0, 2, 1)
    wr, wal = wb[:, :, o_r:o_r + n_r], wb[:, :, o_r + n_r:]
    wg = w_gate.astype(BF16)
    wo = w_out.astype(BF16)
    kt = jnp.zeros((depth, bp, SB_DIM, seq), F32)
    vt = jnp.zeros((depth, bp, SB_DIM, seq), F32)
    tm_p = min(PROMPT_TILE, seq)
    oa = jnp.zeros((n, SB_DIM), BF16)
    obc = jnp.zeros((n, CONV_DIM + GLA_DIM), BF16)
    outs = [[] for _ in range(6)]
    for l in range(depth):
        shared = (wr[l], wal[l], wg[l], b_gate[l].reshape(1, -1))
        q_p, kt, vt, obc, conv_p, gla_p = _mix_prompt(
            x, obc, kt, vt, l, (wq[l], wkt[l], wvt[l]) + shared,
            conv_w[l], gla_norm_g[l], zero_conv, zero_gla, bp, seq, tm_p)
        q_s, ks, vs, rest_s, la_s = _in_proj(x, n_p, n - n_p, TOKEN_TILE, (wq[l], wk[l], wv[l]) + shared)
        oa = _sb_prompt(oa, q_p, kt, vt, l, bp, seq)
        oa = _sb_decode(oa, q_s, ks, vs, ckt, cvt, l, n_p, bs, ts)
        obc, conv_s, gla_s = _conv_gla(obc, n, rest_s, la_s, conv_w[l], gla_norm_g[l], state_conv[l],
                                       state_gla[l], n_p, bs, ts)
        h, hp, idx, gates, rank, counts = _out_router(oa, obc, x, wo[l, :SB_DIM], wo[l, SB_DIM:], ln1_g[l],
                                                      ln1_b[l], w_router[l], b_router[l], alpha)
        x = _moe(h, hp, idx, gates, rank, counts, w_up, b_up, w_down, b_down, ln2_g[l], ln2_b[l],
                 alpha, l, split=n_p if l == depth - 1 else None)
        outs[0].append(conv_p)
        outs[1].append(gla_p)
        outs[2].append(ks.reshape(bs, ts, SB_HEADS, HEAD_DIM))
        outs[3].append(vs.reshape(bs, ts, SB_HEADS, HEAD_DIM))
        outs[4].append(conv_s)
        outs[5].append(gla_s)
    k_prompt = kt.reshape(depth, bp, SB_HEADS, HEAD_DIM, seq).transpose(0, 1, 4, 2, 3)
    v_prompt = vt.reshape(depth, bp, SB_HEADS, HEAD_DIM, seq).transpose(0, 1, 4, 2, 3)
    st = [jnp.stack(o) for o in outs]
    y_prompt, y_sample = x
    return (y_prompt.reshape(bp, seq, d), y_sample.reshape(bs, ts, d), k_prompt, v_prompt,
            st[0], st[1], st[2], st[3], st[4], st[5])
```

```python
import functools

import jax
import jax.numpy as jnp
from jax import lax
from jax.experimental import pallas as pl
from jax.experimental.pallas import tpu as pltpu
from jax.experimental.pallas import tpu_sc as plsc

F32 = jnp.float32
BF16 = jnp.bfloat16
I32 = jnp.int32
U32 = jnp.uint32

HEAD_DIM = 64
SB_HEADS = 8
SB_DIM = SB_HEADS * HEAD_DIM
CONV_DIM = 256
CONV_W = 3
GLA_HEADS = 4
GLA_DK = 64
GLA_DIM = GLA_HEADS * GLA_DK
GLA_TAU = 16.0
GLA_CHUNK = 64
N_EXPERTS = 32
TOP_K = 4
SWIGLU_LIMIT = 7.0
SWIGLU_ALPHA = 1.702
NORM_EPS = 1e-5

LANES = 128
SC_CORES = 2
SC_SUBCORES = 16
SC_CHUNK = 128
SC_ROW_ALIGN = SC_CORES * SC_SUBCORES * SC_CHUNK
SC_SCATTER_MAX = 104
TOKEN_TILE = 256
WIDE_TILE_MAX = 1280
COMBINE_TILE_MAX = 640
PROMPT_TILE = 512
EXPERT_TILE = 1024
EXPERT_SUB = 256
SB_TQ = 256
SB_BK = 256
SB_GROUP = 256
DEC_TK = 256
GLA_TC = 128
VMEM_LIMIT = 48 * 1024 * 1024
SB_DEAD = -100.0


def _dot(a, b):
    return jnp.dot(a, b, preferred_element_type=F32)


def _dot_nt(a, b):
    return lax.dot_general(a, b, (((1,), (1,)), ((), ())), preferred_element_type=F32)


def _split_bf16(x):
    hi = x.astype(BF16)
    lo = (x - hi.astype(F32)).astype(BF16)
    return hi, lo


def _dot_exact_rhs(x, m):
    hi, lo = _split_bf16(x)
    return _dot(hi, m) + _dot(lo, m)


def _dot_exact_lhs(m, x):
    hi, lo = _split_bf16(x)
    return _dot(m, hi) + _dot(m, lo)


def _pack_bf16_pairs(x):
    c = x.shape[1] // 2
    bits = lax.bitcast_convert_type(x.astype(BF16).astype(F32), U32)
    return (bits[:, :c] >> 16) | (bits[:, c:] & jnp.uint32(0xFFFF0000))


def _unpack_bf16_pairs(w):
    lo = lax.bitcast_convert_type(w << 16, F32)
    hi = lax.bitcast_convert_type(w & jnp.uint32(0xFFFF0000), F32)
    return jnp.concatenate([lo, hi], axis=1).astype(BF16)


def _softplus(z):
    return jnp.maximum(z, 0.0) + jnp.log(1.0 + jnp.exp(-jnp.abs(z)))


def _layer_norm(y, g, b):
    mu = jnp.mean(y, axis=-1, keepdims=True)
    yc = y - mu
    var = jnp.mean(yc * yc, axis=-1, keepdims=True)
    return yc * lax.rsqrt(var + NORM_EPS) * g + b


def _strict_upper(n, copies=1):
    r = lax.broadcasted_iota(I32, (copies * n, n), 0)
    c = lax.broadcasted_iota(I32, (copies * n, n), 1)
    for k in range(1, copies):
        r = r - jnp.where(r >= n, n, 0)
    return jnp.where(r > c, 1.0, 0.0).astype(BF16)


def _project(x_ref, wq_ref, wk_ref, wv_ref, wr_ref, wal_ref, wg_ref, bg_ref,
             q_ref, k_ref, v_ref, r_ref, la_ref, *, transposed_kv):
    xb = x_ref[...].astype(BF16)
    for c in range(0, SB_DIM, 256):
        q_ref[:, c:c + 256] = (_dot(xb, wq_ref[:, c:c + 256]) * (HEAD_DIM ** -0.5)).astype(BF16)
    for c in range(0, r_ref.shape[1], 256):
        r_ref[:, c:c + 256] = _dot(xb, wr_ref[:, c:c + 256])
    al = _dot(xb, wal_ref[...])
    g = _dot(al.astype(BF16), wg_ref[...]) + bg_ref[...]
    la_ref[...] = -_softplus(-g) * (1.0 / GLA_TAU)
    for c in range(0, SB_DIM, 256):
        if transposed_kv:
            k_ref[c:c + 256, :] = _dot_nt(wk_ref[c:c + 256, :], xb)
            v_ref[c:c + 256, :] = _dot_nt(wv_ref[c:c + 256, :], xb)
        else:
            k_ref[:, c:c + 256] = _dot(xb, wk_ref[:, c:c + 256])
            v_ref[:, c:c + 256] = _dot(xb, wv_ref[:, c:c + 256])


def _in_proj(x, row0, rows, tm, weights):
    d = x.shape[1]
    n_r = weights[3].shape[1]
    rb = row0 // tm
    full = lambda a: pl.BlockSpec(a.shape, lambda i: (0,) * a.ndim)
    row = lambda w: pl.BlockSpec((tm, w), lambda i: (i, 0))
    sds = lambda w, dt: jax.ShapeDtypeStruct((rows, w), dt)
    return pl.pallas_call(
        functools.partial(_project, transposed_kv=False),
        grid=(rows // tm,),
        in_specs=[pl.BlockSpec((tm, d), lambda i: (rb + i, 0))] + [full(w) for w in weights],
        out_specs=[row(SB_DIM), row(SB_DIM), row(SB_DIM), row(n_r), row(GLA_DIM)],
        out_shape=[sds(SB_DIM, BF16), sds(SB_DIM, F32), sds(SB_DIM, F32), sds(n_r, F32), sds(GLA_DIM, F32)],
        compiler_params=pltpu.CompilerParams(dimension_semantics=("arbitrary",),
                                             vmem_limit_bytes=VMEM_LIMIT),
        name="in_proj",
    )(x, *weights)


def _sb_weights(z, tri, run, mask):
    sp = _softplus(z)
    l1m = -sp
    lsig = z - sp
    if mask is not None:
        l1m = jnp.where(mask, l1m, 0.0)
    if tri.shape[0] == 2 * z.shape[1]:
        hi = lax.bitcast_convert_type(lax.bitcast_convert_type(l1m, U32) & jnp.uint32(0xFFFF0000), F32)
        parts = jnp.concatenate([hi.astype(BF16), (l1m - hi).astype(BF16)], axis=1)
        rest = _dot(parts, tri) + run
    else:
        rest = _dot_exact_rhs(l1m, tri) + run
    a = jnp.exp(lsig + rest)
    if mask is not None:
        a = jnp.where(mask, a, 0.0)
    return a.astype(BF16), run + jnp.sum(l1m, axis=1, keepdims=True)


def _sb_prompt_kernel(oa_ref, q_ref, kt_ref, vt_ref, o_ref, *, tq, bk):
    del oa_ref
    qi = pl.program_id(2)
    q = q_ref[...]
    n_hh = SB_GROUP // HEAD_DIM
    lane = lax.broadcasted_iota(I32, (1, SB_GROUP), 1)
    in_head = [(lane // HEAD_DIM) == h for h in range(n_hh)]
    qh = [jnp.where(m, q, jnp.zeros_like(q)) for m in in_head]
    tri = _strict_upper(bk, copies=2)
    n_full = (qi * tq) // bk
    qpos = qi * tq + lax.broadcasted_iota(I32, (tq, bk), 0)
    kpos = n_full * bk + lax.broadcasted_iota(I32, (tq, bk), 1)
    diag_mask = kpos < qpos

    def tile(jb, runs, mask):
        ks = pl.multiple_of(jb * bk, bk)
        kt = kt_ref[:, pl.ds(ks, bk)].astype(BF16)
        vt = vt_ref[:, pl.ds(ks, bk)].astype(BF16)
        out = jnp.zeros((tq, SB_GROUP), F32)
        new_runs = []
        for h in range(n_hh):
            a, run = _sb_weights(_dot(qh[h], kt), tri, runs[h], mask)
            out = jnp.where(in_head[h], _dot_nt(a, vt), out)
            new_runs.append(run)
        return out, tuple(new_runs)

    def alive_of(runs):
        m = jnp.max(runs[0])
        for r in runs[1:]:
            m = jnp.maximum(m, jnp.max(r))
        return m > SB_DEAD

    acc, runs = tile(n_full, tuple(jnp.zeros((tq, 1), F32) for _ in range(n_hh)), diag_mask)

    def cond(carry):
        j, alive, _, _ = carry
        return (j >= 0) & alive

    def body(carry):
        j, _, acc, runs = carry
        pv, runs = tile(j, runs, None)
        return j - 1, alive_of(runs), acc + pv, runs

    _, _, acc, _ = lax.while_loop(cond, body, (n_full - 1, jnp.bool_(True), acc, runs))
    o_ref[...] = acc.astype(o_ref.dtype)


def _sb_prompt(oa, q, kt, vt, layer, batch, seq):
    tq, bk = min(SB_TQ, seq), min(SB_BK, seq)
    nq = seq // tq
    hp = SB_DIM // SB_GROUP
    kv_spec = pl.BlockSpec((None, None, SB_GROUP, seq), lambda b, p, i: (layer, b, p, 0))
    return pl.pallas_call(
        functools.partial(_sb_prompt_kernel, tq=tq, bk=bk),
        grid=(batch, hp, nq),
        in_specs=[pl.BlockSpec(memory_space=pl.ANY),
                  pl.BlockSpec((tq, SB_GROUP), lambda b, p, i: (b * nq + i, p)), kv_spec, kv_spec],
        out_specs=pl.BlockSpec((tq, SB_GROUP), lambda b, p, i: (b * nq + i, p)),
        out_shape=jax.ShapeDtypeStruct(oa.shape, oa.dtype),
        input_output_aliases={0: 0},
        compiler_params=pltpu.CompilerParams(
            dimension_semantics=("arbitrary", "arbitrary", "arbitrary"),
            vmem_limit_bytes=VMEM_LIMIT),
        name="sb_prompt",
    )(oa, q, kt, vt)


def _sb_decode_kernel(oa_ref, q_ref, kn_ref, vn_ref, kc_hbm, vc_hbm, o_ref, kbuf, vbuf, sem, acc_ref, run_ref,
                      alive_ref, *, t, tk, bk, layer, nkb):
    del oa_ref
    b = pl.program_id(0)
    q = q_ref[...]

    def block_copies(j, slot):
        cols = pl.ds(pl.multiple_of((nkb - 1 - j) * tk, tk), tk)
        return (pltpu.make_async_copy(kc_hbm.at[layer, b, :, cols], kbuf.at[slot], sem.at[0, slot]),
                pltpu.make_async_copy(vc_hbm.at[layer, b, :, cols], vbuf.at[slot], sem.at[1, slot]))

    for cp in block_copies(0, 0):
        cp.start()

    kn = kn_ref[...].astype(BF16)
    vn = vn_ref[...].astype(BF16)
    lane = lax.broadcasted_iota(I32, (1, SB_DIM), 1)
    r = lax.broadcasted_iota(I32, (t, t), 0)
    c = lax.broadcasted_iota(I32, (t, t), 1)
    mask = c < r
    tri_new = _strict_upper(t)
    for h in range(SB_HEADS):
        qh = jnp.where((lane // HEAD_DIM) == h, q, jnp.zeros_like(q))
        a, run = _sb_weights(_dot_nt(qh, kn), tri_new, jnp.zeros((t, 1), F32), mask)
        acc_ref[h] = _dot(a, vn)[:, h * HEAD_DIM:(h + 1) * HEAD_DIM]
        run_ref[h * t:(h + 1) * t, :] = run
    alive_ref[0] = (jnp.max(run_ref[...]) > SB_DEAD).astype(I32)

    tri = _strict_upper(bk, copies=2)

    def cond(carry):
        j, alive = carry
        return (j < nkb) & (alive > 0)

    def body(carry):
        j, _ = carry
        slot = j % 2

        @pl.when(j + 1 < nkb)
        def _():
            for cp in block_copies(j + 1, 1 - slot):
                cp.start()

        for cp in block_copies(j, slot):
            cp.wait()
        for c in range(tk // bk - 1, -1, -1):
            cols = slice(c * bk, (c + 1) * bk)

            @pl.when(alive_ref[0] > 0)
            def _():
                z = jnp.concatenate(
                    [_dot(q[:, h * HEAD_DIM:(h + 1) * HEAD_DIM],
                          kbuf[slot, h * HEAD_DIM:(h + 1) * HEAD_DIM, cols].astype(BF16))
                     for h in range(SB_HEADS)], axis=0)
                a, run = _sb_weights(z, tri, run_ref[...], None)
                for h in range(SB_HEADS):
                    vt = vbuf[slot, h * HEAD_DIM:(h + 1) * HEAD_DIM, cols].astype(BF16)
                    acc_ref[h] = acc_ref[h] + _dot_nt(a[h * t:(h + 1) * t, :], vt)
                run_ref[...] = run
                alive_ref[0] = (jnp.max(run) > SB_DEAD).astype(I32)

        return j + 1, alive_ref[0]

    j_end, _ = lax.while_loop(cond, body, (jnp.int32(0), alive_ref[0]))

    @pl.when(j_end < nkb)
    def _():
        for cp in block_copies(j_end, j_end % 2):
            cp.wait()

    for h in range(SB_HEADS):
        o_ref[:, h * HEAD_DIM:(h + 1) * HEAD_DIM] = acc_ref[h].astype(o_ref.dtype)


def _sb_decode(oa, q, ks, vs, cache_kt, cache_vt, layer, row0, batch, t):
    past = cache_kt.shape[3]
    tk = min(DEC_TK, past)
    bk = min(SB_BK, tk)
    nkb = past // tk
    rb = row0 // t
    new = pl.BlockSpec((t, SB_DIM), lambda b: (b, 0))
    anywhere = pl.BlockSpec(memory_space=pl.ANY)
    return pl.pallas_call(
        functools.partial(_sb_decode_kernel, t=t, tk=tk, bk=bk, layer=layer, nkb=nkb),
        grid=(batch,),
        in_specs=[anywhere, new, new, new, anywhere, anywhere],
        out_specs=pl.BlockSpec((t, SB_DIM), lambda b: (rb + b, 0)),
        out_shape=jax.ShapeDtypeStruct(oa.shape, oa.dtype),
        input_output_aliases={0: 0},
        scratch_shapes=[pltpu.VMEM((2, SB_DIM, tk), F32),
                        pltpu.VMEM((2, SB_DIM, tk), F32),
                        pltpu.SemaphoreType.DMA((2, 2)),
                        pltpu.VMEM((SB_HEADS, t, HEAD_DIM), F32),
                        pltpu.VMEM((SB_HEADS * t, 1), F32),
                        pltpu.SMEM((1,), I32)],
        compiler_params=pltpu.CompilerParams(dimension_semantics=("arbitrary",),
                                             vmem_limit_bytes=VMEM_LIMIT),
        name="sb_decode",
    )(oa, q, ks, vs, cache_kt, cache_vt)


def _mix_state_load(cprev_ref, sprev_ref, ctail_ref, st_ref):
    n_hh = LANES // GLA_DK
    ctail_ref[...] = cprev_ref[...]
    for p in range(GLA_DIM // LANES):
        st_ref[p] = jnp.zeros((LANES, LANES), F32)
        for hh in range(n_hh):
            st_ref[p, hh * GLA_DK:(hh + 1) * GLA_DK, hh * GLA_DK:(hh + 1) * GLA_DK] = sprev_ref[p * n_hh + hh].T


def _mix_state_store(ctail_ref, st_ref, cnew_ref, snew_ref):
    n_hh = LANES // GLA_DK
    cnew_ref[...] = ctail_ref[...]
    for p in range(GLA_DIM // LANES):
        for hh in range(n_hh):
            blk = st_ref[p, hh * GLA_DK:(hh + 1) * GLA_DK, :]
            snew_ref[p * n_hh + hh] = blk.T[hh * GLA_DK:(hh + 1) * GLA_DK, :]


def _conv_gla_block(bg_ref, cg_ref, u_ref, qc_ref, kc_ref, vc_ref, gc_ref, la_ref, cw_ref, ng_ref,
                    o_ref, ctail_ref, st_ref, tc, chunk):
    n_pair = GLA_DIM // LANES
    n_hh = LANES // GLA_DK

    z = cg_ref[...] * u_ref[...]
    tail = ctail_ref[...]
    row = lax.broadcasted_iota(I32, z.shape, 0)
    z1 = jnp.where(row < 1, tail[1:2, :], pltpu.roll(z, 1, 0))
    z2 = jnp.where(row < 2, jnp.where(row < 1, tail[0:1, :], tail[1:2, :]), pltpu.roll(z, 2, 0))
    cw = cw_ref[...]
    y = z2 * cw[0:1, :] + z1 * cw[1:2, :] + z * cw[2:3, :]
    o_ref[:, 0:CONV_DIM] = (bg_ref[...] * y).astype(o_ref.dtype)
    ctail_ref[...] = z[tc - 2:tc, :]

    r = lax.broadcasted_iota(I32, (tc, tc), 0)
    c = lax.broadcasted_iota(I32, (tc, tc), 1)
    same_chunk = (r // chunk) == (c // chunk)
    chunk_sum = jnp.where(same_chunk, 1.0, 0.0).astype(BF16)
    chunk_cumsum = jnp.where(same_chunk & (c <= r), 1.0, 0.0).astype(BF16)
    lane = lax.broadcasted_iota(I32, (1, LANES), 1)
    lr = lax.broadcasted_iota(I32, (LANES, LANES), 0) // GLA_DK
    lc = lax.broadcasted_iota(I32, (LANES, LANES), 1) // GLA_DK
    same_head = lr == lc
    head_mean = jnp.where(same_head, 1.0 / GLA_DK, 0.0).astype(BF16)
    r2 = lax.broadcasted_iota(I32, (tc, n_hh * tc), 0)
    c2 = lax.broadcasted_iota(I32, (tc, n_hh * tc), 1)
    c2 = c2 - jnp.where(c2 >= tc, tc, 0)
    causal2 = ((r2 // chunk) == (c2 // chunk)) & (c2 <= r2)
    for p in range(n_pair):
        cols = slice(p * LANES, (p + 1) * LANES)
        la = la_ref[:, cols]
        b = _dot_exact_lhs(chunk_cumsum, la)
        tot = _dot_exact_lhs(chunk_sum, la)
        k = kc_ref[:, cols]
        qe = (qc_ref[:, cols] * (GLA_DK ** -0.5) * jnp.exp(b)).astype(BF16)
        ke = (k * jnp.exp(-b)).astype(BF16)
        kd = (k * jnp.exp(tot - b)).astype(BF16)
        decay = jnp.exp(tot)
        vb = vc_ref[:, cols].astype(BF16)
        zeros = jnp.zeros_like(ke)
        ke_st = jnp.concatenate([jnp.where((lane // GLA_DK) == hh, ke, zeros)
                                 for hh in range(n_hh)], axis=0)
        v_st = jnp.concatenate([jnp.where((lane // GLA_DK) == hh, vb, zeros)
                                for hh in range(n_hh)], axis=0)
        a = jnp.where(causal2, _dot_nt(qe, ke_st), 0.0)
        o = _dot(a.astype(BF16), v_st)
        st = st_ref[p]
        from_state = []
        for ci in range(tc // chunk):
            rows = slice(ci * chunk, (ci + 1) * chunk)
            from_state.append(_dot_nt(qe[rows], st.astype(BF16)))
            upd = _dot(vb[rows].T, kd[rows])
            st = jnp.where(same_head, st * decay[ci * chunk:ci * chunk + 1, :] + upd, 0.0)
        st_ref[p] = st
        o = o + jnp.concatenate(from_state, axis=0)
        ms = _dot_exact_rhs(o * o, head_mean)
        o = o * lax.rsqrt(ms + NORM_EPS) * ng_ref[:, cols]
        g = gc_ref[:, cols]
        o = o * (g * (1.0 / (1.0 + jnp.exp(-g))))
        o_ref[:, CONV_DIM + p * LANES:CONV_DIM + (p + 1) * LANES] = o.astype(o_ref.dtype)


def _conv_gla_kernel(*refs, tc, chunk):
    (_, bg_ref, cg_ref, u_ref, qc_ref, kc_ref, vc_ref, gc_ref, la_ref, cw_ref, ng_ref, cprev_ref, sprev_ref,
     o_ref, cnew_ref, snew_ref, ctail_ref, st_ref) = refs
    ti = pl.program_id(1)

    @pl.when(ti == 0)
    def _():
        _mix_state_load(cprev_ref, sprev_ref, ctail_ref, st_ref)

    _conv_gla_block(bg_ref, cg_ref, u_ref, qc_ref, kc_ref, vc_ref, gc_ref, la_ref, cw_ref, ng_ref,
                    o_ref, ctail_ref, st_ref, tc, chunk)

    @pl.when(ti == pl.num_programs(1) - 1)
    def _():
        _mix_state_store(ctail_ref, st_ref, cnew_ref, snew_ref)


def _conv_gla(obc_prev, n, rest, la, conv_w, norm_g, conv_prev, gla_prev, row0, batch, t):
    tc = min(GLA_TC, t)
    chunk = min(GLA_CHUNK, t)
    nt = t // tc
    rb = row0 // tc
    col = lambda j: pl.BlockSpec((tc, CONV_DIM), lambda b, i: (b * nt + i, j))
    const2 = lambda a: pl.BlockSpec(a.shape, lambda b, i: (0, 0))
    ng = norm_g.reshape(1, GLA_DIM)
    args = (rest,) * 7 + (la, conv_w, ng, conv_prev, gla_prev)
    return pl.pallas_call(
        functools.partial(_conv_gla_kernel, tc=tc, chunk=chunk),
        grid=(batch, nt),
        in_specs=[
            pl.BlockSpec(memory_space=pl.ANY), col(0), col(1), col(2), col(3), col(4), col(5), col(6),
            pl.BlockSpec((tc, GLA_DIM), lambda b, i: (b * nt + i, 0)),
            const2(conv_w), const2(ng),
            pl.BlockSpec((None, CONV_W - 1, CONV_DIM), lambda b, i: (b, 0, 0)),
            pl.BlockSpec((None, GLA_HEADS, GLA_DK, GLA_DK), lambda b, i: (b, 0, 0, 0))],
        out_specs=[pl.BlockSpec((tc, CONV_DIM + GLA_DIM), lambda b, i: (rb + b * nt + i, 0)),
                   pl.BlockSpec((None, CONV_W - 1, CONV_DIM), lambda b, i: (b, 0, 0)),
                   pl.BlockSpec((None, GLA_HEADS, GLA_DK, GLA_DK), lambda b, i: (b, 0, 0, 0))],
        out_shape=[jax.ShapeDtypeStruct((n, CONV_DIM + GLA_DIM), BF16),
                   jax.ShapeDtypeStruct((batch, CONV_W - 1, CONV_DIM), F32),
                   jax.ShapeDtypeStruct((batch, GLA_HEADS, GLA_DK, GLA_DK), F32)],
        input_output_aliases={0: 0},
        scratch_shapes=[pltpu.VMEM((CONV_W - 1, CONV_DIM), F32),
                        pltpu.VMEM((GLA_DIM // LANES, LANES, LANES), F32)],
        compiler_params=pltpu.CompilerParams(dimension_semantics=("arbitrary", "arbitrary"),
                                             vmem_limit_bytes=VMEM_LIMIT),
        name="conv_gla",
    )(obc_prev, *args)


def _mix_prompt_kernel(*refs, tc, chunk):
    (_, _, _, x_ref, wq_ref, wk_ref, wv_ref, wr_ref, wal_ref, wg_ref, bg_ref, cw_ref, ng_ref, cprev_ref,
     sprev_ref, q_ref, kt_ref, vt_ref, o_ref, cnew_ref, snew_ref, r_scr, la_scr, ctail_ref, st_ref) = refs
    ti = pl.program_id(1)
    _project(x_ref, wq_ref, wk_ref, wv_ref, wr_ref, wal_ref, wg_ref, bg_ref,
             q_ref, kt_ref, vt_ref, r_scr, la_scr, transposed_kv=True)

    @pl.when(ti == 0)
    def _():
        _mix_state_load(cprev_ref, sprev_ref, ctail_ref, st_ref)

    for s in range(x_ref.shape[0] // tc):
        rows = pl.ds(s * tc, tc)
        part = [r_scr.at[rows, pl.ds(j * CONV_DIM, CONV_DIM)] for j in range(7)]
        _conv_gla_block(*part, la_scr.at[rows], cw_ref, ng_ref, o_ref.at[rows], ctail_ref, st_ref, tc, chunk)

    @pl.when(ti == pl.num_programs(1) - 1)
    def _():
        _mix_state_store(ctail_ref, st_ref, cnew_ref, snew_ref)


def _mix_prompt(x, obc_prev, kt, vt, layer, weights, conv_w, norm_g, conv_prev, gla_prev, batch, seq, tm):
    d = x.shape[1]
    n_p = batch * seq
    nt = seq // tm
    tc = min(GLA_TC, tm)
    wq, wk, wv, wr, wal, wg, bg = weights
    ng = norm_g.reshape(1, GLA_DIM)
    full = lambda a: pl.BlockSpec(a.shape, lambda b, i: (0,) * a.ndim)
    row = lambda w: pl.BlockSpec((tm, w), lambda b, i: (b * nt + i, 0))
    kv_spec = pl.BlockSpec((None, None, SB_DIM, tm), lambda b, i: (layer, b, 0, i))
    conv_spec = pl.BlockSpec((None, CONV_W - 1, CONV_DIM), lambda b, i: (b, 0, 0))
    gla_spec = pl.BlockSpec((None, GLA_HEADS, GLA_DK, GLA_DK), lambda b, i: (b, 0, 0, 0))
    anywhere = pl.BlockSpec(memory_space=pl.ANY)
    consts = (wq, wk, wv, wr, wal, wg, bg, conv_w, ng)
    return pl.pallas_call(
        functools.partial(_mix_prompt_kernel, tc=tc, chunk=min(GLA_CHUNK, tc)),
        grid=(batch, nt),
        in_specs=[anywhere, anywhere, anywhere, row(d)] + [full(w) for w in consts] + [conv_spec, gla_spec],
        out_specs=[row(SB_DIM), kv_spec, kv_spec, row(CONV_DIM + GLA_DIM), conv_spec, gla_spec],
        out_shape=[jax.ShapeDtypeStruct((n_p, SB_DIM), BF16),
                   jax.ShapeDtypeStruct(kt.shape, kt.dtype), jax.ShapeDtypeStruct(vt.shape, vt.dtype),
                   jax.ShapeDtypeStruct(obc_prev.shape, obc_prev.dtype),
                   jax.ShapeDtypeStruct((batch, CONV_W - 1, CONV_DIM), F32),
                   jax.ShapeDtypeStruct((batch, GLA_HEADS, GLA_DK, GLA_DK), F32)],
        input_output_aliases={0: 1, 1: 2, 2: 3},
        scratch_shapes=[pltpu.VMEM((tm, wr.shape[1]), F32),
                        pltpu.VMEM((tm, GLA_DIM), F32),
                        pltpu.VMEM((CONV_W - 1, CONV_DIM), F32),
                        pltpu.VMEM((GLA_DIM // LANES, LANES, LANES), F32)],
        compiler_params=pltpu.CompilerParams(dimension_semantics=("arbitrary", "arbitrary"),
                                             vmem_limit_bytes=VMEM_LIMIT),
        name="mix_prompt",
    )(kt, vt, obc_prev, x, *consts, conv_prev, gla_prev)


def _out_router_kernel(oa_ref, obc_ref, x_ref, wa_ref, wb_ref, g_ref, b_ref, wr_ref, br_ref,
                       h_ref, hp_ref, idx_ref, gate_ref, rank_ref, cnt_ref, carry_ref, *, alpha):
    i = pl.program_id(0)
    tm = x_ref.shape[0]

    @pl.when(i == 0)
    def _():
        carry_ref[...] = jnp.zeros_like(carry_ref)

    m = _dot(oa_ref[...], wa_ref[...]) + _dot(obc_ref[...], wb_ref[...])
    h = _layer_norm(alpha * x_ref[...] + m, g_ref[...], b_ref[...])
    h_ref[...] = h
    hp_ref[...] = lax.bitcast_convert_type(_pack_bf16_pairs(h), F32)

    h_hi, h_lo = _split_bf16(h)
    w_hi, w_lo = _split_bf16(wr_ref[...])
    logit = _dot_nt(w_hi, h_hi) + _dot_nt(w_hi, h_lo) + _dot_nt(w_lo, h_hi) + br_ref[...]
    eid = lax.broadcasted_iota(I32, (N_EXPERTS, tm), 0)
    r = lax.broadcasted_iota(I32, (tm, tm), 0)
    c = lax.broadcasted_iota(I32, (tm, tm), 1)
    before = jnp.where(r < c, 1.0, 0.0).astype(BF16)
    base = carry_ref[...]
    vals, idxs, onehots, bases = [], [], [], []
    for _ in range(TOP_K):
        mx = jnp.max(logit, axis=0, keepdims=True)
        sel = jnp.min(jnp.where(logit == mx, eid, N_EXPERTS), axis=0, keepdims=True)
        hit = eid == sel
        logit = jnp.where(hit, -jnp.inf, logit)
        onehot = jnp.where(hit, 1.0, 0.0)
        onehots.append(onehot)
        bases.append(base)
        base = base + jnp.sum(onehot, axis=1, keepdims=True)
        vals.append(mx)
        idxs.append(sel)
    carry_ref[...] = base
    earlier = _dot(jnp.concatenate(onehots, axis=0).astype(BF16), before)
    ranks = [jnp.sum(onehots[k] * (earlier[k * N_EXPERTS:(k + 1) * N_EXPERTS] + bases[k]),
                     axis=0, keepdims=True) for k in range(TOP_K)]
    e = [jnp.exp(v - vals[0]) for v in vals]
    inv = 1.0 / (e[0] + e[1] + e[2] + e[3])
    idx_ref[...] = jnp.concatenate(idxs, axis=0)
    gate_ref[...] = jnp.concatenate([ek * inv for ek in e], axis=0)
    rank_ref[...] = jnp.concatenate(ranks, axis=0).astype(I32)
    cnt_ref[...] = jnp.broadcast_to(base, cnt_ref.shape).astype(I32)


def _out_router(oa, obc, x, wa, wb, ln_g, ln_b, w_router, b_router, alpha):
    n, d = x.shape
    tm = _wide_tile(n, WIDE_TILE_MAX)
    wr = w_router.T
    br = b_router.reshape(N_EXPERTS, 1)
    g, b = ln_g.reshape(1, d), ln_b.reshape(1, d)
    full = lambda a: pl.BlockSpec(a.shape, lambda i: (0, 0))
    row = lambda w: pl.BlockSpec((tm, w), lambda i: (i, 0))
    colb = pl.BlockSpec((TOP_K, tm), lambda i: (0, i))
    return pl.pallas_call(
        functools.partial(_out_router_kernel, alpha=alpha),
        grid=(n // tm,),
        in_specs=[row(SB_DIM), row(CONV_DIM + GLA_DIM), row(d), full(wa), full(wb), full(g), full(b),
                  full(wr), full(br)],
        out_specs=[row(d), row(d // 2), colb, colb, colb,
                   pl.BlockSpec((N_EXPERTS, LANES), lambda i: (0, 0))],
        out_shape=[jax.ShapeDtypeStruct((n, d), F32),
                   jax.ShapeDtypeStruct((n, d // 2), F32),
                   jax.ShapeDtypeStruct((TOP_K, n), I32),
                   jax.ShapeDtypeStruct((TOP_K, n), F32),
                   jax.ShapeDtypeStruct((TOP_K, n), I32),
                   jax.ShapeDtypeStruct((N_EXPERTS, LANES), I32)],
        scratch_shapes=[pltpu.VMEM((N_EXPERTS, 1), F32)],
        compiler_params=pltpu.CompilerParams(dimension_semantics=("arbitrary",),
                                             vmem_limit_bytes=VMEM_LIMIT),
        name="out_router",
    )(oa, obc, x, wa, wb, g, b, wr, br)


def _sc_gather(table, idx):
    m = idx.shape[0]
    d = table.shape[1]
    per_worker = m // (SC_CORES * SC_SUBCORES)
    half = SC_CHUNK // 2
    n_groups = per_worker // SC_CHUNK
    mesh = plsc.VectorSubcoreMesh(core_axis_name="c", subcore_axis_name="s")

    @functools.partial(
        pl.kernel, mesh=mesh,
        out_type=jax.ShapeDtypeStruct((m, d), table.dtype),
        scratch_types=[pltpu.VMEM((per_worker,), I32)] + [pltpu.VMEM((half, d), table.dtype)] * 2
                      + [pltpu.SemaphoreType.DMA] * 4,
        name="sc_gather",
    )
    def gather(table_hbm, idx_hbm, out_hbm, idx_v, buf0, buf1, g0, g1, w0, w1):
        wid = lax.axis_index("s") * SC_CORES + lax.axis_index("c")
        base = wid * per_worker
        pltpu.sync_copy(idx_hbm.at[pl.ds(base, per_worker)], idx_v)

        def fetch(c, buf, sem):
            off = pl.multiple_of(c * half, 8)
            return pltpu.make_async_copy(table_hbm.at[idx_v.at[pl.ds(off, half)]], buf, sem)

        def put(c, buf, sem):
            off = pl.multiple_of(c * half, 8)
            return pltpu.make_async_copy(buf, out_hbm.at[pl.ds(base + off, half)], sem)

        fetch(0, buf0, g0).start()

        @pl.loop(0, n_groups)
        def _(g):
            c0 = 2 * g
            c1 = c0 + 1

            @pl.when(g > 0)
            def _():
                put(c0 - 1, buf1, w1).wait()

            fetch(c1, buf1, g1).start()
            fetch(c0, buf0, g0).wait()
            put(c0, buf0, w0).start()
            put(c0, buf0, w0).wait()

            @pl.when(g + 1 < n_groups)
            def _():
                fetch(c0 + 2, buf0, g0).start()

            fetch(c1, buf1, g1).wait()
            put(c1, buf1, w1).start()

        put(2 * n_groups - 1, buf1, w1).wait()

    return gather(table, idx)


def _sc_chunk(per_worker):
    return max(c for c in range(8, SC_SCATTER_MAX + 1, 8) if per_worker % c == 0)


def _sc_scatter_rows(h, pos):
    n, d = h.shape
    per_worker = n // (SC_CORES * SC_SUBCORES)
    ch = _sc_chunk(per_worker)
    n_chunks = per_worker // ch
    mesh = plsc.VectorSubcoreMesh(core_axis_name="c", subcore_axis_name="s")

    @functools.partial(
        pl.kernel, mesh=mesh,
        out_type=jax.ShapeDtypeStruct((TOP_K * n, d), h.dtype),
        scratch_types=[pltpu.VMEM((TOP_K * n_chunks, ch), I32), pltpu.VMEM((ch, d), h.dtype),
                       pltpu.VMEM((ch, d), h.dtype)] + [pltpu.SemaphoreType.DMA] * 5,
        name="sc_scatter",
    )
    def scatter(h_hbm, pos_hbm, out_hbm, idx_v, rows0, rows1, isem, r0, r1, w0, w1):
        bufs, rsem, wsem = (rows0, rows1), (r0, r1), (w0, w1)
        wid = lax.axis_index("s") * SC_CORES + lax.axis_index("c")
        base = pl.multiple_of(wid * per_worker, 8)

        def rows_load(c):
            return pltpu.async_copy(h_hbm.at[pl.ds(base + c * ch, ch)], bufs[c % 2], rsem[c % 2])

        idx_loads = [pltpu.async_copy(pos_hbm.at[pl.ds(k * n + base + c * ch, ch)],
                                      idx_v.at[k * n_chunks + c], isem)
                     for k in range(TOP_K) for c in range(n_chunks)]
        load = rows_load(0)
        for cp in idx_loads:
            cp.wait()
        for c in range(n_chunks):
            nxt = rows_load(c + 1) if c + 1 < n_chunks else None
            load.wait()
            writes = [pltpu.async_copy(bufs[c % 2], out_hbm.at[idx_v.at[k * n_chunks + c]], wsem[c % 2])
                      for k in range(TOP_K)]
            for cp in writes:
                cp.wait()
            load = nxt

    return scatter(h, pos)


def _expert_kernel(vt_ref, ve_ref, lo_ref, hi_ref, ord_ref, nxt_ref, x_ref, wu_hbm, bu_ref, wd_hbm, bd_ref,
                   y_ref, wu32_ref, wd32_ref, sem, wu16_ref, wd16_ref, *, layer):
    i = pl.program_id(0)
    ip = jnp.maximum(i - 1, 0)
    e = ve_ref[i]
    tile = vt_ref[i]
    first_visit = (i == 0) | (tile != vt_ref[ip])
    lo = lo_ref[i]
    hi = hi_ref[i]
    tm = x_ref.shape[0]
    dff = wd16_ref.shape[0]

    def weight_copies(expert, slot):
        return (pltpu.make_async_copy(wu_hbm.at[layer, expert], wu32_ref.at[slot], sem.at[0, slot]),
                pltpu.make_async_copy(wd_hbm.at[layer, expert], wd32_ref.at[slot], sem.at[1, slot]))

    @pl.when(i == 0)
    def _():
        for cp in weight_copies(e, 0):
            cp.start()

    @pl.when((i == 0) | (e != ve_ref[ip]))
    def _():
        slot = ord_ref[i] % 2
        for cp in weight_copies(e, slot):
            cp.wait()

        @pl.when(nxt_ref[i] >= 0)
        def _():
            for cp in weight_copies(nxt_ref[i], 1 - slot):
                cp.start()

        step = 128

        def cast(s, _):
            rows = pl.ds(pl.multiple_of(s * step, step), step)
            wu16_ref[rows, :] = wu32_ref[slot, rows, :].astype(BF16)
            return 0

        lax.fori_loop(0, wu16_ref.shape[0] // step, cast, 0)

        def cast_d(s, _):
            rows = pl.ds(pl.multiple_of(s * step, step), step)
            wd16_ref[rows, :] = wd32_ref[slot, rows, :].astype(BF16)
            return 0

        lax.fori_loop(0, dff // step, cast_d, 0)

    @pl.when(first_visit)
    def _():
        y_ref[...] = jnp.zeros_like(y_ref)

    def ffn(rows):
        x = _unpack_bf16_pairs(lax.bitcast_convert_type(x_ref[rows, :], U32))
        glu = jnp.minimum(_dot(x, wu16_ref[:, :dff]) + bu_ref[:, :dff], SWIGLU_LIMIT)
        lin = jnp.clip(_dot(x, wu16_ref[:, dff:]) + bu_ref[:, dff:], -SWIGLU_LIMIT, SWIGLU_LIMIT)
        act = glu * (1.0 / (1.0 + jnp.exp(-SWIGLU_ALPHA * glu))) * (lin + 1.0)
        y = _dot(act.astype(BF16), wd16_ref[...]) + bd_ref[...]
        return lax.bitcast_convert_type(_pack_bf16_pairs(y), F32)

    subs = [slice(s * EXPERT_SUB, (s + 1) * EXPERT_SUB) for s in range(tm // EXPERT_SUB)]
    whole = (lo <= tile * tm) & (hi >= tile * tm + tm)

    @pl.when(whole)
    def _():
        for rows in subs:
            y_ref[rows, :] = ffn(rows)

    for s, rows in enumerate(subs):
        r0 = tile * tm + s * EXPERT_SUB

        @pl.when(jnp.logical_not(whole) & (hi > r0) & (lo < r0 + EXPERT_SUB))
        def _():
            row = r0 + lax.broadcasted_iota(I32, (EXPERT_SUB, 1), 0)
            y_ref[rows, :] = jnp.where((row >= lo) & (row < hi), ffn(rows), y_ref[rows, :])


def _experts(x_sorted, visits, w_up, b_up, w_down, b_down, layer):
    ns = x_sorted.shape[0]
    tm = EXPERT_TILE
    d, dff = w_down.shape[3], w_down.shape[2]
    bu = b_up.reshape(b_up.shape[0], N_EXPERTS, 1, 2 * dff)
    bd = b_down.reshape(b_down.shape[0], N_EXPERTS, 1, d)
    wmap = lambda i, vt, ve, *_: (layer, ve[i], 0, 0)
    xmap = lambda i, vt, *_: (vt[i], 0)
    anywhere = pl.BlockSpec(memory_space=pl.ANY)
    grid_spec = pltpu.PrefetchScalarGridSpec(
        num_scalar_prefetch=len(visits),
        grid=(visits[0].shape[0],),
        in_specs=[pl.BlockSpec((tm, x_sorted.shape[1]), xmap),
                  anywhere,
                  pl.BlockSpec((None, None, 1, 2 * dff), wmap),
                  anywhere,
                  pl.BlockSpec((None, None, 1, d), wmap)],
        out_specs=pl.BlockSpec((tm, d // 2), xmap),
        scratch_shapes=[pltpu.VMEM((2, d, 2 * dff), F32), pltpu.VMEM((2, dff, d), F32),
                        pltpu.SemaphoreType.DMA((2, 2)),
                        pltpu.VMEM((d, 2 * dff), BF16), pltpu.VMEM((dff, d), BF16)],
    )
    return pl.pallas_call(
        functools.partial(_expert_kernel, layer=layer),
        grid_spec=grid_spec,
        out_shape=jax.ShapeDtypeStruct((ns, d // 2), F32),
        compiler_params=pltpu.CompilerParams(dimension_semantics=("arbitrary",),
                                             vmem_limit_bytes=VMEM_LIMIT),
        name="experts",
    )(*visits, x_sorted, w_up, bu, w_down, bd)


def _expert_visits(cnt, n_rows):
    tm = EXPERT_TILE
    n_steps = n_rows // tm + N_EXPERTS
    ends = jnp.cumsum(cnt)
    starts = ends - cnt
    first_tile = starts // tm
    n_vis = jnp.where(cnt > 0, (ends - 1) // tm - first_tile + 1, 0)
    vis_end = jnp.cumsum(n_vis)
    vis_start = vis_end - n_vis
    v = jnp.arange(n_steps, dtype=I32)
    vc = jnp.minimum(v, vis_end[-1] - 1)
    onehot = ((vis_start[None, :] <= vc[:, None]) & (vc[:, None] < vis_end[None, :])).astype(I32)
    pick = lambda a: jnp.sum(onehot * a[None, :], axis=1).astype(I32)
    expert = pick(jnp.arange(N_EXPERTS, dtype=I32))
    tile = pick(first_tile) + vc - pick(vis_start)
    real = v < vis_end[-1]
    lo = jnp.where(real, pick(starts), 0).astype(I32)
    hi = jnp.where(real, pick(ends), 0).astype(I32)
    ids = jnp.arange(N_EXPERTS, dtype=I32)
    used = cnt > 0
    ordinal = jnp.cumsum(used.astype(I32)) - used.astype(I32)
    later = jnp.min(jnp.where((ids[None, :] > ids[:, None]) & used[None, :], ids[None, :], N_EXPERTS), axis=1)
    following = jnp.where(later < N_EXPERTS, later, -1).astype(I32)
    return tile.astype(I32), expert, lo, hi, pick(ordinal), pick(following)


def _combine_kernel(y0_ref, y1_ref, y2_ref, y3_ref, gate_ref, h_ref, g_ref, b_ref, *o_refs, alpha, first_tiles):
    gate = gate_ref[...]
    lo = jnp.zeros(y0_ref.shape, F32)
    hi = jnp.zeros(y0_ref.shape, F32)
    for k, y_ref in enumerate((y0_ref, y1_ref, y2_ref, y3_ref)):
        w = lax.bitcast_convert_type(y_ref[...], U32)
        lo = lo + gate[:, k:k + 1] * lax.bitcast_convert_type(w << 16, F32)
        hi = hi + gate[:, k:k + 1] * lax.bitcast_convert_type(w & jnp.uint32(0xFFFF0000), F32)
    acc = jnp.concatenate([lo, hi], axis=1)
    out = _layer_norm(alpha * h_ref[...] + acc, g_ref[...], b_ref[...])
    if first_tiles is None:
        o_refs[0][...] = out
    else:
        i = pl.program_id(0)

        @pl.when(i < first_tiles)
        def _():
            o_refs[0][...] = out

        @pl.when(i >= first_tiles)
        def _():
            o_refs[1][...] = out


def _combine(y_tok, gates, h, ln_g, ln_b, alpha, split=None):
    n, d = h.shape
    tm = _wide_tile(n, COMBINE_TILE_MAX) if split is None else TOKEN_TILE
    nt = n // tm
    g, b = ln_g.reshape(1, d), ln_b.reshape(1, d)
    full = lambda a: pl.BlockSpec(a.shape, lambda i: (0, 0))
    ysp = lambda k: pl.BlockSpec((tm, d // 2), lambda i: (k * nt + i, 0))
    if split is None:
        first_tiles = None
        out_specs = pl.BlockSpec((tm, d), lambda i: (i, 0))
        out_shape = jax.ShapeDtypeStruct((n, d), F32)
    else:
        first_tiles = split // tm
        out_specs = [pl.BlockSpec((tm, d), lambda i: (jnp.minimum(i, first_tiles - 1), 0)),
                     pl.BlockSpec((tm, d), lambda i: (jnp.maximum(i - first_tiles, 0), 0))]
        out_shape = [jax.ShapeDtypeStruct((split, d), F32), jax.ShapeDtypeStruct((n - split, d), F32)]
    return pl.pallas_call(
        functools.partial(_combine_kernel, alpha=alpha, first_tiles=first_tiles),
        grid=(nt,),
        in_specs=[ysp(0), ysp(1), ysp(2), ysp(3),
                  pl.BlockSpec((tm, TOP_K), lambda i: (i, 0)),
                  pl.BlockSpec((tm, d), lambda i: (i, 0)), full(g), full(b)],
        out_specs=out_specs,
        out_shape=out_shape,
        compiler_params=pltpu.CompilerParams(dimension_semantics=("arbitrary",),
                                             vmem_limit_bytes=VMEM_LIMIT),
        name="combine",
    )(y_tok, y_tok, y_tok, y_tok, gates, h, g, b)


def _round_up(a, m):
    return (a + m - 1) // m * m


def _wide_tile(n, cap):
    return max(t for t in range(TOKEN_TILE, cap + 1, TOKEN_TILE) if n % t == 0)


def _moe(h, hp, idx, gates, rank, counts, w_up, b_up, w_down, b_down, ln_g, ln_b, alpha, layer, split=None):
    n, d = h.shape
    ns = TOP_K * n
    cnt = counts[:, 0]
    starts = jnp.cumsum(cnt) - cnt
    experts = jnp.arange(N_EXPERTS, dtype=I32)
    offs = jnp.sum(jnp.where(idx[:, :, None] == experts, starts, 0), axis=-1)
    pos = (offs + rank).reshape(-1).astype(I32)
    x_sorted = _sc_scatter_rows(hp, pos)
    y_sorted = _experts(x_sorted, _expert_visits(cnt, ns), w_up, b_up, w_down, b_down, layer)
    m2 = _round_up(ns, SC_ROW_ALIGN)
    fill = jnp.arange(ns, m2, dtype=I32) - ns
    y_tok = _sc_gather(y_sorted, jnp.concatenate([pos, fill]))
    return _combine(y_tok, gates.T, h, ln_g, ln_b, alpha, split)


def kernel(x_prompt, x_sample, cache_k, cache_v, state_conv, state_gla, w_in, conv_w, w_gate, b_gate,
           gla_norm_g, w_out, ln1_g, ln1_b, w_router, b_router, w_up, b_up, w_down, b_down, ln2_g, ln2_b):
    depth = w_in.shape[0]
    bp, seq, d = x_prompt.shape
    bs, ts, _ = x_sample.shape
    past = cache_k.shape[2]
    n_p = bp * seq
    n = n_p + bs * ts
    alpha = float((2 * depth) ** 0.25)
    x = jnp.concatenate([x_prompt.reshape(n_p, d), x_sample.reshape(bs * ts, d)], axis=0)
    ckt = cache_k.transpose(0, 1, 3, 4, 2).reshape(depth, bs, SB_DIM, past)
    cvt = cache_v.transpose(0, 1, 3, 4, 2).reshape(depth, bs, SB_DIM, past)
    zero_conv = jnp.zeros((bp, CONV_W - 1, CONV_DIM), F32)
    zero_gla = jnp.zeros((bp, GLA_HEADS, GLA_DK, GLA_DK), F32)
    wb = w_in.astype(BF16)
    o_r = 3 * SB_DIM
    n_r = 3 * CONV_DIM + 4 * GLA_DIM
    wq, wk, wv = wb[:, :, :SB_DIM], wb[:, :, SB_DIM:2 * SB_DIM], wb[:, :, 2 * SB_DIM:o_r]
    wkt, wvt = wk.transpose(0, 2, 1), wv.transpose(0, 2, 1)
    wr, wal = wb[:, :, o_r:o_r + n_r], wb[:, :, o_r + n_r:]
    wg = w_gate.astype(BF16)
    wo = w_out.astype(BF16)
    kt = jnp.zeros((depth, bp, SB_DIM, seq), F32)
    vt = jnp.zeros((depth, bp, SB_DIM, seq), F32)
    tm_p = min(PROMPT_TILE, seq)
    oa = jnp.zeros((n, SB_DIM), BF16)
    obc = jnp.zeros((n, CONV_DIM + GLA_DIM), BF16)
    outs = [[] for _ in range(6)]
    for l in range(depth):
        shared = (wr[l], wal[l], wg[l], b_gate[l].reshape(1, -1))
        q_p, kt, vt, obc, conv_p, gla_p = _mix_prompt(
            x, obc, kt, vt, l, (wq[l], wkt[l], wvt[l]) + shared,
            conv_w[l], gla_norm_g[l], zero_conv, zero_gla, bp, seq, tm_p)
        q_s, ks, vs, rest_s, la_s = _in_proj(x, n_p, n - n_p, TOKEN_TILE, (wq[l], wk[l], wv[l]) + shared)
        oa = _sb_prompt(oa, q_p, kt, vt, l, bp, seq)
        oa = _sb_decode(oa, q_s, ks, vs, ckt, cvt, l, n_p, bs, ts)
        obc, conv_s, gla_s = _conv_gla(obc, n, rest_s, la_s, conv_w[l], gla_norm_g[l], state_conv[l],
                                       state_gla[l], n_p, bs, ts)
        h, hp, idx, gates, rank, counts = _out_router(oa, obc, x, wo[l, :SB_DIM], wo[l, SB_DIM:], ln1_g[l],
                                                      ln1_b[l], w_router[l], b_router[l], alpha)
        x = _moe(h, hp, idx, gates, rank, counts, w_up, b_up, w_down, b_down, ln2_g[l], ln2_b[l],
                 alpha, l, split=n_p if l == depth - 1 else None)
        outs[0].append(conv_p)
        outs[1].append(gla_p)
        outs[2].append(ks.reshape(bs, ts, SB_HEADS, HEAD_DIM))
        outs[3].append(vs.reshape(bs, ts, SB_HEADS, HEAD_DIM))
        outs[4].append(conv_s)
        outs[5].append(gla_s)
    k_prompt = kt.reshape(depth, bp, SB_HEADS, HEAD_DIM, seq).transpose(0, 1, 4, 2, 3)
    v_prompt = vt.reshape(depth, bp, SB_HEADS, HEAD_DIM, seq).transpose(0, 1, 4, 2, 3)
    st = [jnp.stack(o) for o in outs]
    y_prompt, y_sample = x
    return (y_prompt.reshape(bp, seq, d), y_sample.reshape(bs, ts, d), k_prompt, v_prompt,
            st[0], st[1], st[2], st[3], st[4], st[5])
```
